```python
import jax
import jax.numpy as jnp
from jax import lax
import numpy as np

D_MODEL = 2048
BATCH = 2
SEQ = 4096
DEPTH = 4

GRID_W = 64
CTX_LEN = 256
N_MIXERS = 3
NORM_EPS = 1e-6

RW_HEAD = 64
RW_HEADS = D_MODEL // RW_HEAD
RW_DECAY_LORA = 96
RW_AAA_LORA = 96
RW_MV_LORA = 64
RW_GATE_LORA = 256
RW_GN_EPS = 64e-5

ATT_HEAD = 128
ATT_HEADS = D_MODEL // ATT_HEAD
GQA_KV_HEADS = 4
GQA_GROUP = ATT_HEADS // GQA_KV_HEADS
KV_DIM = GQA_KV_HEADS * ATT_HEAD
ROPE_THETA = 10000.0
Q_BLOCK = 128
NA_WIN_R = 8
NA_WIN_C = 16

D_FF = ((8 * D_MODEL // 3 + 255) // 256) * 256
CONV_W = 3

kernel_name = 'hybrid_rwkv7_natten_gqa_convffn_dit'


def rms_norm(x, g):
    xf = x.astype(jnp.float32)
    y = xf * lax.rsqrt(jnp.mean(xf * xf, axis=-1, keepdims=True) + NORM_EPS)
    return (y * g.astype(jnp.float32)).astype(x.dtype)


def modulate(h, shift, scale):
    return h * (1 + scale) + shift


def centred_shift(h):
    hp = jnp.pad(h, ((0, 0), (1, 1), (0, 0)))
    return 0.5 * (hp[:, :-2] + hp[:, 2:]) - h


def dwconv_seq(u, w, b):
    t = u.shape[1]
    p = CONV_W // 2
    up = jnp.pad(u, ((0, 0), (p, p), (0, 0)))
    y = b
    for j in range(CONV_W):
        y = y + up[:, j:j + t] * w[j]
    return y


def conv_ffn(h, w_up, conv_w, conv_b, w_down):
    u = dwconv_seq(h @ w_up, conv_w, conv_b)
    gate, val = jnp.split(u, 2, axis=-1)
    return (jax.nn.silu(gate) * val) @ w_down


def axial_rope(n_tok):
    t = jnp.arange(n_tok, dtype=jnp.int32)
    pos = jnp.stack([t // GRID_W, t % GRID_W], axis=-1).astype(jnp.float32)
    n_freq = ATT_HEAD // 4
    inv = ROPE_THETA ** (-jnp.arange(n_freq, dtype=jnp.float32) / n_freq)
    ang = pos[:, :, None] * inv
    return jnp.cos(ang), jnp.sin(ang)


def apply_rope(x, cos, sin):
    xf = x.astype(jnp.float32).reshape(x.shape[:-1] + (2, 2, ATT_HEAD // 4))
    x1, x2 = xf[..., 0, :], xf[..., 1, :]
    cs, sn = cos[None, :, None], sin[None, :, None]
    out = jnp.stack([x1 * cs - x2 * sn, x1 * sn + x2 * cs], axis=-2)
    return out.reshape(x.shape).astype(x.dtype)


def attend(q, k, v):
    s = jnp.einsum('bqhgd,bkhd->bhgqk', q, k, preferred_element_type=jnp.float32) * (ATT_HEAD ** -0.5)
    p = jax.nn.softmax(s, axis=-1).astype(v.dtype)
    return jnp.einsum('bhgqk,bkhd->bqhgd', p, v)


def gqa_mixer(hc, hl, w_qkv, q_g, k_g, w_o, cos, sin, need_ctx):
    b, n_lat, _ = hl.shape

    def proj(h):
        t = h.shape[1]
        q, k, v = jnp.split(h @ w_qkv, [D_MODEL, D_MODEL + KV_DIM], axis=-1)
        q = rms_norm(q.reshape(b, t, ATT_HEADS, ATT_HEAD), q_g)
        k = rms_norm(k.reshape(b, t, GQA_KV_HEADS, ATT_HEAD), k_g)
        return q, k, v.reshape(b, t, GQA_KV_HEADS, ATT_HEAD)

    ql, kl, vl = proj(hl)
    qc, kc, vc = proj(hc)
    ql = apply_rope(ql, cos, sin)
    kl = apply_rope(kl, cos, sin)
    keys = jnp.concatenate([kl, kc], axis=1)
    vals = jnp.concatenate([vl, vc], axis=1)
    nb = n_lat // Q_BLOCK
    qb = ql.reshape(b, nb, Q_BLOCK, GQA_KV_HEADS, GQA_GROUP, ATT_HEAD).swapaxes(0, 1)
    ob = lax.map(lambda qblk: attend(qblk, keys, vals), qb)
    out_l = ob.swapaxes(0, 1).reshape(b, n_lat, D_MODEL) @ w_o
    out_c = None
    if need_ctx:
        tc = hc.shape[1]
        oc = attend(qc.reshape(b, tc, GQA_KV_HEADS, GQA_GROUP, ATT_HEAD), kc, vc)
        out_c = oc.reshape(b, tc, D_MODEL) @ w_o
    return out_l, out_c


def na_mixer(hc, hl, w_qkv, q_g, k_g, rpb, w_o, need_ctx):
    b, n_lat, _ = hl.shape
    rows = n_lat // GRID_W
    kr = min(NA_WIN_R, rows)
    n_win = kr * NA_WIN_C
    scale = ATT_HEAD ** -0.5

    def proj(h):
        t = h.shape[1]
        q, k, v = jnp.split(h @ w_qkv, 3, axis=-1)
        shp = (b, t, ATT_HEADS, ATT_HEAD)
        return rms_norm(q.reshape(shp), q_g), rms_norm(k.reshape(shp), k_g), v.reshape(shp)

    ql, kl, vl = proj(hl)
    qc, kc, vc = proj(hc)
    grid = (b, rows, GRID_W, ATT_HEADS, ATT_HEAD)
    kg, vg = kl.reshape(grid), vl.reshape(grid)
    cols = jnp.arange(GRID_W)
    col_idx = jnp.clip(cols - NA_WIN_C // 2, 0, GRID_W - NA_WIN_C)[:, None] + jnp.arange(NA_WIN_C)
    rpb_c = rpb[:, :, col_idx - cols[:, None] + NA_WIN_C - 1]

    def row_block(args):
        r, q_r = args
        rs = jnp.clip(r - kr // 2, 0, rows - kr)
        k_win = lax.dynamic_slice_in_dim(kg, rs, kr, axis=1)[:, :, col_idx]
        v_win = lax.dynamic_slice_in_dim(vg, rs, kr, axis=1)[:, :, col_idx]
        bias = rpb_c[:, rs + jnp.arange(kr) - r + NA_WIN_R - 1]
        s_win = (jnp.einsum('bqhd,brqkhd->bhqrk', q_r, k_win, preferred_element_type=jnp.float32) * scale
                 + bias.transpose(0, 2, 1, 3)[None].astype(jnp.float32))
        s_ctx = jnp.einsum('bqhd,bkhd->bhqk', q_r, kc, preferred_element_type=jnp.float32) * scale
        p = jax.nn.softmax(jnp.concatenate([s_win.reshape(b, ATT_HEADS, GRID_W, n_win), s_ctx], axis=-1),
                           axis=-1).astype(v_win.dtype)
        return (jnp.einsum('bhqrk,brqkhd->bqhd', p[..., :n_win].reshape(s_win.shape), v_win)
                + jnp.einsum('bhqk,bkhd->bqhd', p[..., n_win:], vc))

    q_rows = ql.reshape(grid).swapaxes(0, 1)
    o = lax.map(row_block, (jnp.arange(rows), q_rows))
    out_l = o.swapaxes(0, 1).reshape(b, n_lat, D_MODEL) @ w_o
    out_c = None
    if need_ctx:
        tc = hc.shape[1]
        oc = attend(qc.reshape(b, tc, ATT_HEADS, 1, ATT_HEAD), kc, vc)
        out_c = oc.reshape(b, tc, D_MODEL) @ w_o
    return out_l, out_c


def to_heads(t):
    return t.reshape(t.shape[:-1] + (RW_HEADS, RW_HEAD)).astype(jnp.float32)


def dir_lora(z, p1, p2, act):
    return jnp.einsum('zbtl,zld->zbtd', act(jnp.einsum('btd,zdl->zbtl', z, p1)), p2)


def rwkv_stream(h, mu, w_rkv, w0, w1, w2, a0, a1, a2):
    xx = centred_shift(h)
    xm = h[None] + xx[None] * mu[:, None, None, :]
    r, k, v = jnp.einsum('pbtd,pde->pbte', xm[:3], w_rkv)
    wl = (w0[:, None, None] + dir_lora(xm[3], w1, w2, jnp.tanh)).astype(jnp.float32)
    decay = jnp.exp(-jnp.exp(-jax.nn.softplus(-wl) - 0.5))
    a = jax.nn.sigmoid((a0[:, None, None] + dir_lora(xm[4], a1, a2, lambda t: t)).astype(jnp.float32))
    return r, k, v, xm[2], xm[5], to_heads(decay), to_heads(a)


def wkv_scan(r, w, kk, a, k, v, s0, reverse, emit):
    def step(s, inp):
        r_t, w_t, kk_t, a_t, k_t, v_t = inp
        s_kk = jnp.einsum('bhvk,bhk->bhv', s, kk_t)
        s = (s * w_t[:, :, None, :] - s_kk[..., None] * (kk_t * a_t)[:, :, None, :]
             + v_t[..., None] * k_t[:, :, None, :])
        y = jnp.einsum('bhvk,bhk->bhv', s, r_t) if emit else None
        return s, y
    xs = tuple(jnp.moveaxis(t, 1, 0) for t in (r, w, kk, a, k, v))
    s_fin, ys = lax.scan(step, s0, xs, reverse=reverse)
    return s_fin, (jnp.moveaxis(ys, 0, 1) if emit else None)


def rwkv_time_mix(hc, hl, v_first, mu, w_rkv, w0, w1, w2, a0, a1, a2, g1, g2, k_k, k_a, r_k,
                  ln_g, ln_b, w_o, vres, need_ctx):
    feats = []
    firsts = []
    for n, h in enumerate((hc, hl)):
        r, k, v, xv, xg, decay, a = rwkv_stream(h, mu, w_rkv, w0, w1, w2, a0, a1, a2)
        if vres is None:
            firsts.append(v)
        else:
            v0, v1, v2 = vres
            v = v + (v_first[n] - v) * jax.nn.sigmoid(v0 + (xv @ v1) @ v2)
        kk = to_heads(k * k_k)
        kk = kk * lax.rsqrt(jnp.maximum(jnp.sum(kk * kk, axis=-1, keepdims=True), 1e-24))
        kd = to_heads(k)[None] * (1 + (a - 1) * to_heads(k_a))
        feats.append((to_heads(r), kk, to_heads(v), decay, a, kd, xg))
    if vres is None:
        v_first = (firsts[0], firsts[1])
    (rc, kkc, vc, dc, ac, kdc, xgc), (rl, kkl, vl, dl, al, kdl, xgl) = feats
    b = hl.shape[0]
    y_c = 0.0
    y_l = 0.0
    for d, rev in enumerate((False, True)):
        s0 = jnp.zeros((b, RW_HEADS, RW_HEAD, RW_HEAD), jnp.float32)
        s_c, yc_d = wkv_scan(rc, dc[d], kkc, ac[d], kdc[d], vc, s0, rev, need_ctx)
        _, yl_d = wkv_scan(rl, dl[d], kkl, al[d], kdl[d], vl, s_c, rev, True)
        y_l = y_l + yl_d
        if need_ctx:
            y_c = y_c + yc_d

    def readout(h, y, r, kd, v, xg):
        mean = jnp.mean(y, axis=-1, keepdims=True)
        var = jnp.mean(jnp.square(y - mean), axis=-1, keepdims=True)
        yn = (y - mean) * lax.rsqrt(var + RW_GN_EPS)
        bonus = jnp.sum(jnp.sum(r[None] * kd * r_k.astype(jnp.float32), axis=-1, keepdims=True), axis=0) * v
        o = (yn.reshape(h.shape) * ln_g + ln_b + bonus.reshape(h.shape)).astype(h.dtype)
        g = jax.nn.sigmoid(xg @ g1) @ g2
        return (o * g) @ w_o

    out_l = readout(hl, y_l, rl, kdl, vl, xgl)
    out_c = readout(hc, y_c, rc, kdc, vc, xgc) if need_ctx else None
    return out_l, out_c, v_first


def setup_inputs(seed: int = 0) -> dict:
    key = jax.random.key(seed)
    keys = jax.random.split(key, 48)
    ctr = [0]

    def nk():
        ctr[0] += 1
        return keys[ctr[0] - 1]

    def nrm(shape, std=1.0):
        return std * jax.random.normal(nk(), shape, jnp.float32)

    def uni(shape, lo, hi):
        return jax.random.uniform(nk(), shape, jnp.float32, lo, hi)

    d = D_MODEL
    f = D_FF
    n_a = len(range(0, DEPTH, N_MIXERS))
    n_b = len(range(1, DEPTH, N_MIXERS))
    n_c = len(range(2, DEPTH, N_MIXERS))
    n_vr = max(n_a - 1, 0)
    return {
        'x': nrm((BATCH, SEQ, d)),
        'c': nrm((BATCH, d)),
        'ctx': nrm((BATCH, CTX_LEN, d)),
        'c_ctx': nrm((d,)),
        'mod_w': nrm((DEPTH, d, 6 * d), d ** -0.5),
        'mod_b': nrm((DEPTH, 6 * d), 0.02),
        'norm1_g': 1.0 + nrm((DEPTH, d), 0.02),
        'norm2_g': 1.0 + nrm((DEPTH, d), 0.02),
        'ffn_up': nrm((DEPTH, d, 2 * f), d ** -0.5),
        'ffn_conv_w': nrm((DEPTH, CONV_W, 2 * f), CONV_W ** -0.5),
        'ffn_conv_b': nrm((DEPTH, 2 * f), 0.02),
        'ffn_down': nrm((DEPTH, f, d), f ** -0.5),
        'rw_mu': uni((n_a, 6, d), 0.0, 1.0),
        'rw_w_rkv': nrm((n_a, 3, d, d), d ** -0.5),
        'rw_w0': uni((n_a, 2, d), -6.0, 0.0),
        'rw_w1': nrm((n_a, 2, d, RW_DECAY_LORA), d ** -0.5),
        'rw_w2': nrm((n_a, 2, RW_DECAY_LORA, d), 0.1 * RW_DECAY_LORA ** -0.5),
        'rw_a0': nrm((n_a, 2, d), 0.1),
        'rw_a1': nrm((n_a, 2, d, RW_AAA_LORA), d ** -0.5),
        'rw_a2': nrm((n_a, 2, RW_AAA_LORA, d), 0.1 * RW_AAA_LORA ** -0.5),
        'rw_g1': nrm((n_a, d, RW_GATE_LORA), d ** -0.5),
        'rw_g2': nrm((n_a, RW_GATE_LORA, d), RW_GATE_LORA ** -0.5),
        'rw_k_k': 0.85 + nrm((n_a, d), 0.02),
        'rw_k_a': 1.0 + nrm((n_a, d), 0.02),
        'rw_r_k': nrm((n_a, RW_HEADS, RW_HEAD), 0.1),
        'rw_ln_g': 1.0 + nrm((n_a, d), 0.02),
        'rw_ln_b': nrm((n_a, d), 0.02),
        'rw_w_o': nrm((n_a, d, d), d ** -0.5),
        'rw_v0': 1.0 + nrm((n_vr, d), 0.1),
        'rw_v1': nrm((n_vr, d, RW_MV_LORA), d ** -0.5),
        'rw_v2': nrm((n_vr, RW_MV_LORA, d), 0.1 * RW_MV_LORA ** -0.5),
        'na_w_qkv': nrm((n_b, d, 3 * d), d ** -0.5),
        'na_q_g': 1.0 + nrm((n_b, ATT_HEAD), 0.02),
        'na_k_g': 1.0 + nrm((n_b, ATT_HEAD), 0.02),
        'na_rpb': nrm((n_b, ATT_HEADS, 2 * NA_WIN_R - 1, 2 * NA_WIN_C - 1), 0.1),
        'na_w_o': nrm((n_b, d, d), d ** -0.5),
        'ga_w_qkv': nrm((n_c, d, d + 2 * KV_DIM), d ** -0.5),
        'ga_q_g': 1.0 + nrm((n_c, ATT_HEAD), 0.02),
        'ga_k_g': 1.0 + nrm((n_c, ATT_HEAD), 0.02),
        'ga_w_o': nrm((n_c, d, d), d ** -0.5),
    }


def reference(x, c, ctx, c_ctx, mod_w, mod_b, norm1_g, norm2_g, ffn_up, ffn_conv_w, ffn_conv_b, ffn_down,
              rw_mu, rw_w_rkv, rw_w0, rw_w1, rw_w2, rw_a0, rw_a1, rw_a2, rw_g1, rw_g2, rw_k_k, rw_k_a,
              rw_r_k, rw_ln_g, rw_ln_b, rw_w_o, rw_v0, rw_v1, rw_v2,
              na_w_qkv, na_q_g, na_k_g, na_rpb, na_w_o,
              ga_w_qkv, ga_q_g, ga_k_g, ga_w_o):
    b, n_lat, d = x.shape
    cos, sin = axial_rope(n_lat)
    s_lat = jax.nn.silu(c)
    s_ctx = jax.nn.silu(c_ctx)
    xc = ctx
    v_first = None
    for i in range(DEPTH):
        kind, j = i % N_MIXERS, i // N_MIXERS
        need_ctx = i < DEPTH - 1
        m_l = (s_lat @ mod_w[i] + mod_b[i]).reshape(b, 1, 6, d)
        m_c = (s_ctx @ mod_w[i] + mod_b[i]).reshape(6, d)
        hl = modulate(rms_norm(x, norm1_g[i]), m_l[:, :, 0], m_l[:, :, 1])
        hc = modulate(rms_norm(xc, norm1_g[i]), m_c[0], m_c[1])
        if kind == 0:
            vres = None if j == 0 else (rw_v0[j - 1], rw_v1[j - 1], rw_v2[j - 1])
            o_l, o_c, v_first = rwkv_time_mix(hc, hl, v_first, rw_mu[j], rw_w_rkv[j], rw_w0[j], rw_w1[j],
                                              rw_w2[j], rw_a0[j], rw_a1[j], rw_a2[j], rw_g1[j], rw_g2[j],
                                              rw_k_k[j], rw_k_a[j], rw_r_k[j], rw_ln_g[j], rw_ln_b[j],
                                              rw_w_o[j], vres, need_ctx)
        elif kind == 1:
            o_l, o_c = na_mixer(hc, hl, na_w_qkv[j], na_q_g[j], na_k_g[j], na_rpb[j], na_w_o[j], need_ctx)
        else:
            o_l, o_c = gqa_mixer(hc, hl, ga_w_qkv[j], ga_q_g[j], ga_k_g[j], ga_w_o[j], cos, sin, need_ctx)
        x = x + m_l[:, :, 2] * o_l
        h2 = modulate(rms_norm(x, norm2_g[i]), m_l[:, :, 3], m_l[:, :, 4])
        x = x + m_l[:, :, 5] * conv_ffn(h2, ffn_up[i], ffn_conv_w[i], ffn_conv_b[i], ffn_down[i])
        if need_ctx:
            xc = xc + m_c[2] * o_c
            h2c = modulate(rms_norm(xc, norm2_g[i]), m_c[3], m_c[4])
            xc = xc + m_c[5] * conv_ffn(h2c, ffn_up[i], ffn_conv_w[i], ffn_conv_b[i], ffn_down[i])
    return x
```

```python
import functools
import math

import jax
import jax.numpy as jnp
from jax import lax
from jax.experimental import pallas as pl
from jax.experimental.pallas import tpu as pltpu

F32 = jnp.float32
BF16 = jnp.bfloat16

NORM_EPS = 1e-6
GRID_W = 64
ATT_HEAD = 128
RW_HEAD = 64
NA_WIN_R = 8
NA_WIN_C = 16
ROPE_THETA = 10000.0
RW_GN_EPS = 64e-5
LANES = 128
SUBLANES = 8
WKV_CHUNK = 64
WKV_PACK = 4
WKV_INV_BASE = 8
VMEM_LIMIT = 56 * 1024 * 1024
NEG_BIG = -1e30


def _cparams(sem):
    return pltpu.CompilerParams(dimension_semantics=sem, vmem_limit_bytes=VMEM_LIMIT)


def _dot(a, b):
    return jnp.dot(a, b, preferred_element_type=F32)


def _dot_nt(a, b):
    return lax.dot_general(a, b, (((1,), (1,)), ((), ())), preferred_element_type=F32)


def _dot_tn(a, b):
    return lax.dot_general(a, b, (((0,), (0,)), ((), ())), preferred_element_type=F32)


def _normmod(x, g, shift, scale):
    ms = jnp.mean(x * x, axis=-1, keepdims=True)
    y = x * lax.rsqrt(ms + NORM_EPS)
    return (y * g) * (1.0 + scale) + shift


def _silu(x):
    return x * jax.nn.sigmoid(x)


def _seq_edges(tile, rows, n_lat_rows, seq, ctx):
    rid = lax.broadcasted_iota(jnp.int32, (rows, 1), 0)
    base = tile * rows
    period = jnp.where(base >= n_lat_rows, ctx, seq)
    pos = (base + rid) & (period - 1)
    return rid, pos == 0, pos == period - 1


def _shift_rows(u, rid, first, last, prev_row, next_row):
    n = u.shape[0]
    up = pltpu.roll(u, 1, axis=0)
    up = jnp.where(rid == 0, prev_row, up)
    up = jnp.where(first, 0.0, up)
    un = pltpu.roll(u, n - 1, axis=0)
    un = jnp.where(rid == n - 1, next_row, un)
    un = jnp.where(last, 0.0, un)
    return up, un


def _segsum64(s):
    parts = []
    for c in range(s.shape[1] // LANES):
        t = s[:, c * LANES:(c + 1) * LANES]
        lane = lax.broadcasted_iota(jnp.int32, t.shape, 1)
        for d in (1, 2, 4, 8, 16, 32):
            t = t + jnp.where((lane & d) == 0, pltpu.roll(t, LANES - d, axis=1), pltpu.roll(t, d, axis=1))
        parts.append(t)
    return parts[0] if len(parts) == 1 else jnp.concatenate(parts, axis=1)


def _mod_kernel(c_ref, w_ref, b_ref, o_ref):
    s = _silu(c_ref[...]).astype(BF16)
    o_ref[0] = _dot(s, w_ref[0].astype(BF16)) + b_ref[0]


def _modulations(c_all, mod_w, mod_b):
    depth, d, n = mod_w.shape
    tn = n // 8
    return pl.pallas_call(
        _mod_kernel,
        out_shape=jax.ShapeDtypeStruct((depth, SUBLANES, n), F32),
        grid=(depth, n // tn),
        in_specs=[
            pl.BlockSpec((SUBLANES, d), lambda l, j: (0, 0)),
            pl.BlockSpec((1, d, tn), lambda l, j: (l, 0, j)),
            pl.BlockSpec((1, 1, tn), lambda l, j: (l, 0, j)),
        ],
        out_specs=pl.BlockSpec((1, SUBLANES, tn), lambda l, j: (l, 0, j)),
        compiler_params=_cparams(("parallel", "parallel")),
        name="modulation",
    )(c_all, mod_w, mod_b.reshape(depth, 1, n))


def _ffn_kernel(x_ref, xp_ref, xn_ref, mod_ref, g_ref, wug_ref, wuv_ref, cwg_ref, cwv_ref, cbg_ref, cbv_ref,
                wd_ref, o_ref, h_scr, acc_scr, *, tm, n_lat_rows, seq, ctx):
    i = pl.program_id(0)
    j = pl.program_id(1)
    shift = mod_ref[0, 3:4, :]
    scale = mod_ref[0, 4:5, :]

    @pl.when(j == 0)
    def _():
        g = g_ref[...]
        h_scr[0:tm, :] = _normmod(x_ref[...], g, shift, scale).astype(BF16)
        halo = jnp.concatenate([xp_ref[...], xn_ref[...]], axis=0)
        h_scr[tm:tm + 2 * SUBLANES, :] = _normmod(halo, g, shift, scale).astype(BF16)
        acc_scr[...] = jnp.zeros_like(acc_scr)

    rid, first, last = _seq_edges(i, tm, n_lat_rows, seq, ctx)
    hx = h_scr[...]

    def conv(w_ref, cw_ref, cb_ref):
        u = _dot(hx, w_ref[...])
        main = u[0:tm]
        up, un = _shift_rows(main, rid, first, last, u[tm + SUBLANES - 1:tm + SUBLANES],
                             u[tm + SUBLANES:tm + SUBLANES + 1])
        return cb_ref[...] + up * cw_ref[0:1, :] + main * cw_ref[1:2, :] + un * cw_ref[2:3, :]

    act = _silu(conv(wug_ref, cwg_ref, cbg_ref)) * conv(wuv_ref, cwv_ref, cbv_ref)
    acc_scr[...] += _dot(act.astype(BF16), wd_ref[...])

    @pl.when(j == pl.num_programs(1) - 1)
    def _():
        o_ref[...] = x_ref[...] + mod_ref[0, 5:6, :] * acc_scr[...]


def _ffn(xs, mod, g2, wu, cw, cb, wd, *, dm, n_tiles):
    rows, d = xs.shape
    f = wd.shape[0]
    fc = 512
    nfc = f // fc
    tm = dm["tm"]
    hb = tm // SUBLANES
    last_hb = rows // SUBLANES - 1
    modmap = lambda i, j: (jnp.minimum(i // dm["tpb"], dm["nb"]), 0, 0)
    kern = functools.partial(_ffn_kernel, tm=tm, n_lat_rows=dm["n_lat_rows"], seq=dm["seq"], ctx=dm["ctx"])
    return pl.pallas_call(
        kern,
        out_shape=jax.ShapeDtypeStruct((n_tiles * tm, d), F32),
        grid=(n_tiles, nfc),
        in_specs=[
            pl.BlockSpec((tm, d), lambda i, j: (i, 0)),
            pl.BlockSpec((SUBLANES, d), lambda i, j: (jnp.maximum(i * hb - 1, 0), 0)),
            pl.BlockSpec((SUBLANES, d), lambda i, j: (jnp.minimum((i + 1) * hb, last_hb), 0)),
            pl.BlockSpec((1, 6, d), modmap),
            pl.BlockSpec((1, d), lambda i, j: (0, 0)),
            pl.BlockSpec((d, fc), lambda i, j: (0, j)),
            pl.BlockSpec((d, fc), lambda i, j: (0, nfc + j)),
            pl.BlockSpec((3, fc), lambda i, j: (0, j)),
            pl.BlockSpec((3, fc), lambda i, j: (0, nfc + j)),
            pl.BlockSpec((1, fc), lambda i, j: (0, j)),
            pl.BlockSpec((1, fc), lambda i, j: (0, nfc + j)),
            pl.BlockSpec((fc, d), lambda i, j: (j, 0)),
        ],
        out_specs=pl.BlockSpec((tm, d), lambda i, j: (i, 0)),
        scratch_shapes=[pltpu.VMEM((tm + 2 * SUBLANES, d), BF16), pltpu.VMEM((tm, d), F32)],
        compiler_params=_cparams(("parallel", "arbitrary")),
        name="conv_ffn",
    )(xs, xs, xs, mod, g2.reshape(1, d), wu, wu, cw, cw, cb.reshape(1, -1), cb.reshape(1, -1), wd)


def _qkv_kernel(*refs, nq, nk, rope):
    if rope:
        x_ref, mod_ref, g_ref, w_ref, qg_ref, kg_ref, cos_ref, sa_ref, sb_ref, o_ref, h_scr = refs
    else:
        x_ref, mod_ref, g_ref, w_ref, qg_ref, kg_ref, o_ref, h_scr = refs
    j = pl.program_id(1)

    @pl.when(j == 0)
    def _():
        h_scr[...] = _normmod(x_ref[...], g_ref[...], mod_ref[0, 0:1, :], mod_ref[0, 1:2, :]).astype(BF16)

    acc = _dot(h_scr[...], w_ref[...])
    nh = acc.shape[1] // ATT_HEAD

    def headnorm(gain_ref):
        gain = gain_ref[...]
        for hh in range(nh):
            y = acc[:, hh * ATT_HEAD:(hh + 1) * ATT_HEAD]
            y = y * lax.rsqrt(jnp.mean(y * y, axis=-1, keepdims=True) + NORM_EPS) * gain
            if rope:
                y = (y * cos_ref[...] + pltpu.roll(y, ATT_HEAD - 32, axis=1) * sa_ref[...]
                     + pltpu.roll(y, 32, axis=1) * sb_ref[...])
            o_ref[:, hh * ATT_HEAD:(hh + 1) * ATT_HEAD] = y.astype(BF16)

    @pl.when(j < nq)
    def _():
        headnorm(qg_ref)

    @pl.when(jnp.logical_and(j >= nq, j < nq + nk))
    def _():
        headnorm(kg_ref)

    @pl.when(j >= nq + nk)
    def _():
        o_ref[...] = acc.astype(BF16)


def _qkv_proj(xs, mod, g1, w, qg, kg, rope_tabs, *, dm, kv_dim):
    rows, d = xs.shape
    n = w.shape[1]
    tn = min(512, kv_dim)
    tm = dm["tm"]
    modmap = lambda i, j: (jnp.minimum(i // dm["tpb"], dm["nb"]), 0, 0)
    in_specs = [
        pl.BlockSpec((tm, d), lambda i, j: (i, 0)),
        pl.BlockSpec((1, 6, d), modmap),
        pl.BlockSpec((1, d), lambda i, j: (0, 0)),
        pl.BlockSpec((d, tn), lambda i, j: (0, j)),
        pl.BlockSpec((1, ATT_HEAD), lambda i, j: (0, 0)),
        pl.BlockSpec((1, ATT_HEAD), lambda i, j: (0, 0)),
    ]
    args = [xs, mod, g1.reshape(1, d), w, qg.reshape(1, ATT_HEAD), kg.reshape(1, ATT_HEAD)]
    if rope_tabs is not None:
        in_specs += [pl.BlockSpec((tm, ATT_HEAD), lambda i, j: (i, 0))] * 3
        args += list(rope_tabs)
    kern = functools.partial(_qkv_kernel, nq=d // tn, nk=kv_dim // tn, rope=rope_tabs is not None)
    return pl.pallas_call(
        kern,
        out_shape=jax.ShapeDtypeStruct((rows, n), BF16),
        grid=(rows // tm, n // tn),
        in_specs=in_specs,
        out_specs=pl.BlockSpec((tm, tn), lambda i, j: (i, j)),
        scratch_shapes=[pltpu.VMEM((tm, d), BF16)],
        compiler_params=_cparams(("parallel", "arbitrary")),
        name="qkv_proj",
    )(*args)


def _oproj_kernel(a_ref, w_ref, x_ref, mod_ref, o_ref):
    o_ref[...] = x_ref[...] + mod_ref[0, 2:3, :] * _dot(a_ref[...], w_ref[...])


def _out_proj(a, w, xs, mod, *, dm, n_tiles):
    d = xs.shape[1]
    tm = dm["tm"]
    modmap = lambda i: (jnp.minimum(i // dm["tpb"], dm["nb"]), 0, 0)
    return pl.pallas_call(
        _oproj_kernel,
        out_shape=jax.ShapeDtypeStruct((n_tiles * tm, d), F32),
        grid=(n_tiles,),
        in_specs=[
            pl.BlockSpec((tm, d), lambda i: (i, 0)),
            pl.BlockSpec((d, d), lambda i: (0, 0)),
            pl.BlockSpec((tm, d), lambda i: (i, 0)),
            pl.BlockSpec((1, 6, d), modmap),
        ],
        out_specs=pl.BlockSpec((tm, d), lambda i: (i, 0)),
        compiler_params=_cparams(("parallel",)),
        name="out_proj",
    )(a, w, xs, mod)


def _softmax_pv(q, segs):
    ss = [_dot_nt(q, k) for k, _ in segs]
    m = ss[0].max(axis=-1, keepdims=True)
    for s in ss[1:]:
        m = jnp.maximum(m, s.max(axis=-1, keepdims=True))
    ps = [jnp.exp(s - m) for s in ss]
    l = ps[0].sum(axis=-1, keepdims=True)
    for p in ps[1:]:
        l = l + p.sum(axis=-1, keepdims=True)
    o = _dot(ps[0].astype(BF16), segs[0][1])
    for p, (_, v) in zip(ps[1:], segs[1:]):
        o = o + _dot(p.astype(BF16), v)
    return o / l


def _gqa_kernel(q_ref, kl_ref, vl_ref, kc_ref, vc_ref, o_ref, *, group, tq, n_lat_tiles):
    t = pl.program_id(2)
    q = q_ref[...]
    qs = jnp.concatenate([q[:, g * ATT_HEAD:(g + 1) * ATT_HEAD] for g in range(group)], axis=0)

    def emit(o):
        for g in range(group):
            o_ref[:, g * ATT_HEAD:(g + 1) * ATT_HEAD] = o[g * tq:(g + 1) * tq].astype(BF16)

    @pl.when(t < n_lat_tiles)
    def _():
        emit(_softmax_pv(qs, [(kl_ref[...], vl_ref[...]), (kc_ref[...], vc_ref[...])]))

    @pl.when(t >= n_lat_tiles)
    def _():
        emit(_softmax_pv(qs, [(kc_ref[...], vc_ref[...])]))


def _gqa_attention(qkv, *, dm, d, kv_heads):
    rows = qkv.shape[0]
    nb, seq, ctx = dm["nb"], dm["seq"], dm["ctx"]
    group = d // ATT_HEAD // kv_heads
    gw = group * ATT_HEAD
    tq = 128
    nlt, nct = seq // tq, ctx // tq
    kcol = d // ATT_HEAD
    vcol = kcol + kv_heads
    ctx_blk0 = nb * seq // ctx

    def qmap(b, h, t):
        return (jnp.where(t < nlt, b * nlt + t, nb * nlt + b * nct + (t - nlt)), h)

    kern = functools.partial(_gqa_kernel, group=group, tq=tq, n_lat_tiles=nlt)
    return pl.pallas_call(
        kern,
        out_shape=jax.ShapeDtypeStruct((rows, d), BF16),
        grid=(nb, kv_heads, nlt + nct),
        in_specs=[
            pl.BlockSpec((tq, gw), qmap),
            pl.BlockSpec((seq, ATT_HEAD), lambda b, h, t: (b, kcol + h)),
            pl.BlockSpec((seq, ATT_HEAD), lambda b, h, t: (b, vcol + h)),
            pl.BlockSpec((ctx, ATT_HEAD), lambda b, h, t: (ctx_blk0 + b, kcol + h)),
            pl.BlockSpec((ctx, ATT_HEAD), lambda b, h, t: (ctx_blk0 + b, vcol + h)),
        ],
        out_specs=pl.BlockSpec((tq, gw), qmap),
        compiler_params=_cparams(("parallel", "parallel", "arbitrary")),
        name="gqa_attention",
    )(qkv, qkv, qkv, qkv, qkv)


def _na_kernel(q_ref, k_ref, v_ref, kc_ref, vc_ref, bias_ref, o_ref, *, rb, grid_rows, n_row_blocks):
    t = pl.program_id(2)
    win = NA_WIN_R * GRID_W

    @pl.when(t < n_row_blocks)
    def _():
        for rr in range(rb):
            r = t * rb + rr
            rs = jnp.clip(r - NA_WIN_R // 2, 0, grid_rows - NA_WIN_R)
            start = pl.multiple_of(rs * GRID_W, GRID_W)
            q = q_ref[rr * GRID_W:(rr + 1) * GRID_W, :]
            kw = k_ref[pl.ds(start, win), :]
            vw = v_ref[pl.ds(start, win), :]
            sw = _dot_nt(q, kw) + bias_ref[0, r - rs]
            sc = _dot_nt(q, kc_ref[...])
            m = jnp.maximum(sw.max(axis=-1, keepdims=True), sc.max(axis=-1, keepdims=True))
            pw = jnp.exp(sw - m)
            pc = jnp.exp(sc - m)
            l = pw.sum(axis=-1, keepdims=True) + pc.sum(axis=-1, keepdims=True)
            o = _dot(pw.astype(BF16), vw) + _dot(pc.astype(BF16), vc_ref[...])
            o_ref[rr * GRID_W:(rr + 1) * GRID_W, :] = (o / l).astype(BF16)

    @pl.when(t >= n_row_blocks)
    def _():
        o_ref[...] = _softmax_pv(q_ref[...], [(kc_ref[...], vc_ref[...])]).astype(BF16)


def _na_bias_table(rpb):
    qc = jnp.arange(GRID_W)
    kc = jnp.arange(GRID_W)
    cs = jnp.clip(qc - NA_WIN_C // 2, 0, GRID_W - NA_WIN_C)
    inwin = (kc[None, :] >= cs[:, None]) & (kc[None, :] < cs[:, None] + NA_WIN_C)
    cidx = jnp.clip(kc[None, :] - qc[:, None] + NA_WIN_C - 1, 0, 2 * NA_WIN_C - 2)
    off = jnp.arange(NA_WIN_R)
    kr = jnp.arange(NA_WIN_R)
    ridx = kr[None, :] - off[:, None] + NA_WIN_R - 1
    tab = rpb[:, ridx[:, :, None, None], cidx[None, None, :, :]]
    tab = jnp.where(inwin[None, None, None], tab, NEG_BIG)
    tab = tab.transpose(0, 1, 3, 2, 4)
    return tab.reshape(rpb.shape[0], NA_WIN_R, GRID_W, NA_WIN_R * GRID_W).astype(F32)


def _na_attention(qkv, rpb, *, dm, d):
    rows = qkv.shape[0]
    nb, seq, ctx = dm["nb"], dm["seq"], dm["ctx"]
    heads = d // ATT_HEAD
    grid_rows = seq // GRID_W
    rb = ctx // GRID_W
    nrb = grid_rows // rb
    ctx_blk0 = nb * seq // ctx
    bias = _na_bias_table(rpb)

    def qmap(b, h, t):
        return (jnp.where(t < nrb, b * nrb + t, ctx_blk0 + b), h)

    kern = functools.partial(_na_kernel, rb=rb, grid_rows=grid_rows, n_row_blocks=nrb)
    return pl.pallas_call(
        kern,
        out_shape=jax.ShapeDtypeStruct((rows, d), BF16),
        grid=(nb, heads, nrb + 1),
        in_specs=[
            pl.BlockSpec((ctx, ATT_HEAD), qmap),
            pl.BlockSpec((seq, ATT_HEAD), lambda b, h, t: (b, heads + h)),
            pl.BlockSpec((seq, ATT_HEAD), lambda b, h, t: (b, 2 * heads + h)),
            pl.BlockSpec((ctx, ATT_HEAD), lambda b, h, t: (ctx_blk0 + b, heads + h)),
            pl.BlockSpec((ctx, ATT_HEAD), lambda b, h, t: (ctx_blk0 + b, 2 * heads + h)),
            pl.BlockSpec((1, NA_WIN_R, GRID_W, NA_WIN_R * GRID_W), lambda b, h, t: (h, 0, 0, 0)),
        ],
        out_specs=pl.BlockSpec((ctx, ATT_HEAD), qmap),
        compiler_params=_cparams(("parallel", "parallel", "arbitrary")),
        name="na_attention",
    )(qkv, qkv, qkv, qkv, qkv, bias)


def _rw_prep_kernel(x_ref, xp_ref, xn_ref, mod_ref, g_ref, mu_ref, o_ref, *, te, n_lat_rows, seq, ctx):
    i = pl.program_id(0)
    g = g_ref[...]
    shift = mod_ref[0, 0:1, :]
    scale = mod_ref[0, 1:2, :]
    h = _normmod(x_ref[...], g, shift, scale)
    halo = _normmod(jnp.concatenate([xp_ref[...], xn_ref[...]], axis=0), g, shift, scale)
    rid, first, last = _seq_edges(i, te, n_lat_rows, seq, ctx)
    hp, hn = _shift_rows(h, rid, first, last, halo[SUBLANES - 1:SUBLANES], halo[SUBLANES:SUBLANES + 1])
    xx = 0.5 * (hp + hn) - h
    for p in range(6):
        o_ref[p] = (h + xx * mu_ref[p:p + 1, :]).astype(BF16)


def _rw_prep(xs, mod, g1, mu, *, dm):
    rows, d = xs.shape
    te = dm["te"]
    hb = te // SUBLANES
    last_hb = rows // SUBLANES - 1
    modmap = lambda i: (jnp.minimum((i * te) // dm["seq"], dm["nb"]), 0, 0)
    kern = functools.partial(_rw_prep_kernel, te=te, n_lat_rows=dm["n_lat_rows"], seq=dm["seq"], ctx=dm["ctx"])
    return pl.pallas_call(
        kern,
        out_shape=jax.ShapeDtypeStruct((6, rows, d), BF16),
        grid=(rows // te,),
        in_specs=[
            pl.BlockSpec((te, d), lambda i: (i, 0)),
            pl.BlockSpec((SUBLANES, d), lambda i: (jnp.maximum(i * hb - 1, 0), 0)),
            pl.BlockSpec((SUBLANES, d), lambda i: (jnp.minimum((i + 1) * hb, last_hb), 0)),
            pl.BlockSpec((1, 6, d), modmap),
            pl.BlockSpec((1, d), lambda i: (0, 0)),
            pl.BlockSpec((6, d), lambda i: (0, 0)),
        ],
        out_specs=pl.BlockSpec((6, te, d), lambda i: (0, i, 0)),
        compiler_params=_cparams(("parallel",)),
        name="rwkv_prep",
    )(xs, xs, xs, mod, g1.reshape(1, d), mu)


def _rw_lora_kernel(*refs, mix):
    if mix:
        (xw_ref, xa_ref, xg_ref, xv_ref, w1_ref, a1_ref, g1_ref, v1_ref, w2_ref, a2_ref, g2_ref, v2_ref,
         w0_ref, a0_ref, v0_ref, lw_ref, a_ref, g_ref, vg_ref) = refs
    else:
        (xw_ref, xa_ref, xg_ref, w1_ref, a1_ref, g1_ref, w2_ref, a2_ref, g2_ref,
         w0_ref, a0_ref, lw_ref, a_ref, g_ref) = refs
    zw = jnp.tanh(_dot(xw_ref[0], w1_ref[...])).astype(BF16)
    za = _dot(xa_ref[0], a1_ref[...]).astype(BF16)
    zg = jax.nn.sigmoid(_dot(xg_ref[0], g1_ref[...])).astype(BF16)
    for dd in range(2):
        sl = slice(dd * LANES, (dd + 1) * LANES)
        wl = w0_ref[dd:dd + 1, :] + _dot(zw[:, sl], w2_ref[dd])
        lw_ref[dd] = (-math.exp(-0.5)) * jax.nn.sigmoid(wl)
        a_ref[dd] = jax.nn.sigmoid(a0_ref[dd:dd + 1, :] + _dot(za[:, sl], a2_ref[dd])).astype(BF16)
    g_ref[...] = _dot(zg, g2_ref[...]).astype(BF16)
    if mix:
        zv = _dot(xv_ref[0], v1_ref[...]).astype(BF16)
        vg_ref[...] = jax.nn.sigmoid(v0_ref[...] + _dot(zv, v2_ref[...])).astype(BF16)


def _pad_rank(w1, w2):
    r = w1.shape[-1]
    pad = (-r) % LANES
    w1 = jnp.pad(w1, [(0, 0)] * (w1.ndim - 1) + [(0, pad)])
    w2 = jnp.pad(w2, [(0, 0)] * (w2.ndim - 2) + [(0, pad), (0, 0)])
    return w1.astype(BF16), w2.astype(BF16)


def _rw_lora(xm, w0, w1, w2, a0, a1, a2, g1, g2, vres, *, dm):
    _, rows, d = xm.shape
    te = dm["te"]
    mix = vres is not None
    w1p, w2p = _pad_rank(w1, w2)
    a1p, a2p = _pad_rank(a1, a2)
    w1c = jnp.concatenate([w1p[0], w1p[1]], axis=1)
    a1c = jnp.concatenate([a1p[0], a1p[1]], axis=1)
    rg = g1.shape[1]
    full2 = lambda shp: pl.BlockSpec(shp, lambda i: (0,) * len(shp))
    xspec = lambda p: pl.BlockSpec((1, te, d), lambda i, p=p: (p, i, 0))
    in_specs = [xspec(3), xspec(4), xspec(5)]
    args = [xm, xm, xm]
    if mix:
        in_specs.append(xspec(2))
        args.append(xm)
        v1p, v2p = _pad_rank(vres[1], vres[2])
    in_specs += [full2((d, 2 * LANES)), full2((d, 2 * LANES)), full2((d, rg))]
    args += [w1c, a1c, g1.astype(BF16)]
    if mix:
        in_specs.append(full2((d, LANES)))
        args.append(v1p)
    in_specs += [full2((2, LANES, d)), full2((2, LANES, d)), full2((rg, d))]
    args += [w2p, a2p, g2.astype(BF16)]
    if mix:
        in_specs.append(full2((LANES, d)))
        args.append(v2p)
    in_specs += [full2((2, d)), full2((2, d))]
    args += [w0, a0]
    if mix:
        in_specs.append(full2((1, d)))
        args.append(vres[0].reshape(1, d))
    out_shape = [jax.ShapeDtypeStruct((2, rows, d), F32), jax.ShapeDtypeStruct((2, rows, d), BF16),
                 jax.ShapeDtypeStruct((rows, d), BF16)]
    out_specs = [pl.BlockSpec((2, te, d), lambda i: (0, i, 0)), pl.BlockSpec((2, te, d), lambda i: (0, i, 0)),
                 pl.BlockSpec((te, d), lambda i: (i, 0))]
    if mix:
        out_shape.append(jax.ShapeDtypeStruct((rows, d), BF16))
        out_specs.append(pl.BlockSpec((te, d), lambda i: (i, 0)))
    return pl.pallas_call(
        functools.partial(_rw_lora_kernel, mix=mix),
        out_shape=out_shape,
        grid=(rows // te,),
        in_specs=in_specs,
        out_specs=out_specs,
        compiler_params=_cparams(("parallel",)),
        name="rwkv_lora",
    )(*args)


def _rkv_kernel(*refs, mix):
    if mix:
        xm_ref, w_ref, vf_ref, vg_ref, o_ref = refs
    else:
        xm_ref, w_ref, o_ref = refs
    acc = _dot(xm_ref[0], w_ref[0])
    if mix:
        p = pl.program_id(1)

        @pl.when(p == 2)
        def _():
            o_ref[0] = (acc + (vf_ref[0].astype(F32) - acc) * vg_ref[...].astype(F32)).astype(BF16)

        @pl.when(p != 2)
        def _():
            o_ref[0] = acc.astype(BF16)
    else:
        o_ref[0] = acc.astype(BF16)


def _rkv_proj(xm, w, v_first, vgate, *, dm):
    _, rows, d = xm.shape
    tm = dm["tm"]
    mix = v_first is not None
    in_specs = [pl.BlockSpec((1, tm, d), lambda i, p: (p, i, 0)),
                pl.BlockSpec((1, d, d), lambda i, p: (p, 0, 0))]
    args = [xm, w]
    if mix:
        in_specs += [pl.BlockSpec((1, tm, d), lambda i, p: (2, i, 0)), pl.BlockSpec((tm, d), lambda i, p: (i, 0))]
        args += [v_first, vgate]
    return pl.pallas_call(
        functools.partial(_rkv_kernel, mix=mix),
        out_shape=jax.ShapeDtypeStruct((3, rows, d), BF16),
        grid=(rows // tm, 3),
        in_specs=in_specs,
        out_specs=pl.BlockSpec((1, tm, d), lambda i, p: (p, i, 0)),
        compiler_params=_cparams(("parallel", "arbitrary")),
        name="rwkv_rkv_proj",
    )(*args)


def _wkv_kernel(rf_ref, kf_ref, vf_ref, lwf_ref, af_ref, rb_ref, kb_ref, vb_ref, lwb_ref, ab_ref,
                kk_ref, ka_ref, rk_ref, yf_ref, bonf_ref, yb_ref, bonb_ref, s_scr, *, gps):
    c = pl.program_id(2)
    ln = WKV_CHUNK
    pw = WKV_PACK * RW_HEAD

    @pl.when(c == 0)
    def _():
        s_scr[...] = jnp.zeros_like(s_scr)

    row = lax.broadcasted_iota(jnp.int32, (ln, ln), 0)
    col = lax.broadcasted_iota(jnp.int32, (ln, ln), 1)
    trow = lax.broadcasted_iota(jnp.int32, (ln, pw), 0)
    tsrc = lax.broadcasted_iota(jnp.int32, (ln, pw), 1) & (ln - 1)
    hshift = RW_HEAD.bit_length() - 1
    brow = lax.broadcasted_iota(jnp.int32, (pw, pw), 0) >> hshift
    bcol = lax.broadcasted_iota(jnp.int32, (pw, pw), 1) >> hshift
    bmask = brow == bcol
    blk = {}
    n = WKV_INV_BASE
    while n <= ln:
        sh = n.bit_length() - 1
        blk[n] = (tsrc >> sh) == (trow >> sh)
        n *= 2
    k_k = kk_ref[...]
    k_a = ka_ref[...]
    r_k = rk_ref[...]

    def bd(z):
        zb = z.astype(BF16)
        return jnp.where(bmask, jnp.concatenate([zb] * WKV_PACK, axis=0), jnp.zeros((), BF16))

    streams = ((rf_ref, kf_ref, vf_ref, lwf_ref, af_ref, yf_ref, bonf_ref),
               (rb_ref, kb_ref, vb_ref, lwb_ref, ab_ref, yb_ref, bonb_ref))
    for dd, (r_ref, k_ref, v_ref, lw_ref, a_ref, y_ref, bon_ref) in enumerate(streams):
        rev = dd == 1
        r = r_ref[0].astype(F32)
        k = k_ref[0].astype(F32)
        v = v_ref[0].astype(F32)
        a = a_ref[0].astype(F32)
        lw = lw_ref[0]
        kkr = k * k_k
        kk = kkr * lax.rsqrt(jnp.maximum(_segsum64(kkr * kkr), 1e-24))
        kd = k * (1.0 + (a - 1.0) * k_a)
        bvec = kk * a
        bon_ref[...] = (_segsum64(r * kd * r_k) * v).astype(BF16)

        tri = jnp.where((col >= row) if rev else (col <= row), 1.0, 0.0).astype(BF16)
        hi = lw.astype(BF16)
        rem = lw - hi.astype(F32)
        mid = rem.astype(BF16)
        lo = (rem - mid.astype(F32)).astype(BF16)
        cum = _dot(tri, hi) + _dot(tri, mid) + _dot(tri, lo)
        tot = cum[0:1, :] if rev else cum[ln - 1:ln, :]
        w_inv = jnp.exp(-cum)
        w_end = jnp.exp(tot - cum)
        a_t = -kk * jnp.exp(cum - lw)
        r_t = r * jnp.exp(cum)
        b_t = bvec * w_inv
        k_t = kd * w_inv
        b_e = bvec * w_end
        k_e = kd * w_end
        w_tot = jnp.exp(tot)

        if rev:
            strict = tsrc > trow
            incl = tsrc >= trow
        else:
            strict = tsrc < trow
            incl = tsrc <= trow
        eye = jnp.where(tsrc == trow, 1.0, 0.0)

        for gi in range(gps):
            sl = slice(gi * pw, (gi + 1) * pw)
            ar = jnp.concatenate([a_t[:, sl], r_t[:, sl]], axis=0).astype(BF16)
            sb = _dot_nt(ar, bd(b_t[:, sl]))
            sk = _dot_nt(ar, bd(k_t[:, sl]))
            m_ab = jnp.where(strict, sb[:ln], 0.0)
            p_rb = jnp.where(incl, sb[ln:], 0.0)
            m_ak = jnp.where(strict, sk[:ln], 0.0)
            p_rk = jnp.where(incl, sk[ln:], 0.0)
            s0 = s_scr[dd, gi]
            ars = _dot_nt(ar, s0.astype(BF16))

            m0 = jnp.where(blk[WKV_INV_BASE], m_ab, 0.0)
            pinv = eye + m0
            mp = _dot(m0.astype(BF16), bd(m0))
            both = _dot(jnp.concatenate([mp, pinv], axis=0).astype(BF16), bd(mp))
            pinv = pinv + both[ln:]
            pinv = pinv + _dot(pinv.astype(BF16), bd(both[:ln]))
            n = WKV_INV_BASE
            while n < ln:
                off = jnp.where(jnp.logical_and(blk[2 * n], jnp.logical_not(blk[n])), m_ab, 0.0)
                t1 = _dot(off.astype(BF16), bd(pinv))
                pinv = pinv + _dot(pinv.astype(BF16), bd(t1))
                n *= 2

            vg = v[:, sl]
            mv = _dot(jnp.concatenate([m_ak, p_rk], axis=0).astype(BF16), bd(vg))
            u = _dot(pinv.astype(BF16), bd(ars[:ln] + mv[:ln]))
            y_ref[:, sl] = ars[ln:] + _dot(p_rb.astype(BF16), bd(u)) + mv[ln:]
            uv = jnp.concatenate([u, vg], axis=0).astype(BF16)
            bk = jnp.concatenate([b_e[:, sl], k_e[:, sl]], axis=0).astype(BF16)
            s_scr[dd, gi] = s0 * w_tot[:, sl] + jnp.where(bmask, _dot_tn(uv, bk), 0.0)


def _wkv(rkv, lw, a, k_k, k_a, r_k, *, dm, gps):
    _, rows, d = rkv.shape
    nb, seq, ctx = dm["nb"], dm["seq"], dm["ctx"]
    ln = WKV_CHUNK
    sw = gps * WKV_PACK * RW_HEAD
    ncc, nlc = ctx // ln, seq // ln
    ctx_c0 = nb * seq // ln

    def fblk(b, c):
        return jnp.where(c < ncc, ctx_c0 + b * ncc + c, b * nlc + (c - ncc))

    def bblk(b, c):
        return jnp.where(c < ncc, ctx_c0 + b * ncc + (ncc - 1 - c), b * nlc + (nlc - 1 - (c - ncc)))

    def spec3(p, blk):
        return pl.BlockSpec((1, ln, sw), lambda b, s, c, p=p, blk=blk: (p, blk(b, c), s))

    def spec2(blk):
        return pl.BlockSpec((ln, sw), lambda b, s, c, blk=blk: (blk(b, c), s))

    pspec = pl.BlockSpec((1, sw), lambda b, s, c: (0, s))
    in_specs = [spec3(0, fblk), spec3(1, fblk), spec3(2, fblk), spec3(0, fblk), spec3(0, fblk),
                spec3(0, bblk), spec3(1, bblk), spec3(2, bblk), spec3(1, bblk), spec3(1, bblk),
                pspec, pspec, pspec]
    return pl.pallas_call(
        functools.partial(_wkv_kernel, gps=gps),
        out_shape=[jax.ShapeDtypeStruct((rows, d), F32), jax.ShapeDtypeStruct((rows, d), BF16),
                   jax.ShapeDtypeStruct((rows, d), F32), jax.ShapeDtypeStruct((rows, d), BF16)],
        grid=(nb, d // sw, ncc + nlc),
        in_specs=in_specs,
        out_specs=[spec2(fblk), spec2(fblk), spec2(bblk), spec2(bblk)],
        scratch_shapes=[pltpu.VMEM((2, gps, WKV_PACK * RW_HEAD, WKV_PACK * RW_HEAD), F32)],
        compiler_params=_cparams(("parallel", "parallel", "arbitrary")),
        name="wkv_scan",
    )(rkv, rkv, rkv, lw, a, rkv, rkv, rkv, lw, a, k_k.reshape(1, d), k_a.reshape(1, d), r_k.reshape(1, d))


def _rw_out_kernel(yf_ref, yb_ref, bf_ref, bb_ref, g_ref, x_ref, mod_ref, lg_ref, lb_ref, w_ref, o_ref):
    y = yf_ref[...] + yb_ref[...]
    mean = _segsum64(y) * (1.0 / RW_HEAD)
    yc = y - mean
    var = _segsum64(yc * yc) * (1.0 / RW_HEAD)
    yn = yc * lax.rsqrt(var + RW_GN_EPS)
    bonus = bf_ref[...].astype(F32) + bb_ref[...].astype(F32)
    o = (yn * lg_ref[...] + lb_ref[...] + bonus) * g_ref[...].astype(F32)
    o_ref[...] = x_ref[...] + mod_ref[0, 2:3, :] * _dot(o.astype(BF16), w_ref[...])


def _rw_out(yf, bonf, yb, bonb, g, xs, mod, ln_g, ln_b, w_o, *, dm, n_rows):
    d = xs.shape[1]
    te = dm["te"]
    modmap = lambda i: (jnp.minimum((i * te) // dm["seq"], dm["nb"]), 0, 0)
    rspec = pl.BlockSpec((te, d), lambda i: (i, 0))
    vspec = pl.BlockSpec((1, d), lambda i: (0, 0))
    return pl.pallas_call(
        _rw_out_kernel,
        out_shape=jax.ShapeDtypeStruct((n_rows, d), F32),
        grid=(n_rows // te,),
        in_specs=[rspec, rspec, rspec, rspec, rspec, rspec, pl.BlockSpec((1, 6, d), modmap), vspec, vspec,
                  pl.BlockSpec((d, d), lambda i: (0, 0))],
        out_specs=rspec,
        compiler_params=_cparams(("parallel",)),
        name="rwkv_out",
    )(yf, yb, bonf, bonb, g, xs, mod, ln_g.reshape(1, d), ln_b.reshape(1, d), w_o)


def _rope_tables(dm):
    seq, nb, ctx = dm["seq"], dm["nb"], dm["ctx"]
    t = jnp.arange(seq, dtype=jnp.int32)
    pos = jnp.stack([t // GRID_W, t % GRID_W], axis=-1).astype(F32)
    n_freq = ATT_HEAD // 4
    inv = ROPE_THETA ** (-jnp.arange(n_freq, dtype=F32) / n_freq)
    ang = pos[:, :, None] * inv
    cos, sin = jnp.cos(ang), jnp.sin(ang)
    zero = jnp.zeros_like(sin)
    cos_t = jnp.stack([cos, cos], axis=2).reshape(seq, ATT_HEAD)
    sa_t = jnp.stack([-sin, zero], axis=2).reshape(seq, ATT_HEAD)
    sb_t = jnp.stack([zero, sin], axis=2).reshape(seq, ATT_HEAD)
    nctx = nb * ctx
    full = lambda tab, fill: jnp.concatenate([jnp.tile(tab, (nb, 1)), jnp.full((nctx, ATT_HEAD), fill, F32)], axis=0)
    return full(cos_t, 1.0), full(sa_t, 0.0), full(sb_t, 0.0)


def kernel(x, c, ctx, c_ctx, mod_w, mod_b, norm1_g, norm2_g, ffn_up, ffn_conv_w, ffn_conv_b, ffn_down, rw_mu, rw_w_rkv, rw_w0, rw_w1, rw_w2, rw_a0, rw_a1, rw_a2, rw_g1, rw_g2, rw_k_k, rw_k_a, rw_r_k, rw_ln_g, rw_ln_b, rw_w_o, rw_v0, rw_v1, rw_v2, na_w_qkv, na_q_g, na_k_g, na_rpb, na_w_o, ga_w_qkv, ga_q_g, ga_k_g, ga_w_o):
    nb, seq, d = x.shape
    nctx = ctx.shape[1]
    depth = mod_w.shape[0]
    tm = nb * nctx
    assert seq % tm == 0 and seq & (seq - 1) == 0 and nctx & (nctx - 1) == 0 and nb + 1 <= SUBLANES
    assert seq // GRID_W >= NA_WIN_R and nctx % GRID_W == 0
    dm = dict(nb=nb, seq=seq, ctx=nctx, tm=tm, te=tm // 2, tpb=seq // tm, n_lat_rows=nb * seq)
    n_lat_tiles = nb * seq // tm
    n_tiles = n_lat_tiles + 1
    att_scale = ATT_HEAD ** -0.5

    xs = jnp.concatenate([x.reshape(nb * seq, d), ctx.reshape(nb * nctx, d)], axis=0)
    c_all = jnp.concatenate([c, c_ctx[None], jnp.zeros((SUBLANES - nb - 1, d), F32)], axis=0)
    mods = _modulations(c_all, mod_w, mod_b)
    rope_tabs = None
    v_first = None

    for i in range(depth):
        kind, j = i % 3, i // 3
        need_ctx = i < depth - 1
        nt_out = n_tiles if need_ctx else n_lat_tiles
        mod = mods[i, :nb + 1].reshape(nb + 1, 6, d)
        if kind == 0:
            xm = _rw_prep(xs, mod, norm1_g[i], rw_mu[j], dm=dm)
            vres = None if j == 0 else (rw_v0[j - 1], rw_v1[j - 1], rw_v2[j - 1])
            lora = _rw_lora(xm, rw_w0[j], rw_w1[j], rw_w2[j], rw_a0[j], rw_a1[j], rw_a2[j], rw_g1[j], rw_g2[j],
                            vres, dm=dm)
            lw, a, g = lora[0], lora[1], lora[2]
            rkv = _rkv_proj(xm, rw_w_rkv[j].astype(BF16), v_first if vres is not None else None,
                            lora[3] if vres is not None else None, dm=dm)
            if v_first is None:
                v_first = rkv
            yf, bonf, yb, bonb = _wkv(rkv, lw, a, rw_k_k[j], rw_k_a[j], rw_r_k[j].reshape(-1), dm=dm, gps=2)
            xs = _rw_out(yf, bonf, yb, bonb, g, xs, mod, rw_ln_g[j], rw_ln_b[j], rw_w_o[j].astype(BF16),
                         dm=dm, n_rows=nt_out * tm)
        elif kind == 1:
            qkv = _qkv_proj(xs, mod, norm1_g[i], na_w_qkv[j].astype(BF16), na_q_g[j] * att_scale, na_k_g[j],
                            None, dm=dm, kv_dim=d)
            o = _na_attention(qkv, na_rpb[j], dm=dm, d=d)
            xs = _out_proj(o, na_w_o[j].astype(BF16), xs, mod, dm=dm, n_tiles=nt_out)
        else:
            if rope_tabs is None:
                rope_tabs = _rope_tables(dm)
            kv_dim = (ga_w_qkv.shape[-1] - d) // 2
            qkv = _qkv_proj(xs, mod, norm1_g[i], ga_w_qkv[j].astype(BF16), ga_q_g[j] * att_scale, ga_k_g[j],
                            rope_tabs, dm=dm, kv_dim=kv_dim)
            o = _gqa_attention(qkv, dm=dm, d=d, kv_heads=kv_dim // ATT_HEAD)
            xs = _out_proj(o, ga_w_o[j].astype(BF16), xs, mod, dm=dm, n_tiles=nt_out)
        xs = _ffn(xs, mod, norm2_g[i], ffn_up[i].astype(BF16), ffn_conv_w[i], ffn_conv_b[i],
                  ffn_down[i].astype(BF16), dm=dm, n_tiles=nt_out)
    return xs[:nb * seq].reshape(nb, seq, d)
```

```python
import functools
import math

import jax
import jax.numpy as jnp
from jax import lax
from jax.experimental import pallas as pl
from jax.experimental.pallas import tpu as pltpu

F32 = jnp.float32
BF16 = jnp.bfloat16

NORM_EPS = 1e-6
GRID_W = 64
ATT_HEAD = 128
RW_HEAD = 64
NA_WIN_R = 8
NA_WIN_C = 16
ROPE_THETA = 10000.0
RW_GN_EPS = 64e-5
LANES = 128
SUBLANES = 8
WKV_CHUNK = 64
WKV_PACK = 4
WKV_INV_BASE = 8
VMEM_LIMIT = 56 * 1024 * 1024
NEG_BIG = -1e30


def _cparams(sem):
    return pltpu.CompilerParams(dimension_semantics=sem, vmem_limit_bytes=VMEM_LIMIT)


def _dot(a, b):
    return jnp.dot(a, b, preferred_element_type=F32)


def _dot_nt(a, b):
    return lax.dot_general(a, b, (((1,), (1,)), ((), ())), preferred_element_type=F32)


def _dot_tn(a, b):
    return lax.dot_general(a, b, (((0,), (0,)), ((), ())), preferred_element_type=F32)


def _normmod(x, g, shift, scale):
    ms = jnp.mean(x * x, axis=-1, keepdims=True)
    y = x * lax.rsqrt(ms + NORM_EPS)
    return (y * g) * (1.0 + scale) + shift


def _silu(x):
    return x * jax.nn.sigmoid(x)


def _seq_edges(tile, rows, n_lat_rows, seq, ctx):
    rid = lax.broadcasted_iota(jnp.int32, (rows, 1), 0)
    base = tile * rows
    period = jnp.where(base >= n_lat_rows, ctx, seq)
    pos = (base + rid) & (period - 1)
    return rid, pos == 0, pos == period - 1


def _shift_rows(u, rid, first, last, prev_row, next_row):
    n = u.shape[0]
    up = pltpu.roll(u, 1, axis=0)
    up = jnp.where(rid == 0, prev_row, up)
    up = jnp.where(first, 0.0, up)
    un = pltpu.roll(u, n - 1, axis=0)
    un = jnp.where(rid == n - 1, next_row, un)
    un = jnp.where(last, 0.0, un)
    return up, un


def _ones_blockdiag64():
    sh = RW_HEAD.bit_length() - 1
    r = lax.broadcasted_iota(jnp.int32, (LANES, LANES), 0) >> sh
    c = lax.broadcasted_iota(jnp.int32, (LANES, LANES), 1) >> sh
    return jnp.where(r == c, 1.0, 0.0).astype(BF16)


def _segsum64_mxu(xs, ones):
    m, n = xs[0].shape
    pieces = []
    for x in xs:
        hi = x.astype(BF16)
        lo = (x - hi.astype(F32)).astype(BF16)
        for part in (hi, lo):
            pieces += [part[:, c * LANES:(c + 1) * LANES] for c in range(n // LANES)]
    res = _dot(jnp.concatenate(pieces, axis=0), ones)
    nslab = n // LANES
    outs = []
    for i in range(len(xs)):
        base = i * 2 * nslab
        cols = [res[(base + c) * m:(base + c + 1) * m] + res[(base + nslab + c) * m:(base + nslab + c + 1) * m]
                for c in range(nslab)]
        outs.append(jnp.concatenate(cols, axis=1))
    return outs


def _mod_kernel(c_ref, w_ref, b_ref, o_ref):
    s = _silu(c_ref[...]).astype(BF16)
    o_ref[0] = _dot(s, w_ref[0].astype(BF16)) + b_ref[0]


def _modulations(c_all, mod_w, mod_b):
    depth, d, n = mod_w.shape
    tn = n // 8
    return pl.pallas_call(
        _mod_kernel,
        out_shape=jax.ShapeDtypeStruct((depth, SUBLANES, n), F32),
        grid=(depth, n // tn),
        in_specs=[
            pl.BlockSpec((SUBLANES, d), lambda l, j: (0, 0)),
            pl.BlockSpec((1, d, tn), lambda l, j: (l, 0, j)),
            pl.BlockSpec((1, 1, tn), lambda l, j: (l, 0, j)),
        ],
        out_specs=pl.BlockSpec((1, SUBLANES, tn), lambda l, j: (l, 0, j)),
        compiler_params=_cparams(("parallel", "parallel")),
        name="modulation",
    )(c_all, mod_w, mod_b.reshape(depth, 1, n))


def _ffn_kernel(x_ref, xp_ref, xn_ref, mod_ref, g_ref, wug_ref, wuv_ref, cwg_ref, cwv_ref, cbg_ref, cbv_ref,
                wd_ref, o_ref, h_scr, acc_scr, *, tm, n_lat_rows, seq, ctx):
    i = pl.program_id(0)
    j = pl.program_id(1)
    shift = mod_ref[0, 3:4, :]
    scale = mod_ref[0, 4:5, :]

    @pl.when(j == 0)
    def _():
        g = g_ref[...]
        h_scr[0:tm, :] = _normmod(x_ref[...], g, shift, scale).astype(BF16)
        halo = jnp.concatenate([xp_ref[...], xn_ref[...]], axis=0)
        h_scr[tm:tm + 2 * SUBLANES, :] = _normmod(halo, g, shift, scale).astype(BF16)
        acc_scr[...] = jnp.zeros_like(acc_scr)

    rid, first, last = _seq_edges(i, tm, n_lat_rows, seq, ctx)
    hx = h_scr[...]

    def conv(w_ref, cw_ref, cb_ref):
        u = _dot(hx, w_ref[...])
        main = u[0:tm]
        up, un = _shift_rows(main, rid, first, last, u[tm + SUBLANES - 1:tm + SUBLANES],
                             u[tm + SUBLANES:tm + SUBLANES + 1])
        return cb_ref[...] + up * cw_ref[0:1, :] + main * cw_ref[1:2, :] + un * cw_ref[2:3, :]

    act = _silu(conv(wug_ref, cwg_ref, cbg_ref)) * conv(wuv_ref, cwv_ref, cbv_ref)
    acc_scr[...] += _dot(act.astype(BF16), wd_ref[...])

    @pl.when(j == pl.num_programs(1) - 1)
    def _():
        o_ref[...] = x_ref[...] + mod_ref[0, 5:6, :] * acc_scr[...]


def _ffn(xs, mod, g2, wu, cw, cb, wd, *, dm, n_tiles):
    rows, d = xs.shape
    f = wd.shape[0]
    fc = 512
    nfc = f // fc
    tm = dm["tm"]
    hb = tm // SUBLANES
    last_hb = rows // SUBLANES - 1
    modmap = lambda i, j: (jnp.minimum(i // dm["tpb"], dm["nb"]), 0, 0)
    kern = functools.partial(_ffn_kernel, tm=tm, n_lat_rows=dm["n_lat_rows"], seq=dm["seq"], ctx=dm["ctx"])
    return pl.pallas_call(
        kern,
        out_shape=jax.ShapeDtypeStruct((n_tiles * tm, d), F32),
        grid=(n_tiles, nfc),
        in_specs=[
            pl.BlockSpec((tm, d), lambda i, j: (i, 0)),
            pl.BlockSpec((SUBLANES, d), lambda i, j: (jnp.maximum(i * hb - 1, 0), 0)),
            pl.BlockSpec((SUBLANES, d), lambda i, j: (jnp.minimum((i + 1) * hb, last_hb), 0)),
            pl.BlockSpec((1, 6, d), modmap),
            pl.BlockSpec((1, d), lambda i, j: (0, 0)),
            pl.BlockSpec((d, fc), lambda i, j: (0, j)),
            pl.BlockSpec((d, fc), lambda i, j: (0, nfc + j)),
            pl.BlockSpec((3, fc), lambda i, j: (0, j)),
            pl.BlockSpec((3, fc), lambda i, j: (0, nfc + j)),
            pl.BlockSpec((1, fc), lambda i, j: (0, j)),
            pl.BlockSpec((1, fc), lambda i, j: (0, nfc + j)),
            pl.BlockSpec((fc, d), lambda i, j: (j, 0)),
        ],
        out_specs=pl.BlockSpec((tm, d), lambda i, j: (i, 0)),
        scratch_shapes=[pltpu.VMEM((tm + 2 * SUBLANES, d), BF16), pltpu.VMEM((tm, d), F32)],
        compiler_params=_cparams(("parallel", "arbitrary")),
        name="conv_ffn",
    )(xs, xs, xs, mod, g2.reshape(1, d), wu, wu, cw, cw, cb.reshape(1, -1), cb.reshape(1, -1), wd)


def _qkv_kernel(*refs, nq, nk, rope):
    if rope:
        x_ref, mod_ref, g_ref, w_ref, qg_ref, kg_ref, cos_ref, sa_ref, sb_ref, o_ref, h_scr = refs
    else:
        x_ref, mod_ref, g_ref, w_ref, qg_ref, kg_ref, o_ref, h_scr = refs
    j = pl.program_id(1)

    @pl.when(j == 0)
    def _():
        h_scr[...] = _normmod(x_ref[...], g_ref[...], mod_ref[0, 0:1, :], mod_ref[0, 1:2, :]).astype(BF16)

    acc = _dot(h_scr[...], w_ref[...])
    nh = acc.shape[1] // ATT_HEAD

    def headnorm(gain_ref):
        gain = gain_ref[...]
        for hh in range(nh):
            y = acc[:, hh * ATT_HEAD:(hh + 1) * ATT_HEAD]
            y = y * lax.rsqrt(jnp.mean(y * y, axis=-1, keepdims=True) + NORM_EPS) * gain
            if rope:
                y = (y * cos_ref[...] + pltpu.roll(y, ATT_HEAD - 32, axis=1) * sa_ref[...]
                     + pltpu.roll(y, 32, axis=1) * sb_ref[...])
            o_ref[:, hh * ATT_HEAD:(hh + 1) * ATT_HEAD] = y.astype(BF16)

    @pl.when(j < nq)
    def _():
        headnorm(qg_ref)

    @pl.when(jnp.logical_and(j >= nq, j < nq + nk))
    def _():
        headnorm(kg_ref)

    @pl.when(j >= nq + nk)
    def _():
        o_ref[...] = acc.astype(BF16)


def _qkv_proj(xs, mod, g1, w, qg, kg, rope_tabs, *, dm, kv_dim):
    rows, d = xs.shape
    n = w.shape[1]
    tn = min(512, kv_dim)
    tm = dm["tm"]
    modmap = lambda i, j: (jnp.minimum(i // dm["tpb"], dm["nb"]), 0, 0)
    in_specs = [
        pl.BlockSpec((tm, d), lambda i, j: (i, 0)),
        pl.BlockSpec((1, 6, d), modmap),
        pl.BlockSpec((1, d), lambda i, j: (0, 0)),
        pl.BlockSpec((d, tn), lambda i, j: (0, j)),
        pl.BlockSpec((1, ATT_HEAD), lambda i, j: (0, 0)),
        pl.BlockSpec((1, ATT_HEAD), lambda i, j: (0, 0)),
    ]
    args = [xs, mod, g1.reshape(1, d), w, qg.reshape(1, ATT_HEAD), kg.reshape(1, ATT_HEAD)]
    if rope_tabs is not None:
        in_specs += [pl.BlockSpec((tm, ATT_HEAD), lambda i, j: (i, 0))] * 3
        args += list(rope_tabs)
    kern = functools.partial(_qkv_kernel, nq=d // tn, nk=kv_dim // tn, rope=rope_tabs is not None)
    return pl.pallas_call(
        kern,
        out_shape=jax.ShapeDtypeStruct((rows, n), BF16),
        grid=(rows // tm, n // tn),
        in_specs=in_specs,
        out_specs=pl.BlockSpec((tm, tn), lambda i, j: (i, j)),
        scratch_shapes=[pltpu.VMEM((tm, d), BF16)],
        compiler_params=_cparams(("parallel", "arbitrary")),
        name="qkv_proj",
    )(*args)


def _oproj_kernel(a_ref, w_ref, x_ref, mod_ref, o_ref):
    o_ref[...] = x_ref[...] + mod_ref[0, 2:3, :] * _dot(a_ref[...], w_ref[...])


def _out_proj(a, w, xs, mod, *, dm, n_tiles):
    d = xs.shape[1]
    tm = dm["tm"]
    modmap = lambda i: (jnp.minimum(i // dm["tpb"], dm["nb"]), 0, 0)
    return pl.pallas_call(
        _oproj_kernel,
        out_shape=jax.ShapeDtypeStruct((n_tiles * tm, d), F32),
        grid=(n_tiles,),
        in_specs=[
            pl.BlockSpec((tm, d), lambda i: (i, 0)),
            pl.BlockSpec((d, d), lambda i: (0, 0)),
            pl.BlockSpec((tm, d), lambda i: (i, 0)),
            pl.BlockSpec((1, 6, d), modmap),
        ],
        out_specs=pl.BlockSpec((tm, d), lambda i: (i, 0)),
        compiler_params=_cparams(("parallel",)),
        name="out_proj",
    )(a, w, xs, mod)


def _softmax_pv(q, segs):
    ss = [_dot_nt(q, k) for k, _ in segs]
    m = ss[0].max(axis=-1, keepdims=True)
    for s in ss[1:]:
        m = jnp.maximum(m, s.max(axis=-1, keepdims=True))
    ps = [jnp.exp(s - m) for s in ss]
    l = ps[0].sum(axis=-1, keepdims=True)
    for p in ps[1:]:
        l = l + p.sum(axis=-1, keepdims=True)
    o = _dot(ps[0].astype(BF16), segs[0][1])
    for p, (_, v) in zip(ps[1:], segs[1:]):
        o = o + _dot(p.astype(BF16), v)
    return o / l


def _gqa_kernel(q_ref, kl_ref, vl_ref, kc_ref, vc_ref, o_ref, *, group, tq, n_lat_tiles):
    t = pl.program_id(2)
    q = q_ref[...]
    qs = jnp.concatenate([q[:, g * ATT_HEAD:(g + 1) * ATT_HEAD] for g in range(group)], axis=0)

    def emit(o):
        for g in range(group):
            o_ref[:, g * ATT_HEAD:(g + 1) * ATT_HEAD] = o[g * tq:(g + 1) * tq].astype(BF16)

    @pl.when(t < n_lat_tiles)
    def _():
        emit(_softmax_pv(qs, [(kl_ref[...], vl_ref[...]), (kc_ref[...], vc_ref[...])]))

    @pl.when(t >= n_lat_tiles)
    def _():
        emit(_softmax_pv(qs, [(kc_ref[...], vc_ref[...])]))


def _gqa_attention(qkv, *, dm, d, kv_heads):
    rows = qkv.shape[0]
    nb, seq, ctx = dm["nb"], dm["seq"], dm["ctx"]
    group = d // ATT_HEAD // kv_heads
    gw = group * ATT_HEAD
    tq = 128
    nlt, nct = seq // tq, ctx // tq
    kcol = d // ATT_HEAD
    vcol = kcol + kv_heads
    ctx_blk0 = nb * seq // ctx

    def qmap(b, h, t):
        return (jnp.where(t < nlt, b * nlt + t, nb * nlt + b * nct + (t - nlt)), h)

    kern = functools.partial(_gqa_kernel, group=group, tq=tq, n_lat_tiles=nlt)
    return pl.pallas_call(
        kern,
        out_shape=jax.ShapeDtypeStruct((rows, d), BF16),
        grid=(nb, kv_heads, nlt + nct),
        in_specs=[
            pl.BlockSpec((tq, gw), qmap),
            pl.BlockSpec((seq, ATT_HEAD), lambda b, h, t: (b, kcol + h)),
            pl.BlockSpec((seq, ATT_HEAD), lambda b, h, t: (b, vcol + h)),
            pl.BlockSpec((ctx, ATT_HEAD), lambda b, h, t: (ctx_blk0 + b, kcol + h)),
            pl.BlockSpec((ctx, ATT_HEAD), lambda b, h, t: (ctx_blk0 + b, vcol + h)),
        ],
        out_specs=pl.BlockSpec((tq, gw), qmap),
        compiler_params=_cparams(("parallel", "parallel", "arbitrary")),
        name="gqa_attention",
    )(qkv, qkv, qkv, qkv, qkv)


def _na_kernel(q_ref, k_ref, v_ref, kc_ref, vc_ref, bias_ref, o_ref, *, rb, grid_rows, n_row_blocks):
    t = pl.program_id(2)
    win = NA_WIN_R * GRID_W

    @pl.when(t < n_row_blocks)
    def _():
        for rr in range(rb):
            r = t * rb + rr
            rs = jnp.clip(r - NA_WIN_R // 2, 0, grid_rows - NA_WIN_R)
            start = pl.multiple_of(rs * GRID_W, GRID_W)
            q = q_ref[rr * GRID_W:(rr + 1) * GRID_W, :]
            kw = k_ref[pl.ds(start, win), :]
            vw = v_ref[pl.ds(start, win), :]
            sw = _dot_nt(q, kw) + bias_ref[0, r - rs]
            sc = _dot_nt(q, kc_ref[...])
            m = jnp.maximum(sw.max(axis=-1, keepdims=True), sc.max(axis=-1, keepdims=True))
            pw = jnp.exp(sw - m)
            pc = jnp.exp(sc - m)
            l = pw.sum(axis=-1, keepdims=True) + pc.sum(axis=-1, keepdims=True)
            o = _dot(pw.astype(BF16), vw) + _dot(pc.astype(BF16), vc_ref[...])
            o_ref[rr * GRID_W:(rr + 1) * GRID_W, :] = (o / l).astype(BF16)

    @pl.when(t >= n_row_blocks)
    def _():
        o_ref[...] = _softmax_pv(q_ref[...], [(kc_ref[...], vc_ref[...])]).astype(BF16)


def _na_bias_table(rpb):
    qc = jnp.arange(GRID_W)
    kc = jnp.arange(GRID_W)
    cs = jnp.clip(qc - NA_WIN_C // 2, 0, GRID_W - NA_WIN_C)
    inwin = (kc[None, :] >= cs[:, None]) & (kc[None, :] < cs[:, None] + NA_WIN_C)
    cidx = kc[None, :] - qc[:, None] + NA_WIN_C - 1
    sel = (cidx[None] == jnp.arange(2 * NA_WIN_C - 1)[:, None, None]) & inwin[None]
    cols = jnp.einsum('hrc,cqk->hrqk', rpb, sel.astype(F32), precision=lax.Precision.HIGHEST)
    cols = jnp.where(inwin[None, None], cols, NEG_BIG)
    tab = jnp.stack([cols[:, NA_WIN_R - 1 - o:2 * NA_WIN_R - 1 - o] for o in range(NA_WIN_R)], axis=1)
    tab = tab.transpose(0, 1, 3, 2, 4)
    return tab.reshape(rpb.shape[0], NA_WIN_R, GRID_W, NA_WIN_R * GRID_W).astype(F32)


def _na_attention(qkv, rpb, *, dm, d):
    rows = qkv.shape[0]
    nb, seq, ctx = dm["nb"], dm["seq"], dm["ctx"]
    heads = d // ATT_HEAD
    grid_rows = seq // GRID_W
    rb = ctx // GRID_W
    nrb = grid_rows // rb
    ctx_blk0 = nb * seq // ctx
    bias = _na_bias_table(rpb)

    def qmap(b, h, t):
        return (jnp.where(t < nrb, b * nrb + t, ctx_blk0 + b), h)

    kern = functools.partial(_na_kernel, rb=rb, grid_rows=grid_rows, n_row_blocks=nrb)
    return pl.pallas_call(
        kern,
        out_shape=jax.ShapeDtypeStruct((rows, d), BF16),
        grid=(nb, heads, nrb + 1),
        in_specs=[
            pl.BlockSpec((ctx, ATT_HEAD), qmap),
            pl.BlockSpec((seq, ATT_HEAD), lambda b, h, t: (b, heads + h)),
            pl.BlockSpec((seq, ATT_HEAD), lambda b, h, t: (b, 2 * heads + h)),
            pl.BlockSpec((ctx, ATT_HEAD), lambda b, h, t: (ctx_blk0 + b, heads + h)),
            pl.BlockSpec((ctx, ATT_HEAD), lambda b, h, t: (ctx_blk0 + b, 2 * heads + h)),
            pl.BlockSpec((1, NA_WIN_R, GRID_W, NA_WIN_R * GRID_W), lambda b, h, t: (h, 0, 0, 0)),
        ],
        out_specs=pl.BlockSpec((ctx, ATT_HEAD), qmap),
        compiler_params=_cparams(("parallel", "parallel", "arbitrary")),
        name="na_attention",
    )(qkv, qkv, qkv, qkv, qkv, bias)


def _rw_prep_kernel(x_ref, xp_ref, xn_ref, mod_ref, g_ref, mu_ref, o_ref, *, te, n_lat_rows, seq, ctx):
    i = pl.program_id(0)
    g = g_ref[...]
    shift = mod_ref[0, 0:1, :]
    scale = mod_ref[0, 1:2, :]
    h = _normmod(x_ref[...], g, shift, scale)
    halo = _normmod(jnp.concatenate([xp_ref[...], xn_ref[...]], axis=0), g, shift, scale)
    rid, first, last = _seq_edges(i, te, n_lat_rows, seq, ctx)
    hp, hn = _shift_rows(h, rid, first, last, halo[SUBLANES - 1:SUBLANES], halo[SUBLANES:SUBLANES + 1])
    xx = 0.5 * (hp + hn) - h
    for p in range(6):
        o_ref[p] = (h + xx * mu_ref[p:p + 1, :]).astype(BF16)


def _rw_prep(xs, mod, g1, mu, *, dm):
    rows, d = xs.shape
    te = dm["te"]
    hb = te // SUBLANES
    last_hb = rows // SUBLANES - 1
    modmap = lambda i: (jnp.minimum((i * te) // dm["seq"], dm["nb"]), 0, 0)
    kern = functools.partial(_rw_prep_kernel, te=te, n_lat_rows=dm["n_lat_rows"], seq=dm["seq"], ctx=dm["ctx"])
    return pl.pallas_call(
        kern,
        out_shape=jax.ShapeDtypeStruct((6, rows, d), BF16),
        grid=(rows // te,),
        in_specs=[
            pl.BlockSpec((te, d), lambda i: (i, 0)),
            pl.BlockSpec((SUBLANES, d), lambda i: (jnp.maximum(i * hb - 1, 0), 0)),
            pl.BlockSpec((SUBLANES, d), lambda i: (jnp.minimum((i + 1) * hb, last_hb), 0)),
            pl.BlockSpec((1, 6, d), modmap),
            pl.BlockSpec((1, d), lambda i: (0, 0)),
            pl.BlockSpec((6, d), lambda i: (0, 0)),
        ],
        out_specs=pl.BlockSpec((6, te, d), lambda i: (0, i, 0)),
        compiler_params=_cparams(("parallel",)),
        name="rwkv_prep",
    )(xs, xs, xs, mod, g1.reshape(1, d), mu)


def _rw_lora_kernel(*refs, mix):
    if mix:
        (xw_ref, xa_ref, xg_ref, xv_ref, w1_ref, a1_ref, g1_ref, v1_ref, w2_ref, a2_ref, g2_ref, v2_ref,
         w0_ref, a0_ref, v0_ref, lw_ref, a_ref, g_ref, vg_ref) = refs
    else:
        (xw_ref, xa_ref, xg_ref, w1_ref, a1_ref, g1_ref, w2_ref, a2_ref, g2_ref,
         w0_ref, a0_ref, lw_ref, a_ref, g_ref) = refs
    zw = jnp.tanh(_dot(xw_ref[0], w1_ref[...])).astype(BF16)
    za = _dot(xa_ref[0], a1_ref[...]).astype(BF16)
    zg = jax.nn.sigmoid(_dot(xg_ref[0], g1_ref[...])).astype(BF16)
    for dd in range(2):
        sl = slice(dd * LANES, (dd + 1) * LANES)
        wl = w0_ref[dd:dd + 1, :] + _dot(zw[:, sl], w2_ref[dd])
        lw_ref[dd] = (-math.exp(-0.5)) * jax.nn.sigmoid(wl)
        a_ref[dd] = jax.nn.sigmoid(a0_ref[dd:dd + 1, :] + _dot(za[:, sl], a2_ref[dd])).astype(BF16)
    g_ref[...] = _dot(zg, g2_ref[...]).astype(BF16)
    if mix:
        zv = _dot(xv_ref[0], v1_ref[...]).astype(BF16)
        vg_ref[...] = jax.nn.sigmoid(v0_ref[...] + _dot(zv, v2_ref[...])).astype(BF16)


def _pad_rank(w1, w2):
    r = w1.shape[-1]
    pad = (-r) % LANES
    w1 = jnp.pad(w1, [(0, 0)] * (w1.ndim - 1) + [(0, pad)])
    w2 = jnp.pad(w2, [(0, 0)] * (w2.ndim - 2) + [(0, pad), (0, 0)])
    return w1.astype(BF16), w2.astype(BF16)


def _rw_lora(xm, w0, w1, w2, a0, a1, a2, g1, g2, vres, *, dm):
    _, rows, d = xm.shape
    te = dm["te"]
    mix = vres is not None
    w1p, w2p = _pad_rank(w1, w2)
    a1p, a2p = _pad_rank(a1, a2)
    w1c = jnp.concatenate([w1p[0], w1p[1]], axis=1)
    a1c = jnp.concatenate([a1p[0], a1p[1]], axis=1)
    rg = g1.shape[1]
    full2 = lambda shp: pl.BlockSpec(shp, lambda i: (0,) * len(shp))
    xspec = lambda p: pl.BlockSpec((1, te, d), lambda i, p=p: (p, i, 0))
    in_specs = [xspec(3), xspec(4), xspec(5)]
    args = [xm, xm, xm]
    if mix:
        in_specs.append(xspec(2))
        args.append(xm)
        v1p, v2p = _pad_rank(vres[1], vres[2])
    in_specs += [full2((d, 2 * LANES)), full2((d, 2 * LANES)), full2((d, rg))]
    args += [w1c, a1c, g1.astype(BF16)]
    if mix:
        in_specs.append(full2((d, LANES)))
        args.append(v1p)
    in_specs += [full2((2, LANES, d)), full2((2, LANES, d)), full2((rg, d))]
    args += [w2p, a2p, g2.astype(BF16)]
    if mix:
        in_specs.append(full2((LANES, d)))
        args.append(v2p)
    in_specs += [full2((2, d)), full2((2, d))]
    args += [w0, a0]
    if mix:
        in_specs.append(full2((1, d)))
        args.append(vres[0].reshape(1, d))
    out_shape = [jax.ShapeDtypeStruct((2, rows, d), F32), jax.ShapeDtypeStruct((2, rows, d), BF16),
                 jax.ShapeDtypeStruct((rows, d), BF16)]
    out_specs = [pl.BlockSpec((2, te, d), lambda i: (0, i, 0)), pl.BlockSpec((2, te, d), lambda i: (0, i, 0)),
                 pl.BlockSpec((te, d), lambda i: (i, 0))]
    if mix:
        out_shape.append(jax.ShapeDtypeStruct((rows, d), BF16))
        out_specs.append(pl.BlockSpec((te, d), lambda i: (i, 0)))
    return pl.pallas_call(
        functools.partial(_rw_lora_kernel, mix=mix),
        out_shape=out_shape,
        grid=(rows // te,),
        in_specs=in_specs,
        out_specs=out_specs,
        compiler_params=_cparams(("parallel",)),
        name="rwkv_lora",
    )(*args)


def _rkv_kernel(*refs, mix):
    if mix:
        xm_ref, w_ref, vf_ref, vg_ref, o_ref = refs
    else:
        xm_ref, w_ref, o_ref = refs
    acc = _dot(xm_ref[0], w_ref[0])
    if mix:
        p = pl.program_id(1)

        @pl.when(p == 2)
        def _():
            o_ref[0] = (acc + (vf_ref[0].astype(F32) - acc) * vg_ref[...].astype(F32)).astype(BF16)

        @pl.when(p != 2)
        def _():
            o_ref[0] = acc.astype(BF16)
    else:
        o_ref[0] = acc.astype(BF16)


def _rkv_proj(xm, w, v_first, vgate, *, dm):
    _, rows, d = xm.shape
    tm = dm["tm"]
    mix = v_first is not None
    in_specs = [pl.BlockSpec((1, tm, d), lambda i, p: (p, i, 0)),
                pl.BlockSpec((1, d, d), lambda i, p: (p, 0, 0))]
    args = [xm, w]
    if mix:
        in_specs += [pl.BlockSpec((1, tm, d), lambda i, p: (2, i, 0)), pl.BlockSpec((tm, d), lambda i, p: (i, 0))]
        args += [v_first, vgate]
    return pl.pallas_call(
        functools.partial(_rkv_kernel, mix=mix),
        out_shape=jax.ShapeDtypeStruct((3, rows, d), BF16),
        grid=(rows // tm, 3),
        in_specs=in_specs,
        out_specs=pl.BlockSpec((1, tm, d), lambda i, p: (p, i, 0)),
        compiler_params=_cparams(("parallel", "arbitrary")),
        name="rwkv_rkv_proj",
    )(*args)


def _wkv_kernel(rf_ref, kf_ref, vf_ref, lwf_ref, af_ref, rb_ref, kb_ref, vb_ref, lwb_ref, ab_ref,
                kk_ref, ka_ref, rk_ref, yf_ref, bonf_ref, yb_ref, bonb_ref, s_scr, bm_scr, *, gps):
    c = pl.program_id(2)
    ln = WKV_CHUNK
    pw = WKV_PACK * RW_HEAD

    hshift = RW_HEAD.bit_length() - 1

    @pl.when(c == 0)
    def _():
        s_scr[...] = jnp.zeros_like(s_scr)
        brow = lax.broadcasted_iota(jnp.int32, (pw, pw), 0) >> hshift
        bcol = lax.broadcasted_iota(jnp.int32, (pw, pw), 1) >> hshift
        bm_scr[...] = jnp.where(brow == bcol, 1.0, 0.0).astype(BF16)

    row = lax.broadcasted_iota(jnp.int32, (ln, ln), 0)
    col = lax.broadcasted_iota(jnp.int32, (ln, ln), 1)
    trow = lax.broadcasted_iota(jnp.int32, (ln, pw), 0)
    tsrc = lax.broadcasted_iota(jnp.int32, (ln, pw), 1) & (ln - 1)
    ones64 = _ones_blockdiag64()

    def fmask(cond):
        return jnp.where(cond, 1.0, 0.0)

    def same_block(n):
        sh = n.bit_length() - 1
        return (tsrc >> sh) == (trow >> sh)

    eye = fmask(tsrc == trow)
    base_f = fmask(same_block(WKV_INV_BASE))
    off_f = {}
    n = WKV_INV_BASE
    while n < ln:
        off_f[n] = fmask(jnp.logical_and(same_block(2 * n), jnp.logical_not(same_block(n))))
        n *= 2
    k_k = kk_ref[...]
    k_a = ka_ref[...]
    r_k = rk_ref[...]

    def bd(z):
        zb = z.astype(BF16)
        return jnp.concatenate([zb] * WKV_PACK, axis=0) * bm_scr[...]

    streams = ((rf_ref, kf_ref, vf_ref, lwf_ref, af_ref, yf_ref, bonf_ref),
               (rb_ref, kb_ref, vb_ref, lwb_ref, ab_ref, yb_ref, bonb_ref))
    units = []
    for dd, (r_ref, k_ref, v_ref, lw_ref, a_ref, y_ref, bon_ref) in enumerate(streams):
        rev = dd == 1
        r = r_ref[0].astype(F32)
        k = k_ref[0].astype(F32)
        v = v_ref[0].astype(F32)
        a = a_ref[0].astype(F32)
        lw = lw_ref[0]
        kkr = k * k_k
        kd = k * (1.0 + (a - 1.0) * k_a)
        ssq, rkd = _segsum64_mxu([kkr * kkr, r * kd * r_k], ones64)
        kk = kkr * lax.rsqrt(jnp.maximum(ssq, 1e-24))
        bvec = kk * a
        bon_ref[...] = (rkd * v).astype(BF16)

        tri = jnp.where((col >= row) if rev else (col <= row), 1.0, 0.0).astype(BF16)
        hi = lw.astype(BF16)
        rem = lw - hi.astype(F32)
        mid = rem.astype(BF16)
        lo = (rem - mid.astype(F32)).astype(BF16)
        cum = _dot(tri, hi) + _dot(tri, mid) + _dot(tri, lo)
        tot = cum[0:1, :] if rev else cum[ln - 1:ln, :]
        w_inv = jnp.exp(-cum)
        w_end = jnp.exp(tot - cum)
        a_t = -kk * jnp.exp(cum - lw)
        r_t = r * jnp.exp(cum)
        b_t = bvec * w_inv
        k_t = kd * w_inv
        b_e = bvec * w_end
        k_e = kd * w_end
        w_tot = jnp.exp(tot)

        strict = fmask(tsrc > trow) if rev else fmask(tsrc < trow)
        incl = strict + eye

        for gi in range(gps):
            sl = slice(gi * pw, (gi + 1) * pw)
            units.append(dict(
                dd=dd, gi=gi, sl=sl, y_ref=y_ref, strict=strict, incl=incl,
                ar=jnp.concatenate([a_t[:, sl], r_t[:, sl]], axis=0).astype(BF16),
                b_t=b_t[:, sl], k_t=k_t[:, sl], v=v[:, sl], w_tot=w_tot[:, sl],
                bk=jnp.concatenate([b_e[:, sl], k_e[:, sl]], axis=0).astype(BF16)))

    for un in units:
        sb = _dot_nt(un["ar"], bd(un["b_t"]))
        sk = _dot_nt(un["ar"], bd(un["k_t"]))
        un["m_ab"] = sb[:ln] * un["strict"]
        un["p_rb"] = sb[ln:] * un["incl"]
        un["m_ak"] = sk[:ln] * un["strict"]
        un["p_rk"] = sk[ln:] * un["incl"]
    for un in units:
        un["s0"] = s_scr[un["dd"], un["gi"]]
        un["ars"] = _dot_nt(un["ar"], un["s0"].astype(BF16))
        un["mv"] = _dot(jnp.concatenate([un["m_ak"], un["p_rk"]], axis=0).astype(BF16), bd(un["v"]))
    for un in units:
        m0 = un["m_ab"] * base_f
        un["pinv"] = eye + m0
        un["mp"] = _dot(m0.astype(BF16), bd(m0))
    for un in units:
        both = _dot(jnp.concatenate([un["mp"], un["pinv"]], axis=0).astype(BF16), bd(un["mp"]))
        un["pinv"] = un["pinv"] + both[ln:]
        un["mp"] = both[:ln]
    for un in units:
        un["pinv"] = un["pinv"] + _dot(un["pinv"].astype(BF16), bd(un["mp"]))
    n = WKV_INV_BASE
    while n < ln:
        for un in units:
            un["t1"] = _dot((un["m_ab"] * off_f[n]).astype(BF16), bd(un["pinv"]))
        for un in units:
            un["pinv"] = un["pinv"] + _dot(un["pinv"].astype(BF16), bd(un["t1"]))
        n *= 2
    for un in units:
        un["u"] = _dot(un["pinv"].astype(BF16), bd(un["ars"][:ln] + un["mv"][:ln]))
    for un in units:
        un["y_ref"][:, un["sl"]] = un["ars"][ln:] + _dot(un["p_rb"].astype(BF16), bd(un["u"])) + un["mv"][ln:]
        uv = jnp.concatenate([un["u"], un["v"]], axis=0).astype(BF16)
        s_scr[un["dd"], un["gi"]] = un["s0"] * un["w_tot"] + _dot_tn(uv, un["bk"]) * bm_scr[...].astype(F32)


def _wkv(rkv, lw, a, k_k, k_a, r_k, *, dm, gps):
    _, rows, d = rkv.shape
    nb, seq, ctx = dm["nb"], dm["seq"], dm["ctx"]
    ln = WKV_CHUNK
    sw = gps * WKV_PACK * RW_HEAD
    ncc, nlc = ctx // ln, seq // ln
    ctx_c0 = nb * seq // ln

    def fblk(b, c):
        return jnp.where(c < ncc, ctx_c0 + b * ncc + c, b * nlc + (c - ncc))

    def bblk(b, c):
        return jnp.where(c < ncc, ctx_c0 + b * ncc + (ncc - 1 - c), b * nlc + (nlc - 1 - (c - ncc)))

    def spec3(p, blk):
        return pl.BlockSpec((1, ln, sw), lambda b, s, c, p=p, blk=blk: (p, blk(b, c), s))

    def spec2(blk):
        return pl.BlockSpec((ln, sw), lambda b, s, c, blk=blk: (blk(b, c), s))

    pspec = pl.BlockSpec((1, sw), lambda b, s, c: (0, s))
    in_specs = [spec3(0, fblk), spec3(1, fblk), spec3(2, fblk), spec3(0, fblk), spec3(0, fblk),
                spec3(0, bblk), spec3(1, bblk), spec3(2, bblk), spec3(1, bblk), spec3(1, bblk),
                pspec, pspec, pspec]
    return pl.pallas_call(
        functools.partial(_wkv_kernel, gps=gps),
        out_shape=[jax.ShapeDtypeStruct((rows, d), F32), jax.ShapeDtypeStruct((rows, d), BF16),
                   jax.ShapeDtypeStruct((rows, d), F32), jax.ShapeDtypeStruct((rows, d), BF16)],
        grid=(nb, d // sw, ncc + nlc),
        in_specs=in_specs,
        out_specs=[spec2(fblk), spec2(fblk), spec2(bblk), spec2(bblk)],
        scratch_shapes=[pltpu.VMEM((2, gps, WKV_PACK * RW_HEAD, WKV_PACK * RW_HEAD), F32),
                        pltpu.VMEM((WKV_PACK * RW_HEAD, WKV_PACK * RW_HEAD), BF16)],
        compiler_params=_cparams(("parallel", "parallel", "arbitrary")),
        name="wkv_scan",
    )(rkv, rkv, rkv, lw, a, rkv, rkv, rkv, lw, a, k_k.reshape(1, d), k_a.reshape(1, d), r_k.reshape(1, d))


def _rw_out_kernel(yf_ref, yb_ref, bf_ref, bb_ref, g_ref, x_ref, mod_ref, lg_ref, lb_ref, w_ref, o_ref):
    y = yf_ref[...] + yb_ref[...]
    ones64 = _ones_blockdiag64()
    mean = _segsum64_mxu([y], ones64)[0] * (1.0 / RW_HEAD)
    yc = y - mean
    var = _segsum64_mxu([yc * yc], ones64)[0] * (1.0 / RW_HEAD)
    yn = yc * lax.rsqrt(var + RW_GN_EPS)
    bonus = bf_ref[...].astype(F32) + bb_ref[...].astype(F32)
    o = (yn * lg_ref[...] + lb_ref[...] + bonus) * g_ref[...].astype(F32)
    o_ref[...] = x_ref[...] + mod_ref[0, 2:3, :] * _dot(o.astype(BF16), w_ref[...])


def _rw_out(yf, bonf, yb, bonb, g, xs, mod, ln_g, ln_b, w_o, *, dm, n_rows):
    d = xs.shape[1]
    te = dm["te"]
    modmap = lambda i: (jnp.minimum((i * te) // dm["seq"], dm["nb"]), 0, 0)
    rspec = pl.BlockSpec((te, d), lambda i: (i, 0))
    vspec = pl.BlockSpec((1, d), lambda i: (0, 0))
    return pl.pallas_call(
        _rw_out_kernel,
        out_shape=jax.ShapeDtypeStruct((n_rows, d), F32),
        grid=(n_rows // te,),
        in_specs=[rspec, rspec, rspec, rspec, rspec, rspec, pl.BlockSpec((1, 6, d), modmap), vspec, vspec,
                  pl.BlockSpec((d, d), lambda i: (0, 0))],
        out_specs=rspec,
        compiler_params=_cparams(("parallel",)),
        name="rwkv_out",
    )(yf, yb, bonf, bonb, g, xs, mod, ln_g.reshape(1, d), ln_b.reshape(1, d), w_o)


def _rope_tables(dm):
    seq, nb, ctx = dm["seq"], dm["nb"], dm["ctx"]
    t = jnp.arange(seq, dtype=jnp.int32)
    pos = jnp.stack([t // GRID_W, t % GRID_W], axis=-1).astype(F32)
    n_freq = ATT_HEAD // 4
    inv = ROPE_THETA ** (-jnp.arange(n_freq, dtype=F32) / n_freq)
    ang = pos[:, :, None] * inv
    cos, sin = jnp.cos(ang), jnp.sin(ang)
    zero = jnp.zeros_like(sin)
    cos_t = jnp.stack([cos, cos], axis=2).reshape(seq, ATT_HEAD)
    sa_t = jnp.stack([-sin, zero], axis=2).reshape(seq, ATT_HEAD)
    sb_t = jnp.stack([zero, sin], axis=2).reshape(seq, ATT_HEAD)
    nctx = nb * ctx
    full = lambda tab, fill: jnp.concatenate([jnp.tile(tab, (nb, 1)), jnp.full((nctx, ATT_HEAD), fill, F32)], axis=0)
    return full(cos_t, 1.0), full(sa_t, 0.0), full(sb_t, 0.0)


def kernel(x, c, ctx, c_ctx, mod_w, mod_b, norm1_g, norm2_g, ffn_up, ffn_conv_w, ffn_conv_b, ffn_down, rw_mu, rw_w_rkv, rw_w0, rw_w1, rw_w2, rw_a0, rw_a1, rw_a2, rw_g1, rw_g2, rw_k_k, rw_k_a, rw_r_k, rw_ln_g, rw_ln_b, rw_w_o, rw_v0, rw_v1, rw_v2, na_w_qkv, na_q_g, na_k_g, na_rpb, na_w_o, ga_w_qkv, ga_q_g, ga_k_g, ga_w_o):
    nb, seq, d = x.shape
    nctx = ctx.shape[1]
    depth = mod_w.shape[0]
    tm = nb * nctx
    assert seq % tm == 0 and seq & (seq - 1) == 0 and nctx & (nctx - 1) == 0 and nb + 1 <= SUBLANES
    assert seq // GRID_W >= NA_WIN_R and nctx % GRID_W == 0
    dm = dict(nb=nb, seq=seq, ctx=nctx, tm=tm, te=tm // 2, tpb=seq // tm, n_lat_rows=nb * seq)
    n_lat_tiles = nb * seq // tm
    n_tiles = n_lat_tiles + 1
    att_scale = ATT_HEAD ** -0.5

    xs = jnp.concatenate([x.reshape(nb * seq, d), ctx.reshape(nb * nctx, d)], axis=0)
    c_all = jnp.concatenate([c, c_ctx[None], jnp.zeros((SUBLANES - nb - 1, d), F32)], axis=0)
    mods = _modulations(c_all, mod_w, mod_b)
    rope_tabs = None
    v_first = None

    for i in range(depth):
        kind, j = i % 3, i // 3
        need_ctx = i < depth - 1
        nt_out = n_tiles if need_ctx else n_lat_tiles
        mod = mods[i, :nb + 1].reshape(nb + 1, 6, d)
        if kind == 0:
            xm = _rw_prep(xs, mod, norm1_g[i], rw_mu[j], dm=dm)
            vres = None if j == 0 else (rw_v0[j - 1], rw_v1[j - 1], rw_v2[j - 1])
            lora = _rw_lora(xm, rw_w0[j], rw_w1[j], rw_w2[j], rw_a0[j], rw_a1[j], rw_a2[j], rw_g1[j], rw_g2[j],
                            vres, dm=dm)
            lw, a, g = lora[0], lora[1], lora[2]
            rkv = _rkv_proj(xm, rw_w_rkv[j].astype(BF16), v_first if vres is not None else None,
                            lora[3] if vres is not None else None, dm=dm)
            if v_first is None:
                v_first = rkv
            yf, bonf, yb, bonb = _wkv(rkv, lw, a, rw_k_k[j], rw_k_a[j], rw_r_k[j].reshape(-1), dm=dm,
                                      gps=min(8, d // (WKV_PACK * RW_HEAD)))
            xs = _rw_out(yf, bonf, yb, bonb, g, xs, mod, rw_ln_g[j], rw_ln_b[j], rw_w_o[j].astype(BF16),
                         dm=dm, n_rows=nt_out * tm)
        elif kind == 1:
            qkv = _qkv_proj(xs, mod, norm1_g[i], na_w_qkv[j].astype(BF16), na_q_g[j] * att_scale, na_k_g[j],
                            None, dm=dm, kv_dim=d)
            o = _na_attention(qkv, na_rpb[j], dm=dm, d=d)
            xs = _out_proj(o, na_w_o[j].astype(BF16), xs, mod, dm=dm, n_tiles=nt_out)
        else:
            if rope_tabs is None:
                rope_tabs = _rope_tables(dm)
            kv_dim = (ga_w_qkv.shape[-1] - d) // 2
            qkv = _qkv_proj(xs, mod, norm1_g[i], ga_w_qkv[j].astype(BF16), ga_q_g[j] * att_scale, ga_k_g[j],
                            rope_tabs, dm=dm, kv_dim=kv_dim)
            o = _gqa_attention(qkv, dm=dm, d=d, kv_heads=kv_dim // ATT_HEAD)
            xs = _out_proj(o, ga_w_o[j].astype(BF16), xs, mod, dm=dm, n_tiles=nt_out)
        xs = _ffn(xs, mod, norm2_g[i], ffn_up[i].astype(BF16), ffn_conv_w[i], ffn_conv_b[i],
                  ffn_down[i].astype(BF16), dm=dm, n_tiles=nt_out)
    return xs[:nb * seq].reshape(nb, seq, d)
```

```python
import functools
import math

import jax
import jax.numpy as jnp
from jax import lax
from jax.experimental import pallas as pl
from jax.experimental.pallas import tpu as pltpu

F32 = jnp.float32
BF16 = jnp.bfloat16

NORM_EPS = 1e-6
GRID_W = 64
ATT_HEAD = 128
RW_HEAD = 64
NA_WIN_R = 8
NA_WIN_C = 16
ROPE_THETA = 10000.0
RW_GN_EPS = 64e-5
LANES = 128
SUBLANES = 8
WKV_CHUNK = 64
WKV_PACK = 4
WKV_INV_BASE = 8
VMEM_LIMIT = 56 * 1024 * 1024
NEG_BIG = -1e30


def _cparams(sem):
    return pltpu.CompilerParams(dimension_semantics=sem, vmem_limit_bytes=VMEM_LIMIT)


def _dot(a, b):
    return jnp.dot(a, b, preferred_element_type=F32)


def _dot_nt(a, b):
    return lax.dot_general(a, b, (((1,), (1,)), ((), ())), preferred_element_type=F32)


def _dot_tn(a, b):
    return lax.dot_general(a, b, (((0,), (0,)), ((), ())), preferred_element_type=F32)


def _normmod(x, g, shift, scale):
    ms = jnp.mean(x * x, axis=-1, keepdims=True)
    y = x * lax.rsqrt(ms + NORM_EPS)
    return (y * g) * (1.0 + scale) + shift


def _silu(x):
    return x * jax.nn.sigmoid(x)


def _seq_edges(tile, rows, n_lat_rows, seq, ctx):
    rid = lax.broadcasted_iota(jnp.int32, (rows, 1), 0)
    base = tile * rows
    period = jnp.where(base >= n_lat_rows, ctx, seq)
    pos = (base + rid) & (period - 1)
    return rid, pos == 0, pos == period - 1


def _shift_rows(u, rid, first, last, prev_row, next_row):
    n = u.shape[0]
    up = pltpu.roll(u, 1, axis=0)
    up = jnp.where(rid == 0, prev_row, up)
    up = jnp.where(first, 0.0, up)
    un = pltpu.roll(u, n - 1, axis=0)
    un = jnp.where(rid == n - 1, next_row, un)
    un = jnp.where(last, 0.0, un)
    return up, un


def _ones_blockdiag64():
    sh = RW_HEAD.bit_length() - 1
    r = lax.broadcasted_iota(jnp.int32, (LANES, LANES), 0) >> sh
    c = lax.broadcasted_iota(jnp.int32, (LANES, LANES), 1) >> sh
    return jnp.where(r == c, 1.0, 0.0).astype(BF16)


def _segsum64_mxu(xs, ones):
    m, n = xs[0].shape
    pieces = []
    for x in xs:
        hi = x.astype(BF16)
        lo = (x - hi.astype(F32)).astype(BF16)
        for part in (hi, lo):
            pieces += [part[:, c * LANES:(c + 1) * LANES] for c in range(n // LANES)]
    res = _dot(jnp.concatenate(pieces, axis=0), ones)
    nslab = n // LANES
    outs = []
    for i in range(len(xs)):
        base = i * 2 * nslab
        cols = [res[(base + c) * m:(base + c + 1) * m] + res[(base + nslab + c) * m:(base + nslab + c + 1) * m]
                for c in range(nslab)]
        outs.append(jnp.concatenate(cols, axis=1))
    return outs


def _mod_kernel(c_ref, w_ref, b_ref, o_ref):
    s = _silu(c_ref[...]).astype(BF16)
    o_ref[0] = _dot(s, w_ref[0].astype(BF16)) + b_ref[0]


def _modulations(c_all, mod_w, mod_b):
    depth, d, n = mod_w.shape
    tn = n // 8
    return pl.pallas_call(
        _mod_kernel,
        out_shape=jax.ShapeDtypeStruct((depth, SUBLANES, n), F32),
        grid=(depth, n // tn),
        in_specs=[
            pl.BlockSpec((SUBLANES, d), lambda l, j: (0, 0)),
            pl.BlockSpec((1, d, tn), lambda l, j: (l, 0, j)),
            pl.BlockSpec((1, 1, tn), lambda l, j: (l, 0, j)),
        ],
        out_specs=pl.BlockSpec((1, SUBLANES, tn), lambda l, j: (l, 0, j)),
        compiler_params=_cparams(("parallel", "parallel")),
        name="modulation",
    )(c_all, mod_w, mod_b.reshape(depth, 1, n))


def _ffn_kernel(x_ref, xp_ref, xn_ref, mod_ref, g_ref, wug_ref, wuv_ref, cwg_ref, cwv_ref, cbg_ref, cbv_ref,
                wd_ref, o_ref, h_scr, acc_scr, *, tm, n_lat_rows, seq, ctx):
    i = pl.program_id(0)
    j = pl.program_id(1)
    shift = mod_ref[0, 3:4, :]
    scale = mod_ref[0, 4:5, :]

    @pl.when(j == 0)
    def _():
        g = g_ref[...]
        h_scr[0:tm, :] = _normmod(x_ref[...], g, shift, scale).astype(BF16)
        halo = jnp.concatenate([xp_ref[...], xn_ref[...]], axis=0)
        h_scr[tm:tm + 2 * SUBLANES, :] = _normmod(halo, g, shift, scale).astype(BF16)
        acc_scr[...] = jnp.zeros_like(acc_scr)

    rid, first, last = _seq_edges(i, tm, n_lat_rows, seq, ctx)
    hx = h_scr[...]

    def conv(w_ref, cw_ref, cb_ref):
        u = _dot(hx, w_ref[...])
        main = u[0:tm]
        up, un = _shift_rows(main, rid, first, last, u[tm + SUBLANES - 1:tm + SUBLANES],
                             u[tm + SUBLANES:tm + SUBLANES + 1])
        return cb_ref[...] + up * cw_ref[0:1, :] + main * cw_ref[1:2, :] + un * cw_ref[2:3, :]

    act = _silu(conv(wug_ref, cwg_ref, cbg_ref)) * conv(wuv_ref, cwv_ref, cbv_ref)
    acc_scr[...] += _dot(act.astype(BF16), wd_ref[...])

    @pl.when(j == pl.num_programs(1) - 1)
    def _():
        o_ref[...] = x_ref[...] + mod_ref[0, 5:6, :] * acc_scr[...]


def _ffn(xs, mod, g2, wu, cw, cb, wd, *, dm, n_tiles):
    rows, d = xs.shape
    f = wd.shape[0]
    fc = 512
    nfc = f // fc
    tm = dm["tm"]
    hb = tm // SUBLANES
    last_hb = rows // SUBLANES - 1
    modmap = lambda i, j: (jnp.minimum(i // dm["tpb"], dm["nb"]), 0, 0)
    kern = functools.partial(_ffn_kernel, tm=tm, n_lat_rows=dm["n_lat_rows"], seq=dm["seq"], ctx=dm["ctx"])
    return pl.pallas_call(
        kern,
        out_shape=jax.ShapeDtypeStruct((n_tiles * tm, d), F32),
        grid=(n_tiles, nfc),
        in_specs=[
            pl.BlockSpec((tm, d), lambda i, j: (i, 0)),
            pl.BlockSpec((SUBLANES, d), lambda i, j: (jnp.maximum(i * hb - 1, 0), 0)),
            pl.BlockSpec((SUBLANES, d), lambda i, j: (jnp.minimum((i + 1) * hb, last_hb), 0)),
            pl.BlockSpec((1, 6, d), modmap),
            pl.BlockSpec((1, d), lambda i, j: (0, 0)),
            pl.BlockSpec((d, fc), lambda i, j: (0, j)),
            pl.BlockSpec((d, fc), lambda i, j: (0, nfc + j)),
            pl.BlockSpec((3, fc), lambda i, j: (0, j)),
            pl.BlockSpec((3, fc), lambda i, j: (0, nfc + j)),
            pl.BlockSpec((1, fc), lambda i, j: (0, j)),
            pl.BlockSpec((1, fc), lambda i, j: (0, nfc + j)),
            pl.BlockSpec((fc, d), lambda i, j: (j, 0)),
        ],
        out_specs=pl.BlockSpec((tm, d), lambda i, j: (i, 0)),
        scratch_shapes=[pltpu.VMEM((tm + 2 * SUBLANES, d), BF16), pltpu.VMEM((tm, d), F32)],
        compiler_params=_cparams(("parallel", "arbitrary")),
        name="conv_ffn",
    )(xs, xs, xs, mod, g2.reshape(1, d), wu, wu, cw, cw, cb.reshape(1, -1), cb.reshape(1, -1), wd)


def _qkv_kernel(*refs, block_kinds, rope):
    if rope:
        x_ref, mod_ref, g_ref, w_ref, qg_ref, kg_ref, cos_ref, sa_ref, sb_ref, o_ref, h_scr = refs
    else:
        x_ref, mod_ref, g_ref, w_ref, qg_ref, kg_ref, o_ref, h_scr = refs
    j = pl.program_id(1)

    @pl.when(j == 0)
    def _():
        h_scr[...] = _normmod(x_ref[...], g_ref[...], mod_ref[0, 0:1, :], mod_ref[0, 1:2, :]).astype(BF16)

    acc = _dot(h_scr[...], w_ref[...])

    def emit(kinds):
        for hh, kind in enumerate(kinds):
            hs = slice(hh * ATT_HEAD, (hh + 1) * ATT_HEAD)
            y = acc[:, hs]
            if kind != "v":
                gain = qg_ref[...] if kind == "q" else kg_ref[...]
                y = y * lax.rsqrt(jnp.mean(y * y, axis=-1, keepdims=True) + NORM_EPS) * gain
                if rope:
                    y = (y * cos_ref[...] + pltpu.roll(y, ATT_HEAD - 32, axis=1) * sa_ref[...]
                         + pltpu.roll(y, 32, axis=1) * sb_ref[...])
            o_ref[:, hs] = y.astype(BF16)

    for jb, kinds in enumerate(block_kinds):
        pl.when(j == jb)(functools.partial(emit, kinds))


def _qkv_proj(xs, mod, g1, w, qg, kg, rope_tabs, *, dm, kv_dim):
    rows, d = xs.shape
    n = w.shape[1]
    tn = min(d, 2 * kv_dim)
    hpb = tn // ATT_HEAD
    kinds = ["q"] * (d // ATT_HEAD) + ["k"] * (kv_dim // ATT_HEAD) + ["v"] * (kv_dim // ATT_HEAD)
    block_kinds = tuple(tuple(kinds[b * hpb:(b + 1) * hpb]) for b in range(n // tn))
    tm = dm["tm"]
    modmap = lambda i, j: (jnp.minimum(i // dm["tpb"], dm["nb"]), 0, 0)
    in_specs = [
        pl.BlockSpec((tm, d), lambda i, j: (i, 0)),
        pl.BlockSpec((1, 6, d), modmap),
        pl.BlockSpec((1, d), lambda i, j: (0, 0)),
        pl.BlockSpec((d, tn), lambda i, j: (0, j)),
        pl.BlockSpec((1, ATT_HEAD), lambda i, j: (0, 0)),
        pl.BlockSpec((1, ATT_HEAD), lambda i, j: (0, 0)),
    ]
    args = [xs, mod, g1.reshape(1, d), w, qg.reshape(1, ATT_HEAD), kg.reshape(1, ATT_HEAD)]
    if rope_tabs is not None:
        in_specs += [pl.BlockSpec((tm, ATT_HEAD), lambda i, j: (i, 0))] * 3
        args += list(rope_tabs)
    kern = functools.partial(_qkv_kernel, block_kinds=block_kinds, rope=rope_tabs is not None)
    return pl.pallas_call(
        kern,
        out_shape=jax.ShapeDtypeStruct((rows, n), BF16),
        grid=(rows // tm, n // tn),
        in_specs=in_specs,
        out_specs=pl.BlockSpec((tm, tn), lambda i, j: (i, j)),
        scratch_shapes=[pltpu.VMEM((tm, d), BF16)],
        compiler_params=_cparams(("parallel", "arbitrary")),
        name="qkv_proj",
    )(*args)


def _oproj_kernel(a_ref, w_ref, x_ref, mod_ref, o_ref):
    o_ref[...] = x_ref[...] + mod_ref[0, 2:3, :] * _dot(a_ref[...], w_ref[...])


def _out_proj(a, w, xs, mod, *, dm, n_tiles):
    d = xs.shape[1]
    tm = dm["tm"]
    modmap = lambda i: (jnp.minimum(i // dm["tpb"], dm["nb"]), 0, 0)
    return pl.pallas_call(
        _oproj_kernel,
        out_shape=jax.ShapeDtypeStruct((n_tiles * tm, d), F32),
        grid=(n_tiles,),
        in_specs=[
            pl.BlockSpec((tm, d), lambda i: (i, 0)),
            pl.BlockSpec((d, d), lambda i: (0, 0)),
            pl.BlockSpec((tm, d), lambda i: (i, 0)),
            pl.BlockSpec((1, 6, d), modmap),
        ],
        out_specs=pl.BlockSpec((tm, d), lambda i: (i, 0)),
        compiler_params=_cparams(("parallel",)),
        name="out_proj",
    )(a, w, xs, mod)


def _softmax_pv(q, segs):
    ss = [_dot_nt(q, k) for k, _ in segs]
    m = ss[0].max(axis=-1, keepdims=True)
    for s in ss[1:]:
        m = jnp.maximum(m, s.max(axis=-1, keepdims=True))
    ps = [jnp.exp(s - m) for s in ss]
    l = ps[0].sum(axis=-1, keepdims=True)
    for p in ps[1:]:
        l = l + p.sum(axis=-1, keepdims=True)
    o = _dot(ps[0].astype(BF16), segs[0][1])
    for p, (_, v) in zip(ps[1:], segs[1:]):
        o = o + _dot(p.astype(BF16), v)
    return o / l


def _lane_tiles(x, op):
    acc = x[:, 0:LANES]
    for j in range(1, x.shape[1] // LANES):
        acc = op(acc, x[:, j * LANES:(j + 1) * LANES])
    return acc


def _gqa_kernel(q_ref, kl_ref, vl_ref, kc_ref, vc_ref, o_ref, s_scr, *, group, n_lat_tiles, kchunk):
    t = pl.program_id(2)
    seq, ctx = kl_ref.shape[0], kc_ref.shape[0]

    def run(chunks):
        m = [None] * group
        mrun = [None] * group
        lrun = [None] * group
        o = [None] * group
        for g in range(group + 1):
            for k_ref, v_ref, st, sz, off in chunks:
                if g < group:
                    s = _dot_nt(q_ref[:, g * ATT_HEAD:(g + 1) * ATT_HEAD], k_ref[st:st + sz, :])
                    s_scr[g % 2, :, off:off + sz] = s
                    tmax = _lane_tiles(s, jnp.maximum)
                    mrun[g] = tmax if mrun[g] is None else jnp.maximum(mrun[g], tmax)
                if g >= 1:
                    h = g - 1
                    p = jnp.exp(s_scr[h % 2, :, off:off + sz] - m[h])
                    psum = _lane_tiles(p, jnp.add)
                    pv = _dot(p.astype(BF16), v_ref[st:st + sz, :])
                    lrun[h] = psum if lrun[h] is None else lrun[h] + psum
                    o[h] = pv if o[h] is None else o[h] + pv
            if g < group:
                m[g] = mrun[g].max(axis=-1, keepdims=True)
        for g in range(group):
            l = lrun[g].sum(axis=-1, keepdims=True)
            o_ref[:, g * ATT_HEAD:(g + 1) * ATT_HEAD] = (o[g] / l).astype(BF16)

    lat_chunks = [(kl_ref, vl_ref, st, kchunk, st) for st in range(0, seq, kchunk)]
    ctx_chunk = (kc_ref, vc_ref, 0, ctx, seq)

    @pl.when(t < n_lat_tiles)
    def _():
        run(lat_chunks + [ctx_chunk])

    @pl.when(t >= n_lat_tiles)
    def _():
        run([ctx_chunk])


def _gqa_attention(qkv, *, dm, d, kv_heads):
    rows = qkv.shape[0]
    nb, seq, ctx = dm["nb"], dm["seq"], dm["ctx"]
    group = d // ATT_HEAD // kv_heads
    gw = group * ATT_HEAD
    tq = 128
    nlt, nct = seq // tq, ctx // tq
    kcol = d // ATT_HEAD
    vcol = kcol + kv_heads
    ctx_blk0 = nb * seq // ctx

    def qmap(b, h, t):
        return (jnp.where(t < nlt, b * nlt + t, nb * nlt + b * nct + (t - nlt)), h)

    kern = functools.partial(_gqa_kernel, group=group, n_lat_tiles=nlt, kchunk=min(512, seq))
    return pl.pallas_call(
        kern,
        out_shape=jax.ShapeDtypeStruct((rows, d), BF16),
        scratch_shapes=[pltpu.VMEM((2, tq, seq + ctx), F32)],
        grid=(nb, kv_heads, nlt + nct),
        in_specs=[
            pl.BlockSpec((tq, gw), qmap),
            pl.BlockSpec((seq, ATT_HEAD), lambda b, h, t: (b, kcol + h)),
            pl.BlockSpec((seq, ATT_HEAD), lambda b, h, t: (b, vcol + h)),
            pl.BlockSpec((ctx, ATT_HEAD), lambda b, h, t: (ctx_blk0 + b, kcol + h)),
            pl.BlockSpec((ctx, ATT_HEAD), lambda b, h, t: (ctx_blk0 + b, vcol + h)),
        ],
        out_specs=pl.BlockSpec((tq, gw), qmap),
        compiler_params=_cparams(("parallel", "parallel", "arbitrary")),
        name="gqa_attention",
    )(qkv, qkv, qkv, qkv, qkv)


def _na_kernel(q_ref, k_ref, v_ref, kc_ref, vc_ref, bias_ref, o_ref, *, rb, hps, grid_rows, n_row_blocks):
    t = pl.program_id(2)
    win = NA_WIN_R * GRID_W

    @pl.when(t < n_row_blocks)
    def _():
        units = []
        for hh in range(hps):
            hs = slice(hh * ATT_HEAD, (hh + 1) * ATT_HEAD)
            for rr in range(rb):
                r = t * rb + rr
                rs = jnp.clip(r - NA_WIN_R // 2, 0, grid_rows - NA_WIN_R)
                units.append(dict(hh=hh, hs=hs, rows=slice(rr * GRID_W, (rr + 1) * GRID_W), off=r - rs,
                                  start=pl.multiple_of(rs * GRID_W, GRID_W)))
        for un in units:
            q = q_ref[un["rows"], un["hs"]]
            un["sw"] = _dot_nt(q, k_ref[pl.ds(un["start"], win), un["hs"]]) + bias_ref[un["hh"], un["off"]]
            un["sc"] = _dot_nt(q, kc_ref[:, un["hs"]])
        for un in units:
            m = jnp.maximum(un["sw"].max(axis=-1, keepdims=True), un["sc"].max(axis=-1, keepdims=True))
            un["pw"] = jnp.exp(un["sw"] - m)
            un["pc"] = jnp.exp(un["sc"] - m)
        for un in units:
            l = un["pw"].sum(axis=-1, keepdims=True) + un["pc"].sum(axis=-1, keepdims=True)
            o = (_dot(un["pw"].astype(BF16), v_ref[pl.ds(un["start"], win), un["hs"]])
                 + _dot(un["pc"].astype(BF16), vc_ref[:, un["hs"]]))
            o_ref[un["rows"], un["hs"]] = (o / l).astype(BF16)

    @pl.when(t >= n_row_blocks)
    def _():
        for hh in range(hps):
            hs = slice(hh * ATT_HEAD, (hh + 1) * ATT_HEAD)
            o_ref[:, hs] = _softmax_pv(q_ref[:, hs], [(kc_ref[:, hs], vc_ref[:, hs])]).astype(BF16)


def _na_bias_table(rpb):
    qc = jnp.arange(GRID_W)
    kc = jnp.arange(GRID_W)
    cs = jnp.clip(qc - NA_WIN_C // 2, 0, GRID_W - NA_WIN_C)
    inwin = (kc[None, :] >= cs[:, None]) & (kc[None, :] < cs[:, None] + NA_WIN_C)
    cidx = kc[None, :] - qc[:, None] + NA_WIN_C - 1
    sel = (cidx[None] == jnp.arange(2 * NA_WIN_C - 1)[:, None, None]) & inwin[None]
    cols = jnp.einsum('hrc,cqk->hrqk', rpb, sel.astype(F32), precision=lax.Precision.HIGHEST)
    cols = jnp.where(inwin[None, None], cols, NEG_BIG)
    tab = jnp.stack([cols[:, NA_WIN_R - 1 - o:2 * NA_WIN_R - 1 - o] for o in range(NA_WIN_R)], axis=1)
    tab = tab.transpose(0, 1, 3, 2, 4)
    return tab.reshape(rpb.shape[0], NA_WIN_R, GRID_W, NA_WIN_R * GRID_W).astype(F32)


def _na_attention(qkv, rpb, *, dm, d):
    rows = qkv.shape[0]
    nb, seq, ctx = dm["nb"], dm["seq"], dm["ctx"]
    heads = d // ATT_HEAD
    grid_rows = seq // GRID_W
    rb = ctx // GRID_W
    nrb = grid_rows // rb
    ctx_blk0 = nb * seq // ctx
    bias = _na_bias_table(rpb)

    def qmap(b, h, t):
        return (jnp.where(t < nrb, b * nrb + t, ctx_blk0 + b), h)

    hps = 2
    hw = hps * ATT_HEAD
    ng = heads // hps
    kern = functools.partial(_na_kernel, rb=rb, hps=hps, grid_rows=grid_rows, n_row_blocks=nrb)
    return pl.pallas_call(
        kern,
        out_shape=jax.ShapeDtypeStruct((rows, d), BF16),
        grid=(nb, ng, nrb + 1),
        in_specs=[
            pl.BlockSpec((ctx, hw), qmap),
            pl.BlockSpec((seq, hw), lambda b, h, t: (b, ng + h)),
            pl.BlockSpec((seq, hw), lambda b, h, t: (b, 2 * ng + h)),
            pl.BlockSpec((ctx, hw), lambda b, h, t: (ctx_blk0 + b, ng + h)),
            pl.BlockSpec((ctx, hw), lambda b, h, t: (ctx_blk0 + b, 2 * ng + h)),
            pl.BlockSpec((hps, NA_WIN_R, GRID_W, NA_WIN_R * GRID_W), lambda b, h, t: (h, 0, 0, 0)),
        ],
        out_specs=pl.BlockSpec((ctx, hw), qmap),
        compiler_params=_cparams(("parallel", "parallel", "arbitrary")),
        name="na_attention",
    )(qkv, qkv, qkv, qkv, qkv, bias)


def _rw_prep_kernel(x_ref, xp_ref, xn_ref, mod_ref, g_ref, mu_ref, o_ref, *, te, n_lat_rows, seq, ctx):
    i = pl.program_id(0)
    g = g_ref[...]
    shift = mod_ref[0, 0:1, :]
    scale = mod_ref[0, 1:2, :]
    h = _normmod(x_ref[...], g, shift, scale)
    halo = _normmod(jnp.concatenate([xp_ref[...], xn_ref[...]], axis=0), g, shift, scale)
    rid, first, last = _seq_edges(i, te, n_lat_rows, seq, ctx)
    hp, hn = _shift_rows(h, rid, first, last, halo[SUBLANES - 1:SUBLANES], halo[SUBLANES:SUBLANES + 1])
    xx = 0.5 * (hp + hn) - h
    for p in range(6):
        o_ref[p] = (h + xx * mu_ref[p:p + 1, :]).astype(BF16)


def _rw_prep(xs, mod, g1, mu, *, dm):
    rows, d = xs.shape
    te = dm["te"]
    hb = te // SUBLANES
    last_hb = rows // SUBLANES - 1
    modmap = lambda i: (jnp.minimum((i * te) // dm["seq"], dm["nb"]), 0, 0)
    kern = functools.partial(_rw_prep_kernel, te=te, n_lat_rows=dm["n_lat_rows"], seq=dm["seq"], ctx=dm["ctx"])
    return pl.pallas_call(
        kern,
        out_shape=jax.ShapeDtypeStruct((6, rows, d), BF16),
        grid=(rows // te,),
        in_specs=[
            pl.BlockSpec((te, d), lambda i: (i, 0)),
            pl.BlockSpec((SUBLANES, d), lambda i: (jnp.maximum(i * hb - 1, 0), 0)),
            pl.BlockSpec((SUBLANES, d), lambda i: (jnp.minimum((i + 1) * hb, last_hb), 0)),
            pl.BlockSpec((1, 6, d), modmap),
            pl.BlockSpec((1, d), lambda i: (0, 0)),
            pl.BlockSpec((6, d), lambda i: (0, 0)),
        ],
        out_specs=pl.BlockSpec((6, te, d), lambda i: (0, i, 0)),
        compiler_params=_cparams(("parallel",)),
        name="rwkv_prep",
    )(xs, xs, xs, mod, g1.reshape(1, d), mu)


def _rw_lora_kernel(*refs, mix):
    if mix:
        (xw_ref, xa_ref, xg_ref, xv_ref, w1_ref, a1_ref, g1_ref, v1_ref, w2_ref, a2_ref, g2_ref, v2_ref,
         w0_ref, a0_ref, v0_ref, lw_ref, a_ref, g_ref, vg_ref) = refs
    else:
        (xw_ref, xa_ref, xg_ref, w1_ref, a1_ref, g1_ref, w2_ref, a2_ref, g2_ref,
         w0_ref, a0_ref, lw_ref, a_ref, g_ref) = refs
    zw = jnp.tanh(_dot(xw_ref[0], w1_ref[...])).astype(BF16)
    za = _dot(xa_ref[0], a1_ref[...]).astype(BF16)
    zg = jax.nn.sigmoid(_dot(xg_ref[0], g1_ref[...])).astype(BF16)
    for dd in range(2):
        sl = slice(dd * LANES, (dd + 1) * LANES)
        wl = w0_ref[dd:dd + 1, :] + _dot(zw[:, sl], w2_ref[dd])
        lw_ref[dd] = (-math.exp(-0.5)) * jax.nn.sigmoid(wl)
        a_ref[dd] = jax.nn.sigmoid(a0_ref[dd:dd + 1, :] + _dot(za[:, sl], a2_ref[dd])).astype(BF16)
    g_ref[...] = _dot(zg, g2_ref[...]).astype(BF16)
    if mix:
        zv = _dot(xv_ref[0], v1_ref[...]).astype(BF16)
        vg_ref[...] = jax.nn.sigmoid(v0_ref[...] + _dot(zv, v2_ref[...])).astype(BF16)


def _pad_rank(w1, w2):
    r = w1.shape[-1]
    pad = (-r) % LANES
    w1 = jnp.pad(w1, [(0, 0)] * (w1.ndim - 1) + [(0, pad)])
    w2 = jnp.pad(w2, [(0, 0)] * (w2.ndim - 2) + [(0, pad), (0, 0)])
    return w1.astype(BF16), w2.astype(BF16)


def _rw_lora(xm, w0, w1, w2, a0, a1, a2, g1, g2, vres, *, dm):
    _, rows, d = xm.shape
    te = dm["te"]
    mix = vres is not None
    w1p, w2p = _pad_rank(w1, w2)
    a1p, a2p = _pad_rank(a1, a2)
    w1c = jnp.concatenate([w1p[0], w1p[1]], axis=1)
    a1c = jnp.concatenate([a1p[0], a1p[1]], axis=1)
    rg = g1.shape[1]
    full2 = lambda shp: pl.BlockSpec(shp, lambda i: (0,) * len(shp))
    xspec = lambda p: pl.BlockSpec((1, te, d), lambda i, p=p: (p, i, 0))
    in_specs = [xspec(3), xspec(4), xspec(5)]
    args = [xm, xm, xm]
    if mix:
        in_specs.append(xspec(2))
        args.append(xm)
        v1p, v2p = _pad_rank(vres[1], vres[2])
    in_specs += [full2((d, 2 * LANES)), full2((d, 2 * LANES)), full2((d, rg))]
    args += [w1c, a1c, g1.astype(BF16)]
    if mix:
        in_specs.append(full2((d, LANES)))
        args.append(v1p)
    in_specs += [full2((2, LANES, d)), full2((2, LANES, d)), full2((rg, d))]
    args += [w2p, a2p, g2.astype(BF16)]
    if mix:
        in_specs.append(full2((LANES, d)))
        args.append(v2p)
    in_specs += [full2((2, d)), full2((2, d))]
    args += [w0, a0]
    if mix:
        in_specs.append(full2((1, d)))
        args.append(vres[0].reshape(1, d))
    out_shape = [jax.ShapeDtypeStruct((2, rows, d), F32), jax.ShapeDtypeStruct((2, rows, d), BF16),
                 jax.ShapeDtypeStruct((rows, d), BF16)]
    out_specs = [pl.BlockSpec((2, te, d), lambda i: (0, i, 0)), pl.BlockSpec((2, te, d), lambda i: (0, i, 0)),
                 pl.BlockSpec((te, d), lambda i: (i, 0))]
    if mix:
        out_shape.append(jax.ShapeDtypeStruct((rows, d), BF16))
        out_specs.append(pl.BlockSpec((te, d), lambda i: (i, 0)))
    return pl.pallas_call(
        functools.partial(_rw_lora_kernel, mix=mix),
        out_shape=out_shape,
        grid=(rows // te,),
        in_specs=in_specs,
        out_specs=out_specs,
        compiler_params=_cparams(("parallel",)),
        name="rwkv_lora",
    )(*args)


def _rkv_kernel(*refs, mix):
    if mix:
        xm_ref, w_ref, vf_ref, vg_ref, o_ref = refs
    else:
        xm_ref, w_ref, o_ref = refs
    acc = _dot(xm_ref[0], w_ref[0])
    if mix:
        p = pl.program_id(1)

        @pl.when(p == 2)
        def _():
            o_ref[0] = (acc + (vf_ref[0].astype(F32) - acc) * vg_ref[...].astype(F32)).astype(BF16)

        @pl.when(p != 2)
        def _():
            o_ref[0] = acc.astype(BF16)
    else:
        o_ref[0] = acc.astype(BF16)


def _rkv_proj(xm, w, v_first, vgate, *, dm):
    _, rows, d = xm.shape
    tm = dm["tm"]
    mix = v_first is not None
    in_specs = [pl.BlockSpec((1, tm, d), lambda i, p: (p, i, 0)),
                pl.BlockSpec((1, d, d), lambda i, p: (p, 0, 0))]
    args = [xm, w]
    if mix:
        in_specs += [pl.BlockSpec((1, tm, d), lambda i, p: (2, i, 0)), pl.BlockSpec((tm, d), lambda i, p: (i, 0))]
        args += [v_first, vgate]
    return pl.pallas_call(
        functools.partial(_rkv_kernel, mix=mix),
        out_shape=jax.ShapeDtypeStruct((3, rows, d), BF16),
        grid=(rows // tm, 3),
        in_specs=in_specs,
        out_specs=pl.BlockSpec((1, tm, d), lambda i, p: (p, i, 0)),
        compiler_params=_cparams(("parallel", "arbitrary")),
        name="rwkv_rkv_proj",
    )(*args)


def _wkv_kernel(rf_ref, kf_ref, vf_ref, lwf_ref, af_ref, rb_ref, kb_ref, vb_ref, lwb_ref, ab_ref,
                kk_ref, ka_ref, rk_ref, yf_ref, bonf_ref, yb_ref, bonb_ref, s_scr, *, gps):
    c = pl.program_id(2)
    ln = WKV_CHUNK
    pw = WKV_PACK * RW_HEAD

    hpt = LANES // RW_HEAD

    @pl.when(c == 0)
    def _():
        s_scr[...] = jnp.zeros_like(s_scr)

    row = lax.broadcasted_iota(jnp.int32, (ln, ln), 0)
    col = lax.broadcasted_iota(jnp.int32, (ln, ln), 1)
    trow = lax.broadcasted_iota(jnp.int32, (ln, pw), 0)
    tsrc = lax.broadcasted_iota(jnp.int32, (ln, pw), 1) & (ln - 1)
    ones64 = _ones_blockdiag64()

    def fmask(cond):
        return jnp.where(cond, 1.0, 0.0)

    def same_block(n):
        sh = n.bit_length() - 1
        return (tsrc >> sh) == (trow >> sh)

    eye = fmask(tsrc == trow)
    base_f = fmask(same_block(WKV_INV_BASE))
    off_f = {}
    n = WKV_INV_BASE
    while n < ln:
        off_f[n] = fmask(jnp.logical_and(same_block(2 * n), jnp.logical_not(same_block(n))))
        n *= 2
    k_k = kk_ref[...]
    k_a = ka_ref[...]
    r_k = rk_ref[...]

    lane = lax.broadcasted_iota(jnp.int32, (ln, LANES), 1)
    half_f = [fmask((lane >> (RW_HEAD.bit_length() - 1)) == hf) for hf in range(hpt)]
    half_b = [hm.astype(BF16) for hm in half_f]
    zeros_b = jnp.zeros((ln, LANES), BF16)

    def bd(z):
        zb = z.astype(BF16)
        blocks = []
        for jh in range(WKV_PACK):
            lt, hf = divmod(jh, hpt)
            piece = zb[:, lt * LANES:(lt + 1) * LANES] * half_b[hf]
            blocks.append(jnp.concatenate([piece if tt == lt else zeros_b for tt in range(pw // LANES)], axis=1))
        return jnp.concatenate(blocks, axis=0)

    streams = ((rf_ref, kf_ref, vf_ref, lwf_ref, af_ref, yf_ref, bonf_ref),
               (rb_ref, kb_ref, vb_ref, lwb_ref, ab_ref, yb_ref, bonb_ref))
    units = []
    for dd, (r_ref, k_ref, v_ref, lw_ref, a_ref, y_ref, bon_ref) in enumerate(streams):
        rev = dd == 1
        r = r_ref[0].astype(F32)
        k = k_ref[0].astype(F32)
        v = v_ref[0].astype(F32)
        a = a_ref[0].astype(F32)
        lw = lw_ref[0]
        kkr = k * k_k
        kd = k * (1.0 + (a - 1.0) * k_a)
        ssq, rkd = _segsum64_mxu([kkr * kkr, r * kd * r_k], ones64)
        kk = kkr * lax.rsqrt(jnp.maximum(ssq, 1e-24))
        bvec = kk * a
        bon_ref[...] = (rkd * v).astype(BF16)

        tri = jnp.where((col >= row) if rev else (col <= row), 1.0, 0.0).astype(BF16)
        hi = lw.astype(BF16)
        rem = lw - hi.astype(F32)
        mid = rem.astype(BF16)
        lo = (rem - mid.astype(F32)).astype(BF16)
        cum = _dot(tri, hi) + _dot(tri, mid) + _dot(tri, lo)
        tot = cum[0:1, :] if rev else cum[ln - 1:ln, :]
        w_inv = jnp.exp(-cum)
        w_end = jnp.exp(tot - cum)
        a_t = -kk * jnp.exp(cum - lw)
        r_t = r * jnp.exp(cum)
        b_t = bvec * w_inv
        k_t = kd * w_inv
        b_e = bvec * w_end
        k_e = kd * w_end
        w_tot = jnp.exp(tot)

        strict = fmask(tsrc > trow) if rev else fmask(tsrc < trow)
        incl = strict + eye

        for gi in range(gps):
            sl = slice(gi * pw, (gi + 1) * pw)
            units.append(dict(
                dd=dd, gi=gi, sl=sl, y_ref=y_ref, strict=strict, incl=incl,
                ar=jnp.concatenate([a_t[:, sl], r_t[:, sl]], axis=0).astype(BF16),
                b_t=b_t[:, sl], k_t=k_t[:, sl], v=v[:, sl], w_tot=w_tot[:, sl],
                bk=jnp.concatenate([b_e[:, sl], k_e[:, sl]], axis=0).astype(BF16)))

    for un in units:
        sb = _dot_nt(un["ar"], bd(un["b_t"]))
        sk = _dot_nt(un["ar"], bd(un["k_t"]))
        un["m_ab"] = sb[:ln] * un["strict"]
        un["p_rb"] = sb[ln:] * un["incl"]
        un["m_ak"] = sk[:ln] * un["strict"]
        un["p_rk"] = sk[ln:] * un["incl"]
    for un in units:
        un["s0"] = s_scr[un["dd"], un["gi"]]
        un["ars"] = _dot_nt(un["ar"], un["s0"].astype(BF16))
        un["mv"] = _dot(jnp.concatenate([un["m_ak"], un["p_rk"]], axis=0).astype(BF16), bd(un["v"]))
    for un in units:
        m0 = un["m_ab"] * base_f
        un["pinv"] = eye + m0
        un["mp"] = _dot(m0.astype(BF16), bd(m0))
    for un in units:
        both = _dot(jnp.concatenate([un["mp"], un["pinv"]], axis=0).astype(BF16), bd(un["mp"]))
        un["pinv"] = un["pinv"] + both[ln:]
        un["mp"] = both[:ln]
    for un in units:
        un["pinv"] = un["pinv"] + _dot(un["pinv"].astype(BF16), bd(un["mp"]))
    n = WKV_INV_BASE
    while n < ln:
        for un in units:
            un["t1"] = _dot((un["m_ab"] * off_f[n]).astype(BF16), bd(un["pinv"]))
        for un in units:
            un["pinv"] = un["pinv"] + _dot(un["pinv"].astype(BF16), bd(un["t1"]))
        n *= 2
    for un in units:
        un["u"] = _dot(un["pinv"].astype(BF16), bd(un["ars"][:ln] + un["mv"][:ln]))
    for un in units:
        un["y_ref"][:, un["sl"]] = un["ars"][ln:] + _dot(un["p_rb"].astype(BF16), bd(un["u"])) + un["mv"][ln:]
        uv = jnp.concatenate([un["u"], un["v"]], axis=0).astype(BF16)
        res = _dot_tn(uv, un["bk"])
        for jh in range(WKV_PACK):
            lt, hf = divmod(jh, hpt)
            rsl = slice(jh * RW_HEAD, (jh + 1) * RW_HEAD)
            csl = slice(lt * LANES, (lt + 1) * LANES)
            s_scr[un["dd"], un["gi"], rsl, csl] = (un["s0"][rsl, csl] * un["w_tot"][:, csl]
                                                    + res[rsl, csl] * half_f[hf])


def _wkv(rkv, lw, a, k_k, k_a, r_k, *, dm, gps):
    _, rows, d = rkv.shape
    nb, seq, ctx = dm["nb"], dm["seq"], dm["ctx"]
    ln = WKV_CHUNK
    sw = gps * WKV_PACK * RW_HEAD
    ncc, nlc = ctx // ln, seq // ln
    ctx_c0 = nb * seq // ln

    def fblk(b, c):
        return jnp.where(c < ncc, ctx_c0 + b * ncc + c, b * nlc + (c - ncc))

    def bblk(b, c):
        return jnp.where(c < ncc, ctx_c0 + b * ncc + (ncc - 1 - c), b * nlc + (nlc - 1 - (c - ncc)))

    def spec3(p, blk):
        return pl.BlockSpec((1, ln, sw), lambda b, s, c, p=p, blk=blk: (p, blk(b, c), s))

    def spec2(blk):
        return pl.BlockSpec((ln, sw), lambda b, s, c, blk=blk: (blk(b, c), s))

    pspec = pl.BlockSpec((1, sw), lambda b, s, c: (0, s))
    in_specs = [spec3(0, fblk), spec3(1, fblk), spec3(2, fblk), spec3(0, fblk), spec3(0, fblk),
                spec3(0, bblk), spec3(1, bblk), spec3(2, bblk), spec3(1, bblk), spec3(1, bblk),
                pspec, pspec, pspec]
    return pl.pallas_call(
        functools.partial(_wkv_kernel, gps=gps),
        out_shape=[jax.ShapeDtypeStruct((rows, d), F32), jax.ShapeDtypeStruct((rows, d), BF16),
                   jax.ShapeDtypeStruct((rows, d), F32), jax.ShapeDtypeStruct((rows, d), BF16)],
        grid=(nb, d // sw, ncc + nlc),
        in_specs=in_specs,
        out_specs=[spec2(fblk), spec2(fblk), spec2(bblk), spec2(bblk)],
        scratch_shapes=[pltpu.VMEM((2, gps, WKV_PACK * RW_HEAD, WKV_PACK * RW_HEAD), F32)],
        compiler_params=_cparams(("parallel", "parallel", "arbitrary")),
        name="wkv_scan",
    )(rkv, rkv, rkv, lw, a, rkv, rkv, rkv, lw, a, k_k.reshape(1, d), k_a.reshape(1, d), r_k.reshape(1, d))


def _rw_out_kernel(yf_ref, yb_ref, bf_ref, bb_ref, g_ref, x_ref, mod_ref, lg_ref, lb_ref, w_ref, o_ref):
    y = yf_ref[...] + yb_ref[...]
    ones64 = _ones_blockdiag64()
    mean = _segsum64_mxu([y], ones64)[0] * (1.0 / RW_HEAD)
    yc = y - mean
    var = _segsum64_mxu([yc * yc], ones64)[0] * (1.0 / RW_HEAD)
    yn = yc * lax.rsqrt(var + RW_GN_EPS)
    bonus = bf_ref[...].astype(F32) + bb_ref[...].astype(F32)
    o = (yn * lg_ref[...] + lb_ref[...] + bonus) * g_ref[...].astype(F32)
    o_ref[...] = x_ref[...] + mod_ref[0, 2:3, :] * _dot(o.astype(BF16), w_ref[...])


def _rw_out(yf, bonf, yb, bonb, g, xs, mod, ln_g, ln_b, w_o, *, dm, n_rows):
    d = xs.shape[1]
    te = dm["te"]
    modmap = lambda i: (jnp.minimum((i * te) // dm["seq"], dm["nb"]), 0, 0)
    rspec = pl.BlockSpec((te, d), lambda i: (i, 0))
    vspec = pl.BlockSpec((1, d), lambda i: (0, 0))
    return pl.pallas_call(
        _rw_out_kernel,
        out_shape=jax.ShapeDtypeStruct((n_rows, d), F32),
        grid=(n_rows // te,),
        in_specs=[rspec, rspec, rspec, rspec, rspec, rspec, pl.BlockSpec((1, 6, d), modmap), vspec, vspec,
                  pl.BlockSpec((d, d), lambda i: (0, 0))],
        out_specs=rspec,
        compiler_params=_cparams(("parallel",)),
        name="rwkv_out",
    )(yf, yb, bonf, bonb, g, xs, mod, ln_g.reshape(1, d), ln_b.reshape(1, d), w_o)


def _rope_tables(dm):
    seq, nb, ctx = dm["seq"], dm["nb"], dm["ctx"]
    t = jnp.arange(seq, dtype=jnp.int32)
    pos = jnp.stack([t // GRID_W, t % GRID_W], axis=-1).astype(F32)
    n_freq = ATT_HEAD // 4
    inv = ROPE_THETA ** (-jnp.arange(n_freq, dtype=F32) / n_freq)
    ang = pos[:, :, None] * inv
    cos, sin = jnp.cos(ang), jnp.sin(ang)
    zero = jnp.zeros_like(sin)
    cos_t = jnp.stack([cos, cos], axis=2).reshape(seq, ATT_HEAD)
    sa_t = jnp.stack([-sin, zero], axis=2).reshape(seq, ATT_HEAD)
    sb_t = jnp.stack([zero, sin], axis=2).reshape(seq, ATT_HEAD)
    nctx = nb * ctx
    full = lambda tab, fill: jnp.concatenate([jnp.tile(tab, (nb, 1)), jnp.full((nctx, ATT_HEAD), fill, F32)], axis=0)
    return full(cos_t, 1.0), full(sa_t, 0.0), full(sb_t, 0.0)


def kernel(x, c, ctx, c_ctx, mod_w, mod_b, norm1_g, norm2_g, ffn_up, ffn_conv_w, ffn_conv_b, ffn_down, rw_mu, rw_w_rkv, rw_w0, rw_w1, rw_w2, rw_a0, rw_a1, rw_a2, rw_g1, rw_g2, rw_k_k, rw_k_a, rw_r_k, rw_ln_g, rw_ln_b, rw_w_o, rw_v0, rw_v1, rw_v2, na_w_qkv, na_q_g, na_k_g, na_rpb, na_w_o, ga_w_qkv, ga_q_g, ga_k_g, ga_w_o):
    nb, seq, d = x.shape
    nctx = ctx.shape[1]
    depth = mod_w.shape[0]
    tm = nb * nctx
    assert seq % tm == 0 and seq & (seq - 1) == 0 and nctx & (nctx - 1) == 0 and nb + 1 <= SUBLANES
    assert seq // GRID_W >= NA_WIN_R and nctx % GRID_W == 0
    dm = dict(nb=nb, seq=seq, ctx=nctx, tm=tm, te=tm // 2, tpb=seq // tm, n_lat_rows=nb * seq)
    n_lat_tiles = nb * seq // tm
    n_tiles = n_lat_tiles + 1
    att_scale = ATT_HEAD ** -0.5

    xs = jnp.concatenate([x.reshape(nb * seq, d), ctx.reshape(nb * nctx, d)], axis=0)
    c_all = jnp.concatenate([c, c_ctx[None], jnp.zeros((SUBLANES - nb - 1, d), F32)], axis=0)
    mods = _modulations(c_all, mod_w, mod_b)
    rope_tabs = None
    v_first = None

    for i in range(depth):
        kind, j = i % 3, i // 3
        need_ctx = i < depth - 1
        nt_out = n_tiles if need_ctx else n_lat_tiles
        mod = mods[i, :nb + 1].reshape(nb + 1, 6, d)
        if kind == 0:
            xm = _rw_prep(xs, mod, norm1_g[i], rw_mu[j], dm=dm)
            vres = None if j == 0 else (rw_v0[j - 1], rw_v1[j - 1], rw_v2[j - 1])
            lora = _rw_lora(xm, rw_w0[j], rw_w1[j], rw_w2[j], rw_a0[j], rw_a1[j], rw_a2[j], rw_g1[j], rw_g2[j],
                            vres, dm=dm)
            lw, a, g = lora[0], lora[1], lora[2]
            rkv = _rkv_proj(xm, rw_w_rkv[j].astype(BF16), v_first if vres is not None else None,
                            lora[3] if vres is not None else None, dm=dm)
            if v_first is None:
                v_first = rkv
            yf, bonf, yb, bonb = _wkv(rkv, lw, a, rw_k_k[j], rw_k_a[j], rw_r_k[j].reshape(-1), dm=dm,
                                      gps=min(8, d // (WKV_PACK * RW_HEAD)))
            xs = _rw_out(yf, bonf, yb, bonb, g, xs, mod, rw_ln_g[j], rw_ln_b[j], rw_w_o[j].astype(BF16),
                         dm=dm, n_rows=nt_out * tm)
        elif kind == 1:
            qkv = _qkv_proj(xs, mod, norm1_g[i], na_w_qkv[j].astype(BF16), na_q_g[j] * att_scale, na_k_g[j],
                            None, dm=dm, kv_dim=d)
            o = _na_attention(qkv, na_rpb[j], dm=dm, d=d)
            xs = _out_proj(o, na_w_o[j].astype(BF16), xs, mod, dm=dm, n_tiles=nt_out)
        else:
            if rope_tabs is None:
                rope_tabs = _rope_tables(dm)
            kv_dim = (ga_w_qkv.shape[-1] - d) // 2
            qkv = _qkv_proj(xs, mod, norm1_g[i], ga_w_qkv[j].astype(BF16), ga_q_g[j] * att_scale, ga_k_g[j],
                            rope_tabs, dm=dm, kv_dim=kv_dim)
            o = _gqa_attention(qkv, dm=dm, d=d, kv_heads=kv_dim // ATT_HEAD)
            xs = _out_proj(o, ga_w_o[j].astype(BF16), xs, mod, dm=dm, n_tiles=nt_out)
        xs = _ffn(xs, mod, norm2_g[i], ffn_up[i].astype(BF16), ffn_conv_w[i], ffn_conv_b[i],
                  ffn_down[i].astype(BF16), dm=dm, n_tiles=nt_out)
    return xs[:nb * seq].reshape(nb, seq, d)
```

```python
import functools
import math

import jax
import jax.numpy as jnp
from jax import lax
from jax.experimental import pallas as pl
from jax.experimental.pallas import tpu as pltpu

F32 = jnp.float32
BF16 = jnp.bfloat16

NORM_EPS = 1e-6
GRID_W = 64
ATT_HEAD = 128
RW_HEAD = 64
NA_WIN_R = 8
NA_WIN_C = 16
ROPE_THETA = 10000.0
RW_GN_EPS = 64e-5
LANES = 128
SUBLANES = 8
WKV_CHUNK = 64
WKV_PACK = 4
WKV_INV_BASE = 8
FFN_ROW_PIECES = 4
VMEM_LIMIT = 56 * 1024 * 1024
NEG_BIG = -1e30


def _cparams(sem):
    return pltpu.CompilerParams(dimension_semantics=sem, vmem_limit_bytes=VMEM_LIMIT)


def _dot(a, b):
    return jnp.dot(a, b, preferred_element_type=F32)


def _dot_nt(a, b):
    return lax.dot_general(a, b, (((1,), (1,)), ((), ())), preferred_element_type=F32)


def _dot_tn(a, b):
    return lax.dot_general(a, b, (((0,), (0,)), ((), ())), preferred_element_type=F32)


def _normmod(x, g, shift, scale):
    ms = jnp.mean(x * x, axis=-1, keepdims=True)
    y = x * lax.rsqrt(ms + NORM_EPS)
    return (y * g) * (1.0 + scale) + shift


def _silu(x):
    return x * jax.nn.sigmoid(x)


def _seq_edges(tile, rows, n_lat_rows, seq, ctx):
    rid = lax.broadcasted_iota(jnp.int32, (rows, 1), 0)
    base = tile * rows
    period = jnp.where(base >= n_lat_rows, ctx, seq)
    pos = (base + rid) & (period - 1)
    return rid, pos == 0, pos == period - 1


def _shift_rows(u, rid, first, last, prev_row, next_row):
    n = u.shape[0]
    up = pltpu.roll(u, 1, axis=0)
    up = jnp.where(rid == 0, prev_row, up)
    up = jnp.where(first, 0.0, up)
    un = pltpu.roll(u, n - 1, axis=0)
    un = jnp.where(rid == n - 1, next_row, un)
    un = jnp.where(last, 0.0, un)
    return up, un


def _ones_blockdiag64():
    sh = RW_HEAD.bit_length() - 1
    r = lax.broadcasted_iota(jnp.int32, (LANES, LANES), 0) >> sh
    c = lax.broadcasted_iota(jnp.int32, (LANES, LANES), 1) >> sh
    return jnp.where(r == c, 1.0, 0.0).astype(BF16)


def _segsum64_mxu(xs, ones):
    m, n = xs[0].shape
    pieces = []
    for x in xs:
        hi = x.astype(BF16)
        lo = (x - hi.astype(F32)).astype(BF16)
        for part in (hi, lo):
            pieces += [part[:, c * LANES:(c + 1) * LANES] for c in range(n // LANES)]
    res = _dot(jnp.concatenate(pieces, axis=0), ones)
    nslab = n // LANES
    outs = []
    for i in range(len(xs)):
        base = i * 2 * nslab
        cols = [res[(base + c) * m:(base + c + 1) * m] + res[(base + nslab + c) * m:(base + nslab + c + 1) * m]
                for c in range(nslab)]
        outs.append(jnp.concatenate(cols, axis=1))
    return outs


def _mod_kernel(c_ref, w_ref, b_ref, o_ref):
    s = _silu(c_ref[...]).astype(BF16)
    o_ref[0] = _dot(s, w_ref[0].astype(BF16)) + b_ref[0]


def _modulations(c_all, mod_w, mod_b):
    depth, d, n = mod_w.shape
    tn = n // 8
    return pl.pallas_call(
        _mod_kernel,
        out_shape=jax.ShapeDtypeStruct((depth, SUBLANES, n), F32),
        grid=(depth, n // tn),
        in_specs=[
            pl.BlockSpec((SUBLANES, d), lambda l, j: (0, 0)),
            pl.BlockSpec((1, d, tn), lambda l, j: (l, 0, j)),
            pl.BlockSpec((1, 1, tn), lambda l, j: (l, 0, j)),
        ],
        out_specs=pl.BlockSpec((1, SUBLANES, tn), lambda l, j: (l, 0, j)),
        compiler_params=_cparams(("parallel", "parallel")),
        name="modulation",
    )(c_all, mod_w, mod_b.reshape(depth, 1, n))


def _ffn_kernel(*refs, tm, tile_off, aliased, n_lat_rows, seq, ctx):
    (x_ref, xp_ref, xn_ref, mod_ref, g_ref, wug_ref, wuv_ref, cwg_ref, cwv_ref, cbg_ref, cbv_ref,
     wd_ref) = refs[:12]
    o_ref, h_scr = refs[-2:]
    i = pl.program_id(0) + tile_off
    j = pl.program_id(1)
    shift = mod_ref[0, 3:4, :]
    scale = mod_ref[0, 4:5, :]
    hr = 2 * SUBLANES
    th = tm // FFN_ROW_PIECES

    @pl.when(j == 0)
    def _():
        g = g_ref[...]
        halo = jnp.concatenate([xp_ref[...], xn_ref[...]], axis=0)
        h_scr[0:hr, :] = _normmod(halo, g, shift, scale).astype(BF16)
        h_scr[hr:hr + tm, :] = _normmod(x_ref[...], g, shift, scale).astype(BF16)
        o_ref[...] = jnp.zeros_like(o_ref)

    rid, first, last = _seq_edges(i, tm, n_lat_rows, seq, ctx)
    rid_h = rid[0:th]

    def up_proj(p):
        lo = 0 if p == 0 else hr + p * th
        hp = h_scr[lo:hr + (p + 1) * th, :]
        ug, uv = _dot(hp, wug_ref[...]), _dot(hp, wuv_ref[...])
        if p == 0:
            return dict(g=ug[hr:], v=uv[hr:], halo_g=ug[0:hr], halo_v=uv[0:hr])
        return dict(g=ug, v=uv)

    def conv(main, prev_row, next_row, fm, lm, cw_ref, cb_ref):
        up, un = _shift_rows(main, rid_h, fm, lm, prev_row, next_row)
        return cb_ref[...] + up * cw_ref[0:1, :] + main * cw_ref[1:2, :] + un * cw_ref[2:3, :]

    pv, nx = SUBLANES - 1, SUBLANES

    def act_down(p, us):
        rs = slice(p * th, (p + 1) * th)
        rows = {}
        for key in ("g", "v"):
            prev_row = us[0]["halo_" + key][pv:pv + 1] if p == 0 else us[p - 1][key][th - 1:th]
            next_row = us[0]["halo_" + key][nx:nx + 1] if p == FFN_ROW_PIECES - 1 else us[p + 1][key][0:1]
            rows[key] = (prev_row, next_row)
        a = (_silu(conv(us[p]["g"], *rows["g"], first[rs], last[rs], cwg_ref, cbg_ref))
             * conv(us[p]["v"], *rows["v"], first[rs], last[rs], cwv_ref, cbv_ref))
        o_ref[rs, :] += _dot(a.astype(BF16), wd_ref[...])

    us = {0: up_proj(0)}
    for p in range(FFN_ROW_PIECES):
        if p + 1 < FFN_ROW_PIECES:
            us[p + 1] = up_proj(p + 1)
        act_down(p, us)

    @pl.when(j == pl.num_programs(1) - 1)
    def _():
        o_ref[...] = x_ref[...] + mod_ref[0, 5:6, :] * o_ref[...]


def _ffn_call(xs, mod, g2, wu, cw, cb, wd, prev, *, dm, tm, tile_off, n_tiles, out_rows):
    rows, d = xs.shape
    f = wd.shape[0]
    fc = 512
    nfc = f // fc
    hb = tm // SUBLANES
    last_hb = rows // SUBLANES - 1
    seq, nb = dm["seq"], dm["nb"]
    assert seq % tm == 0 or tile_off * tm >= dm["n_lat_rows"]
    modmap = lambda i, j: (jnp.minimum(((i + tile_off) * tm) // seq, nb), 0, 0)
    kern = functools.partial(_ffn_kernel, tm=tm, tile_off=tile_off, aliased=prev is not None,
                             n_lat_rows=dm["n_lat_rows"], seq=seq, ctx=dm["ctx"])
    in_specs = [
        pl.BlockSpec((tm, d), lambda i, j: (i + tile_off, 0), pipeline_mode=pl.Buffered(1)),
        pl.BlockSpec((SUBLANES, d), lambda i, j: (jnp.maximum((i + tile_off) * hb - 1, 0), 0)),
        pl.BlockSpec((SUBLANES, d), lambda i, j: (jnp.minimum((i + tile_off + 1) * hb, last_hb), 0)),
        pl.BlockSpec((1, 6, d), modmap),
        pl.BlockSpec((1, d), lambda i, j: (0, 0)),
        pl.BlockSpec((d, fc), lambda i, j: (0, j)),
        pl.BlockSpec((d, fc), lambda i, j: (0, nfc + j)),
        pl.BlockSpec((3, fc), lambda i, j: (0, j)),
        pl.BlockSpec((3, fc), lambda i, j: (0, nfc + j)),
        pl.BlockSpec((1, fc), lambda i, j: (0, j)),
        pl.BlockSpec((1, fc), lambda i, j: (0, nfc + j)),
        pl.BlockSpec((fc, d), lambda i, j: (j, 0)),
    ]
    args = [xs, xs, xs, mod, g2.reshape(1, d), wu, wu, cw, cw, cb.reshape(1, -1), cb.reshape(1, -1), wd]
    aliases = {}
    if prev is not None:
        in_specs.append(pl.BlockSpec(memory_space=pl.ANY))
        args.append(prev)
        aliases = {len(args) - 1: 0}
    return pl.pallas_call(
        kern,
        out_shape=jax.ShapeDtypeStruct((out_rows, d), F32),
        grid=(n_tiles, nfc),
        in_specs=in_specs,
        out_specs=pl.BlockSpec((tm, d), lambda i, j: (i + tile_off, 0)),
        scratch_shapes=[pltpu.VMEM((tm + 2 * SUBLANES, d), BF16)],
        input_output_aliases=aliases,
        compiler_params=_cparams(("parallel", "arbitrary")),
        name="conv_ffn",
    )(*args)


def _ffn(xs, mod, g2, wu, cw, cb, wd, *, dm, need_ctx):
    rows = xs.shape[0]
    n_lat_rows, tm = dm["n_lat_rows"], dm["tm"]
    big = 2 * tm
    out_rows = rows if need_ctx else n_lat_rows
    y = _ffn_call(xs, mod, g2, wu, cw, cb, wd, None, dm=dm, tm=big, tile_off=0, n_tiles=n_lat_rows // big,
                  out_rows=out_rows)
    if need_ctx:
        y = _ffn_call(xs, mod, g2, wu, cw, cb, wd, y, dm=dm, tm=tm, tile_off=n_lat_rows // tm, n_tiles=1,
                      out_rows=out_rows)
    return y


def _qkv_kernel(*refs, block_kinds, rope):
    if rope:
        x_ref, mod_ref, g_ref, w_ref, qg_ref, kg_ref, cos_ref, sa_ref, sb_ref, o_ref, h_scr = refs
    else:
        x_ref, mod_ref, g_ref, w_ref, qg_ref, kg_ref, o_ref, h_scr = refs
    j = pl.program_id(1)

    @pl.when(j == 0)
    def _():
        h_scr[...] = _normmod(x_ref[...], g_ref[...], mod_ref[0, 0:1, :], mod_ref[0, 1:2, :]).astype(BF16)

    acc = _dot(h_scr[...], w_ref[...])

    def emit(kinds):
        for hh, kind in enumerate(kinds):
            hs = slice(hh * ATT_HEAD, (hh + 1) * ATT_HEAD)
            y = acc[:, hs]
            if kind != "v":
                gain = qg_ref[...] if kind == "q" else kg_ref[...]
                y = y * lax.rsqrt(jnp.mean(y * y, axis=-1, keepdims=True) + NORM_EPS) * gain
                if rope:
                    y = (y * cos_ref[...] + pltpu.roll(y, ATT_HEAD - 32, axis=1) * sa_ref[...]
                         + pltpu.roll(y, 32, axis=1) * sb_ref[...])
            o_ref[:, hs] = y.astype(BF16)

    for jb, kinds in enumerate(block_kinds):
        pl.when(j == jb)(functools.partial(emit, kinds))


def _qkv_proj(xs, mod, g1, w, qg, kg, rope_tabs, *, dm, kv_dim):
    rows, d = xs.shape
    n = w.shape[1]
    tn = min(d, 2 * kv_dim)
    hpb = tn // ATT_HEAD
    kinds = ["q"] * (d // ATT_HEAD) + ["k"] * (kv_dim // ATT_HEAD) + ["v"] * (kv_dim // ATT_HEAD)
    block_kinds = tuple(tuple(kinds[b * hpb:(b + 1) * hpb]) for b in range(n // tn))
    tm = dm["tm"]
    modmap = lambda i, j: (jnp.minimum(i // dm["tpb"], dm["nb"]), 0, 0)
    in_specs = [
        pl.BlockSpec((tm, d), lambda i, j: (i, 0)),
        pl.BlockSpec((1, 6, d), modmap),
        pl.BlockSpec((1, d), lambda i, j: (0, 0)),
        pl.BlockSpec((d, tn), lambda i, j: (0, j)),
        pl.BlockSpec((1, ATT_HEAD), lambda i, j: (0, 0)),
        pl.BlockSpec((1, ATT_HEAD), lambda i, j: (0, 0)),
    ]
    args = [xs, mod, g1.reshape(1, d), w, qg.reshape(1, ATT_HEAD), kg.reshape(1, ATT_HEAD)]
    if rope_tabs is not None:
        in_specs += [pl.BlockSpec((tm, ATT_HEAD), lambda i, j: (i, 0))] * 3
        args += list(rope_tabs)
    kern = functools.partial(_qkv_kernel, block_kinds=block_kinds, rope=rope_tabs is not None)
    return pl.pallas_call(
        kern,
        out_shape=jax.ShapeDtypeStruct((rows, n), BF16),
        grid=(rows // tm, n // tn),
        in_specs=in_specs,
        out_specs=pl.BlockSpec((tm, tn), lambda i, j: (i, j)),
        scratch_shapes=[pltpu.VMEM((tm, d), BF16)],
        compiler_params=_cparams(("parallel", "arbitrary")),
        name="qkv_proj",
    )(*args)


def _oproj_kernel(a_ref, w_ref, x_ref, mod_ref, o_ref):
    o_ref[...] = x_ref[...] + mod_ref[0, 2:3, :] * _dot(a_ref[...], w_ref[...])


def _out_proj(a, w, xs, mod, *, dm, n_tiles):
    d = xs.shape[1]
    tm = dm["tm"]
    modmap = lambda i: (jnp.minimum(i // dm["tpb"], dm["nb"]), 0, 0)
    return pl.pallas_call(
        _oproj_kernel,
        out_shape=jax.ShapeDtypeStruct((n_tiles * tm, d), F32),
        grid=(n_tiles,),
        in_specs=[
            pl.BlockSpec((tm, d), lambda i: (i, 0)),
            pl.BlockSpec((d, d), lambda i: (0, 0)),
            pl.BlockSpec((tm, d), lambda i: (i, 0)),
            pl.BlockSpec((1, 6, d), modmap),
        ],
        out_specs=pl.BlockSpec((tm, d), lambda i: (i, 0)),
        compiler_params=_cparams(("parallel",)),
        name="out_proj",
    )(a, w, xs, mod)


def _softmax_pv(q, segs):
    ss = [_dot_nt(q, k) for k, _ in segs]
    m = ss[0].max(axis=-1, keepdims=True)
    for s in ss[1:]:
        m = jnp.maximum(m, s.max(axis=-1, keepdims=True))
    ps = [jnp.exp(s - m) for s in ss]
    l = ps[0].sum(axis=-1, keepdims=True)
    for p in ps[1:]:
        l = l + p.sum(axis=-1, keepdims=True)
    o = _dot(ps[0].astype(BF16), segs[0][1])
    for p, (_, v) in zip(ps[1:], segs[1:]):
        o = o + _dot(p.astype(BF16), v)
    return o / l


def _lane_tiles(x, op):
    acc = x[:, 0:LANES]
    for j in range(1, x.shape[1] // LANES):
        acc = op(acc, x[:, j * LANES:(j + 1) * LANES])
    return acc


def _gqa_kernel(q_ref, kl_ref, vl_ref, kc_ref, vc_ref, o_ref, s_scr, *, group, n_lat_tiles, kchunk):
    t = pl.program_id(2)
    seq, ctx = kl_ref.shape[0], kc_ref.shape[0]

    def run(chunks):
        m = [None] * group
        mrun = [None] * group
        lrun = [None] * group
        o = [None] * group
        for g in range(group + 1):
            for k_ref, v_ref, st, sz, off in chunks:
                if g < group:
                    s = _dot_nt(q_ref[:, g * ATT_HEAD:(g + 1) * ATT_HEAD], k_ref[st:st + sz, :])
                    s_scr[g % 2, :, off:off + sz] = s
                    tmax = _lane_tiles(s, jnp.maximum)
                    mrun[g] = tmax if mrun[g] is None else jnp.maximum(mrun[g], tmax)
                if g >= 1:
                    h = g - 1
                    p = jnp.exp(s_scr[h % 2, :, off:off + sz] - m[h])
                    psum = _lane_tiles(p, jnp.add)
                    pv = _dot(p.astype(BF16), v_ref[st:st + sz, :])
                    lrun[h] = psum if lrun[h] is None else lrun[h] + psum
                    o[h] = pv if o[h] is None else o[h] + pv
            if g < group:
                m[g] = mrun[g].max(axis=-1, keepdims=True)
        for g in range(group):
            l = lrun[g].sum(axis=-1, keepdims=True)
            o_ref[:, g * ATT_HEAD:(g + 1) * ATT_HEAD] = (o[g] / l).astype(BF16)

    lat_chunks = [(kl_ref, vl_ref, st, kchunk, st) for st in range(0, seq, kchunk)]
    ctx_chunk = (kc_ref, vc_ref, 0, ctx, seq)

    @pl.when(t < n_lat_tiles)
    def _():
        run(lat_chunks + [ctx_chunk])

    @pl.when(t >= n_lat_tiles)
    def _():
        run([ctx_chunk])


def _gqa_attention(qkv, *, dm, d, kv_heads):
    rows = qkv.shape[0]
    nb, seq, ctx = dm["nb"], dm["seq"], dm["ctx"]
    group = d // ATT_HEAD // kv_heads
    gw = group * ATT_HEAD
    tq = 128
    nlt, nct = seq // tq, ctx // tq
    kcol = d // ATT_HEAD
    vcol = kcol + kv_heads
    ctx_blk0 = nb * seq // ctx

    def qmap(b, h, t):
        return (jnp.where(t < nlt, b * nlt + t, nb * nlt + b * nct + (t - nlt)), h)

    kern = functools.partial(_gqa_kernel, group=group, n_lat_tiles=nlt, kchunk=min(512, seq))
    return pl.pallas_call(
        kern,
        out_shape=jax.ShapeDtypeStruct((rows, d), BF16),
        scratch_shapes=[pltpu.VMEM((2, tq, seq + ctx), F32)],
        grid=(nb, kv_heads, nlt + nct),
        in_specs=[
            pl.BlockSpec((tq, gw), qmap),
            pl.BlockSpec((seq, ATT_HEAD), lambda b, h, t: (b, kcol + h)),
            pl.BlockSpec((seq, ATT_HEAD), lambda b, h, t: (b, vcol + h)),
            pl.BlockSpec((ctx, ATT_HEAD), lambda b, h, t: (ctx_blk0 + b, kcol + h)),
            pl.BlockSpec((ctx, ATT_HEAD), lambda b, h, t: (ctx_blk0 + b, vcol + h)),
        ],
        out_specs=pl.BlockSpec((tq, gw), qmap),
        compiler_params=_cparams(("parallel", "parallel", "arbitrary")),
        name="gqa_attention",
    )(qkv, qkv, qkv, qkv, qkv)


def _na_kernel(q_ref, k_ref, v_ref, kc_ref, vc_ref, bias_ref, o_ref, *, rb, hps, grid_rows, n_row_blocks):
    t = pl.program_id(2)
    win = NA_WIN_R * GRID_W

    @pl.when(t < n_row_blocks)
    def _():
        units = []
        for hh in range(hps):
            hs = slice(hh * ATT_HEAD, (hh + 1) * ATT_HEAD)
            for rr in range(rb):
                r = t * rb + rr
                rs = jnp.clip(r - NA_WIN_R // 2, 0, grid_rows - NA_WIN_R)
                units.append(dict(hh=hh, hs=hs, rows=slice(rr * GRID_W, (rr + 1) * GRID_W), off=r - rs,
                                  start=pl.multiple_of(rs * GRID_W, GRID_W)))
        for un in units:
            q = q_ref[un["rows"], un["hs"]]
            un["sw"] = _dot_nt(q, k_ref[pl.ds(un["start"], win), un["hs"]]) + bias_ref[un["hh"], un["off"]]
            un["sc"] = _dot_nt(q, kc_ref[:, un["hs"]])
        for un in units:
            m = jnp.maximum(un["sw"].max(axis=-1, keepdims=True), un["sc"].max(axis=-1, keepdims=True))
            un["pw"] = jnp.exp(un["sw"] - m)
            un["pc"] = jnp.exp(un["sc"] - m)
        for un in units:
            l = un["pw"].sum(axis=-1, keepdims=True) + un["pc"].sum(axis=-1, keepdims=True)
            o = (_dot(un["pw"].astype(BF16), v_ref[pl.ds(un["start"], win), un["hs"]])
                 + _dot(un["pc"].astype(BF16), vc_ref[:, un["hs"]]))
            o_ref[un["rows"], un["hs"]] = (o / l).astype(BF16)

    @pl.when(t >= n_row_blocks)
    def _():
        for hh in range(hps):
            hs = slice(hh * ATT_HEAD, (hh + 1) * ATT_HEAD)
            o_ref[:, hs] = _softmax_pv(q_ref[:, hs], [(kc_ref[:, hs], vc_ref[:, hs])]).astype(BF16)


def _na_bias_table(rpb):
    qc = jnp.arange(GRID_W)
    kc = jnp.arange(GRID_W)
    cs = jnp.clip(qc - NA_WIN_C // 2, 0, GRID_W - NA_WIN_C)
    inwin = (kc[None, :] >= cs[:, None]) & (kc[None, :] < cs[:, None] + NA_WIN_C)
    cidx = kc[None, :] - qc[:, None] + NA_WIN_C - 1
    sel = (cidx[None] == jnp.arange(2 * NA_WIN_C - 1)[:, None, None]) & inwin[None]
    cols = jnp.einsum('hrc,cqk->hrqk', rpb, sel.astype(F32), precision=lax.Precision.HIGHEST)
    cols = jnp.where(inwin[None, None], cols, NEG_BIG)
    tab = jnp.stack([cols[:, NA_WIN_R - 1 - o:2 * NA_WIN_R - 1 - o] for o in range(NA_WIN_R)], axis=1)
    tab = tab.transpose(0, 1, 3, 2, 4)
    return tab.reshape(rpb.shape[0], NA_WIN_R, GRID_W, NA_WIN_R * GRID_W).astype(F32)


def _na_attention(qkv, rpb, *, dm, d):
    rows = qkv.shape[0]
    nb, seq, ctx = dm["nb"], dm["seq"], dm["ctx"]
    heads = d // ATT_HEAD
    grid_rows = seq // GRID_W
    rb = ctx // GRID_W
    nrb = grid_rows // rb
    ctx_blk0 = nb * seq // ctx
    bias = _na_bias_table(rpb)

    def qmap(b, h, t):
        return (jnp.where(t < nrb, b * nrb + t, ctx_blk0 + b), h)

    hps = 2
    hw = hps * ATT_HEAD
    ng = heads // hps
    kern = functools.partial(_na_kernel, rb=rb, hps=hps, grid_rows=grid_rows, n_row_blocks=nrb)
    return pl.pallas_call(
        kern,
        out_shape=jax.ShapeDtypeStruct((rows, d), BF16),
        grid=(nb, ng, nrb + 1),
        in_specs=[
            pl.BlockSpec((ctx, hw), qmap),
            pl.BlockSpec((seq, hw), lambda b, h, t: (b, ng + h)),
            pl.BlockSpec((seq, hw), lambda b, h, t: (b, 2 * ng + h)),
            pl.BlockSpec((ctx, hw), lambda b, h, t: (ctx_blk0 + b, ng + h)),
            pl.BlockSpec((ctx, hw), lambda b, h, t: (ctx_blk0 + b, 2 * ng + h)),
            pl.BlockSpec((hps, NA_WIN_R, GRID_W, NA_WIN_R * GRID_W), lambda b, h, t: (h, 0, 0, 0)),
        ],
        out_specs=pl.BlockSpec((ctx, hw), qmap),
        compiler_params=_cparams(("parallel", "parallel", "arbitrary")),
        name="na_attention",
    )(qkv, qkv, qkv, qkv, qkv, bias)


def _rw_prep_kernel(x_ref, xp_ref, xn_ref, mod_ref, g_ref, mu_ref, o_ref, *, te, n_lat_rows, seq, ctx):
    i = pl.program_id(0)
    g = g_ref[...]
    shift = mod_ref[0, 0:1, :]
    scale = mod_ref[0, 1:2, :]
    h = _normmod(x_ref[...], g, shift, scale)
    halo = _normmod(jnp.concatenate([xp_ref[...], xn_ref[...]], axis=0), g, shift, scale)
    rid, first, last = _seq_edges(i, te, n_lat_rows, seq, ctx)
    hp, hn = _shift_rows(h, rid, first, last, halo[SUBLANES - 1:SUBLANES], halo[SUBLANES:SUBLANES + 1])
    xx = 0.5 * (hp + hn) - h
    for p in range(6):
        o_ref[p] = (h + xx * mu_ref[p:p + 1, :]).astype(BF16)


def _rw_prep(xs, mod, g1, mu, *, dm):
    rows, d = xs.shape
    te = dm["te"]
    hb = te // SUBLANES
    last_hb = rows // SUBLANES - 1
    modmap = lambda i: (jnp.minimum((i * te) // dm["seq"], dm["nb"]), 0, 0)
    kern = functools.partial(_rw_prep_kernel, te=te, n_lat_rows=dm["n_lat_rows"], seq=dm["seq"], ctx=dm["ctx"])
    return pl.pallas_call(
        kern,
        out_shape=jax.ShapeDtypeStruct((6, rows, d), BF16),
        grid=(rows // te,),
        in_specs=[
            pl.BlockSpec((te, d), lambda i: (i, 0)),
            pl.BlockSpec((SUBLANES, d), lambda i: (jnp.maximum(i * hb - 1, 0), 0)),
            pl.BlockSpec((SUBLANES, d), lambda i: (jnp.minimum((i + 1) * hb, last_hb), 0)),
            pl.BlockSpec((1, 6, d), modmap),
            pl.BlockSpec((1, d), lambda i: (0, 0)),
            pl.BlockSpec((6, d), lambda i: (0, 0)),
        ],
        out_specs=pl.BlockSpec((6, te, d), lambda i: (0, i, 0)),
        compiler_params=_cparams(("parallel",)),
        name="rwkv_prep",
    )(xs, xs, xs, mod, g1.reshape(1, d), mu)


def _rw_lora_kernel(*refs, mix):
    if mix:
        (xw_ref, xa_ref, xg_ref, xv_ref, w1_ref, a1_ref, g1_ref, v1_ref, w2_ref, a2_ref, g2_ref, v2_ref,
         w0_ref, a0_ref, v0_ref, lw_ref, a_ref, g_ref, vg_ref) = refs
    else:
        (xw_ref, xa_ref, xg_ref, w1_ref, a1_ref, g1_ref, w2_ref, a2_ref, g2_ref,
         w0_ref, a0_ref, lw_ref, a_ref, g_ref) = refs
    zw = jnp.tanh(_dot(xw_ref[0], w1_ref[...])).astype(BF16)
    za = _dot(xa_ref[0], a1_ref[...]).astype(BF16)
    zg = jax.nn.sigmoid(_dot(xg_ref[0], g1_ref[...])).astype(BF16)
    for dd in range(2):
        sl = slice(dd * LANES, (dd + 1) * LANES)
        wl = w0_ref[dd:dd + 1, :] + _dot(zw[:, sl], w2_ref[dd])
        lw_ref[dd] = (-math.exp(-0.5)) * jax.nn.sigmoid(wl)
        a_ref[dd] = jax.nn.sigmoid(a0_ref[dd:dd + 1, :] + _dot(za[:, sl], a2_ref[dd])).astype(BF16)
    g_ref[...] = _dot(zg, g2_ref[...]).astype(BF16)
    if mix:
        zv = _dot(xv_ref[0], v1_ref[...]).astype(BF16)
        vg_ref[...] = jax.nn.sigmoid(v0_ref[...] + _dot(zv, v2_ref[...])).astype(BF16)


def _pad_rank(w1, w2):
    r = w1.shape[-1]
    pad = (-r) % LANES
    w1 = jnp.pad(w1, [(0, 0)] * (w1.ndim - 1) + [(0, pad)])
    w2 = jnp.pad(w2, [(0, 0)] * (w2.ndim - 2) + [(0, pad), (0, 0)])
    return w1.astype(BF16), w2.astype(BF16)


def _rw_lora(xm, w0, w1, w2, a0, a1, a2, g1, g2, vres, *, dm):
    _, rows, d = xm.shape
    te = dm["te"]
    mix = vres is not None
    w1p, w2p = _pad_rank(w1, w2)
    a1p, a2p = _pad_rank(a1, a2)
    w1c = jnp.concatenate([w1p[0], w1p[1]], axis=1)
    a1c = jnp.concatenate([a1p[0], a1p[1]], axis=1)
    rg = g1.shape[1]
    full2 = lambda shp: pl.BlockSpec(shp, lambda i: (0,) * len(shp))
    xspec = lambda p: pl.BlockSpec((1, te, d), lambda i, p=p: (p, i, 0))
    in_specs = [xspec(3), xspec(4), xspec(5)]
    args = [xm, xm, xm]
    if mix:
        in_specs.append(xspec(2))
        args.append(xm)
        v1p, v2p = _pad_rank(vres[1], vres[2])
    in_specs += [full2((d, 2 * LANES)), full2((d, 2 * LANES)), full2((d, rg))]
    args += [w1c, a1c, g1.astype(BF16)]
    if mix:
        in_specs.append(full2((d, LANES)))
        args.append(v1p)
    in_specs += [full2((2, LANES, d)), full2((2, LANES, d)), full2((rg, d))]
    args += [w2p, a2p, g2.astype(BF16)]
    if mix:
        in_specs.append(full2((LANES, d)))
        args.append(v2p)
    in_specs += [full2((2, d)), full2((2, d))]
    args += [w0, a0]
    if mix:
        in_specs.append(full2((1, d)))
        args.append(vres[0].reshape(1, d))
    out_shape = [jax.ShapeDtypeStruct((2, rows, d), F32), jax.ShapeDtypeStruct((2, rows, d), BF16),
                 jax.ShapeDtypeStruct((rows, d), BF16)]
    out_specs = [pl.BlockSpec((2, te, d), lambda i: (0, i, 0)), pl.BlockSpec((2, te, d), lambda i: (0, i, 0)),
                 pl.BlockSpec((te, d), lambda i: (i, 0))]
    if mix:
        out_shape.append(jax.ShapeDtypeStruct((rows, d), BF16))
        out_specs.append(pl.BlockSpec((te, d), lambda i: (i, 0)))
    return pl.pallas_call(
        functools.partial(_rw_lora_kernel, mix=mix),
        out_shape=out_shape,
        grid=(rows // te,),
        in_specs=in_specs,
        out_specs=out_specs,
        compiler_params=_cparams(("parallel",)),
        name="rwkv_lora",
    )(*args)


def _rkv_kernel(*refs, mix):
    if mix:
        xm_ref, w_ref, vf_ref, vg_ref, o_ref = refs
    else:
        xm_ref, w_ref, o_ref = refs
    acc = _dot(xm_ref[0], w_ref[0])
    if mix:
        p = pl.program_id(1)

        @pl.when(p == 2)
        def _():
            o_ref[0] = (acc + (vf_ref[0].astype(F32) - acc) * vg_ref[...].astype(F32)).astype(BF16)

        @pl.when(p != 2)
        def _():
            o_ref[0] = acc.astype(BF16)
    else:
        o_ref[0] = acc.astype(BF16)


def _rkv_proj(xm, w, v_first, vgate, *, dm):
    _, rows, d = xm.shape
    tm = dm["tm"]
    mix = v_first is not None
    in_specs = [pl.BlockSpec((1, tm, d), lambda i, p: (p, i, 0)),
                pl.BlockSpec((1, d, d), lambda i, p: (p, 0, 0))]
    args = [xm, w]
    if mix:
        in_specs += [pl.BlockSpec((1, tm, d), lambda i, p: (2, i, 0)), pl.BlockSpec((tm, d), lambda i, p: (i, 0))]
        args += [v_first, vgate]
    return pl.pallas_call(
        functools.partial(_rkv_kernel, mix=mix),
        out_shape=jax.ShapeDtypeStruct((3, rows, d), BF16),
        grid=(rows // tm, 3),
        in_specs=in_specs,
        out_specs=pl.BlockSpec((1, tm, d), lambda i, p: (p, i, 0)),
        compiler_params=_cparams(("parallel", "arbitrary")),
        name="rwkv_rkv_proj",
    )(*args)


def _wkv_kernel(rf_ref, kf_ref, vf_ref, lwf_ref, af_ref, rb_ref, kb_ref, vb_ref, lwb_ref, ab_ref,
                kk_ref, ka_ref, rk_ref, yf_ref, bonf_ref, yb_ref, bonb_ref, s_scr, *, gps):
    c = pl.program_id(2)
    ln = WKV_CHUNK
    pw = WKV_PACK * RW_HEAD

    hpt = LANES // RW_HEAD

    @pl.when(c == 0)
    def _():
        s_scr[...] = jnp.zeros_like(s_scr)

    row = lax.broadcasted_iota(jnp.int32, (ln, ln), 0)
    col = lax.broadcasted_iota(jnp.int32, (ln, ln), 1)
    trow = lax.broadcasted_iota(jnp.int32, (ln, pw), 0)
    tsrc = lax.broadcasted_iota(jnp.int32, (ln, pw), 1) & (ln - 1)
    ones64 = _ones_blockdiag64()

    def fmask(cond):
        return jnp.where(cond, 1.0, 0.0)

    def same_block(n):
        sh = n.bit_length() - 1
        return (tsrc >> sh) == (trow >> sh)

    eye = fmask(tsrc == trow)
    base_f = fmask(same_block(WKV_INV_BASE))
    off_f = {}
    n = WKV_INV_BASE
    while n < ln:
        off_f[n] = fmask(jnp.logical_and(same_block(2 * n), jnp.logical_not(same_block(n))))
        n *= 2
    k_k = kk_ref[...]
    k_a = ka_ref[...]
    r_k = rk_ref[...]

    lane = lax.broadcasted_iota(jnp.int32, (ln, LANES), 1)
    half_f = [fmask((lane >> (RW_HEAD.bit_length() - 1)) == hf) for hf in range(hpt)]
    half_b = [hm.astype(BF16) for hm in half_f]
    zeros_b = jnp.zeros((ln, LANES), BF16)

    def bd(z):
        zb = z.astype(BF16)
        blocks = []
        for jh in range(WKV_PACK):
            lt, hf = divmod(jh, hpt)
            piece = zb[:, lt * LANES:(lt + 1) * LANES] * half_b[hf]
            blocks.append(jnp.concatenate([piece if tt == lt else zeros_b for tt in range(pw // LANES)], axis=1))
        return jnp.concatenate(blocks, axis=0)

    streams = ((rf_ref, kf_ref, vf_ref, lwf_ref, af_ref, yf_ref, bonf_ref),
               (rb_ref, kb_ref, vb_ref, lwb_ref, ab_ref, yb_ref, bonb_ref))
    units = []
    for dd, (r_ref, k_ref, v_ref, lw_ref, a_ref, y_ref, bon_ref) in enumerate(streams):
        rev = dd == 1
        r = r_ref[0].astype(F32)
        k = k_ref[0].astype(F32)
        v = v_ref[0].astype(F32)
        a = a_ref[0].astype(F32)
        lw = lw_ref[0]
        kkr = k * k_k
        kd = k * (1.0 + (a - 1.0) * k_a)
        ssq, rkd = _segsum64_mxu([kkr * kkr, r * kd * r_k], ones64)
        kk = kkr * lax.rsqrt(jnp.maximum(ssq, 1e-24))
        bvec = kk * a
        bon_ref[...] = (rkd * v).astype(BF16)

        tri = jnp.where((col >= row) if rev else (col <= row), 1.0, 0.0).astype(BF16)
        hi = lw.astype(BF16)
        rem = lw - hi.astype(F32)
        mid = rem.astype(BF16)
        lo = (rem - mid.astype(F32)).astype(BF16)
        cum = _dot(tri, hi) + _dot(tri, mid) + _dot(tri, lo)
        tot = cum[0:1, :] if rev else cum[ln - 1:ln, :]
        w_inv = jnp.exp(-cum)
        w_end = jnp.exp(tot - cum)
        a_t = -kk * jnp.exp(cum - lw)
        r_t = r * jnp.exp(cum)
        b_t = bvec * w_inv
        k_t = kd * w_inv
        b_e = bvec * w_end
        k_e = kd * w_end
        w_tot = jnp.exp(tot)

        strict = fmask(tsrc > trow) if rev else fmask(tsrc < trow)
        incl = strict + eye

        for gi in range(gps):
            sl = slice(gi * pw, (gi + 1) * pw)
            units.append(dict(
                dd=dd, gi=gi, sl=sl, y_ref=y_ref, strict=strict, incl=incl,
                ar=jnp.concatenate([a_t[:, sl], r_t[:, sl]], axis=0).astype(BF16),
                b_t=b_t[:, sl], k_t=k_t[:, sl], v=v[:, sl], w_tot=w_tot[:, sl],
                bk=jnp.concatenate([b_e[:, sl], k_e[:, sl]], axis=0).astype(BF16)))

    for un in units:
        sb = _dot_nt(un["ar"], bd(un["b_t"]))
        sk = _dot_nt(un["ar"], bd(un["k_t"]))
        un["m_ab"] = sb[:ln] * un["strict"]
        un["p_rb"] = sb[ln:] * un["incl"]
        un["m_ak"] = sk[:ln] * un["strict"]
        un["p_rk"] = sk[ln:] * un["incl"]
    for un in units:
        un["s0"] = s_scr[un["dd"], un["gi"]]
        un["ars"] = _dot_nt(un["ar"], un["s0"].astype(BF16))
        un["mv"] = _dot(jnp.concatenate([un["m_ak"], un["p_rk"]], axis=0).astype(BF16), bd(un["v"]))
    for un in units:
        m0 = un["m_ab"] * base_f
        un["pinv"] = eye + m0
        un["mp"] = _dot(m0.astype(BF16), bd(m0))
    for un in units:
        both = _dot(jnp.concatenate([un["mp"], un["pinv"]], axis=0).astype(BF16), bd(un["mp"]))
        un["pinv"] = un["pinv"] + both[ln:]
        un["mp"] = both[:ln]
    for un in units:
        un["pinv"] = un["pinv"] + _dot(un["pinv"].astype(BF16), bd(un["mp"]))
    n = WKV_INV_BASE
    while n < ln:
        for un in units:
            un["t1"] = _dot((un["m_ab"] * off_f[n]).astype(BF16), bd(un["pinv"]))
        for un in units:
            un["pinv"] = un["pinv"] + _dot(un["pinv"].astype(BF16), bd(un["t1"]))
        n *= 2
    for un in units:
        un["u"] = _dot(un["pinv"].astype(BF16), bd(un["ars"][:ln] + un["mv"][:ln]))
    for un in units:
        un["y_ref"][:, un["sl"]] = un["ars"][ln:] + _dot(un["p_rb"].astype(BF16), bd(un["u"])) + un["mv"][ln:]
        uv = jnp.concatenate([un["u"], un["v"]], axis=0).astype(BF16)
        res = _dot_tn(uv, un["bk"])
        for jh in range(WKV_PACK):
            lt, hf = divmod(jh, hpt)
            rsl = slice(jh * RW_HEAD, (jh + 1) * RW_HEAD)
            csl = slice(lt * LANES, (lt + 1) * LANES)
            s_scr[un["dd"], un["gi"], rsl, csl] = (un["s0"][rsl, csl] * un["w_tot"][:, csl]
                                                    + res[rsl, csl] * half_f[hf])


def _wkv(rkv, lw, a, k_k, k_a, r_k, *, dm, gps):
    _, rows, d = rkv.shape
    nb, seq, ctx = dm["nb"], dm["seq"], dm["ctx"]
    ln = WKV_CHUNK
    sw = gps * WKV_PACK * RW_HEAD
    ncc, nlc = ctx // ln, seq // ln
    ctx_c0 = nb * seq // ln

    def fblk(b, c):
        return jnp.where(c < ncc, ctx_c0 + b * ncc + c, b * nlc + (c - ncc))

    def bblk(b, c):
        return jnp.where(c < ncc, ctx_c0 + b * ncc + (ncc - 1 - c), b * nlc + (nlc - 1 - (c - ncc)))

    def spec3(p, blk):
        return pl.BlockSpec((1, ln, sw), lambda b, s, c, p=p, blk=blk: (p, blk(b, c), s))

    def spec2(blk):
        return pl.BlockSpec((ln, sw), lambda b, s, c, blk=blk: (blk(b, c), s))

    pspec = pl.BlockSpec((1, sw), lambda b, s, c: (0, s))
    in_specs = [spec3(0, fblk), spec3(1, fblk), spec3(2, fblk), spec3(0, fblk), spec3(0, fblk),
                spec3(0, bblk), spec3(1, bblk), spec3(2, bblk), spec3(1, bblk), spec3(1, bblk),
                pspec, pspec, pspec]
    return pl.pallas_call(
        functools.partial(_wkv_kernel, gps=gps),
        out_shape=[jax.ShapeDtypeStruct((rows, d), F32), jax.ShapeDtypeStruct((rows, d), BF16),
                   jax.ShapeDtypeStruct((rows, d), F32), jax.ShapeDtypeStruct((rows, d), BF16)],
        grid=(nb, d // sw, ncc + nlc),
        in_specs=in_specs,
        out_specs=[spec2(fblk), spec2(fblk), spec2(bblk), spec2(bblk)],
        scratch_shapes=[pltpu.VMEM((2, gps, WKV_PACK * RW_HEAD, WKV_PACK * RW_HEAD), F32)],
        compiler_params=_cparams(("parallel", "parallel", "arbitrary")),
        name="wkv_scan",
    )(rkv, rkv, rkv, lw, a, rkv, rkv, rkv, lw, a, k_k.reshape(1, d), k_a.reshape(1, d), r_k.reshape(1, d))


def _rw_out_kernel(yf_ref, yb_ref, bf_ref, bb_ref, g_ref, x_ref, mod_ref, lg_ref, lb_ref, w_ref, o_ref):
    y = yf_ref[...] + yb_ref[...]
    ones64 = _ones_blockdiag64()
    mean = _segsum64_mxu([y], ones64)[0] * (1.0 / RW_HEAD)
    yc = y - mean
    var = _segsum64_mxu([yc * yc], ones64)[0] * (1.0 / RW_HEAD)
    yn = yc * lax.rsqrt(var + RW_GN_EPS)
    bonus = bf_ref[...].astype(F32) + bb_ref[...].astype(F32)
    o = (yn * lg_ref[...] + lb_ref[...] + bonus) * g_ref[...].astype(F32)
    o_ref[...] = x_ref[...] + mod_ref[0, 2:3, :] * _dot(o.astype(BF16), w_ref[...])


def _rw_out(yf, bonf, yb, bonb, g, xs, mod, ln_g, ln_b, w_o, *, dm, n_rows):
    d = xs.shape[1]
    te = dm["te"]
    modmap = lambda i: (jnp.minimum((i * te) // dm["seq"], dm["nb"]), 0, 0)
    rspec = pl.BlockSpec((te, d), lambda i: (i, 0))
    vspec = pl.BlockSpec((1, d), lambda i: (0, 0))
    return pl.pallas_call(
        _rw_out_kernel,
        out_shape=jax.ShapeDtypeStruct((n_rows, d), F32),
        grid=(n_rows // te,),
        in_specs=[rspec, rspec, rspec, rspec, rspec, rspec, pl.BlockSpec((1, 6, d), modmap), vspec, vspec,
                  pl.BlockSpec((d, d), lambda i: (0, 0))],
        out_specs=rspec,
        compiler_params=_cparams(("parallel",)),
        name="rwkv_out",
    )(yf, yb, bonf, bonb, g, xs, mod, ln_g.reshape(1, d), ln_b.reshape(1, d), w_o)


def _rope_tables(dm):
    seq, nb, ctx = dm["seq"], dm["nb"], dm["ctx"]
    t = jnp.arange(seq, dtype=jnp.int32)
    pos = jnp.stack([t // GRID_W, t % GRID_W], axis=-1).astype(F32)
    n_freq = ATT_HEAD // 4
    inv = ROPE_THETA ** (-jnp.arange(n_freq, dtype=F32) / n_freq)
    ang = pos[:, :, None] * inv
    cos, sin = jnp.cos(ang), jnp.sin(ang)
    zero = jnp.zeros_like(sin)
    cos_t = jnp.stack([cos, cos], axis=2).reshape(seq, ATT_HEAD)
    sa_t = jnp.stack([-sin, zero], axis=2).reshape(seq, ATT_HEAD)
    sb_t = jnp.stack([zero, sin], axis=2).reshape(seq, ATT_HEAD)
    nctx = nb * ctx
    full = lambda tab, fill: jnp.concatenate([jnp.tile(tab, (nb, 1)), jnp.full((nctx, ATT_HEAD), fill, F32)], axis=0)
    return full(cos_t, 1.0), full(sa_t, 0.0), full(sb_t, 0.0)


def kernel(x, c, ctx, c_ctx, mod_w, mod_b, norm1_g, norm2_g, ffn_up, ffn_conv_w, ffn_conv_b, ffn_down, rw_mu, rw_w_rkv, rw_w0, rw_w1, rw_w2, rw_a0, rw_a1, rw_a2, rw_g1, rw_g2, rw_k_k, rw_k_a, rw_r_k, rw_ln_g, rw_ln_b, rw_w_o, rw_v0, rw_v1, rw_v2, na_w_qkv, na_q_g, na_k_g, na_rpb, na_w_o, ga_w_qkv, ga_q_g, ga_k_g, ga_w_o):
    nb, seq, d = x.shape
    nctx = ctx.shape[1]
    depth = mod_w.shape[0]
    tm = nb * nctx
    assert seq % tm == 0 and seq & (seq - 1) == 0 and nctx & (nctx - 1) == 0 and nb + 1 <= SUBLANES
    assert seq // GRID_W >= NA_WIN_R and nctx % GRID_W == 0
    dm = dict(nb=nb, seq=seq, ctx=nctx, tm=tm, te=tm // 2, tpb=seq // tm, n_lat_rows=nb * seq)
    n_lat_tiles = nb * seq // tm
    n_tiles = n_lat_tiles + 1
    att_scale = ATT_HEAD ** -0.5

    xs = jnp.concatenate([x.reshape(nb * seq, d), ctx.reshape(nb * nctx, d)], axis=0)
    c_all = jnp.concatenate([c, c_ctx[None], jnp.zeros((SUBLANES - nb - 1, d), F32)], axis=0)
    mods = _modulations(c_all, mod_w, mod_b)
    rope_tabs = None
    v_first = None

    for i in range(depth):
        kind, j = i % 3, i // 3
        need_ctx = i < depth - 1
        nt_out = n_tiles if need_ctx else n_lat_tiles
        mod = mods[i, :nb + 1].reshape(nb + 1, 6, d)
        if kind == 0:
            xm = _rw_prep(xs, mod, norm1_g[i], rw_mu[j], dm=dm)
            vres = None if j == 0 else (rw_v0[j - 1], rw_v1[j - 1], rw_v2[j - 1])
            lora = _rw_lora(xm, rw_w0[j], rw_w1[j], rw_w2[j], rw_a0[j], rw_a1[j], rw_a2[j], rw_g1[j], rw_g2[j],
                            vres, dm=dm)
            lw, a, g = lora[0], lora[1], lora[2]
            rkv = _rkv_proj(xm, rw_w_rkv[j].astype(BF16), v_first if vres is not None else None,
                            lora[3] if vres is not None else None, dm=dm)
            if v_first is None:
                v_first = rkv
            yf, bonf, yb, bonb = _wkv(rkv, lw, a, rw_k_k[j], rw_k_a[j], rw_r_k[j].reshape(-1), dm=dm,
                                      gps=min(8, d // (WKV_PACK * RW_HEAD)))
            xs = _rw_out(yf, bonf, yb, bonb, g, xs, mod, rw_ln_g[j], rw_ln_b[j], rw_w_o[j].astype(BF16),
                         dm=dm, n_rows=nt_out * tm)
        elif kind == 1:
            qkv = _qkv_proj(xs, mod, norm1_g[i], na_w_qkv[j].astype(BF16), na_q_g[j] * att_scale, na_k_g[j],
                            None, dm=dm, kv_dim=d)
            o = _na_attention(qkv, na_rpb[j], dm=dm, d=d)
            xs = _out_proj(o, na_w_o[j].astype(BF16), xs, mod, dm=dm, n_tiles=nt_out)
        else:
            if rope_tabs is None:
                rope_tabs = _rope_tables(dm)
            kv_dim = (ga_w_qkv.shape[-1] - d) // 2
            qkv = _qkv_proj(xs, mod, norm1_g[i], ga_w_qkv[j].astype(BF16), ga_q_g[j] * att_scale, ga_k_g[j],
                            rope_tabs, dm=dm, kv_dim=kv_dim)
            o = _gqa_attention(qkv, dm=dm, d=d, kv_heads=kv_dim // ATT_HEAD)
            xs = _out_proj(o, ga_w_o[j].astype(BF16), xs, mod, dm=dm, n_tiles=nt_out)
        xs = _ffn(xs, mod, norm2_g[i], ffn_up[i].astype(BF16), ffn_conv_w[i], ffn_conv_b[i],
                  ffn_down[i].astype(BF16), dm=dm, need_ctx=need_ctx)
    return xs[:nb * seq].reshape(nb, seq, d)
```

```python
import functools
import math

import jax
import jax.numpy as jnp
from jax import lax
from jax.experimental import pallas as pl
from jax.experimental.pallas import tpu as pltpu

F32 = jnp.float32
BF16 = jnp.bfloat16

NORM_EPS = 1e-6
GRID_W = 64
ATT_HEAD = 128
RW_HEAD = 64
NA_WIN_R = 8
NA_WIN_C = 16
ROPE_THETA = 10000.0
RW_GN_EPS = 64e-5
LANES = 128
SUBLANES = 8
WKV_CHUNK = 64
WKV_PACK = 4
WKV_INV_BASE = 8
FFN_ROW_PIECES = 4
VMEM_LIMIT = 56 * 1024 * 1024
NEG_BIG = -1e30


def _cparams(sem):
    return pltpu.CompilerParams(dimension_semantics=sem, vmem_limit_bytes=VMEM_LIMIT)


def _dot(a, b):
    return jnp.dot(a, b, preferred_element_type=F32)


def _dot_nt(a, b):
    return lax.dot_general(a, b, (((1,), (1,)), ((), ())), preferred_element_type=F32)


def _dot_tn(a, b):
    return lax.dot_general(a, b, (((0,), (0,)), ((), ())), preferred_element_type=F32)


def _normmod(x, g, shift, scale):
    ms = jnp.mean(x * x, axis=-1, keepdims=True)
    y = x * lax.rsqrt(ms + NORM_EPS)
    return (y * g) * (1.0 + scale) + shift


def _silu(x):
    return x * jax.nn.sigmoid(x)


def _seq_edges(tile, rows, n_lat_rows, seq, ctx):
    rid = lax.broadcasted_iota(jnp.int32, (rows, 1), 0)
    base = tile * rows
    period = jnp.where(base >= n_lat_rows, ctx, seq)
    pos = (base + rid) & (period - 1)
    return rid, pos == 0, pos == period - 1


def _shift_rows(u, rid, first, last, prev_row, next_row):
    n = u.shape[0]
    up = pltpu.roll(u, 1, axis=0)
    up = jnp.where(rid == 0, prev_row, up)
    up = jnp.where(first, 0.0, up)
    un = pltpu.roll(u, n - 1, axis=0)
    un = jnp.where(rid == n - 1, next_row, un)
    un = jnp.where(last, 0.0, un)
    return up, un


def _ones_blockdiag64():
    sh = RW_HEAD.bit_length() - 1
    r = lax.broadcasted_iota(jnp.int32, (LANES, LANES), 0) >> sh
    c = lax.broadcasted_iota(jnp.int32, (LANES, LANES), 1) >> sh
    return jnp.where(r == c, 1.0, 0.0).astype(BF16)


def _segsum64_mxu(xs, ones):
    m, n = xs[0].shape
    pieces = []
    for x in xs:
        hi = x.astype(BF16)
        lo = (x - hi.astype(F32)).astype(BF16)
        for part in (hi, lo):
            pieces += [part[:, c * LANES:(c + 1) * LANES] for c in range(n // LANES)]
    res = _dot(jnp.concatenate(pieces, axis=0), ones)
    nslab = n // LANES
    outs = []
    for i in range(len(xs)):
        base = i * 2 * nslab
        cols = [res[(base + c) * m:(base + c + 1) * m] + res[(base + nslab + c) * m:(base + nslab + c + 1) * m]
                for c in range(nslab)]
        outs.append(jnp.concatenate(cols, axis=1))
    return outs


def _mod_kernel(c_ref, w_ref, b_ref, o_ref):
    s = _silu(c_ref[...]).astype(BF16)
    o_ref[0] = _dot(s, w_ref[0].astype(BF16)) + b_ref[0]


def _modulations(c_all, mod_w, mod_b):
    depth, d, n = mod_w.shape
    tn = n // 8
    return pl.pallas_call(
        _mod_kernel,
        out_shape=jax.ShapeDtypeStruct((depth, SUBLANES, n), F32),
        grid=(depth, n // tn),
        in_specs=[
            pl.BlockSpec((SUBLANES, d), lambda l, j: (0, 0)),
            pl.BlockSpec((1, d, tn), lambda l, j: (l, 0, j)),
            pl.BlockSpec((1, 1, tn), lambda l, j: (l, 0, j)),
        ],
        out_specs=pl.BlockSpec((1, SUBLANES, tn), lambda l, j: (l, 0, j)),
        compiler_params=_cparams(("parallel", "parallel")),
        name="modulation",
    )(c_all, mod_w, mod_b.reshape(depth, 1, n))


def _ffn_kernel(*refs, tm, tile_off, aliased, n_lat_rows, seq, ctx):
    (x_ref, xp_ref, xn_ref, mod_ref, g_ref, wug_ref, wuv_ref, cwg_ref, cwv_ref, cbg_ref, cbv_ref,
     wd_ref) = refs[:12]
    o_ref, h_scr = refs[-2:]
    i = pl.program_id(0) + tile_off
    j = pl.program_id(1)
    shift = mod_ref[0, 3:4, :]
    scale = mod_ref[0, 4:5, :]
    hr = 2 * SUBLANES
    th = tm // FFN_ROW_PIECES

    @pl.when(j == 0)
    def _():
        g = g_ref[...]
        halo = jnp.concatenate([xp_ref[...], xn_ref[...]], axis=0)
        h_scr[0:hr, :] = _normmod(halo, g, shift, scale).astype(BF16)
        h_scr[hr:hr + tm, :] = _normmod(x_ref[...], g, shift, scale).astype(BF16)
        o_ref[...] = jnp.zeros_like(o_ref)

    rid, first, last = _seq_edges(i, tm, n_lat_rows, seq, ctx)
    rid_h = rid[0:th]

    def up_proj(p):
        lo = 0 if p == 0 else hr + p * th
        hp = h_scr[lo:hr + (p + 1) * th, :]
        ug, uv = _dot(hp, wug_ref[...]), _dot(hp, wuv_ref[...])
        if p == 0:
            return dict(g=ug[hr:], v=uv[hr:], halo_g=ug[0:hr], halo_v=uv[0:hr])
        return dict(g=ug, v=uv)

    def conv(main, prev_row, next_row, fm, lm, cw_ref, cb_ref):
        up, un = _shift_rows(main, rid_h, fm, lm, prev_row, next_row)
        return cb_ref[...] + up * cw_ref[0:1, :] + main * cw_ref[1:2, :] + un * cw_ref[2:3, :]

    pv, nx = SUBLANES - 1, SUBLANES

    def act_down(p, us):
        rs = slice(p * th, (p + 1) * th)
        rows = {}
        for key in ("g", "v"):
            prev_row = us[0]["halo_" + key][pv:pv + 1] if p == 0 else us[p - 1][key][th - 1:th]
            next_row = us[0]["halo_" + key][nx:nx + 1] if p == FFN_ROW_PIECES - 1 else us[p + 1][key][0:1]
            rows[key] = (prev_row, next_row)
        a = (_silu(conv(us[p]["g"], *rows["g"], first[rs], last[rs], cwg_ref, cbg_ref))
             * conv(us[p]["v"], *rows["v"], first[rs], last[rs], cwv_ref, cbv_ref))
        o_ref[rs, :] += _dot(a.astype(BF16), wd_ref[...])

    us = {0: up_proj(0)}
    for p in range(FFN_ROW_PIECES):
        if p + 1 < FFN_ROW_PIECES:
            us[p + 1] = up_proj(p + 1)
        act_down(p, us)

    @pl.when(j == pl.num_programs(1) - 1)
    def _():
        o_ref[...] = x_ref[...] + mod_ref[0, 5:6, :] * o_ref[...]


FFN_COLS = 512


def _ffn_weights(ffn_up, ffn_down):
    depth, d, f2 = ffn_up.shape
    wu = ffn_up.astype(BF16).reshape(depth, d, f2 // FFN_COLS, FFN_COLS).transpose(0, 2, 1, 3)
    return wu, ffn_down.astype(BF16)


def _ffn_call(xs, mod, g2, wu, cw, cb, wd, prev, *, layer, dm, tm, tile_off, n_tiles, out_rows):
    rows, d = xs.shape
    f = wd.shape[1]
    fc = FFN_COLS
    nfc = f // fc
    hb = tm // SUBLANES
    last_hb = rows // SUBLANES - 1
    seq, nb = dm["seq"], dm["nb"]
    assert seq % tm == 0 or tile_off * tm >= dm["n_lat_rows"]
    modmap = lambda i, j: (jnp.minimum(((i + tile_off) * tm) // seq, nb), 0, 0)
    kern = functools.partial(_ffn_kernel, tm=tm, tile_off=tile_off, aliased=prev is not None,
                             n_lat_rows=dm["n_lat_rows"], seq=seq, ctx=dm["ctx"])
    in_specs = [
        pl.BlockSpec((tm, d), lambda i, j: (i + tile_off, 0), pipeline_mode=pl.Buffered(1)),
        pl.BlockSpec((SUBLANES, d), lambda i, j: (jnp.maximum((i + tile_off) * hb - 1, 0), 0)),
        pl.BlockSpec((SUBLANES, d), lambda i, j: (jnp.minimum((i + tile_off + 1) * hb, last_hb), 0)),
        pl.BlockSpec((1, 6, d), modmap),
        pl.BlockSpec((1, d), lambda i, j: (0, 0)),
        pl.BlockSpec((None, None, d, fc), lambda i, j: (layer, j, 0, 0)),
        pl.BlockSpec((None, None, d, fc), lambda i, j: (layer, nfc + j, 0, 0)),
        pl.BlockSpec((3, fc), lambda i, j: (0, j)),
        pl.BlockSpec((3, fc), lambda i, j: (0, nfc + j)),
        pl.BlockSpec((1, fc), lambda i, j: (0, j)),
        pl.BlockSpec((1, fc), lambda i, j: (0, nfc + j)),
        pl.BlockSpec((None, fc, d), lambda i, j: (layer, j, 0)),
    ]
    args = [xs, xs, xs, mod, g2.reshape(1, d), wu, wu, cw, cw, cb.reshape(1, -1), cb.reshape(1, -1), wd]
    aliases = {}
    if prev is not None:
        in_specs.append(pl.BlockSpec(memory_space=pl.ANY))
        args.append(prev)
        aliases = {len(args) - 1: 0}
    return pl.pallas_call(
        kern,
        out_shape=jax.ShapeDtypeStruct((out_rows, d), F32),
        grid=(n_tiles, nfc),
        in_specs=in_specs,
        out_specs=pl.BlockSpec((tm, d), lambda i, j: (i + tile_off, 0)),
        scratch_shapes=[pltpu.VMEM((tm + 2 * SUBLANES, d), BF16)],
        input_output_aliases=aliases,
        compiler_params=_cparams(("parallel", "arbitrary")),
        name="conv_ffn",
    )(*args)


def _ffn(xs, mod, g2, wu, cw, cb, wd, *, layer, dm, need_ctx):
    rows = xs.shape[0]
    n_lat_rows, tm = dm["n_lat_rows"], dm["tm"]
    big = 2 * tm
    out_rows = rows if need_ctx else n_lat_rows
    y = _ffn_call(xs, mod, g2, wu, cw, cb, wd, None, layer=layer, dm=dm, tm=big, tile_off=0,
                  n_tiles=n_lat_rows // big, out_rows=out_rows)
    if need_ctx:
        y = _ffn_call(xs, mod, g2, wu, cw, cb, wd, y, layer=layer, dm=dm, tm=tm, tile_off=n_lat_rows // tm,
                      n_tiles=1, out_rows=out_rows)
    return y


def _qkv_kernel(*refs, block_kinds, rope):
    if rope:
        x_ref, mod_ref, g_ref, w_ref, qg_ref, kg_ref, cos_ref, sa_ref, sb_ref, o_ref, h_scr = refs
    else:
        x_ref, mod_ref, g_ref, w_ref, qg_ref, kg_ref, o_ref, h_scr = refs
    j = pl.program_id(1)

    @pl.when(j == 0)
    def _():
        h_scr[...] = _normmod(x_ref[...], g_ref[...], mod_ref[0, 0:1, :], mod_ref[0, 1:2, :]).astype(BF16)

    acc = _dot(h_scr[...], w_ref[...])

    def emit(kinds):
        for hh, kind in enumerate(kinds):
            hs = slice(hh * ATT_HEAD, (hh + 1) * ATT_HEAD)
            y = acc[:, hs]
            if kind != "v":
                gain = qg_ref[...] if kind == "q" else kg_ref[...]
                y = y * lax.rsqrt(jnp.mean(y * y, axis=-1, keepdims=True) + NORM_EPS) * gain
                if rope:
                    y = (y * cos_ref[...] + pltpu.roll(y, ATT_HEAD - 32, axis=1) * sa_ref[...]
                         + pltpu.roll(y, 32, axis=1) * sb_ref[...])
            o_ref[:, hs] = y.astype(BF16)

    for jb, kinds in enumerate(block_kinds):
        pl.when(j == jb)(functools.partial(emit, kinds))


def _qkv_proj(xs, mod, g1, w, qg, kg, rope_tabs, *, dm, kv_dim):
    rows, d = xs.shape
    n = w.shape[1]
    tn = min(d, 2 * kv_dim)
    hpb = tn // ATT_HEAD
    kinds = ["q"] * (d // ATT_HEAD) + ["k"] * (kv_dim // ATT_HEAD) + ["v"] * (kv_dim // ATT_HEAD)
    block_kinds = tuple(tuple(kinds[b * hpb:(b + 1) * hpb]) for b in range(n // tn))
    tm = dm["tm"]
    modmap = lambda i, j: (jnp.minimum(i // dm["tpb"], dm["nb"]), 0, 0)
    in_specs = [
        pl.BlockSpec((tm, d), lambda i, j: (i, 0)),
        pl.BlockSpec((1, 6, d), modmap),
        pl.BlockSpec((1, d), lambda i, j: (0, 0)),
        pl.BlockSpec((d, tn), lambda i, j: (0, j)),
        pl.BlockSpec((1, ATT_HEAD), lambda i, j: (0, 0)),
        pl.BlockSpec((1, ATT_HEAD), lambda i, j: (0, 0)),
    ]
    args = [xs, mod, g1.reshape(1, d), w, qg.reshape(1, ATT_HEAD), kg.reshape(1, ATT_HEAD)]
    if rope_tabs is not None:
        in_specs += [pl.BlockSpec((tm, ATT_HEAD), lambda i, j: (i, 0))] * 3
        args += list(rope_tabs)
    kern = functools.partial(_qkv_kernel, block_kinds=block_kinds, rope=rope_tabs is not None)
    return pl.pallas_call(
        kern,
        out_shape=jax.ShapeDtypeStruct((rows, n), BF16),
        grid=(rows // tm, n // tn),
        in_specs=in_specs,
        out_specs=pl.BlockSpec((tm, tn), lambda i, j: (i, j)),
        scratch_shapes=[pltpu.VMEM((tm, d), BF16)],
        compiler_params=_cparams(("parallel", "arbitrary")),
        name="qkv_proj",
    )(*args)


def _oproj_kernel(a_ref, w_ref, x_ref, mod_ref, o_ref):
    o_ref[...] = x_ref[...] + mod_ref[0, 2:3, :] * _dot(a_ref[...], w_ref[...])


def _out_proj(a, w, xs, mod, *, dm, n_tiles):
    d = xs.shape[1]
    tm = dm["tm"]
    modmap = lambda i: (jnp.minimum(i // dm["tpb"], dm["nb"]), 0, 0)
    return pl.pallas_call(
        _oproj_kernel,
        out_shape=jax.ShapeDtypeStruct((n_tiles * tm, d), F32),
        grid=(n_tiles,),
        in_specs=[
            pl.BlockSpec((tm, d), lambda i: (i, 0)),
            pl.BlockSpec((d, d), lambda i: (0, 0)),
            pl.BlockSpec((tm, d), lambda i: (i, 0)),
            pl.BlockSpec((1, 6, d), modmap),
        ],
        out_specs=pl.BlockSpec((tm, d), lambda i: (i, 0)),
        compiler_params=_cparams(("parallel",)),
        name="out_proj",
    )(a, w, xs, mod)


def _softmax_pv(q, segs):
    ss = [_dot_nt(q, k) for k, _ in segs]
    m = ss[0].max(axis=-1, keepdims=True)
    for s in ss[1:]:
        m = jnp.maximum(m, s.max(axis=-1, keepdims=True))
    ps = [jnp.exp(s - m) for s in ss]
    l = ps[0].sum(axis=-1, keepdims=True)
    for p in ps[1:]:
        l = l + p.sum(axis=-1, keepdims=True)
    o = _dot(ps[0].astype(BF16), segs[0][1])
    for p, (_, v) in zip(ps[1:], segs[1:]):
        o = o + _dot(p.astype(BF16), v)
    return o / l


def _lane_tiles(x, op):
    acc = x[:, 0:LANES]
    for j in range(1, x.shape[1] // LANES):
        acc = op(acc, x[:, j * LANES:(j + 1) * LANES])
    return acc


def _gqa_kernel(q_ref, kl_ref, vl_ref, kc_ref, vc_ref, o_ref, s_scr, *, group, n_lat_tiles, kchunk):
    t = pl.program_id(2)
    seq, ctx = kl_ref.shape[0], kc_ref.shape[0]

    def run(chunks):
        m = [None] * group
        mrun = [None] * group
        lrun = [None] * group
        o = [None] * group
        for g in range(group + 1):
            for k_ref, v_ref, st, sz, off in chunks:
                if g < group:
                    s = _dot_nt(q_ref[:, g * ATT_HEAD:(g + 1) * ATT_HEAD], k_ref[st:st + sz, :])
                    s_scr[g % 2, :, off:off + sz] = s
                    tmax = _lane_tiles(s, jnp.maximum)
                    mrun[g] = tmax if mrun[g] is None else jnp.maximum(mrun[g], tmax)
                if g >= 1:
                    h = g - 1
                    p = jnp.exp(s_scr[h % 2, :, off:off + sz] - m[h])
                    psum = _lane_tiles(p, jnp.add)
                    pv = _dot(p.astype(BF16), v_ref[st:st + sz, :])
                    lrun[h] = psum if lrun[h] is None else lrun[h] + psum
                    o[h] = pv if o[h] is None else o[h] + pv
            if g < group:
                m[g] = mrun[g].max(axis=-1, keepdims=True)
        for g in range(group):
            l = lrun[g].sum(axis=-1, keepdims=True)
            o_ref[:, g * ATT_HEAD:(g + 1) * ATT_HEAD] = (o[g] / l).astype(BF16)

    lat_chunks = [(kl_ref, vl_ref, st, kchunk, st) for st in range(0, seq, kchunk)]
    ctx_chunk = (kc_ref, vc_ref, 0, ctx, seq)

    @pl.when(t < n_lat_tiles)
    def _():
        run(lat_chunks + [ctx_chunk])

    @pl.when(t >= n_lat_tiles)
    def _():
        run([ctx_chunk])


def _gqa_attention(qkv, *, dm, d, kv_heads):
    rows = qkv.shape[0]
    nb, seq, ctx = dm["nb"], dm["seq"], dm["ctx"]
    group = d // ATT_HEAD // kv_heads
    gw = group * ATT_HEAD
    tq = 128
    nlt, nct = seq // tq, ctx // tq
    kcol = d // ATT_HEAD
    vcol = kcol + kv_heads
    ctx_blk0 = nb * seq // ctx

    def qmap(b, h, t):
        return (jnp.where(t < nlt, b * nlt + t, nb * nlt + b * nct + (t - nlt)), h)

    kern = functools.partial(_gqa_kernel, group=group, n_lat_tiles=nlt, kchunk=min(512, seq))
    return pl.pallas_call(
        kern,
        out_shape=jax.ShapeDtypeStruct((rows, d), BF16),
        scratch_shapes=[pltpu.VMEM((2, tq, seq + ctx), F32)],
        grid=(nb, kv_heads, nlt + nct),
        in_specs=[
            pl.BlockSpec((tq, gw), qmap),
            pl.BlockSpec((seq, ATT_HEAD), lambda b, h, t: (b, kcol + h)),
            pl.BlockSpec((seq, ATT_HEAD), lambda b, h, t: (b, vcol + h)),
            pl.BlockSpec((ctx, ATT_HEAD), lambda b, h, t: (ctx_blk0 + b, kcol + h)),
            pl.BlockSpec((ctx, ATT_HEAD), lambda b, h, t: (ctx_blk0 + b, vcol + h)),
        ],
        out_specs=pl.BlockSpec((tq, gw), qmap),
        compiler_params=_cparams(("parallel", "parallel", "arbitrary")),
        name="gqa_attention",
    )(qkv, qkv, qkv, qkv, qkv)


def _na_kernel(q_ref, k_ref, v_ref, kc_ref, vc_ref, bias_ref, o_ref, *, rb, hps, grid_rows, n_row_blocks):
    t = pl.program_id(2)
    win = NA_WIN_R * GRID_W

    @pl.when(t < n_row_blocks)
    def _():
        units = []
        for hh in range(hps):
            hs = slice(hh * ATT_HEAD, (hh + 1) * ATT_HEAD)
            for rr in range(rb):
                r = t * rb + rr
                rs = jnp.clip(r - NA_WIN_R // 2, 0, grid_rows - NA_WIN_R)
                units.append(dict(hh=hh, hs=hs, rows=slice(rr * GRID_W, (rr + 1) * GRID_W), off=r - rs,
                                  start=pl.multiple_of(rs * GRID_W, GRID_W)))
        for un in units:
            q = q_ref[un["rows"], un["hs"]]
            un["sw"] = _dot_nt(q, k_ref[pl.ds(un["start"], win), un["hs"]]) + bias_ref[un["hh"], un["off"]]
            un["sc"] = _dot_nt(q, kc_ref[:, un["hs"]])
        for un in units:
            m = jnp.maximum(un["sw"].max(axis=-1, keepdims=True), un["sc"].max(axis=-1, keepdims=True))
            un["pw"] = jnp.exp(un["sw"] - m)
            un["pc"] = jnp.exp(un["sc"] - m)
        for un in units:
            l = un["pw"].sum(axis=-1, keepdims=True) + un["pc"].sum(axis=-1, keepdims=True)
            o = (_dot(un["pw"].astype(BF16), v_ref[pl.ds(un["start"], win), un["hs"]])
                 + _dot(un["pc"].astype(BF16), vc_ref[:, un["hs"]]))
            o_ref[un["rows"], un["hs"]] = (o / l).astype(BF16)

    @pl.when(t >= n_row_blocks)
    def _():
        for hh in range(hps):
            hs = slice(hh * ATT_HEAD, (hh + 1) * ATT_HEAD)
            o_ref[:, hs] = _softmax_pv(q_ref[:, hs], [(kc_ref[:, hs], vc_ref[:, hs])]).astype(BF16)


def _na_bias_table(rpb):
    qc = jnp.arange(GRID_W)
    kc = jnp.arange(GRID_W)
    cs = jnp.clip(qc - NA_WIN_C // 2, 0, GRID_W - NA_WIN_C)
    inwin = (kc[None, :] >= cs[:, None]) & (kc[None, :] < cs[:, None] + NA_WIN_C)
    cidx = kc[None, :] - qc[:, None] + NA_WIN_C - 1
    sel = (cidx[None] == jnp.arange(2 * NA_WIN_C - 1)[:, None, None]) & inwin[None]
    cols = jnp.einsum('hrc,cqk->hrqk', rpb, sel.astype(F32), precision=lax.Precision.HIGHEST)
    cols = jnp.where(inwin[None, None], cols, NEG_BIG)
    tab = jnp.stack([cols[:, NA_WIN_R - 1 - o:2 * NA_WIN_R - 1 - o] for o in range(NA_WIN_R)], axis=1)
    tab = tab.transpose(0, 1, 3, 2, 4)
    return tab.reshape(rpb.shape[0], NA_WIN_R, GRID_W, NA_WIN_R * GRID_W).astype(F32)


def _na_attention(qkv, rpb, *, dm, d):
    rows = qkv.shape[0]
    nb, seq, ctx = dm["nb"], dm["seq"], dm["ctx"]
    heads = d // ATT_HEAD
    grid_rows = seq // GRID_W
    rb = ctx // GRID_W
    nrb = grid_rows // rb
    ctx_blk0 = nb * seq // ctx
    bias = _na_bias_table(rpb)

    def qmap(b, h, t):
        return (jnp.where(t < nrb, b * nrb + t, ctx_blk0 + b), h)

    hps = 2
    hw = hps * ATT_HEAD
    ng = heads // hps
    kern = functools.partial(_na_kernel, rb=rb, hps=hps, grid_rows=grid_rows, n_row_blocks=nrb)
    return pl.pallas_call(
        kern,
        out_shape=jax.ShapeDtypeStruct((rows, d), BF16),
        grid=(nb, ng, nrb + 1),
        in_specs=[
            pl.BlockSpec((ctx, hw), qmap),
            pl.BlockSpec((seq, hw), lambda b, h, t: (b, ng + h)),
            pl.BlockSpec((seq, hw), lambda b, h, t: (b, 2 * ng + h)),
            pl.BlockSpec((ctx, hw), lambda b, h, t: (ctx_blk0 + b, ng + h)),
            pl.BlockSpec((ctx, hw), lambda b, h, t: (ctx_blk0 + b, 2 * ng + h)),
            pl.BlockSpec((hps, NA_WIN_R, GRID_W, NA_WIN_R * GRID_W), lambda b, h, t: (h, 0, 0, 0)),
        ],
        out_specs=pl.BlockSpec((ctx, hw), qmap),
        compiler_params=_cparams(("parallel", "parallel", "arbitrary")),
        name="na_attention",
    )(qkv, qkv, qkv, qkv, qkv, bias)


def _rw_prep_kernel(*refs, mix, te, n_lat_rows, seq, ctx):
    (x_ref, xp_ref, xn_ref, mod_ref, g_ref, mu_ref, w1_ref, a1_ref, g1_ref, w2_ref, a2_ref, g2_ref,
     w0_ref, a0_ref) = refs[:14]
    if mix:
        v1_ref, v2_ref, v0_ref, xm_ref, lw_ref, a_ref, go_ref, vg_ref = refs[14:]
    else:
        xm_ref, lw_ref, a_ref, go_ref = refs[14:]
    i = pl.program_id(0)
    g = g_ref[...]
    shift = mod_ref[0, 0:1, :]
    scale = mod_ref[0, 1:2, :]
    h = _normmod(x_ref[...], g, shift, scale)
    halo = _normmod(jnp.concatenate([xp_ref[...], xn_ref[...]], axis=0), g, shift, scale)
    rid, first, last = _seq_edges(i, te, n_lat_rows, seq, ctx)
    hp, hn = _shift_rows(h, rid, first, last, halo[SUBLANES - 1:SUBLANES], halo[SUBLANES:SUBLANES + 1])
    xx = 0.5 * (hp + hn) - h

    def mixed(p):
        return (h + xx * mu_ref[p:p + 1, :]).astype(BF16)

    xv = mixed(2)
    xm_ref[0] = mixed(0)
    xm_ref[1] = mixed(1)
    xm_ref[2] = xv
    zw = jnp.tanh(_dot(mixed(3), w1_ref[...])).astype(BF16)
    za = _dot(mixed(4), a1_ref[...]).astype(BF16)
    zg = jax.nn.sigmoid(_dot(mixed(5), g1_ref[...])).astype(BF16)
    for dd in range(2):
        sl = slice(dd * LANES, (dd + 1) * LANES)
        wl = w0_ref[dd:dd + 1, :] + _dot(zw[:, sl], w2_ref[dd])
        lw_ref[dd] = (-math.exp(-0.5)) * jax.nn.sigmoid(wl)
        a_ref[dd] = jax.nn.sigmoid(a0_ref[dd:dd + 1, :] + _dot(za[:, sl], a2_ref[dd])).astype(BF16)
    go_ref[...] = _dot(zg, g2_ref[...]).astype(BF16)
    if mix:
        zv = _dot(xv, v1_ref[...]).astype(BF16)
        vg_ref[...] = jax.nn.sigmoid(v0_ref[...] + _dot(zv, v2_ref[...])).astype(BF16)


def _pad_rank(w1, w2):
    r = w1.shape[-1]
    pad = (-r) % LANES
    w1 = jnp.pad(w1, [(0, 0)] * (w1.ndim - 1) + [(0, pad)])
    w2 = jnp.pad(w2, [(0, 0)] * (w2.ndim - 2) + [(0, pad), (0, 0)])
    return w1.astype(BF16), w2.astype(BF16)


def _rw_prep(xs, mod, g1n, mu, w0, w1, w2, a0, a1, a2, g1, g2, vres, *, dm):
    rows, d = xs.shape
    te = dm["te"]
    hb = te // SUBLANES
    last_hb = rows // SUBLANES - 1
    mix = vres is not None
    w1p, w2p = _pad_rank(w1, w2)
    a1p, a2p = _pad_rank(a1, a2)
    w1c = jnp.concatenate([w1p[0], w1p[1]], axis=1)
    a1c = jnp.concatenate([a1p[0], a1p[1]], axis=1)
    rg = g1.shape[1]
    modmap = lambda i: (jnp.minimum((i * te) // dm["seq"], dm["nb"]), 0, 0)
    full = lambda shp: pl.BlockSpec(shp, lambda i: (0,) * len(shp))
    in_specs = [
        pl.BlockSpec((te, d), lambda i: (i, 0)),
        pl.BlockSpec((SUBLANES, d), lambda i: (jnp.maximum(i * hb - 1, 0), 0)),
        pl.BlockSpec((SUBLANES, d), lambda i: (jnp.minimum((i + 1) * hb, last_hb), 0)),
        pl.BlockSpec((1, 6, d), modmap),
        full((1, d)), full((6, d)),
        full((d, 2 * LANES)), full((d, 2 * LANES)), full((d, rg)),
        full((2, LANES, d)), full((2, LANES, d)), full((rg, d)),
        full((2, d)), full((2, d)),
    ]
    args = [xs, xs, xs, mod, g1n.reshape(1, d), mu, w1c, a1c, g1.astype(BF16), w2p, a2p, g2.astype(BF16), w0, a0]
    row_spec = pl.BlockSpec((te, d), lambda i: (i, 0))
    out_shape = [jax.ShapeDtypeStruct((3, rows, d), BF16), jax.ShapeDtypeStruct((2, rows, d), F32),
                 jax.ShapeDtypeStruct((2, rows, d), BF16), jax.ShapeDtypeStruct((rows, d), BF16)]
    out_specs = [pl.BlockSpec((3, te, d), lambda i: (0, i, 0)), pl.BlockSpec((2, te, d), lambda i: (0, i, 0)),
                 pl.BlockSpec((2, te, d), lambda i: (0, i, 0)), row_spec]
    if mix:
        v1p, v2p = _pad_rank(vres[1], vres[2])
        in_specs += [full((d, LANES)), full((LANES, d)), full((1, d))]
        args += [v1p, v2p, vres[0].reshape(1, d)]
        out_shape.append(jax.ShapeDtypeStruct((rows, d), BF16))
        out_specs.append(row_spec)
    kern = functools.partial(_rw_prep_kernel, mix=mix, te=te, n_lat_rows=dm["n_lat_rows"], seq=dm["seq"],
                             ctx=dm["ctx"])
    return pl.pallas_call(
        kern,
        out_shape=out_shape,
        grid=(rows // te,),
        in_specs=in_specs,
        out_specs=out_specs,
        compiler_params=_cparams(("parallel",)),
        name="rwkv_prep",
    )(*args)


def _rkv_kernel(*refs, mix):
    if mix:
        xm_ref, w_ref, vf_ref, vg_ref, o_ref = refs
    else:
        xm_ref, w_ref, o_ref = refs
    acc = _dot(xm_ref[0], w_ref[0])
    if mix:
        p = pl.program_id(1)

        @pl.when(p == 2)
        def _():
            o_ref[0] = (acc + (vf_ref[0].astype(F32) - acc) * vg_ref[...].astype(F32)).astype(BF16)

        @pl.when(p != 2)
        def _():
            o_ref[0] = acc.astype(BF16)
    else:
        o_ref[0] = acc.astype(BF16)


def _rkv_proj(xm, w, v_first, vgate, *, layer, dm):
    _, rows, d = xm.shape
    tm = dm["tm"]
    mix = v_first is not None
    in_specs = [pl.BlockSpec((1, tm, d), lambda i, p: (p, i, 0)),
                pl.BlockSpec((None, 1, d, d), lambda i, p: (layer, p, 0, 0))]
    args = [xm, w]
    if mix:
        in_specs += [pl.BlockSpec((1, tm, d), lambda i, p: (2, i, 0)), pl.BlockSpec((tm, d), lambda i, p: (i, 0))]
        args += [v_first, vgate]
    return pl.pallas_call(
        functools.partial(_rkv_kernel, mix=mix),
        out_shape=jax.ShapeDtypeStruct((3, rows, d), BF16),
        grid=(rows // tm, 3),
        in_specs=in_specs,
        out_specs=pl.BlockSpec((1, tm, d), lambda i, p: (p, i, 0)),
        compiler_params=_cparams(("parallel", "arbitrary")),
        name="rwkv_rkv_proj",
    )(*args)


def _wkv_kernel(rf_ref, kf_ref, vf_ref, lwf_ref, af_ref, rb_ref, kb_ref, vb_ref, lwb_ref, ab_ref,
                kk_ref, ka_ref, rk_ref, yf_ref, bonf_ref, yb_ref, bonb_ref, s_scr, *, gps):
    c = pl.program_id(2)
    ln = WKV_CHUNK
    pw = WKV_PACK * RW_HEAD

    hpt = LANES // RW_HEAD

    @pl.when(c == 0)
    def _():
        s_scr[...] = jnp.zeros_like(s_scr)

    row = lax.broadcasted_iota(jnp.int32, (ln, ln), 0)
    col = lax.broadcasted_iota(jnp.int32, (ln, ln), 1)
    trow = lax.broadcasted_iota(jnp.int32, (ln, pw), 0)
    tsrc = lax.broadcasted_iota(jnp.int32, (ln, pw), 1) & (ln - 1)
    ones64 = _ones_blockdiag64()

    def fmask(cond):
        return jnp.where(cond, 1.0, 0.0)

    def same_block(n):
        sh = n.bit_length() - 1
        return (tsrc >> sh) == (trow >> sh)

    eye = fmask(tsrc == trow)
    base_f = fmask(same_block(WKV_INV_BASE))
    off_f = {}
    n = WKV_INV_BASE
    while n < ln:
        off_f[n] = fmask(jnp.logical_and(same_block(2 * n), jnp.logical_not(same_block(n))))
        n *= 2
    k_k = kk_ref[...]
    k_a = ka_ref[...]
    r_k = rk_ref[...]

    lane = lax.broadcasted_iota(jnp.int32, (ln, LANES), 1)
    half_f = [fmask((lane >> (RW_HEAD.bit_length() - 1)) == hf) for hf in range(hpt)]
    half_b = [hm.astype(BF16) for hm in half_f]
    zeros_b = jnp.zeros((ln, LANES), BF16)

    def bd(z):
        zb = z.astype(BF16)
        blocks = []
        for jh in range(WKV_PACK):
            lt, hf = divmod(jh, hpt)
            piece = zb[:, lt * LANES:(lt + 1) * LANES] * half_b[hf]
            blocks.append(jnp.concatenate([piece if tt == lt else zeros_b for tt in range(pw // LANES)], axis=1))
        return jnp.concatenate(blocks, axis=0)

    streams = ((rf_ref, kf_ref, vf_ref, lwf_ref, af_ref, yf_ref, bonf_ref),
               (rb_ref, kb_ref, vb_ref, lwb_ref, ab_ref, yb_ref, bonb_ref))
    units = []
    for dd, (r_ref, k_ref, v_ref, lw_ref, a_ref, y_ref, bon_ref) in enumerate(streams):
        rev = dd == 1
        r = r_ref[0].astype(F32)
        k = k_ref[0].astype(F32)
        v = v_ref[0].astype(F32)
        a = a_ref[0].astype(F32)
        lw = lw_ref[0]
        kkr = k * k_k
        kd = k * (1.0 + (a - 1.0) * k_a)
        ssq, rkd = _segsum64_mxu([kkr * kkr, r * kd * r_k], ones64)
        kk = kkr * lax.rsqrt(jnp.maximum(ssq, 1e-24))
        bvec = kk * a
        bon_ref[...] = (rkd * v).astype(BF16)

        tri = jnp.where((col >= row) if rev else (col <= row), 1.0, 0.0).astype(BF16)
        hi = lw.astype(BF16)
        rem = lw - hi.astype(F32)
        mid = rem.astype(BF16)
        lo = (rem - mid.astype(F32)).astype(BF16)
        cum = _dot(tri, hi) + _dot(tri, mid) + _dot(tri, lo)
        tot = cum[0:1, :] if rev else cum[ln - 1:ln, :]
        w_inv = jnp.exp(-cum)
        w_end = jnp.exp(tot - cum)
        a_t = -kk * jnp.exp(cum - lw)
        r_t = r * jnp.exp(cum)
        b_t = bvec * w_inv
        k_t = kd * w_inv
        b_e = bvec * w_end
        k_e = kd * w_end
        w_tot = jnp.exp(tot)

        strict = fmask(tsrc > trow) if rev else fmask(tsrc < trow)
        incl = strict + eye

        for gi in range(gps):
            sl = slice(gi * pw, (gi + 1) * pw)
            units.append(dict(
                dd=dd, gi=gi, sl=sl, y_ref=y_ref, strict=strict, incl=incl,
                ar=jnp.concatenate([a_t[:, sl], r_t[:, sl]], axis=0).astype(BF16),
                b_t=b_t[:, sl], k_t=k_t[:, sl], v=v[:, sl], w_tot=w_tot[:, sl],
                bk=jnp.concatenate([b_e[:, sl], k_e[:, sl]], axis=0).astype(BF16)))

    for un in units:
        sb = _dot_nt(un["ar"], bd(un["b_t"]))
        sk = _dot_nt(un["ar"], bd(un["k_t"]))
        un["m_ab"] = sb[:ln] * un["strict"]
        un["p_rb"] = sb[ln:] * un["incl"]
        un["m_ak"] = sk[:ln] * un["strict"]
        un["p_rk"] = sk[ln:] * un["incl"]
    for un in units:
        un["s0"] = s_scr[un["dd"], un["gi"]]
        un["ars"] = _dot_nt(un["ar"], un["s0"].astype(BF16))
        un["mv"] = _dot(jnp.concatenate([un["m_ak"], un["p_rk"]], axis=0).astype(BF16), bd(un["v"]))
    for un in units:
        m0 = un["m_ab"] * base_f
        un["pinv"] = eye + m0
        un["mp"] = _dot(m0.astype(BF16), bd(m0))
    for un in units:
        both = _dot(jnp.concatenate([un["mp"], un["pinv"]], axis=0).astype(BF16), bd(un["mp"]))
        un["pinv"] = un["pinv"] + both[ln:]
        un["mp"] = both[:ln]
    for un in units:
        un["pinv"] = un["pinv"] + _dot(un["pinv"].astype(BF16), bd(un["mp"]))
    n = WKV_INV_BASE
    while n < ln:
        for un in units:
            un["t1"] = _dot((un["m_ab"] * off_f[n]).astype(BF16), bd(un["pinv"]))
        for un in units:
            un["pinv"] = un["pinv"] + _dot(un["pinv"].astype(BF16), bd(un["t1"]))
        n *= 2
    for un in units:
        un["u"] = _dot(un["pinv"].astype(BF16), bd(un["ars"][:ln] + un["mv"][:ln]))
    for un in units:
        un["y_ref"][:, un["sl"]] = un["ars"][ln:] + _dot(un["p_rb"].astype(BF16), bd(un["u"])) + un["mv"][ln:]
        uv = jnp.concatenate([un["u"], un["v"]], axis=0).astype(BF16)
        res = _dot_tn(uv, un["bk"])
        for jh in range(WKV_PACK):
            lt, hf = divmod(jh, hpt)
            rsl = slice(jh * RW_HEAD, (jh + 1) * RW_HEAD)
            csl = slice(lt * LANES, (lt + 1) * LANES)
            s_scr[un["dd"], un["gi"], rsl, csl] = (un["s0"][rsl, csl] * un["w_tot"][:, csl]
                                                    + res[rsl, csl] * half_f[hf])


def _wkv(rkv, lw, a, k_k, k_a, r_k, *, dm, gps):
    _, rows, d = rkv.shape
    nb, seq, ctx = dm["nb"], dm["seq"], dm["ctx"]
    ln = WKV_CHUNK
    sw = gps * WKV_PACK * RW_HEAD
    ncc, nlc = ctx // ln, seq // ln
    ctx_c0 = nb * seq // ln

    def fblk(b, c):
        return jnp.where(c < ncc, ctx_c0 + b * ncc + c, b * nlc + (c - ncc))

    def bblk(b, c):
        return jnp.where(c < ncc, ctx_c0 + b * ncc + (ncc - 1 - c), b * nlc + (nlc - 1 - (c - ncc)))

    def spec3(p, blk):
        return pl.BlockSpec((1, ln, sw), lambda b, s, c, p=p, blk=blk: (p, blk(b, c), s))

    def spec2(blk):
        return pl.BlockSpec((ln, sw), lambda b, s, c, blk=blk: (blk(b, c), s))

    pspec = pl.BlockSpec((1, sw), lambda b, s, c: (0, s))
    in_specs = [spec3(0, fblk), spec3(1, fblk), spec3(2, fblk), spec3(0, fblk), spec3(0, fblk),
                spec3(0, bblk), spec3(1, bblk), spec3(2, bblk), spec3(1, bblk), spec3(1, bblk),
                pspec, pspec, pspec]
    return pl.pallas_call(
        functools.partial(_wkv_kernel, gps=gps),
        out_shape=[jax.ShapeDtypeStruct((rows, d), F32), jax.ShapeDtypeStruct((rows, d), BF16),
                   jax.ShapeDtypeStruct((rows, d), F32), jax.ShapeDtypeStruct((rows, d), BF16)],
        grid=(nb, d // sw, ncc + nlc),
        in_specs=in_specs,
        out_specs=[spec2(fblk), spec2(fblk), spec2(bblk), spec2(bblk)],
        scratch_shapes=[pltpu.VMEM((2, gps, WKV_PACK * RW_HEAD, WKV_PACK * RW_HEAD), F32)],
        compiler_params=_cparams(("parallel", "parallel", "arbitrary")),
        name="wkv_scan",
    )(rkv, rkv, rkv, lw, a, rkv, rkv, rkv, lw, a, k_k.reshape(1, d), k_a.reshape(1, d), r_k.reshape(1, d))


def _rw_out_kernel(yf_ref, yb_ref, bf_ref, bb_ref, g_ref, x_ref, mod_ref, lg_ref, lb_ref, w_ref, o_ref):
    y = yf_ref[...] + yb_ref[...]
    ones64 = _ones_blockdiag64()
    mean = _segsum64_mxu([y], ones64)[0] * (1.0 / RW_HEAD)
    yc = y - mean
    var = _segsum64_mxu([yc * yc], ones64)[0] * (1.0 / RW_HEAD)
    yn = yc * lax.rsqrt(var + RW_GN_EPS)
    bonus = bf_ref[...].astype(F32) + bb_ref[...].astype(F32)
    o = (yn * lg_ref[...] + lb_ref[...] + bonus) * g_ref[...].astype(F32)
    o_ref[...] = x_ref[...] + mod_ref[0, 2:3, :] * _dot(o.astype(BF16), w_ref[...])


def _rw_out(yf, bonf, yb, bonb, g, xs, mod, ln_g, ln_b, w_o, *, dm, n_rows):
    d = xs.shape[1]
    te = dm["te"]
    modmap = lambda i: (jnp.minimum((i * te) // dm["seq"], dm["nb"]), 0, 0)
    rspec = pl.BlockSpec((te, d), lambda i: (i, 0))
    vspec = pl.BlockSpec((1, d), lambda i: (0, 0))
    return pl.pallas_call(
        _rw_out_kernel,
        out_shape=jax.ShapeDtypeStruct((n_rows, d), F32),
        grid=(n_rows // te,),
        in_specs=[rspec, rspec, rspec, rspec, rspec, rspec, pl.BlockSpec((1, 6, d), modmap), vspec, vspec,
                  pl.BlockSpec((d, d), lambda i: (0, 0))],
        out_specs=rspec,
        compiler_params=_cparams(("parallel",)),
        name="rwkv_out",
    )(yf, yb, bonf, bonb, g, xs, mod, ln_g.reshape(1, d), ln_b.reshape(1, d), w_o)


def _rope_tables(dm):
    seq, nb, ctx = dm["seq"], dm["nb"], dm["ctx"]
    t = jnp.arange(seq, dtype=jnp.int32)
    pos = jnp.stack([t // GRID_W, t % GRID_W], axis=-1).astype(F32)
    n_freq = ATT_HEAD // 4
    inv = ROPE_THETA ** (-jnp.arange(n_freq, dtype=F32) / n_freq)
    ang = pos[:, :, None] * inv
    cos, sin = jnp.cos(ang), jnp.sin(ang)
    zero = jnp.zeros_like(sin)
    cos_t = jnp.stack([cos, cos], axis=2).reshape(seq, ATT_HEAD)
    sa_t = jnp.stack([-sin, zero], axis=2).reshape(seq, ATT_HEAD)
    sb_t = jnp.stack([zero, sin], axis=2).reshape(seq, ATT_HEAD)
    nctx = nb * ctx
    full = lambda tab, fill: jnp.concatenate([jnp.tile(tab, (nb, 1)), jnp.full((nctx, ATT_HEAD), fill, F32)], axis=0)
    return full(cos_t, 1.0), full(sa_t, 0.0), full(sb_t, 0.0)


def kernel(x, c, ctx, c_ctx, mod_w, mod_b, norm1_g, norm2_g, ffn_up, ffn_conv_w, ffn_conv_b, ffn_down, rw_mu, rw_w_rkv, rw_w0, rw_w1, rw_w2, rw_a0, rw_a1, rw_a2, rw_g1, rw_g2, rw_k_k, rw_k_a, rw_r_k, rw_ln_g, rw_ln_b, rw_w_o, rw_v0, rw_v1, rw_v2, na_w_qkv, na_q_g, na_k_g, na_rpb, na_w_o, ga_w_qkv, ga_q_g, ga_k_g, ga_w_o):
    nb, seq, d = x.shape
    nctx = ctx.shape[1]
    depth = mod_w.shape[0]
    tm = nb * nctx
    assert seq % tm == 0 and seq & (seq - 1) == 0 and nctx & (nctx - 1) == 0 and nb + 1 <= SUBLANES
    assert seq // GRID_W >= NA_WIN_R and nctx % GRID_W == 0
    dm = dict(nb=nb, seq=seq, ctx=nctx, tm=tm, te=tm // 2, tpb=seq // tm, n_lat_rows=nb * seq)
    n_lat_tiles = nb * seq // tm
    n_tiles = n_lat_tiles + 1
    att_scale = ATT_HEAD ** -0.5

    xs = jnp.concatenate([x.reshape(nb * seq, d), ctx.reshape(nb * nctx, d)], axis=0)
    c_all = jnp.concatenate([c, c_ctx[None], jnp.zeros((SUBLANES - nb - 1, d), F32)], axis=0)
    mods = _modulations(c_all, mod_w, mod_b)
    rope_tabs = None
    v_first = None
    ffn_wu, ffn_wd = _ffn_weights(ffn_up, ffn_down)
    rw_rkv_w = rw_w_rkv.astype(BF16)

    for i in range(depth):
        kind, j = i % 3, i // 3
        need_ctx = i < depth - 1
        nt_out = n_tiles if need_ctx else n_lat_tiles
        mod = mods[i, :nb + 1].reshape(nb + 1, 6, d)
        if kind == 0:
            vres = None if j == 0 else (rw_v0[j - 1], rw_v1[j - 1], rw_v2[j - 1])
            prep = _rw_prep(xs, mod, norm1_g[i], rw_mu[j], rw_w0[j], rw_w1[j], rw_w2[j], rw_a0[j], rw_a1[j],
                            rw_a2[j], rw_g1[j], rw_g2[j], vres, dm=dm)
            xm, lw, a, g = prep[0], prep[1], prep[2], prep[3]
            rkv = _rkv_proj(xm, rw_rkv_w, v_first if vres is not None else None,
                            prep[4] if vres is not None else None, layer=j, dm=dm)
            if v_first is None:
                v_first = rkv
            yf, bonf, yb, bonb = _wkv(rkv, lw, a, rw_k_k[j], rw_k_a[j], rw_r_k[j].reshape(-1), dm=dm,
                                      gps=min(8, d // (WKV_PACK * RW_HEAD)))
            xs = _rw_out(yf, bonf, yb, bonb, g, xs, mod, rw_ln_g[j], rw_ln_b[j], rw_w_o[j].astype(BF16),
                         dm=dm, n_rows=nt_out * tm)
        elif kind == 1:
            qkv = _qkv_proj(xs, mod, norm1_g[i], na_w_qkv[j].astype(BF16), na_q_g[j] * att_scale, na_k_g[j],
                            None, dm=dm, kv_dim=d)
            o = _na_attention(qkv, na_rpb[j], dm=dm, d=d)
            xs = _out_proj(o, na_w_o[j].astype(BF16), xs, mod, dm=dm, n_tiles=nt_out)
        else:
            if rope_tabs is None:
                rope_tabs = _rope_tables(dm)
            kv_dim = (ga_w_qkv.shape[-1] - d) // 2
            qkv = _qkv_proj(xs, mod, norm1_g[i], ga_w_qkv[j].astype(BF16), ga_q_g[j] * att_scale, ga_k_g[j],
                            rope_tabs, dm=dm, kv_dim=kv_dim)
            o = _gqa_attention(qkv, dm=dm, d=d, kv_heads=kv_dim // ATT_HEAD)
            xs = _out_proj(o, ga_w_o[j].astype(BF16), xs, mod, dm=dm, n_tiles=nt_out)
        xs = _ffn(xs, mod, norm2_g[i], ffn_wu, ffn_conv_w[i], ffn_conv_b[i], ffn_wd, layer=i, dm=dm,
                  need_ctx=need_ctx)
    return xs[:nb * seq].reshape(nb, seq, d)
```

```python
import functools
import math

import jax
import jax.numpy as jnp
from jax import lax
from jax.experimental import pallas as pl
from jax.experimental.pallas import tpu as pltpu

F32 = jnp.float32
BF16 = jnp.bfloat16

NORM_EPS = 1e-6
GRID_W = 64
ATT_HEAD = 128
RW_HEAD = 64
NA_WIN_R = 8
NA_WIN_C = 16
ROPE_THETA = 10000.0
RW_GN_EPS = 64e-5
LANES = 128
SUBLANES = 8
WKV_CHUNK = 64
WKV_PACK = 4
WKV_INV_BASE = 8
FFN_ROW_PIECES = 2
VMEM_LIMIT = 56 * 1024 * 1024
NEG_BIG = -1e30


def _cparams(sem):
    return pltpu.CompilerParams(dimension_semantics=sem, vmem_limit_bytes=VMEM_LIMIT)


def _dot(a, b):
    return jnp.dot(a, b, preferred_element_type=F32)


def _dot_nt(a, b):
    return lax.dot_general(a, b, (((1,), (1,)), ((), ())), preferred_element_type=F32)


def _dot_tn(a, b):
    return lax.dot_general(a, b, (((0,), (0,)), ((), ())), preferred_element_type=F32)


def _normmod(x, g, shift, scale):
    ms = jnp.mean(x * x, axis=-1, keepdims=True)
    y = x * lax.rsqrt(ms + NORM_EPS)
    return (y * g) * (1.0 + scale) + shift


def _silu(x):
    return x * jax.nn.sigmoid(x)


def _seq_edges(tile, rows, n_lat_rows, seq, ctx):
    rid = lax.broadcasted_iota(jnp.int32, (rows, 1), 0)
    base = tile * rows
    period = jnp.where(base >= n_lat_rows, ctx, seq)
    pos = (base + rid) & (period - 1)
    return rid, pos == 0, pos == period - 1


def _shift_rows(u, rid, first, last, prev_row, next_row):
    n = u.shape[0]
    up = pltpu.roll(u, 1, axis=0)
    up = jnp.where(rid == 0, prev_row, up)
    up = jnp.where(first, 0.0, up)
    un = pltpu.roll(u, n - 1, axis=0)
    un = jnp.where(rid == n - 1, next_row, un)
    un = jnp.where(last, 0.0, un)
    return up, un


def _ones_blockdiag64():
    sh = RW_HEAD.bit_length() - 1
    r = lax.broadcasted_iota(jnp.int32, (LANES, LANES), 0) >> sh
    c = lax.broadcasted_iota(jnp.int32, (LANES, LANES), 1) >> sh
    return jnp.where(r == c, 1.0, 0.0).astype(BF16)


def _segsum64_mxu(xs, ones):
    m, n = xs[0].shape
    pieces = []
    for x in xs:
        hi = x.astype(BF16)
        lo = (x - hi.astype(F32)).astype(BF16)
        for part in (hi, lo):
            pieces += [part[:, c * LANES:(c + 1) * LANES] for c in range(n // LANES)]
    res = _dot(jnp.concatenate(pieces, axis=0), ones)
    nslab = n // LANES
    outs = []
    for i in range(len(xs)):
        base = i * 2 * nslab
        cols = [res[(base + c) * m:(base + c + 1) * m] + res[(base + nslab + c) * m:(base + nslab + c + 1) * m]
                for c in range(nslab)]
        outs.append(jnp.concatenate(cols, axis=1))
    return outs


def _mod_kernel(c_ref, w_ref, b_ref, o_ref):
    s = _silu(c_ref[...]).astype(BF16)
    o_ref[0] = _dot(s, w_ref[0].astype(BF16)) + b_ref[0]


def _modulations(c_all, mod_w, mod_b):
    depth, d, n = mod_w.shape
    tn = n // 8
    return pl.pallas_call(
        _mod_kernel,
        out_shape=jax.ShapeDtypeStruct((depth, SUBLANES, n), F32),
        grid=(depth, n // tn),
        in_specs=[
            pl.BlockSpec((SUBLANES, d), lambda l, j: (0, 0)),
            pl.BlockSpec((1, d, tn), lambda l, j: (l, 0, j)),
            pl.BlockSpec((1, 1, tn), lambda l, j: (l, 0, j)),
        ],
        out_specs=pl.BlockSpec((1, SUBLANES, tn), lambda l, j: (l, 0, j)),
        compiler_params=_cparams(("parallel", "parallel")),
        name="modulation",
    )(c_all, mod_w, mod_b.reshape(depth, 1, n))


def _ffn_kernel(*refs, tm, tile_off, aliased, n_lat_rows, seq, ctx):
    (x_ref, xp_ref, xn_ref, mod_ref, g_ref, wug_ref, wuv_ref, cwg_ref, cwv_ref, cbg_ref, cbv_ref,
     wd_ref) = refs[:12]
    o_ref, h_scr = refs[-2:]
    i = pl.program_id(0) + tile_off
    j = pl.program_id(1)
    shift = mod_ref[0, 3:4, :]
    scale = mod_ref[0, 4:5, :]
    hr = 2 * SUBLANES
    th = tm // FFN_ROW_PIECES

    @pl.when(j == 0)
    def _():
        g = g_ref[...]
        halo = jnp.concatenate([xp_ref[...], xn_ref[...]], axis=0)
        h_scr[0:hr, :] = _normmod(halo, g, shift, scale).astype(BF16)
        h_scr[hr:hr + tm, :] = _normmod(x_ref[...], g, shift, scale).astype(BF16)
        o_ref[...] = jnp.zeros_like(o_ref)

    rid, first, last = _seq_edges(i, tm, n_lat_rows, seq, ctx)
    rid_h = rid[0:th]

    def up_proj(p):
        lo = 0 if p == 0 else hr + p * th
        hp = h_scr[lo:hr + (p + 1) * th, :]
        ug, uv = _dot(hp, wug_ref[...]), _dot(hp, wuv_ref[...])
        if p == 0:
            return dict(g=ug[hr:], v=uv[hr:], halo_g=ug[0:hr], halo_v=uv[0:hr])
        return dict(g=ug, v=uv)

    def conv(main, prev_row, next_row, fm, lm, cw_ref, cb_ref):
        up, un = _shift_rows(main, rid_h, fm, lm, prev_row, next_row)
        return cb_ref[...] + up * cw_ref[0:1, :] + main * cw_ref[1:2, :] + un * cw_ref[2:3, :]

    pv, nx = SUBLANES - 1, SUBLANES

    def act_down(p, us):
        rs = slice(p * th, (p + 1) * th)
        rows = {}
        for key in ("g", "v"):
            prev_row = us[0]["halo_" + key][pv:pv + 1] if p == 0 else us[p - 1][key][th - 1:th]
            next_row = us[0]["halo_" + key][nx:nx + 1] if p == FFN_ROW_PIECES - 1 else us[p + 1][key][0:1]
            rows[key] = (prev_row, next_row)
        a = (_silu(conv(us[p]["g"], *rows["g"], first[rs], last[rs], cwg_ref, cbg_ref))
             * conv(us[p]["v"], *rows["v"], first[rs], last[rs], cwv_ref, cbv_ref))
        o_ref[rs, :] += _dot(a.astype(BF16), wd_ref[...])

    us = {0: up_proj(0)}
    for p in range(FFN_ROW_PIECES):
        if p + 1 < FFN_ROW_PIECES:
            us[p + 1] = up_proj(p + 1)
        act_down(p, us)

    @pl.when(j == pl.num_programs(1) - 1)
    def _():
        o_ref[...] = x_ref[...] + mod_ref[0, 5:6, :] * o_ref[...]


FFN_COLS = 512


def _ffn_weights(ffn_up, ffn_down):
    return ffn_up.astype(BF16), ffn_down.astype(BF16)


def _ffn_call(xs, mod, g2, wu, cw, cb, wd, prev, *, layer, dm, tm, tile_off, n_tiles, out_rows):
    rows, d = xs.shape
    f = wd.shape[1]
    fc = FFN_COLS
    nfc = f // fc
    hb = tm // SUBLANES
    last_hb = rows // SUBLANES - 1
    seq, nb = dm["seq"], dm["nb"]
    assert seq % tm == 0 or tile_off * tm >= dm["n_lat_rows"]
    modmap = lambda i, j: (jnp.minimum(((i + tile_off) * tm) // seq, nb), 0, 0)
    kern = functools.partial(_ffn_kernel, tm=tm, tile_off=tile_off, aliased=prev is not None,
                             n_lat_rows=dm["n_lat_rows"], seq=seq, ctx=dm["ctx"])
    in_specs = [
        pl.BlockSpec((tm, d), lambda i, j: (i + tile_off, 0), pipeline_mode=pl.Buffered(1)),
        pl.BlockSpec((SUBLANES, d), lambda i, j: (jnp.maximum((i + tile_off) * hb - 1, 0), 0)),
        pl.BlockSpec((SUBLANES, d), lambda i, j: (jnp.minimum((i + tile_off + 1) * hb, last_hb), 0)),
        pl.BlockSpec((1, 6, d), modmap),
        pl.BlockSpec((1, d), lambda i, j: (0, 0)),
        pl.BlockSpec((None, d, fc), lambda i, j: (layer, 0, j)),
        pl.BlockSpec((None, d, fc), lambda i, j: (layer, 0, nfc + j)),
        pl.BlockSpec((3, fc), lambda i, j: (0, j)),
        pl.BlockSpec((3, fc), lambda i, j: (0, nfc + j)),
        pl.BlockSpec((1, fc), lambda i, j: (0, j)),
        pl.BlockSpec((1, fc), lambda i, j: (0, nfc + j)),
        pl.BlockSpec((None, fc, d), lambda i, j: (layer, j, 0)),
    ]
    args = [xs, xs, xs, mod, g2.reshape(1, d), wu, wu, cw, cw, cb.reshape(1, -1), cb.reshape(1, -1), wd]
    aliases = {}
    if prev is not None:
        in_specs.append(pl.BlockSpec(memory_space=pl.ANY))
        args.append(prev)
        aliases = {len(args) - 1: 0}
    return pl.pallas_call(
        kern,
        out_shape=jax.ShapeDtypeStruct((out_rows, d), F32),
        grid=(n_tiles, nfc),
        in_specs=in_specs,
        out_specs=pl.BlockSpec((tm, d), lambda i, j: (i + tile_off, 0)),
        scratch_shapes=[pltpu.VMEM((tm + 2 * SUBLANES, d), BF16)],
        input_output_aliases=aliases,
        compiler_params=_cparams(("parallel", "arbitrary")),
        name="conv_ffn",
    )(*args)


def _ffn(xs, mod, g2, wu, cw, cb, wd, *, layer, dm, need_ctx):
    rows = xs.shape[0]
    n_lat_rows, tm = dm["n_lat_rows"], dm["tm"]
    big = 2 * tm
    out_rows = rows if need_ctx else n_lat_rows
    y = _ffn_call(xs, mod, g2, wu, cw, cb, wd, None, layer=layer, dm=dm, tm=big, tile_off=0,
                  n_tiles=n_lat_rows // big, out_rows=out_rows)
    if need_ctx:
        y = _ffn_call(xs, mod, g2, wu, cw, cb, wd, y, layer=layer, dm=dm, tm=tm, tile_off=n_lat_rows // tm,
                      n_tiles=1, out_rows=out_rows)
    return y


def _qkv_kernel(*refs, block_kinds, rope):
    if rope:
        x_ref, mod_ref, g_ref, w_ref, qg_ref, kg_ref, cos_ref, sa_ref, sb_ref, o_ref, h_scr = refs
    else:
        x_ref, mod_ref, g_ref, w_ref, qg_ref, kg_ref, o_ref, h_scr = refs
    j = pl.program_id(1)

    @pl.when(j == 0)
    def _():
        h_scr[...] = _normmod(x_ref[...], g_ref[...], mod_ref[0, 0:1, :], mod_ref[0, 1:2, :]).astype(BF16)

    acc = _dot(h_scr[...], w_ref[...])

    def emit(kinds):
        for hh, kind in enumerate(kinds):
            hs = slice(hh * ATT_HEAD, (hh + 1) * ATT_HEAD)
            y = acc[:, hs]
            if kind != "v":
                gain = qg_ref[...] if kind == "q" else kg_ref[...]
                y = y * lax.rsqrt(jnp.mean(y * y, axis=-1, keepdims=True) + NORM_EPS) * gain
                if rope:
                    y = (y * cos_ref[...] + pltpu.roll(y, ATT_HEAD - 32, axis=1) * sa_ref[...]
                         + pltpu.roll(y, 32, axis=1) * sb_ref[...])
            o_ref[:, hs] = y.astype(BF16)

    for jb, kinds in enumerate(block_kinds):
        pl.when(j == jb)(functools.partial(emit, kinds))


def _qkv_proj(xs, mod, g1, w, qg, kg, rope_tabs, *, dm, kv_dim):
    rows, d = xs.shape
    n = w.shape[1]
    tn = min(d, 2 * kv_dim)
    hpb = tn // ATT_HEAD
    kinds = ["q"] * (d // ATT_HEAD) + ["k"] * (kv_dim // ATT_HEAD) + ["v"] * (kv_dim // ATT_HEAD)
    block_kinds = tuple(tuple(kinds[b * hpb:(b + 1) * hpb]) for b in range(n // tn))
    tm = dm["tm"]
    modmap = lambda i, j: (jnp.minimum(i // dm["tpb"], dm["nb"]), 0, 0)
    in_specs = [
        pl.BlockSpec((tm, d), lambda i, j: (i, 0)),
        pl.BlockSpec((1, 6, d), modmap),
        pl.BlockSpec((1, d), lambda i, j: (0, 0)),
        pl.BlockSpec((d, tn), lambda i, j: (0, j)),
        pl.BlockSpec((1, ATT_HEAD), lambda i, j: (0, 0)),
        pl.BlockSpec((1, ATT_HEAD), lambda i, j: (0, 0)),
    ]
    args = [xs, mod, g1.reshape(1, d), w, qg.reshape(1, ATT_HEAD), kg.reshape(1, ATT_HEAD)]
    if rope_tabs is not None:
        in_specs += [pl.BlockSpec((tm, ATT_HEAD), lambda i, j: (i, 0))] * 3
        args += list(rope_tabs)
    kern = functools.partial(_qkv_kernel, block_kinds=block_kinds, rope=rope_tabs is not None)
    return pl.pallas_call(
        kern,
        out_shape=jax.ShapeDtypeStruct((rows, n), BF16),
        grid=(rows // tm, n // tn),
        in_specs=in_specs,
        out_specs=pl.BlockSpec((tm, tn), lambda i, j: (i, j)),
        scratch_shapes=[pltpu.VMEM((tm, d), BF16)],
        compiler_params=_cparams(("parallel", "arbitrary")),
        name="qkv_proj",
    )(*args)


def _oproj_kernel(a_ref, w_ref, x_ref, mod_ref, o_ref):
    o_ref[...] = x_ref[...] + mod_ref[0, 2:3, :] * _dot(a_ref[...], w_ref[...])


def _out_proj(a, w, xs, mod, *, dm, n_tiles):
    d = xs.shape[1]
    tm = dm["tm"]
    modmap = lambda i: (jnp.minimum(i // dm["tpb"], dm["nb"]), 0, 0)
    return pl.pallas_call(
        _oproj_kernel,
        out_shape=jax.ShapeDtypeStruct((n_tiles * tm, d), F32),
        grid=(n_tiles,),
        in_specs=[
            pl.BlockSpec((tm, d), lambda i: (i, 0)),
            pl.BlockSpec((d, d), lambda i: (0, 0)),
            pl.BlockSpec((tm, d), lambda i: (i, 0)),
            pl.BlockSpec((1, 6, d), modmap),
        ],
        out_specs=pl.BlockSpec((tm, d), lambda i: (i, 0)),
        compiler_params=_cparams(("parallel",)),
        name="out_proj",
    )(a, w, xs, mod)


def _softmax_pv(q, segs):
    ss = [_dot_nt(q, k) for k, _ in segs]
    m = ss[0].max(axis=-1, keepdims=True)
    for s in ss[1:]:
        m = jnp.maximum(m, s.max(axis=-1, keepdims=True))
    ps = [jnp.exp(s - m) for s in ss]
    l = ps[0].sum(axis=-1, keepdims=True)
    for p in ps[1:]:
        l = l + p.sum(axis=-1, keepdims=True)
    o = _dot(ps[0].astype(BF16), segs[0][1])
    for p, (_, v) in zip(ps[1:], segs[1:]):
        o = o + _dot(p.astype(BF16), v)
    return o / l


def _lane_tiles(x, op):
    acc = x[:, 0:LANES]
    for j in range(1, x.shape[1] // LANES):
        acc = op(acc, x[:, j * LANES:(j + 1) * LANES])
    return acc


def _gqa_kernel(q_ref, kl_ref, vl_ref, kc_ref, vc_ref, o_ref, s_scr, *, group, n_lat_tiles, kchunk):
    t = pl.program_id(2)
    seq, ctx = kl_ref.shape[0], kc_ref.shape[0]

    def run(chunks):
        m = [None] * group
        mrun = [None] * group
        lrun = [None] * group
        o = [None] * group
        for g in range(group + 1):
            for k_ref, v_ref, st, sz, off in chunks:
                if g < group:
                    s = _dot_nt(q_ref[:, g * ATT_HEAD:(g + 1) * ATT_HEAD], k_ref[st:st + sz, :])
                    s_scr[g % 2, :, off:off + sz] = s
                    tmax = _lane_tiles(s, jnp.maximum)
                    mrun[g] = tmax if mrun[g] is None else jnp.maximum(mrun[g], tmax)
                if g >= 1:
                    h = g - 1
                    p = jnp.exp(s_scr[h % 2, :, off:off + sz] - m[h])
                    psum = _lane_tiles(p, jnp.add)
                    pv = _dot(p.astype(BF16), v_ref[st:st + sz, :])
                    lrun[h] = psum if lrun[h] is None else lrun[h] + psum
                    o[h] = pv if o[h] is None else o[h] + pv
            if g < group:
                m[g] = mrun[g].max(axis=-1, keepdims=True)
        for g in range(group):
            l = lrun[g].sum(axis=-1, keepdims=True)
            o_ref[:, g * ATT_HEAD:(g + 1) * ATT_HEAD] = (o[g] / l).astype(BF16)

    lat_chunks = [(kl_ref, vl_ref, st, kchunk, st) for st in range(0, seq, kchunk)]
    ctx_chunk = (kc_ref, vc_ref, 0, ctx, seq)

    @pl.when(t < n_lat_tiles)
    def _():
        run(lat_chunks + [ctx_chunk])

    @pl.when(t >= n_lat_tiles)
    def _():
        run([ctx_chunk])


def _gqa_attention(qkv, *, dm, d, kv_heads):
    rows = qkv.shape[0]
    nb, seq, ctx = dm["nb"], dm["seq"], dm["ctx"]
    group = d // ATT_HEAD // kv_heads
    gw = group * ATT_HEAD
    tq = min(256, ctx)
    nlt, nct = seq // tq, ctx // tq
    kcol = d // ATT_HEAD
    vcol = kcol + kv_heads
    ctx_blk0 = nb * seq // ctx

    def qmap(b, h, t):
        return (jnp.where(t < nlt, b * nlt + t, nb * nlt + b * nct + (t - nlt)), h)

    kern = functools.partial(_gqa_kernel, group=group, n_lat_tiles=nlt, kchunk=min(512, seq))
    return pl.pallas_call(
        kern,
        out_shape=jax.ShapeDtypeStruct((rows, d), BF16),
        scratch_shapes=[pltpu.VMEM((2, tq, seq + ctx), F32)],
        grid=(nb, kv_heads, nlt + nct),
        in_specs=[
            pl.BlockSpec((tq, gw), qmap),
            pl.BlockSpec((seq, ATT_HEAD), lambda b, h, t: (b, kcol + h)),
            pl.BlockSpec((seq, ATT_HEAD), lambda b, h, t: (b, vcol + h)),
            pl.BlockSpec((ctx, ATT_HEAD), lambda b, h, t: (ctx_blk0 + b, kcol + h)),
            pl.BlockSpec((ctx, ATT_HEAD), lambda b, h, t: (ctx_blk0 + b, vcol + h)),
        ],
        out_specs=pl.BlockSpec((tq, gw), qmap),
        compiler_params=_cparams(("parallel", "parallel", "arbitrary")),
        name="gqa_attention",
    )(qkv, qkv, qkv, qkv, qkv)


def _na_kernel(q_ref, k_ref, v_ref, kc_ref, vc_ref, bias_ref, o_ref, *, rb, hps, grid_rows, n_row_blocks):
    t = pl.program_id(2)
    win = NA_WIN_R * GRID_W

    @pl.when(t < n_row_blocks)
    def _():
        units = []
        for hh in range(hps):
            hs = slice(hh * ATT_HEAD, (hh + 1) * ATT_HEAD)
            for rr in range(rb):
                r = t * rb + rr
                rs = jnp.clip(r - NA_WIN_R // 2, 0, grid_rows - NA_WIN_R)
                units.append(dict(hh=hh, hs=hs, rows=slice(rr * GRID_W, (rr + 1) * GRID_W), off=r - rs,
                                  start=pl.multiple_of(rs * GRID_W, GRID_W)))
        for un in units:
            q = q_ref[un["rows"], un["hs"]]
            un["sw"] = _dot_nt(q, k_ref[pl.ds(un["start"], win), un["hs"]]) + bias_ref[un["hh"], un["off"]]
            un["sc"] = _dot_nt(q, kc_ref[:, un["hs"]])
        for un in units:
            m = jnp.maximum(un["sw"].max(axis=-1, keepdims=True), un["sc"].max(axis=-1, keepdims=True))
            un["pw"] = jnp.exp(un["sw"] - m)
            un["pc"] = jnp.exp(un["sc"] - m)
        for un in units:
            l = un["pw"].sum(axis=-1, keepdims=True) + un["pc"].sum(axis=-1, keepdims=True)
            o = (_dot(un["pw"].astype(BF16), v_ref[pl.ds(un["start"], win), un["hs"]])
                 + _dot(un["pc"].astype(BF16), vc_ref[:, un["hs"]]))
            o_ref[un["rows"], un["hs"]] = (o / l).astype(BF16)

    @pl.when(t >= n_row_blocks)
    def _():
        for hh in range(hps):
            hs = slice(hh * ATT_HEAD, (hh + 1) * ATT_HEAD)
            o_ref[:, hs] = _softmax_pv(q_ref[:, hs], [(kc_ref[:, hs], vc_ref[:, hs])]).astype(BF16)


def _na_bias_table(rpb):
    qc = jnp.arange(GRID_W)
    kc = jnp.arange(GRID_W)
    cs = jnp.clip(qc - NA_WIN_C // 2, 0, GRID_W - NA_WIN_C)
    inwin = (kc[None, :] >= cs[:, None]) & (kc[None, :] < cs[:, None] + NA_WIN_C)
    cidx = kc[None, :] - qc[:, None] + NA_WIN_C - 1
    sel = (cidx[None] == jnp.arange(2 * NA_WIN_C - 1)[:, None, None]) & inwin[None]
    cols = jnp.einsum('hrc,cqk->hrqk', rpb, sel.astype(F32), precision=lax.Precision.HIGHEST)
    cols = jnp.where(inwin[None, None], cols, NEG_BIG)
    tab = jnp.stack([cols[:, NA_WIN_R - 1 - o:2 * NA_WIN_R - 1 - o] for o in range(NA_WIN_R)], axis=1)
    tab = tab.transpose(0, 1, 3, 2, 4)
    return tab.reshape(rpb.shape[0], NA_WIN_R, GRID_W, NA_WIN_R * GRID_W).astype(F32)


def _na_attention(qkv, rpb, *, dm, d):
    rows = qkv.shape[0]
    nb, seq, ctx = dm["nb"], dm["seq"], dm["ctx"]
    heads = d // ATT_HEAD
    grid_rows = seq // GRID_W
    rb = ctx // GRID_W
    nrb = grid_rows // rb
    ctx_blk0 = nb * seq // ctx
    bias = _na_bias_table(rpb)

    def qmap(b, h, t):
        return (jnp.where(t < nrb, b * nrb + t, ctx_blk0 + b), h)

    hps = 4 if heads % 4 == 0 else 2
    hw = hps * ATT_HEAD
    ng = heads // hps
    kern = functools.partial(_na_kernel, rb=rb, hps=hps, grid_rows=grid_rows, n_row_blocks=nrb)
    return pl.pallas_call(
        kern,
        out_shape=jax.ShapeDtypeStruct((rows, d), BF16),
        grid=(nb, ng, nrb + 1),
        in_specs=[
            pl.BlockSpec((ctx, hw), qmap),
            pl.BlockSpec((seq, hw), lambda b, h, t: (b, ng + h)),
            pl.BlockSpec((seq, hw), lambda b, h, t: (b, 2 * ng + h)),
            pl.BlockSpec((ctx, hw), lambda b, h, t: (ctx_blk0 + b, ng + h)),
            pl.BlockSpec((ctx, hw), lambda b, h, t: (ctx_blk0 + b, 2 * ng + h)),
            pl.BlockSpec((hps, NA_WIN_R, GRID_W, NA_WIN_R * GRID_W), lambda b, h, t: (h, 0, 0, 0)),
        ],
        out_specs=pl.BlockSpec((ctx, hw), qmap),
        compiler_params=_cparams(("parallel", "parallel", "arbitrary")),
        name="na_attention",
    )(qkv, qkv, qkv, qkv, qkv, bias)


def _rw_prep_kernel(*refs, mix, te, n_lat_rows, seq, ctx):
    (x_ref, xp_ref, xn_ref, mod_ref, g_ref, mu_ref, w1_ref, a1_ref, g1_ref, w2_ref, a2_ref, g2_ref,
     w0_ref, a0_ref) = refs[:14]
    if mix:
        v1_ref, v2_ref, v0_ref, xm_ref, lw_ref, a_ref, go_ref, vg_ref = refs[14:]
    else:
        xm_ref, lw_ref, a_ref, go_ref = refs[14:]
    i = pl.program_id(0)
    g = g_ref[...]
    shift = mod_ref[0, 0:1, :]
    scale = mod_ref[0, 1:2, :]
    h = _normmod(x_ref[...], g, shift, scale)
    halo = _normmod(jnp.concatenate([xp_ref[...], xn_ref[...]], axis=0), g, shift, scale)
    rid, first, last = _seq_edges(i, te, n_lat_rows, seq, ctx)
    hp, hn = _shift_rows(h, rid, first, last, halo[SUBLANES - 1:SUBLANES], halo[SUBLANES:SUBLANES + 1])
    xx = 0.5 * (hp + hn) - h

    def mixed(p):
        return (h + xx * mu_ref[p:p + 1, :]).astype(BF16)

    xv = mixed(2)
    xm_ref[0] = mixed(0)
    xm_ref[1] = mixed(1)
    xm_ref[2] = xv
    zw = jnp.tanh(_dot(mixed(3), w1_ref[...])).astype(BF16)
    za = _dot(mixed(4), a1_ref[...]).astype(BF16)
    zg = jax.nn.sigmoid(_dot(mixed(5), g1_ref[...])).astype(BF16)
    for dd in range(2):
        sl = slice(dd * LANES, (dd + 1) * LANES)
        wl = w0_ref[dd:dd + 1, :] + _dot(zw[:, sl], w2_ref[dd])
        lw_ref[dd] = (-math.exp(-0.5)) * jax.nn.sigmoid(wl)
        a_ref[dd] = jax.nn.sigmoid(a0_ref[dd:dd + 1, :] + _dot(za[:, sl], a2_ref[dd])).astype(BF16)
    go_ref[...] = _dot(zg, g2_ref[...]).astype(BF16)
    if mix:
        zv = _dot(xv, v1_ref[...]).astype(BF16)
        vg_ref[...] = jax.nn.sigmoid(v0_ref[...] + _dot(zv, v2_ref[...])).astype(BF16)


def _pad_rank(w1, w2):
    r = w1.shape[-1]
    pad = (-r) % LANES
    w1 = jnp.pad(w1, [(0, 0)] * (w1.ndim - 1) + [(0, pad)])
    w2 = jnp.pad(w2, [(0, 0)] * (w2.ndim - 2) + [(0, pad), (0, 0)])
    return w1.astype(BF16), w2.astype(BF16)


def _rw_prep(xs, mod, g1n, mu, w0, w1, w2, a0, a1, a2, g1, g2, vres, *, dm):
    rows, d = xs.shape
    te = dm["te"]
    hb = te // SUBLANES
    last_hb = rows // SUBLANES - 1
    mix = vres is not None
    w1p, w2p = _pad_rank(w1, w2)
    a1p, a2p = _pad_rank(a1, a2)
    w1c = jnp.concatenate([w1p[0], w1p[1]], axis=1)
    a1c = jnp.concatenate([a1p[0], a1p[1]], axis=1)
    rg = g1.shape[1]
    modmap = lambda i: (jnp.minimum((i * te) // dm["seq"], dm["nb"]), 0, 0)
    full = lambda shp: pl.BlockSpec(shp, lambda i: (0,) * len(shp))
    in_specs = [
        pl.BlockSpec((te, d), lambda i: (i, 0)),
        pl.BlockSpec((SUBLANES, d), lambda i: (jnp.maximum(i * hb - 1, 0), 0)),
        pl.BlockSpec((SUBLANES, d), lambda i: (jnp.minimum((i + 1) * hb, last_hb), 0)),
        pl.BlockSpec((1, 6, d), modmap),
        full((1, d)), full((6, d)),
        full((d, 2 * LANES)), full((d, 2 * LANES)), full((d, rg)),
        full((2, LANES, d)), full((2, LANES, d)), full((rg, d)),
        full((2, d)), full((2, d)),
    ]
    args = [xs, xs, xs, mod, g1n.reshape(1, d), mu, w1c, a1c, g1.astype(BF16), w2p, a2p, g2.astype(BF16), w0, a0]
    row_spec = pl.BlockSpec((te, d), lambda i: (i, 0))
    out_shape = [jax.ShapeDtypeStruct((3, rows, d), BF16), jax.ShapeDtypeStruct((2, rows, d), F32),
                 jax.ShapeDtypeStruct((2, rows, d), BF16), jax.ShapeDtypeStruct((rows, d), BF16)]
    out_specs = [pl.BlockSpec((3, te, d), lambda i: (0, i, 0)), pl.BlockSpec((2, te, d), lambda i: (0, i, 0)),
                 pl.BlockSpec((2, te, d), lambda i: (0, i, 0)), row_spec]
    if mix:
        v1p, v2p = _pad_rank(vres[1], vres[2])
        in_specs += [full((d, LANES)), full((LANES, d)), full((1, d))]
        args += [v1p, v2p, vres[0].reshape(1, d)]
        out_shape.append(jax.ShapeDtypeStruct((rows, d), BF16))
        out_specs.append(row_spec)
    kern = functools.partial(_rw_prep_kernel, mix=mix, te=te, n_lat_rows=dm["n_lat_rows"], seq=dm["seq"],
                             ctx=dm["ctx"])
    return pl.pallas_call(
        kern,
        out_shape=out_shape,
        grid=(rows // te,),
        in_specs=in_specs,
        out_specs=out_specs,
        compiler_params=_cparams(("parallel",)),
        name="rwkv_prep",
    )(*args)


def _rkv_kernel(*refs, mix):
    if mix:
        xm_ref, w_ref, vf_ref, vg_ref, o_ref = refs
    else:
        xm_ref, w_ref, o_ref = refs
    acc = _dot(xm_ref[0], w_ref[0])
    if mix:
        p = pl.program_id(1)

        @pl.when(p == 2)
        def _():
            o_ref[0] = (acc + (vf_ref[0].astype(F32) - acc) * vg_ref[...].astype(F32)).astype(BF16)

        @pl.when(p != 2)
        def _():
            o_ref[0] = acc.astype(BF16)
    else:
        o_ref[0] = acc.astype(BF16)


def _rkv_proj(xm, w, v_first, vgate, *, layer, dm):
    _, rows, d = xm.shape
    tm = dm["tm"]
    mix = v_first is not None
    in_specs = [pl.BlockSpec((1, tm, d), lambda i, p: (p, i, 0)),
                pl.BlockSpec((None, 1, d, d), lambda i, p: (layer, p, 0, 0))]
    args = [xm, w]
    if mix:
        in_specs += [pl.BlockSpec((1, tm, d), lambda i, p: (2, i, 0)), pl.BlockSpec((tm, d), lambda i, p: (i, 0))]
        args += [v_first, vgate]
    return pl.pallas_call(
        functools.partial(_rkv_kernel, mix=mix),
        out_shape=jax.ShapeDtypeStruct((3, rows, d), BF16),
        grid=(rows // tm, 3),
        in_specs=in_specs,
        out_specs=pl.BlockSpec((1, tm, d), lambda i, p: (p, i, 0)),
        compiler_params=_cparams(("parallel", "arbitrary")),
        name="rwkv_rkv_proj",
    )(*args)


def _wkv_kernel(rf_ref, kf_ref, vf_ref, lwf_ref, af_ref, rb_ref, kb_ref, vb_ref, lwb_ref, ab_ref,
                kk_ref, ka_ref, rk_ref, yf_ref, bonf_ref, yb_ref, bonb_ref, s_scr, *, gps):
    c = pl.program_id(2)
    ln = WKV_CHUNK
    pw = WKV_PACK * RW_HEAD

    hpt = LANES // RW_HEAD

    @pl.when(c == 0)
    def _():
        s_scr[...] = jnp.zeros_like(s_scr)

    row = lax.broadcasted_iota(jnp.int32, (ln, ln), 0)
    col = lax.broadcasted_iota(jnp.int32, (ln, ln), 1)
    trow = lax.broadcasted_iota(jnp.int32, (ln, pw), 0)
    tsrc = lax.broadcasted_iota(jnp.int32, (ln, pw), 1) & (ln - 1)
    ones64 = _ones_blockdiag64()

    def fmask(cond):
        return jnp.where(cond, 1.0, 0.0)

    def same_block(n):
        sh = n.bit_length() - 1
        return (tsrc >> sh) == (trow >> sh)

    eye = fmask(tsrc == trow)
    base_f = fmask(same_block(WKV_INV_BASE))
    off_f = {}
    n = WKV_INV_BASE
    while n < ln:
        off_f[n] = fmask(jnp.logical_and(same_block(2 * n), jnp.logical_not(same_block(n))))
        n *= 2
    k_k = kk_ref[...]
    k_a = ka_ref[...]
    r_k = rk_ref[...]

    lane = lax.broadcasted_iota(jnp.int32, (ln, LANES), 1)
    half_f = [fmask((lane >> (RW_HEAD.bit_length() - 1)) == hf) for hf in range(hpt)]
    half_b = [hm.astype(BF16) for hm in half_f]
    zeros_b = jnp.zeros((ln, LANES), BF16)

    def bd(z):
        zb = z.astype(BF16)
        blocks = []
        for jh in range(WKV_PACK):
            lt, hf = divmod(jh, hpt)
            piece = zb[:, lt * LANES:(lt + 1) * LANES] * half_b[hf]
            blocks.append(jnp.concatenate([piece if tt == lt else zeros_b for tt in range(pw // LANES)], axis=1))
        return jnp.concatenate(blocks, axis=0)

    streams = ((rf_ref, kf_ref, vf_ref, lwf_ref, af_ref, yf_ref, bonf_ref),
               (rb_ref, kb_ref, vb_ref, lwb_ref, ab_ref, yb_ref, bonb_ref))
    units = []
    for dd, (r_ref, k_ref, v_ref, lw_ref, a_ref, y_ref, bon_ref) in enumerate(streams):
        rev = dd == 1
        r = r_ref[0].astype(F32)
        k = k_ref[0].astype(F32)
        v = v_ref[0].astype(F32)
        a = a_ref[0].astype(F32)
        lw = lw_ref[0]
        kkr = k * k_k
        kd = k * (1.0 + (a - 1.0) * k_a)
        ssq, rkd = _segsum64_mxu([kkr * kkr, r * kd * r_k], ones64)
        kk = kkr * lax.rsqrt(jnp.maximum(ssq, 1e-24))
        bvec = kk * a
        bon_ref[...] = (rkd * v).astype(BF16)

        tri = jnp.where((col >= row) if rev else (col <= row), 1.0, 0.0).astype(BF16)
        hi = lw.astype(BF16)
        rem = lw - hi.astype(F32)
        mid = rem.astype(BF16)
        lo = (rem - mid.astype(F32)).astype(BF16)
        cum = _dot(tri, hi) + _dot(tri, mid) + _dot(tri, lo)
        tot = cum[0:1, :] if rev else cum[ln - 1:ln, :]
        w_inv = jnp.exp(-cum)
        w_end = jnp.exp(tot - cum)
        a_t = -kk * jnp.exp(cum - lw)
        r_t = r * jnp.exp(cum)
        b_t = bvec * w_inv
        k_t = kd * w_inv
        b_e = bvec * w_end
        k_e = kd * w_end
        w_tot = jnp.exp(tot)

        strict = fmask(tsrc > trow) if rev else fmask(tsrc < trow)
        incl = strict + eye

        for gi in range(gps):
            sl = slice(gi * pw, (gi + 1) * pw)
            units.append(dict(
                dd=dd, gi=gi, sl=sl, y_ref=y_ref, strict=strict, incl=incl,
                ar=jnp.concatenate([a_t[:, sl], r_t[:, sl]], axis=0).astype(BF16),
                b_t=b_t[:, sl], k_t=k_t[:, sl], v=v[:, sl], w_tot=w_tot[:, sl],
                bk=jnp.concatenate([b_e[:, sl], k_e[:, sl]], axis=0).astype(BF16)))

    for un in units:
        sb = _dot_nt(un["ar"], bd(un["b_t"]))
        sk = _dot_nt(un["ar"], bd(un["k_t"]))
        un["m_ab"] = sb[:ln] * un["strict"]
        un["p_rb"] = sb[ln:] * un["incl"]
        un["m_ak"] = sk[:ln] * un["strict"]
        un["p_rk"] = sk[ln:] * un["incl"]
    for un in units:
        un["s0"] = s_scr[un["dd"], un["gi"]]
        un["ars"] = _dot_nt(un["ar"], un["s0"].astype(BF16))
        un["mv"] = _dot(jnp.concatenate([un["m_ak"], un["p_rk"]], axis=0).astype(BF16), bd(un["v"]))
    for un in units:
        m0 = un["m_ab"] * base_f
        un["pinv"] = eye + m0
        un["mp"] = _dot(m0.astype(BF16), bd(m0))
    for un in units:
        both = _dot(jnp.concatenate([un["mp"], un["pinv"]], axis=0).astype(BF16), bd(un["mp"]))
        un["pinv"] = un["pinv"] + both[ln:]
        un["mp"] = both[:ln]
    for un in units:
        un["pinv"] = un["pinv"] + _dot(un["pinv"].astype(BF16), bd(un["mp"]))
    n = WKV_INV_BASE
    while n < ln:
        for un in units:
            un["t1"] = _dot((un["m_ab"] * off_f[n]).astype(BF16), bd(un["pinv"]))
        for un in units:
            un["pinv"] = un["pinv"] + _dot(un["pinv"].astype(BF16), bd(un["t1"]))
        n *= 2
    for un in units:
        un["u"] = _dot(un["pinv"].astype(BF16), bd(un["ars"][:ln] + un["mv"][:ln]))
    for un in units:
        un["y_ref"][:, un["sl"]] = un["ars"][ln:] + _dot(un["p_rb"].astype(BF16), bd(un["u"])) + un["mv"][ln:]
        uv = jnp.concatenate([un["u"], un["v"]], axis=0).astype(BF16)
        res = _dot_tn(uv, un["bk"])
        for jh in range(WKV_PACK):
            lt, hf = divmod(jh, hpt)
            rsl = slice(jh * RW_HEAD, (jh + 1) * RW_HEAD)
            csl = slice(lt * LANES, (lt + 1) * LANES)
            s_scr[un["dd"], un["gi"], rsl, csl] = (un["s0"][rsl, csl] * un["w_tot"][:, csl]
                                                    + res[rsl, csl] * half_f[hf])


def _wkv(rkv, lw, a, k_k, k_a, r_k, *, dm, gps):
    _, rows, d = rkv.shape
    nb, seq, ctx = dm["nb"], dm["seq"], dm["ctx"]
    ln = WKV_CHUNK
    sw = gps * WKV_PACK * RW_HEAD
    ncc, nlc = ctx // ln, seq // ln
    ctx_c0 = nb * seq // ln

    def fblk(b, c):
        return jnp.where(c < ncc, ctx_c0 + b * ncc + c, b * nlc + (c - ncc))

    def bblk(b, c):
        return jnp.where(c < ncc, ctx_c0 + b * ncc + (ncc - 1 - c), b * nlc + (nlc - 1 - (c - ncc)))

    def spec3(p, blk):
        return pl.BlockSpec((1, ln, sw), lambda b, s, c, p=p, blk=blk: (p, blk(b, c), s))

    def spec2(blk):
        return pl.BlockSpec((ln, sw), lambda b, s, c, blk=blk: (blk(b, c), s))

    pspec = pl.BlockSpec((1, sw), lambda b, s, c: (0, s))
    in_specs = [spec3(0, fblk), spec3(1, fblk), spec3(2, fblk), spec3(0, fblk), spec3(0, fblk),
                spec3(0, bblk), spec3(1, bblk), spec3(2, bblk), spec3(1, bblk), spec3(1, bblk),
                pspec, pspec, pspec]
    return pl.pallas_call(
        functools.partial(_wkv_kernel, gps=gps),
        out_shape=[jax.ShapeDtypeStruct((rows, d), F32), jax.ShapeDtypeStruct((rows, d), BF16),
                   jax.ShapeDtypeStruct((rows, d), F32), jax.ShapeDtypeStruct((rows, d), BF16)],
        grid=(nb, d // sw, ncc + nlc),
        in_specs=in_specs,
        out_specs=[spec2(fblk), spec2(fblk), spec2(bblk), spec2(bblk)],
        scratch_shapes=[pltpu.VMEM((2, gps, WKV_PACK * RW_HEAD, WKV_PACK * RW_HEAD), F32)],
        compiler_params=_cparams(("parallel", "parallel", "arbitrary")),
        name="wkv_scan",
    )(rkv, rkv, rkv, lw, a, rkv, rkv, rkv, lw, a, k_k.reshape(1, d), k_a.reshape(1, d), r_k.reshape(1, d))


def _rw_out_kernel(yf_ref, yb_ref, bf_ref, bb_ref, g_ref, x_ref, mod_ref, lg_ref, lb_ref, w_ref, o_ref):
    y = yf_ref[...] + yb_ref[...]
    ones64 = _ones_blockdiag64()
    mean = _segsum64_mxu([y], ones64)[0] * (1.0 / RW_HEAD)
    yc = y - mean
    var = _segsum64_mxu([yc * yc], ones64)[0] * (1.0 / RW_HEAD)
    yn = yc * lax.rsqrt(var + RW_GN_EPS)
    bonus = bf_ref[...].astype(F32) + bb_ref[...].astype(F32)
    o = (yn * lg_ref[...] + lb_ref[...] + bonus) * g_ref[...].astype(F32)
    o_ref[...] = x_ref[...] + mod_ref[0, 2:3, :] * _dot(o.astype(BF16), w_ref[...])


def _rw_out(yf, bonf, yb, bonb, g, xs, mod, ln_g, ln_b, w_o, *, dm, n_rows):
    d = xs.shape[1]
    te = dm["te"]
    modmap = lambda i: (jnp.minimum((i * te) // dm["seq"], dm["nb"]), 0, 0)
    rspec = pl.BlockSpec((te, d), lambda i: (i, 0))
    vspec = pl.BlockSpec((1, d), lambda i: (0, 0))
    return pl.pallas_call(
        _rw_out_kernel,
        out_shape=jax.ShapeDtypeStruct((n_rows, d), F32),
        grid=(n_rows // te,),
        in_specs=[rspec, rspec, rspec, rspec, rspec, rspec, pl.BlockSpec((1, 6, d), modmap), vspec, vspec,
                  pl.BlockSpec((d, d), lambda i: (0, 0))],
        out_specs=rspec,
        compiler_params=_cparams(("parallel",)),
        name="rwkv_out",
    )(yf, yb, bonf, bonb, g, xs, mod, ln_g.reshape(1, d), ln_b.reshape(1, d), w_o)


def _rope_tables(dm):
    seq, nb, ctx = dm["seq"], dm["nb"], dm["ctx"]
    t = jnp.arange(seq, dtype=jnp.int32)
    pos = jnp.stack([t // GRID_W, t % GRID_W], axis=-1).astype(F32)
    n_freq = ATT_HEAD // 4
    inv = ROPE_THETA ** (-jnp.arange(n_freq, dtype=F32) / n_freq)
    ang = pos[:, :, None] * inv
    cos, sin = jnp.cos(ang), jnp.sin(ang)
    zero = jnp.zeros_like(sin)
    cos_t = jnp.stack([cos, cos], axis=2).reshape(seq, ATT_HEAD)
    sa_t = jnp.stack([-sin, zero], axis=2).reshape(seq, ATT_HEAD)
    sb_t = jnp.stack([zero, sin], axis=2).reshape(seq, ATT_HEAD)
    nctx = nb * ctx
    full = lambda tab, fill: jnp.concatenate([jnp.tile(tab, (nb, 1)), jnp.full((nctx, ATT_HEAD), fill, F32)], axis=0)
    return full(cos_t, 1.0), full(sa_t, 0.0), full(sb_t, 0.0)


def kernel(x, c, ctx, c_ctx, mod_w, mod_b, norm1_g, norm2_g, ffn_up, ffn_conv_w, ffn_conv_b, ffn_down, rw_mu, rw_w_rkv, rw_w0, rw_w1, rw_w2, rw_a0, rw_a1, rw_a2, rw_g1, rw_g2, rw_k_k, rw_k_a, rw_r_k, rw_ln_g, rw_ln_b, rw_w_o, rw_v0, rw_v1, rw_v2, na_w_qkv, na_q_g, na_k_g, na_rpb, na_w_o, ga_w_qkv, ga_q_g, ga_k_g, ga_w_o):
    nb, seq, d = x.shape
    nctx = ctx.shape[1]
    depth = mod_w.shape[0]
    tm = nb * nctx
    assert seq % tm == 0 and seq & (seq - 1) == 0 and nctx & (nctx - 1) == 0 and nb + 1 <= SUBLANES
    assert seq // GRID_W >= NA_WIN_R and nctx % GRID_W == 0
    dm = dict(nb=nb, seq=seq, ctx=nctx, tm=tm, te=tm // 2, tpb=seq // tm, n_lat_rows=nb * seq)
    n_lat_tiles = nb * seq // tm
    n_tiles = n_lat_tiles + 1
    att_scale = ATT_HEAD ** -0.5

    xs = jnp.concatenate([x.reshape(nb * seq, d), ctx.reshape(nb * nctx, d)], axis=0)
    c_all = jnp.concatenate([c, c_ctx[None], jnp.zeros((SUBLANES - nb - 1, d), F32)], axis=0)
    mods = _modulations(c_all, mod_w, mod_b)
    rope_tabs = None
    v_first = None
    ffn_wu, ffn_wd = _ffn_weights(ffn_up, ffn_down)
    rw_rkv_w = rw_w_rkv.astype(BF16)

    for i in range(depth):
        kind, j = i % 3, i // 3
        need_ctx = i < depth - 1
        nt_out = n_tiles if need_ctx else n_lat_tiles
        mod = mods[i, :nb + 1].reshape(nb + 1, 6, d)
        if kind == 0:
            vres = None if j == 0 else (rw_v0[j - 1], rw_v1[j - 1], rw_v2[j - 1])
            prep = _rw_prep(xs, mod, norm1_g[i], rw_mu[j], rw_w0[j], rw_w1[j], rw_w2[j], rw_a0[j], rw_a1[j],
                            rw_a2[j], rw_g1[j], rw_g2[j], vres, dm=dm)
            xm, lw, a, g = prep[0], prep[1], prep[2], prep[3]
            rkv = _rkv_proj(xm, rw_rkv_w, v_first if vres is not None else None,
                            prep[4] if vres is not None else None, layer=j, dm=dm)
            if v_first is None:
                v_first = rkv
            yf, bonf, yb, bonb = _wkv(rkv, lw, a, rw_k_k[j], rw_k_a[j], rw_r_k[j].reshape(-1), dm=dm,
                                      gps=min(8, d // (WKV_PACK * RW_HEAD)))
            xs = _rw_out(yf, bonf, yb, bonb, g, xs, mod, rw_ln_g[j], rw_ln_b[j], rw_w_o[j].astype(BF16),
                         dm=dm, n_rows=nt_out * tm)
        elif kind == 1:
            qkv = _qkv_proj(xs, mod, norm1_g[i], na_w_qkv[j].astype(BF16), na_q_g[j] * att_scale, na_k_g[j],
                            None, dm=dm, kv_dim=d)
            o = _na_attention(qkv, na_rpb[j], dm=dm, d=d)
            xs = _out_proj(o, na_w_o[j].astype(BF16), xs, mod, dm=dm, n_tiles=nt_out)
        else:
            if rope_tabs is None:
                rope_tabs = _rope_tables(dm)
            kv_dim = (ga_w_qkv.shape[-1] - d) // 2
            qkv = _qkv_proj(xs, mod, norm1_g[i], ga_w_qkv[j].astype(BF16), ga_q_g[j] * att_scale, ga_k_g[j],
                            rope_tabs, dm=dm, kv_dim=kv_dim)
            o = _gqa_attention(qkv, dm=dm, d=d, kv_heads=kv_dim // ATT_HEAD)
            xs = _out_proj(o, ga_w_o[j].astype(BF16), xs, mod, dm=dm, n_tiles=nt_out)
        xs = _ffn(xs, mod, norm2_g[i], ffn_wu, ffn_conv_w[i], ffn_conv_b[i], ffn_wd, layer=i, dm=dm,
                  need_ctx=need_ctx)
    return xs[:nb * seq].reshape(nb, seq, d)
```

```python
import functools
import math

import jax
import jax.numpy as jnp
from jax import lax
from jax.experimental import pallas as pl
from jax.experimental.pallas import tpu as pltpu

F32 = jnp.float32
BF16 = jnp.bfloat16

NORM_EPS = 1e-6
GRID_W = 64
ATT_HEAD = 128
RW_HEAD = 64
NA_WIN_R = 8
NA_WIN_C = 16
ROPE_THETA = 10000.0
RW_GN_EPS = 64e-5
LANES = 128
SUBLANES = 8
WKV_CHUNK = 64
WKV_PACK = 4
WKV_INV_BASE = 8
FFN_ROW_PIECES = 1
VMEM_LIMIT = 56 * 1024 * 1024
NEG_BIG = -1e30


def _cparams(sem):
    return pltpu.CompilerParams(dimension_semantics=sem, vmem_limit_bytes=VMEM_LIMIT)


def _dot(a, b):
    return jnp.dot(a, b, preferred_element_type=F32)


def _dot_nt(a, b):
    return lax.dot_general(a, b, (((1,), (1,)), ((), ())), preferred_element_type=F32)


def _dot_tn(a, b):
    return lax.dot_general(a, b, (((0,), (0,)), ((), ())), preferred_element_type=F32)


def _normmod(x, g, shift, scale):
    ms = jnp.mean(x * x, axis=-1, keepdims=True)
    y = x * lax.rsqrt(ms + NORM_EPS)
    return (y * g) * (1.0 + scale) + shift


def _silu(x):
    return x * jax.nn.sigmoid(x)


def _seq_edges(tile, rows, n_lat_rows, seq, ctx):
    rid = lax.broadcasted_iota(jnp.int32, (rows, 1), 0)
    base = tile * rows
    period = jnp.where(base >= n_lat_rows, ctx, seq)
    pos = (base + rid) & (period - 1)
    return rid, pos == 0, pos == period - 1


def _shift_rows(u, rid, first, last, prev_row, next_row):
    n = u.shape[0]
    up = pltpu.roll(u, 1, axis=0)
    up = jnp.where(rid == 0, prev_row, up)
    up = jnp.where(first, 0.0, up)
    un = pltpu.roll(u, n - 1, axis=0)
    un = jnp.where(rid == n - 1, next_row, un)
    un = jnp.where(last, 0.0, un)
    return up, un


def _ones_blockdiag64():
    sh = RW_HEAD.bit_length() - 1
    r = lax.broadcasted_iota(jnp.int32, (LANES, LANES), 0) >> sh
    c = lax.broadcasted_iota(jnp.int32, (LANES, LANES), 1) >> sh
    return jnp.where(r == c, 1.0, 0.0).astype(BF16)


def _segsum64_mxu(xs, ones, split=True):
    m, n = xs[0].shape
    nslab = n // LANES
    nparts = 2 if split else 1
    pieces = []
    for x in xs:
        hi = x.astype(BF16)
        parts = (hi, (x - hi.astype(F32)).astype(BF16)) if split else (hi,)
        for part in parts:
            pieces += [part[:, c * LANES:(c + 1) * LANES] for c in range(nslab)]
    res = _dot(jnp.concatenate(pieces, axis=0), ones)
    outs = []
    for i in range(len(xs)):
        base = i * nparts * nslab
        cols = []
        for c in range(nslab):
            col = res[(base + c) * m:(base + c + 1) * m]
            if split:
                col = col + res[(base + nslab + c) * m:(base + nslab + c + 1) * m]
            cols.append(col)
        outs.append(jnp.concatenate(cols, axis=1))
    return outs


def _mod_kernel(c_ref, w_ref, b_ref, o_ref):
    s = _silu(c_ref[...]).astype(BF16)
    o_ref[0] = _dot(s, w_ref[0].astype(BF16)) + b_ref[0]


def _modulations(c_all, mod_w, mod_b):
    depth, d, n = mod_w.shape
    tn = n // 8
    return pl.pallas_call(
        _mod_kernel,
        out_shape=jax.ShapeDtypeStruct((depth, SUBLANES, n), F32),
        grid=(depth, n // tn),
        in_specs=[
            pl.BlockSpec((SUBLANES, d), lambda l, j: (0, 0)),
            pl.BlockSpec((1, d, tn), lambda l, j: (l, 0, j)),
            pl.BlockSpec((1, 1, tn), lambda l, j: (l, 0, j)),
        ],
        out_specs=pl.BlockSpec((1, SUBLANES, tn), lambda l, j: (l, 0, j)),
        compiler_params=_cparams(("parallel", "parallel")),
        name="modulation",
    )(c_all, mod_w, mod_b.reshape(depth, 1, n))


def _ffn_kernel(*refs, tm, tile_off, aliased, n_lat_rows, seq, ctx):
    (x_ref, xp_ref, xn_ref, mod_ref, g_ref, wug_ref, wuv_ref, cwg_ref, cwv_ref, cbg_ref, cbv_ref,
     wd_ref) = refs[:12]
    o_ref, h_scr = refs[-2:]
    i = pl.program_id(0) + tile_off
    j = pl.program_id(1)
    shift = mod_ref[0, 3:4, :]
    scale = mod_ref[0, 4:5, :]
    hr = 2 * SUBLANES
    th = tm // FFN_ROW_PIECES

    @pl.when(j == 0)
    def _():
        g = g_ref[...]
        halo = jnp.concatenate([xp_ref[...], xn_ref[...]], axis=0)
        h_scr[0:hr, :] = _normmod(halo, g, shift, scale).astype(BF16)
        h_scr[hr:hr + tm, :] = _normmod(x_ref[...], g, shift, scale).astype(BF16)
        o_ref[...] = jnp.zeros_like(o_ref)

    rid, first, last = _seq_edges(i, tm, n_lat_rows, seq, ctx)
    rid_h = rid[0:th]

    def up_proj(p):
        lo = 0 if p == 0 else hr + p * th
        hp = h_scr[lo:hr + (p + 1) * th, :]
        ug, uv = _dot(hp, wug_ref[...]), _dot(hp, wuv_ref[...])
        if p == 0:
            return dict(g=ug[hr:], v=uv[hr:], halo_g=ug[0:hr], halo_v=uv[0:hr])
        return dict(g=ug, v=uv)

    def conv(main, prev_row, next_row, fm, lm, cw_ref, cb_ref):
        up, un = _shift_rows(main, rid_h, fm, lm, prev_row, next_row)
        return cb_ref[...] + up * cw_ref[0:1, :] + main * cw_ref[1:2, :] + un * cw_ref[2:3, :]

    pv, nx = SUBLANES - 1, SUBLANES

    def act_down(p, us):
        rs = slice(p * th, (p + 1) * th)
        rows = {}
        for key in ("g", "v"):
            prev_row = us[0]["halo_" + key][pv:pv + 1] if p == 0 else us[p - 1][key][th - 1:th]
            next_row = us[0]["halo_" + key][nx:nx + 1] if p == FFN_ROW_PIECES - 1 else us[p + 1][key][0:1]
            rows[key] = (prev_row, next_row)
        a = (_silu(conv(us[p]["g"], *rows["g"], first[rs], last[rs], cwg_ref, cbg_ref))
             * conv(us[p]["v"], *rows["v"], first[rs], last[rs], cwv_ref, cbv_ref))
        o_ref[rs, :] += _dot(a.astype(BF16), wd_ref[...])

    us = {0: up_proj(0)}
    for p in range(FFN_ROW_PIECES):
        if p + 1 < FFN_ROW_PIECES:
            us[p + 1] = up_proj(p + 1)
        act_down(p, us)

    @pl.when(j == pl.num_programs(1) - 1)
    def _():
        o_ref[...] = x_ref[...] + mod_ref[0, 5:6, :] * o_ref[...]


FFN_COLS = 512


def _ffn_weights(ffn_up, ffn_down):
    return ffn_up.astype(BF16), ffn_down.astype(BF16)


def _ffn_call(xs, mod, g2, wu, cw, cb, wd, prev, *, layer, dm, tm, tile_off, n_tiles, out_rows):
    rows, d = xs.shape
    f = wd.shape[1]
    fc = FFN_COLS
    nfc = f // fc
    hb = tm // SUBLANES
    last_hb = rows // SUBLANES - 1
    seq, nb = dm["seq"], dm["nb"]
    assert seq % tm == 0 or tile_off * tm >= dm["n_lat_rows"]
    modmap = lambda i, j: (jnp.minimum(((i + tile_off) * tm) // seq, nb), 0, 0)
    kern = functools.partial(_ffn_kernel, tm=tm, tile_off=tile_off, aliased=prev is not None,
                             n_lat_rows=dm["n_lat_rows"], seq=seq, ctx=dm["ctx"])
    in_specs = [
        pl.BlockSpec((tm, d), lambda i, j: (i + tile_off, 0), pipeline_mode=pl.Buffered(1)),
        pl.BlockSpec((SUBLANES, d), lambda i, j: (jnp.maximum((i + tile_off) * hb - 1, 0), 0)),
        pl.BlockSpec((SUBLANES, d), lambda i, j: (jnp.minimum((i + tile_off + 1) * hb, last_hb), 0)),
        pl.BlockSpec((1, 6, d), modmap),
        pl.BlockSpec((1, d), lambda i, j: (0, 0)),
        pl.BlockSpec((None, d, fc), lambda i, j: (layer, 0, j)),
        pl.BlockSpec((None, d, fc), lambda i, j: (layer, 0, nfc + j)),
        pl.BlockSpec((3, fc), lambda i, j: (0, j)),
        pl.BlockSpec((3, fc), lambda i, j: (0, nfc + j)),
        pl.BlockSpec((1, fc), lambda i, j: (0, j)),
        pl.BlockSpec((1, fc), lambda i, j: (0, nfc + j)),
        pl.BlockSpec((None, fc, d), lambda i, j: (layer, j, 0)),
    ]
    args = [xs, xs, xs, mod, g2.reshape(1, d), wu, wu, cw, cw, cb.reshape(1, -1), cb.reshape(1, -1), wd]
    aliases = {}
    if prev is not None:
        in_specs.append(pl.BlockSpec(memory_space=pl.ANY))
        args.append(prev)
        aliases = {len(args) - 1: 0}
    return pl.pallas_call(
        kern,
        out_shape=jax.ShapeDtypeStruct((out_rows, d), F32),
        grid=(n_tiles, nfc),
        in_specs=in_specs,
        out_specs=pl.BlockSpec((tm, d), lambda i, j: (i + tile_off, 0)),
        scratch_shapes=[pltpu.VMEM((tm + 2 * SUBLANES, d), BF16)],
        input_output_aliases=aliases,
        compiler_params=_cparams(("parallel", "arbitrary")),
        name="conv_ffn",
    )(*args)


def _ffn(xs, mod, g2, wu, cw, cb, wd, *, layer, dm, need_ctx):
    rows = xs.shape[0]
    n_lat_rows, tm = dm["n_lat_rows"], dm["tm"]
    big = 2 * tm
    out_rows = rows if need_ctx else n_lat_rows
    y = _ffn_call(xs, mod, g2, wu, cw, cb, wd, None, layer=layer, dm=dm, tm=big, tile_off=0,
                  n_tiles=n_lat_rows // big, out_rows=out_rows)
    if need_ctx:
        y = _ffn_call(xs, mod, g2, wu, cw, cb, wd, y, layer=layer, dm=dm, tm=tm, tile_off=n_lat_rows // tm,
                      n_tiles=1, out_rows=out_rows)
    return y


def _qkv_kernel(*refs, block_kinds, rope):
    if rope:
        x_ref, mod_ref, g_ref, w_ref, qg_ref, kg_ref, cos_ref, sa_ref, sb_ref, o_ref, h_scr = refs
    else:
        x_ref, mod_ref, g_ref, w_ref, qg_ref, kg_ref, o_ref, h_scr = refs
    j = pl.program_id(1)

    @pl.when(j == 0)
    def _():
        h_scr[...] = _normmod(x_ref[...], g_ref[...], mod_ref[0, 0:1, :], mod_ref[0, 1:2, :]).astype(BF16)

    acc = _dot(h_scr[...], w_ref[...])

    def emit(kinds):
        for hh, kind in enumerate(kinds):
            hs = slice(hh * ATT_HEAD, (hh + 1) * ATT_HEAD)
            y = acc[:, hs]
            if kind != "v":
                gain = qg_ref[...] if kind == "q" else kg_ref[...]
                y = y * lax.rsqrt(jnp.mean(y * y, axis=-1, keepdims=True) + NORM_EPS) * gain
                if rope:
                    y = (y * cos_ref[...] + pltpu.roll(y, ATT_HEAD - 32, axis=1) * sa_ref[...]
                         + pltpu.roll(y, 32, axis=1) * sb_ref[...])
            o_ref[:, hs] = y.astype(BF16)

    for jb, kinds in enumerate(block_kinds):
        pl.when(j == jb)(functools.partial(emit, kinds))


def _qkv_proj(xs, mod, g1, w, qg, kg, rope_tabs, *, dm, kv_dim):
    rows, d = xs.shape
    n = w.shape[1]
    tn = min(d, 2 * kv_dim)
    hpb = tn // ATT_HEAD
    kinds = ["q"] * (d // ATT_HEAD) + ["k"] * (kv_dim // ATT_HEAD) + ["v"] * (kv_dim // ATT_HEAD)
    block_kinds = tuple(tuple(kinds[b * hpb:(b + 1) * hpb]) for b in range(n // tn))
    tm = dm["tm"]
    modmap = lambda i, j: (jnp.minimum(i // dm["tpb"], dm["nb"]), 0, 0)
    in_specs = [
        pl.BlockSpec((tm, d), lambda i, j: (i, 0)),
        pl.BlockSpec((1, 6, d), modmap),
        pl.BlockSpec((1, d), lambda i, j: (0, 0)),
        pl.BlockSpec((d, tn), lambda i, j: (0, j)),
        pl.BlockSpec((1, ATT_HEAD), lambda i, j: (0, 0)),
        pl.BlockSpec((1, ATT_HEAD), lambda i, j: (0, 0)),
    ]
    args = [xs, mod, g1.reshape(1, d), w, qg.reshape(1, ATT_HEAD), kg.reshape(1, ATT_HEAD)]
    if rope_tabs is not None:
        in_specs += [pl.BlockSpec((tm, ATT_HEAD), lambda i, j: (i, 0))] * 3
        args += list(rope_tabs)
    kern = functools.partial(_qkv_kernel, block_kinds=block_kinds, rope=rope_tabs is not None)
    return pl.pallas_call(
        kern,
        out_shape=jax.ShapeDtypeStruct((rows, n), BF16),
        grid=(rows // tm, n // tn),
        in_specs=in_specs,
        out_specs=pl.BlockSpec((tm, tn), lambda i, j: (i, j)),
        scratch_shapes=[pltpu.VMEM((tm, d), BF16)],
        compiler_params=_cparams(("parallel", "arbitrary")),
        name="qkv_proj",
    )(*args)


def _oproj_kernel(a_ref, w_ref, x_ref, mod_ref, o_ref):
    o_ref[...] = x_ref[...] + mod_ref[0, 2:3, :] * _dot(a_ref[...], w_ref[...])


def _out_proj(a, w, xs, mod, *, dm, n_tiles):
    d = xs.shape[1]
    tm = dm["tm"]
    modmap = lambda i: (jnp.minimum(i // dm["tpb"], dm["nb"]), 0, 0)
    return pl.pallas_call(
        _oproj_kernel,
        out_shape=jax.ShapeDtypeStruct((n_tiles * tm, d), F32),
        grid=(n_tiles,),
        in_specs=[
            pl.BlockSpec((tm, d), lambda i: (i, 0)),
            pl.BlockSpec((d, d), lambda i: (0, 0)),
            pl.BlockSpec((tm, d), lambda i: (i, 0)),
            pl.BlockSpec((1, 6, d), modmap),
        ],
        out_specs=pl.BlockSpec((tm, d), lambda i: (i, 0)),
        compiler_params=_cparams(("parallel",)),
        name="out_proj",
    )(a, w, xs, mod)


def _softmax_pv(q, segs):
    ss = [_dot_nt(q, k) for k, _ in segs]
    m = ss[0].max(axis=-1, keepdims=True)
    for s in ss[1:]:
        m = jnp.maximum(m, s.max(axis=-1, keepdims=True))
    ps = [jnp.exp2(s - m) for s in ss]
    l = ps[0].sum(axis=-1, keepdims=True)
    for p in ps[1:]:
        l = l + p.sum(axis=-1, keepdims=True)
    o = _dot(ps[0].astype(BF16), segs[0][1])
    for p, (_, v) in zip(ps[1:], segs[1:]):
        o = o + _dot(p.astype(BF16), v)
    return o / l


def _lane_tiles(x, op):
    acc = x[:, 0:LANES]
    for j in range(1, x.shape[1] // LANES):
        acc = op(acc, x[:, j * LANES:(j + 1) * LANES])
    return acc


def _gqa_kernel(q_ref, kl_ref, vl_ref, kc_ref, vc_ref, o_ref, s_scr, *, group, n_lat_tiles, kchunk):
    t = pl.program_id(2)
    seq, ctx = kl_ref.shape[0], kc_ref.shape[0]

    def run(chunks):
        m = [None] * group
        mrun = [None] * group
        lrun = [None] * group
        o = [None] * group
        for g in range(group + 1):
            for k_ref, v_ref, st, sz, off in chunks:
                if g < group:
                    s = _dot_nt(q_ref[:, g * ATT_HEAD:(g + 1) * ATT_HEAD], k_ref[st:st + sz, :])
                    s_scr[g % 2, :, off:off + sz] = s
                    tmax = _lane_tiles(s, jnp.maximum)
                    mrun[g] = tmax if mrun[g] is None else jnp.maximum(mrun[g], tmax)
                if g >= 1:
                    h = g - 1
                    p = jnp.exp2(s_scr[h % 2, :, off:off + sz] - m[h])
                    psum = _lane_tiles(p, jnp.add)
                    pv = _dot(p.astype(BF16), v_ref[st:st + sz, :])
                    lrun[h] = psum if lrun[h] is None else lrun[h] + psum
                    o[h] = pv if o[h] is None else o[h] + pv
            if g < group:
                m[g] = mrun[g].max(axis=-1, keepdims=True)
        for g in range(group):
            l = lrun[g].sum(axis=-1, keepdims=True)
            o_ref[:, g * ATT_HEAD:(g + 1) * ATT_HEAD] = (o[g] / l).astype(BF16)

    lat_chunks = [(kl_ref, vl_ref, st, kchunk, st) for st in range(0, seq, kchunk)]
    ctx_chunk = (kc_ref, vc_ref, 0, ctx, seq)

    @pl.when(t < n_lat_tiles)
    def _():
        run(lat_chunks + [ctx_chunk])

    @pl.when(t >= n_lat_tiles)
    def _():
        run([ctx_chunk])


def _gqa_attention(qkv, *, dm, d, kv_heads):
    rows = qkv.shape[0]
    nb, seq, ctx = dm["nb"], dm["seq"], dm["ctx"]
    group = d // ATT_HEAD // kv_heads
    gw = group * ATT_HEAD
    tq = min(256, ctx)
    nlt, nct = seq // tq, ctx // tq
    kcol = d // ATT_HEAD
    vcol = kcol + kv_heads
    ctx_blk0 = nb * seq // ctx

    def qmap(b, h, t):
        return (jnp.where(t < nlt, b * nlt + t, nb * nlt + b * nct + (t - nlt)), h)

    kern = functools.partial(_gqa_kernel, group=group, n_lat_tiles=nlt, kchunk=min(512, seq))
    return pl.pallas_call(
        kern,
        out_shape=jax.ShapeDtypeStruct((rows, d), BF16),
        scratch_shapes=[pltpu.VMEM((2, tq, seq + ctx), F32)],
        grid=(nb, kv_heads, nlt + nct),
        in_specs=[
            pl.BlockSpec((tq, gw), qmap),
            pl.BlockSpec((seq, ATT_HEAD), lambda b, h, t: (b, kcol + h)),
            pl.BlockSpec((seq, ATT_HEAD), lambda b, h, t: (b, vcol + h)),
            pl.BlockSpec((ctx, ATT_HEAD), lambda b, h, t: (ctx_blk0 + b, kcol + h)),
            pl.BlockSpec((ctx, ATT_HEAD), lambda b, h, t: (ctx_blk0 + b, vcol + h)),
        ],
        out_specs=pl.BlockSpec((tq, gw), qmap),
        compiler_params=_cparams(("parallel", "parallel", "arbitrary")),
        name="gqa_attention",
    )(qkv, qkv, qkv, qkv, qkv)


def _na_kernel(q_ref, k_ref, v_ref, kc_ref, vc_ref, bias_ref, o_ref, *, rb, hps, grid_rows, n_row_blocks):
    t = pl.program_id(2)
    win = NA_WIN_R * GRID_W

    @pl.when(t < n_row_blocks)
    def _():
        units = []
        for hh in range(hps):
            hs = slice(hh * ATT_HEAD, (hh + 1) * ATT_HEAD)
            for rr in range(rb):
                r = t * rb + rr
                rs = jnp.clip(r - NA_WIN_R // 2, 0, grid_rows - NA_WIN_R)
                units.append(dict(hh=hh, hs=hs, rows=slice(rr * GRID_W, (rr + 1) * GRID_W), off=r - rs,
                                  start=pl.multiple_of(rs * GRID_W, GRID_W)))
        for un in units:
            q = q_ref[un["rows"], un["hs"]]
            un["sw"] = _dot_nt(q, k_ref[pl.ds(un["start"], win), un["hs"]]) + bias_ref[un["hh"], un["off"]]
            un["sc"] = _dot_nt(q, kc_ref[:, un["hs"]])
        for un in units:
            m = jnp.maximum(un["sw"].max(axis=-1, keepdims=True), un["sc"].max(axis=-1, keepdims=True))
            un["pw"] = jnp.exp2(un["sw"] - m)
            un["pc"] = jnp.exp2(un["sc"] - m)
        for un in units:
            l = un["pw"].sum(axis=-1, keepdims=True) + un["pc"].sum(axis=-1, keepdims=True)
            o = (_dot(un["pw"].astype(BF16), v_ref[pl.ds(un["start"], win), un["hs"]])
                 + _dot(un["pc"].astype(BF16), vc_ref[:, un["hs"]]))
            o_ref[un["rows"], un["hs"]] = (o / l).astype(BF16)

    @pl.when(t >= n_row_blocks)
    def _():
        for hh in range(hps):
            hs = slice(hh * ATT_HEAD, (hh + 1) * ATT_HEAD)
            o_ref[:, hs] = _softmax_pv(q_ref[:, hs], [(kc_ref[:, hs], vc_ref[:, hs])]).astype(BF16)


def _na_bias_table(rpb):
    qc = jnp.arange(GRID_W)
    kc = jnp.arange(GRID_W)
    cs = jnp.clip(qc - NA_WIN_C // 2, 0, GRID_W - NA_WIN_C)
    inwin = (kc[None, :] >= cs[:, None]) & (kc[None, :] < cs[:, None] + NA_WIN_C)
    cidx = kc[None, :] - qc[:, None] + NA_WIN_C - 1
    sel = (cidx[None] == jnp.arange(2 * NA_WIN_C - 1)[:, None, None]) & inwin[None]
    cols = jnp.einsum('hrc,cqk->hrqk', rpb, sel.astype(F32), precision=lax.Precision.HIGHEST)
    cols = jnp.where(inwin[None, None], cols * math.log2(math.e), NEG_BIG)
    tab = jnp.stack([cols[:, NA_WIN_R - 1 - o:2 * NA_WIN_R - 1 - o] for o in range(NA_WIN_R)], axis=1)
    tab = tab.transpose(0, 1, 3, 2, 4)
    return tab.reshape(rpb.shape[0], NA_WIN_R, GRID_W, NA_WIN_R * GRID_W).astype(F32)


def _na_attention(qkv, rpb, *, dm, d):
    rows = qkv.shape[0]
    nb, seq, ctx = dm["nb"], dm["seq"], dm["ctx"]
    heads = d // ATT_HEAD
    grid_rows = seq // GRID_W
    rb = ctx // GRID_W
    nrb = grid_rows // rb
    ctx_blk0 = nb * seq // ctx
    bias = _na_bias_table(rpb)

    def qmap(b, h, t):
        return (jnp.where(t < nrb, b * nrb + t, ctx_blk0 + b), h)

    hps = 4 if heads % 4 == 0 else 2
    hw = hps * ATT_HEAD
    ng = heads // hps
    kern = functools.partial(_na_kernel, rb=rb, hps=hps, grid_rows=grid_rows, n_row_blocks=nrb)
    return pl.pallas_call(
        kern,
        out_shape=jax.ShapeDtypeStruct((rows, d), BF16),
        grid=(nb, ng, nrb + 1),
        in_specs=[
            pl.BlockSpec((ctx, hw), qmap),
            pl.BlockSpec((seq, hw), lambda b, h, t: (b, ng + h)),
            pl.BlockSpec((seq, hw), lambda b, h, t: (b, 2 * ng + h)),
            pl.BlockSpec((ctx, hw), lambda b, h, t: (ctx_blk0 + b, ng + h)),
            pl.BlockSpec((ctx, hw), lambda b, h, t: (ctx_blk0 + b, 2 * ng + h)),
            pl.BlockSpec((hps, NA_WIN_R, GRID_W, NA_WIN_R * GRID_W), lambda b, h, t: (h, 0, 0, 0)),
        ],
        out_specs=pl.BlockSpec((ctx, hw), qmap),
        compiler_params=_cparams(("parallel", "parallel", "arbitrary")),
        name="na_attention",
    )(qkv, qkv, qkv, qkv, qkv, bias)


def _rw_prep_kernel(*refs, mix, te, n_lat_rows, seq, ctx):
    (x_ref, xp_ref, xn_ref, mod_ref, g_ref, mu_ref, w1_ref, a1_ref, g1_ref, w2_ref, a2_ref, g2_ref,
     w0_ref, a0_ref) = refs[:14]
    if mix:
        v1_ref, v2_ref, v0_ref, xm_ref, lw_ref, a_ref, go_ref, vg_ref = refs[14:]
    else:
        xm_ref, lw_ref, a_ref, go_ref = refs[14:]
    i = pl.program_id(0)
    g = g_ref[...]
    shift = mod_ref[0, 0:1, :]
    scale = mod_ref[0, 1:2, :]
    h = _normmod(x_ref[...], g, shift, scale)
    halo = _normmod(jnp.concatenate([xp_ref[...], xn_ref[...]], axis=0), g, shift, scale)
    rid, first, last = _seq_edges(i, te, n_lat_rows, seq, ctx)
    hp, hn = _shift_rows(h, rid, first, last, halo[SUBLANES - 1:SUBLANES], halo[SUBLANES:SUBLANES + 1])
    xx = 0.5 * (hp + hn) - h

    def mixed(p):
        return (h + xx * mu_ref[p:p + 1, :]).astype(BF16)

    xv = mixed(2)
    xm_ref[0] = mixed(0)
    xm_ref[1] = mixed(1)
    xm_ref[2] = xv
    zw = jnp.tanh(_dot(mixed(3), w1_ref[...])).astype(BF16)
    za = _dot(mixed(4), a1_ref[...]).astype(BF16)
    zg = jax.nn.sigmoid(_dot(mixed(5), g1_ref[...])).astype(BF16)
    for dd in range(2):
        sl = slice(dd * LANES, (dd + 1) * LANES)
        wl = w0_ref[dd:dd + 1, :] + _dot(zw[:, sl], w2_ref[dd])
        lw_ref[dd] = (-math.exp(-0.5)) * jax.nn.sigmoid(wl)
        a_ref[dd] = jax.nn.sigmoid(a0_ref[dd:dd + 1, :] + _dot(za[:, sl], a2_ref[dd])).astype(BF16)
    go_ref[...] = _dot(zg, g2_ref[...]).astype(BF16)
    if mix:
        zv = _dot(xv, v1_ref[...]).astype(BF16)
        vg_ref[...] = jax.nn.sigmoid(v0_ref[...] + _dot(zv, v2_ref[...])).astype(BF16)


def _pad_rank(w1, w2):
    r = w1.shape[-1]
    pad = (-r) % LANES
    w1 = jnp.pad(w1, [(0, 0)] * (w1.ndim - 1) + [(0, pad)])
    w2 = jnp.pad(w2, [(0, 0)] * (w2.ndim - 2) + [(0, pad), (0, 0)])
    return w1.astype(BF16), w2.astype(BF16)


def _rw_prep(xs, mod, g1n, mu, w0, w1, w2, a0, a1, a2, g1, g2, vres, *, dm):
    rows, d = xs.shape
    te = dm["te"]
    hb = te // SUBLANES
    last_hb = rows // SUBLANES - 1
    mix = vres is not None
    w1p, w2p = _pad_rank(w1, w2)
    a1p, a2p = _pad_rank(a1, a2)
    w1c = jnp.concatenate([w1p[0], w1p[1]], axis=1)
    a1c = jnp.concatenate([a1p[0], a1p[1]], axis=1)
    rg = g1.shape[1]
    modmap = lambda i: (jnp.minimum((i * te) // dm["seq"], dm["nb"]), 0, 0)
    full = lambda shp: pl.BlockSpec(shp, lambda i: (0,) * len(shp))
    in_specs = [
        pl.BlockSpec((te, d), lambda i: (i, 0)),
        pl.BlockSpec((SUBLANES, d), lambda i: (jnp.maximum(i * hb - 1, 0), 0)),
        pl.BlockSpec((SUBLANES, d), lambda i: (jnp.minimum((i + 1) * hb, last_hb), 0)),
        pl.BlockSpec((1, 6, d), modmap),
        full((1, d)), full((6, d)),
        full((d, 2 * LANES)), full((d, 2 * LANES)), full((d, rg)),
        full((2, LANES, d)), full((2, LANES, d)), full((rg, d)),
        full((2, d)), full((2, d)),
    ]
    args = [xs, xs, xs, mod, g1n.reshape(1, d), mu, w1c, a1c, g1.astype(BF16), w2p, a2p, g2.astype(BF16), w0, a0]
    row_spec = pl.BlockSpec((te, d), lambda i: (i, 0))
    out_shape = [jax.ShapeDtypeStruct((3, rows, d), BF16), jax.ShapeDtypeStruct((2, rows, d), F32),
                 jax.ShapeDtypeStruct((2, rows, d), BF16), jax.ShapeDtypeStruct((rows, d), BF16)]
    out_specs = [pl.BlockSpec((3, te, d), lambda i: (0, i, 0)), pl.BlockSpec((2, te, d), lambda i: (0, i, 0)),
                 pl.BlockSpec((2, te, d), lambda i: (0, i, 0)), row_spec]
    if mix:
        v1p, v2p = _pad_rank(vres[1], vres[2])
        in_specs += [full((d, LANES)), full((LANES, d)), full((1, d))]
        args += [v1p, v2p, vres[0].reshape(1, d)]
        out_shape.append(jax.ShapeDtypeStruct((rows, d), BF16))
        out_specs.append(row_spec)
    kern = functools.partial(_rw_prep_kernel, mix=mix, te=te, n_lat_rows=dm["n_lat_rows"], seq=dm["seq"],
                             ctx=dm["ctx"])
    return pl.pallas_call(
        kern,
        out_shape=out_shape,
        grid=(rows // te,),
        in_specs=in_specs,
        out_specs=out_specs,
        compiler_params=_cparams(("parallel",)),
        name="rwkv_prep",
    )(*args)


def _rkv_kernel(*refs, mix):
    if mix:
        xm_ref, w_ref, vf_ref, vg_ref, o_ref = refs
    else:
        xm_ref, w_ref, o_ref = refs
    acc = _dot(xm_ref[0], w_ref[0])
    if mix:
        p = pl.program_id(1)

        @pl.when(p == 2)
        def _():
            o_ref[0] = (acc + (vf_ref[0].astype(F32) - acc) * vg_ref[...].astype(F32)).astype(BF16)

        @pl.when(p != 2)
        def _():
            o_ref[0] = acc.astype(BF16)
    else:
        o_ref[0] = acc.astype(BF16)


def _rkv_proj(xm, w, v_first, vgate, *, layer, dm):
    _, rows, d = xm.shape
    tm = dm["tm"]
    mix = v_first is not None
    in_specs = [pl.BlockSpec((1, tm, d), lambda i, p: (p, i, 0)),
                pl.BlockSpec((None, 1, d, d), lambda i, p: (layer, p, 0, 0))]
    args = [xm, w]
    if mix:
        in_specs += [pl.BlockSpec((1, tm, d), lambda i, p: (2, i, 0)), pl.BlockSpec((tm, d), lambda i, p: (i, 0))]
        args += [v_first, vgate]
    return pl.pallas_call(
        functools.partial(_rkv_kernel, mix=mix),
        out_shape=jax.ShapeDtypeStruct((3, rows, d), BF16),
        grid=(rows // tm, 3),
        in_specs=in_specs,
        out_specs=pl.BlockSpec((1, tm, d), lambda i, p: (p, i, 0)),
        compiler_params=_cparams(("parallel", "arbitrary")),
        name="rwkv_rkv_proj",
    )(*args)


def _wkv_kernel(rf_ref, kf_ref, vf_ref, lwf_ref, af_ref, rb_ref, kb_ref, vb_ref, lwb_ref, ab_ref,
                kk_ref, ka_ref, rk_ref, yf_ref, bonf_ref, yb_ref, bonb_ref, s_scr, *, gps):
    c = pl.program_id(2)
    ln = WKV_CHUNK
    pw = WKV_PACK * RW_HEAD

    hpt = LANES // RW_HEAD

    @pl.when(c == 0)
    def _():
        s_scr[...] = jnp.zeros_like(s_scr)

    row = lax.broadcasted_iota(jnp.int32, (ln, ln), 0)
    col = lax.broadcasted_iota(jnp.int32, (ln, ln), 1)
    trow = lax.broadcasted_iota(jnp.int32, (ln, pw), 0)
    tsrc = lax.broadcasted_iota(jnp.int32, (ln, pw), 1) & (ln - 1)
    ones64 = _ones_blockdiag64()

    def fmask(cond):
        return jnp.where(cond, 1.0, 0.0)

    def same_block(n):
        sh = n.bit_length() - 1
        return (tsrc >> sh) == (trow >> sh)

    eye = fmask(tsrc == trow)
    base_f = fmask(same_block(WKV_INV_BASE))
    off_f = {}
    n = WKV_INV_BASE
    while n < ln:
        off_f[n] = fmask(jnp.logical_and(same_block(2 * n), jnp.logical_not(same_block(n))))
        n *= 2
    k_k = kk_ref[...]
    k_a = ka_ref[...]
    r_k = rk_ref[...]

    lane = lax.broadcasted_iota(jnp.int32, (ln, LANES), 1)
    half_f = [fmask((lane >> (RW_HEAD.bit_length() - 1)) == hf) for hf in range(hpt)]
    half_b = [hm.astype(BF16) for hm in half_f]
    zeros_b = jnp.zeros((ln, LANES), BF16)

    def bd(z):
        zb = z.astype(BF16)
        blocks = []
        for jh in range(WKV_PACK):
            lt, hf = divmod(jh, hpt)
            piece = zb[:, lt * LANES:(lt + 1) * LANES] * half_b[hf]
            blocks.append(jnp.concatenate([piece if tt == lt else zeros_b for tt in range(pw // LANES)], axis=1))
        return jnp.concatenate(blocks, axis=0)

    streams = ((rf_ref, kf_ref, vf_ref, lwf_ref, af_ref, yf_ref, bonf_ref),
               (rb_ref, kb_ref, vb_ref, lwb_ref, ab_ref, yb_ref, bonb_ref))
    units = []
    for dd, (r_ref, k_ref, v_ref, lw_ref, a_ref, y_ref, bon_ref) in enumerate(streams):
        rev = dd == 1
        r = r_ref[0].astype(F32)
        k = k_ref[0].astype(F32)
        v = v_ref[0].astype(F32)
        a = a_ref[0].astype(F32)
        lw = lw_ref[0]
        kkr = k * k_k
        kd = k * (1.0 + (a - 1.0) * k_a)
        ssq, rkd = _segsum64_mxu([kkr * kkr, r * kd * r_k], ones64, split=False)
        kk = kkr * lax.rsqrt(jnp.maximum(ssq, 1e-24))
        bvec = kk * a
        bon_ref[...] = (rkd * v).astype(BF16)

        tri = jnp.where((col >= row) if rev else (col <= row), 1.0, 0.0).astype(BF16)
        hi = lw.astype(BF16)
        lo = (lw - hi.astype(F32)).astype(BF16)
        cum = _dot(tri, hi) + _dot(tri, lo)
        tot = cum[0:1, :] if rev else cum[ln - 1:ln, :]
        w_inv = jnp.exp(-cum)
        w_end = jnp.exp(tot - cum)
        a_t = -kk * jnp.exp(cum - lw)
        r_t = r * jnp.exp(cum)
        b_t = bvec * w_inv
        k_t = kd * w_inv
        b_e = bvec * w_end
        k_e = kd * w_end
        w_tot = jnp.exp(tot)

        strict = fmask(tsrc > trow) if rev else fmask(tsrc < trow)
        incl = strict + eye

        for gi in range(gps):
            sl = slice(gi * pw, (gi + 1) * pw)
            units.append(dict(
                dd=dd, gi=gi, sl=sl, y_ref=y_ref, strict=strict, incl=incl,
                ar=jnp.concatenate([a_t[:, sl], r_t[:, sl]], axis=0).astype(BF16),
                b_t=b_t[:, sl], k_t=k_t[:, sl], v=v[:, sl], w_tot=w_tot[:, sl],
                bk=jnp.concatenate([b_e[:, sl], k_e[:, sl]], axis=0).astype(BF16)))

    for un in units:
        sb = _dot_nt(un["ar"], bd(un["b_t"]))
        sk = _dot_nt(un["ar"], bd(un["k_t"]))
        un["m_ab"] = sb[:ln] * un["strict"]
        un["p_rb"] = sb[ln:] * un["incl"]
        un["m_ak"] = sk[:ln] * un["strict"]
        un["p_rk"] = sk[ln:] * un["incl"]
    for un in units:
        un["s0"] = s_scr[un["dd"], un["gi"]]
        un["ars"] = _dot_nt(un["ar"], un["s0"].astype(BF16))
        un["mv"] = _dot(jnp.concatenate([un["m_ak"], un["p_rk"]], axis=0).astype(BF16), bd(un["v"]))
    for un in units:
        m0 = un["m_ab"] * base_f
        un["pinv"] = eye + m0
        un["mp"] = _dot(m0.astype(BF16), bd(m0))
    for un in units:
        both = _dot(jnp.concatenate([un["mp"], un["pinv"]], axis=0).astype(BF16), bd(un["mp"]))
        un["pinv"] = un["pinv"] + both[ln:]
        un["mp"] = both[:ln]
    for un in units:
        un["pinv"] = un["pinv"] + _dot(un["pinv"].astype(BF16), bd(un["mp"]))
    n = WKV_INV_BASE
    while n < ln:
        for un in units:
            un["t1"] = _dot((un["m_ab"] * off_f[n]).astype(BF16), bd(un["pinv"]))
        for un in units:
            un["pinv"] = un["pinv"] + _dot(un["pinv"].astype(BF16), bd(un["t1"]))
        n *= 2
    for un in units:
        un["u"] = _dot(un["pinv"].astype(BF16), bd(un["ars"][:ln] + un["mv"][:ln]))
    for un in units:
        un["y_ref"][:, un["sl"]] = un["ars"][ln:] + _dot(un["p_rb"].astype(BF16), bd(un["u"])) + un["mv"][ln:]
        uv = jnp.concatenate([un["u"], un["v"]], axis=0).astype(BF16)
        res = _dot_tn(uv, un["bk"])
        for jh in range(WKV_PACK):
            lt, hf = divmod(jh, hpt)
            rsl = slice(jh * RW_HEAD, (jh + 1) * RW_HEAD)
            csl = slice(lt * LANES, (lt + 1) * LANES)
            s_scr[un["dd"], un["gi"], rsl, csl] = (un["s0"][rsl, csl] * un["w_tot"][:, csl]
                                                    + res[rsl, csl] * half_f[hf])


def _wkv(rkv, lw, a, k_k, k_a, r_k, *, dm, gps):
    _, rows, d = rkv.shape
    nb, seq, ctx = dm["nb"], dm["seq"], dm["ctx"]
    ln = WKV_CHUNK
    sw = gps * WKV_PACK * RW_HEAD
    ncc, nlc = ctx // ln, seq // ln
    ctx_c0 = nb * seq // ln

    def fblk(b, c):
        return jnp.where(c < ncc, ctx_c0 + b * ncc + c, b * nlc + (c - ncc))

    def bblk(b, c):
        return jnp.where(c < ncc, ctx_c0 + b * ncc + (ncc - 1 - c), b * nlc + (nlc - 1 - (c - ncc)))

    def spec3(p, blk):
        return pl.BlockSpec((1, ln, sw), lambda b, s, c, p=p, blk=blk: (p, blk(b, c), s))

    def spec2(blk):
        return pl.BlockSpec((ln, sw), lambda b, s, c, blk=blk: (blk(b, c), s))

    pspec = pl.BlockSpec((1, sw), lambda b, s, c: (0, s))
    in_specs = [spec3(0, fblk), spec3(1, fblk), spec3(2, fblk), spec3(0, fblk), spec3(0, fblk),
                spec3(0, bblk), spec3(1, bblk), spec3(2, bblk), spec3(1, bblk), spec3(1, bblk),
                pspec, pspec, pspec]
    return pl.pallas_call(
        functools.partial(_wkv_kernel, gps=gps),
        out_shape=[jax.ShapeDtypeStruct((rows, d), F32), jax.ShapeDtypeStruct((rows, d), BF16),
                   jax.ShapeDtypeStruct((rows, d), F32), jax.ShapeDtypeStruct((rows, d), BF16)],
        grid=(nb, d // sw, ncc + nlc),
        in_specs=in_specs,
        out_specs=[spec2(fblk), spec2(fblk), spec2(bblk), spec2(bblk)],
        scratch_shapes=[pltpu.VMEM((2, gps, WKV_PACK * RW_HEAD, WKV_PACK * RW_HEAD), F32)],
        compiler_params=_cparams(("parallel", "parallel", "arbitrary")),
        name="wkv_scan",
    )(rkv, rkv, rkv, lw, a, rkv, rkv, rkv, lw, a, k_k.reshape(1, d), k_a.reshape(1, d), r_k.reshape(1, d))


def _rw_out_kernel(yf_ref, yb_ref, bf_ref, bb_ref, g_ref, x_ref, mod_ref, lg_ref, lb_ref, w_ref, o_ref):
    y = yf_ref[...] + yb_ref[...]
    ones64 = _ones_blockdiag64()
    mean = _segsum64_mxu([y], ones64)[0] * (1.0 / RW_HEAD)
    yc = y - mean
    var = _segsum64_mxu([yc * yc], ones64)[0] * (1.0 / RW_HEAD)
    yn = yc * lax.rsqrt(var + RW_GN_EPS)
    bonus = bf_ref[...].astype(F32) + bb_ref[...].astype(F32)
    o = (yn * lg_ref[...] + lb_ref[...] + bonus) * g_ref[...].astype(F32)
    o_ref[...] = x_ref[...] + mod_ref[0, 2:3, :] * _dot(o.astype(BF16), w_ref[...])


def _rw_out(yf, bonf, yb, bonb, g, xs, mod, ln_g, ln_b, w_o, *, dm, n_rows):
    d = xs.shape[1]
    te = dm["te"]
    modmap = lambda i: (jnp.minimum((i * te) // dm["seq"], dm["nb"]), 0, 0)
    rspec = pl.BlockSpec((te, d), lambda i: (i, 0))
    vspec = pl.BlockSpec((1, d), lambda i: (0, 0))
    return pl.pallas_call(
        _rw_out_kernel,
        out_shape=jax.ShapeDtypeStruct((n_rows, d), F32),
        grid=(n_rows // te,),
        in_specs=[rspec, rspec, rspec, rspec, rspec, rspec, pl.BlockSpec((1, 6, d), modmap), vspec, vspec,
                  pl.BlockSpec((d, d), lambda i: (0, 0))],
        out_specs=rspec,
        compiler_params=_cparams(("parallel",)),
        name="rwkv_out",
    )(yf, yb, bonf, bonb, g, xs, mod, ln_g.reshape(1, d), ln_b.reshape(1, d), w_o)


def _rope_tables(dm):
    seq, nb, ctx = dm["seq"], dm["nb"], dm["ctx"]
    t = jnp.arange(seq, dtype=jnp.int32)
    pos = jnp.stack([t // GRID_W, t % GRID_W], axis=-1).astype(F32)
    n_freq = ATT_HEAD // 4
    inv = ROPE_THETA ** (-jnp.arange(n_freq, dtype=F32) / n_freq)
    ang = pos[:, :, None] * inv
    cos, sin = jnp.cos(ang), jnp.sin(ang)
    zero = jnp.zeros_like(sin)
    cos_t = jnp.stack([cos, cos], axis=2).reshape(seq, ATT_HEAD)
    sa_t = jnp.stack([-sin, zero], axis=2).reshape(seq, ATT_HEAD)
    sb_t = jnp.stack([zero, sin], axis=2).reshape(seq, ATT_HEAD)
    nctx = nb * ctx
    full = lambda tab, fill: jnp.concatenate([jnp.tile(tab, (nb, 1)), jnp.full((nctx, ATT_HEAD), fill, F32)], axis=0)
    return full(cos_t, 1.0), full(sa_t, 0.0), full(sb_t, 0.0)


def kernel(x, c, ctx, c_ctx, mod_w, mod_b, norm1_g, norm2_g, ffn_up, ffn_conv_w, ffn_conv_b, ffn_down, rw_mu, rw_w_rkv, rw_w0, rw_w1, rw_w2, rw_a0, rw_a1, rw_a2, rw_g1, rw_g2, rw_k_k, rw_k_a, rw_r_k, rw_ln_g, rw_ln_b, rw_w_o, rw_v0, rw_v1, rw_v2, na_w_qkv, na_q_g, na_k_g, na_rpb, na_w_o, ga_w_qkv, ga_q_g, ga_k_g, ga_w_o):
    nb, seq, d = x.shape
    nctx = ctx.shape[1]
    depth = mod_w.shape[0]
    tm = nb * nctx
    assert seq % tm == 0 and seq & (seq - 1) == 0 and nctx & (nctx - 1) == 0 and nb + 1 <= SUBLANES
    assert seq // GRID_W >= NA_WIN_R and nctx % GRID_W == 0
    dm = dict(nb=nb, seq=seq, ctx=nctx, tm=tm, te=tm // 2, tpb=seq // tm, n_lat_rows=nb * seq)
    n_lat_tiles = nb * seq // tm
    n_tiles = n_lat_tiles + 1
    att_scale = ATT_HEAD ** -0.5 * math.log2(math.e)

    xs = jnp.concatenate([x.reshape(nb * seq, d), ctx.reshape(nb * nctx, d)], axis=0)
    c_all = jnp.concatenate([c, c_ctx[None], jnp.zeros((SUBLANES - nb - 1, d), F32)], axis=0)
    mods = _modulations(c_all, mod_w, mod_b)
    rope_tabs = None
    v_first = None
    ffn_wu, ffn_wd = _ffn_weights(ffn_up, ffn_down)
    rw_rkv_w = rw_w_rkv.astype(BF16)

    for i in range(depth):
        kind, j = i % 3, i // 3
        need_ctx = i < depth - 1
        nt_out = n_tiles if need_ctx else n_lat_tiles
        mod = mods[i, :nb + 1].reshape(nb + 1, 6, d)
        if kind == 0:
            vres = None if j == 0 else (rw_v0[j - 1], rw_v1[j - 1], rw_v2[j - 1])
            prep = _rw_prep(xs, mod, norm1_g[i], rw_mu[j], rw_w0[j], rw_w1[j], rw_w2[j], rw_a0[j], rw_a1[j],
                            rw_a2[j], rw_g1[j], rw_g2[j], vres, dm=dm)
            xm, lw, a, g = prep[0], prep[1], prep[2], prep[3]
            rkv = _rkv_proj(xm, rw_rkv_w, v_first if vres is not None else None,
                            prep[4] if vres is not None else None, layer=j, dm=dm)
            if v_first is None:
                v_first = rkv
            yf, bonf, yb, bonb = _wkv(rkv, lw, a, rw_k_k[j], rw_k_a[j], rw_r_k[j].reshape(-1), dm=dm,
                                      gps=min(8, d // (WKV_PACK * RW_HEAD)))
            xs = _rw_out(yf, bonf, yb, bonb, g, xs, mod, rw_ln_g[j], rw_ln_b[j], rw_w_o[j].astype(BF16),
                         dm=dm, n_rows=nt_out * tm)
        elif kind == 1:
            qkv = _qkv_proj(xs, mod, norm1_g[i], na_w_qkv[j].astype(BF16), na_q_g[j] * att_scale, na_k_g[j],
                            None, dm=dm, kv_dim=d)
            o = _na_attention(qkv, na_rpb[j], dm=dm, d=d)
            xs = _out_proj(o, na_w_o[j].astype(BF16), xs, mod, dm=dm, n_tiles=nt_out)
        else:
            if rope_tabs is None:
                rope_tabs = _rope_tables(dm)
            kv_dim = (ga_w_qkv.shape[-1] - d) // 2
            qkv = _qkv_proj(xs, mod, norm1_g[i], ga_w_qkv[j].astype(BF16), ga_q_g[j] * att_scale, ga_k_g[j],
                            rope_tabs, dm=dm, kv_dim=kv_dim)
            o = _gqa_attention(qkv, dm=dm, d=d, kv_heads=kv_dim // ATT_HEAD)
            xs = _out_proj(o, ga_w_o[j].astype(BF16), xs, mod, dm=dm, n_tiles=nt_out)
        xs = _ffn(xs, mod, norm2_g[i], ffn_wu, ffn_conv_w[i], ffn_conv_b[i], ffn_wd, layer=i, dm=dm,
                  need_ctx=need_ctx)
    return xs[:nb * seq].reshape(nb, seq, d)
```

```python
import functools
import math

import jax
import jax.numpy as jnp
from jax import lax
from jax.experimental import pallas as pl
from jax.experimental.pallas import tpu as pltpu

F32 = jnp.float32
BF16 = jnp.bfloat16

NORM_EPS = 1e-6
GRID_W = 64
ATT_HEAD = 128
RW_HEAD = 64
NA_WIN_R = 8
NA_WIN_C = 16
ROPE_THETA = 10000.0
RW_GN_EPS = 64e-5
LANES = 128
SUBLANES = 8
WKV_CHUNK = 64
WKV_PACK = 4
WKV_INV_BASE = 8
FFN_ROW_PIECES = 2
VMEM_LIMIT = 56 * 1024 * 1024
NEG_BIG = -1e30


def _cparams(sem):
    return pltpu.CompilerParams(dimension_semantics=sem, vmem_limit_bytes=VMEM_LIMIT)


def _dot(a, b):
    return jnp.dot(a, b, preferred_element_type=F32)


def _dot_nt(a, b):
    return lax.dot_general(a, b, (((1,), (1,)), ((), ())), preferred_element_type=F32)


def _dot_tn(a, b):
    return lax.dot_general(a, b, (((0,), (0,)), ((), ())), preferred_element_type=F32)


def _normmod(x, g, shift, scale):
    ms = jnp.mean(x * x, axis=-1, keepdims=True)
    y = x * lax.rsqrt(ms + NORM_EPS)
    return (y * g) * (1.0 + scale) + shift


def _silu(x):
    return x * jax.nn.sigmoid(x)


def _seq_edges(tile, rows, n_lat_rows, seq, ctx):
    rid = lax.broadcasted_iota(jnp.int32, (rows, 1), 0)
    base = tile * rows
    period = jnp.where(base >= n_lat_rows, ctx, seq)
    pos = (base + rid) & (period - 1)
    return rid, pos == 0, pos == period - 1


def _shift_rows(u, rid, first, last, prev_row, next_row):
    n = u.shape[0]
    up = pltpu.roll(u, 1, axis=0)
    up = jnp.where(rid == 0, prev_row, up)
    up = jnp.where(first, 0.0, up)
    un = pltpu.roll(u, n - 1, axis=0)
    un = jnp.where(rid == n - 1, next_row, un)
    un = jnp.where(last, 0.0, un)
    return up, un


def _ones_blockdiag64():
    sh = RW_HEAD.bit_length() - 1
    r = lax.broadcasted_iota(jnp.int32, (LANES, LANES), 0) >> sh
    c = lax.broadcasted_iota(jnp.int32, (LANES, LANES), 1) >> sh
    return jnp.where(r == c, 1.0, 0.0).astype(BF16)


def _segsum64_mxu(xs, ones, split=True):
    m, n = xs[0].shape
    nslab = n // LANES
    nparts = 2 if split else 1
    pieces = []
    for x in xs:
        hi = x.astype(BF16)
        parts = (hi, (x - hi.astype(F32)).astype(BF16)) if split else (hi,)
        for part in parts:
            pieces += [part[:, c * LANES:(c + 1) * LANES] for c in range(nslab)]
    res = _dot(jnp.concatenate(pieces, axis=0), ones)
    outs = []
    for i in range(len(xs)):
        base = i * nparts * nslab
        cols = []
        for c in range(nslab):
            col = res[(base + c) * m:(base + c + 1) * m]
            if split:
                col = col + res[(base + nslab + c) * m:(base + nslab + c + 1) * m]
            cols.append(col)
        outs.append(jnp.concatenate(cols, axis=1))
    return outs


def _mod_kernel(c_ref, w_ref, b_ref, o_ref):
    s = _silu(c_ref[...]).astype(BF16)
    o_ref[0] = _dot(s, w_ref[0].astype(BF16)) + b_ref[0]


def _modulations(c_all, mod_w, mod_b):
    depth, d, n = mod_w.shape
    tn = n // 8
    return pl.pallas_call(
        _mod_kernel,
        out_shape=jax.ShapeDtypeStruct((depth, SUBLANES, n), F32),
        grid=(depth, n // tn),
        in_specs=[
            pl.BlockSpec((SUBLANES, d), lambda l, j: (0, 0)),
            pl.BlockSpec((1, d, tn), lambda l, j: (l, 0, j)),
            pl.BlockSpec((1, 1, tn), lambda l, j: (l, 0, j)),
        ],
        out_specs=pl.BlockSpec((1, SUBLANES, tn), lambda l, j: (l, 0, j)),
        compiler_params=_cparams(("parallel", "parallel")),
        name="modulation",
    )(c_all, mod_w, mod_b.reshape(depth, 1, n))


def _ffn_kernel(*refs, tm, tile_off, aliased, n_lat_rows, seq, ctx):
    (x_ref, xp_ref, xn_ref, mod_ref, g_ref, wug_ref, wuv_ref, cwg_ref, cwv_ref, cbg_ref, cbv_ref,
     wd_ref) = refs[:12]
    o_ref, h_scr = refs[-2:]
    i = pl.program_id(0) + tile_off
    j = pl.program_id(1)
    shift = mod_ref[0, 3:4, :]
    scale = mod_ref[0, 4:5, :]
    hr = 2 * SUBLANES
    th = tm // FFN_ROW_PIECES

    @pl.when(j == 0)
    def _():
        g = g_ref[...]
        halo = jnp.concatenate([xp_ref[...], xn_ref[...]], axis=0)
        h_scr[0:hr, :] = _normmod(halo, g, shift, scale).astype(BF16)
        h_scr[hr:hr + tm, :] = _normmod(x_ref[...], g, shift, scale).astype(BF16)
        o_ref[...] = jnp.zeros_like(o_ref)

    rid, first, last = _seq_edges(i, tm, n_lat_rows, seq, ctx)
    rid_h = rid[0:th]

    def up_proj(p):
        lo = 0 if p == 0 else hr + p * th
        hp = h_scr[lo:hr + (p + 1) * th, :]
        ug, uv = _dot(hp, wug_ref[...]), _dot(hp, wuv_ref[...])
        if p == 0:
            return dict(g=ug[hr:], v=uv[hr:], halo_g=ug[0:hr], halo_v=uv[0:hr])
        return dict(g=ug, v=uv)

    def conv(main, prev_row, next_row, fm, lm, cw_ref, cb_ref):
        up, un = _shift_rows(main, rid_h, fm, lm, prev_row, next_row)
        return cb_ref[...] + up * cw_ref[0:1, :] + main * cw_ref[1:2, :] + un * cw_ref[2:3, :]

    pv, nx = SUBLANES - 1, SUBLANES

    def act_down(p, us):
        rs = slice(p * th, (p + 1) * th)
        rows = {}
        for key in ("g", "v"):
            prev_row = us[0]["halo_" + key][pv:pv + 1] if p == 0 else us[p - 1][key][th - 1:th]
            next_row = us[0]["halo_" + key][nx:nx + 1] if p == FFN_ROW_PIECES - 1 else us[p + 1][key][0:1]
            rows[key] = (prev_row, next_row)
        a = (_silu(conv(us[p]["g"], *rows["g"], first[rs], last[rs], cwg_ref, cbg_ref))
             * conv(us[p]["v"], *rows["v"], first[rs], last[rs], cwv_ref, cbv_ref))
        o_ref[rs, :] += _dot(a.astype(BF16), wd_ref[...])

    us = {0: up_proj(0)}
    for p in range(FFN_ROW_PIECES):
        if p + 1 < FFN_ROW_PIECES:
            us[p + 1] = up_proj(p + 1)
        act_down(p, us)

    @pl.when(j == pl.num_programs(1) - 1)
    def _():
        o_ref[...] = x_ref[...] + mod_ref[0, 5:6, :] * o_ref[...]


FFN_COLS = 512


def _ffn_weights(ffn_up, ffn_down):
    return ffn_up.astype(BF16), ffn_down.astype(BF16)


def _ffn_call(xs, mod, g2, wu, cw, cb, wd, prev, *, layer, dm, tm, tile_off, n_tiles, out_rows):
    rows, d = xs.shape
    f = wd.shape[1]
    fc = FFN_COLS
    nfc = f // fc
    hb = tm // SUBLANES
    last_hb = rows // SUBLANES - 1
    seq, nb = dm["seq"], dm["nb"]
    assert seq % tm == 0 or tile_off * tm >= dm["n_lat_rows"]
    modmap = lambda i, j: (jnp.minimum(((i + tile_off) * tm) // seq, nb), 0, 0)
    kern = functools.partial(_ffn_kernel, tm=tm, tile_off=tile_off, aliased=prev is not None,
                             n_lat_rows=dm["n_lat_rows"], seq=seq, ctx=dm["ctx"])
    in_specs = [
        pl.BlockSpec((tm, d), lambda i, j: (i + tile_off, 0), pipeline_mode=pl.Buffered(1)),
        pl.BlockSpec((SUBLANES, d), lambda i, j: (jnp.maximum((i + tile_off) * hb - 1, 0), 0)),
        pl.BlockSpec((SUBLANES, d), lambda i, j: (jnp.minimum((i + tile_off + 1) * hb, last_hb), 0)),
        pl.BlockSpec((1, 6, d), modmap),
        pl.BlockSpec((1, d), lambda i, j: (0, 0)),
        pl.BlockSpec((None, d, fc), lambda i, j: (layer, 0, j)),
        pl.BlockSpec((None, d, fc), lambda i, j: (layer, 0, nfc + j)),
        pl.BlockSpec((3, fc), lambda i, j: (0, j)),
        pl.BlockSpec((3, fc), lambda i, j: (0, nfc + j)),
        pl.BlockSpec((1, fc), lambda i, j: (0, j)),
        pl.BlockSpec((1, fc), lambda i, j: (0, nfc + j)),
        pl.BlockSpec((None, fc, d), lambda i, j: (layer, j, 0)),
    ]
    args = [xs, xs, xs, mod, g2.reshape(1, d), wu, wu, cw, cw, cb.reshape(1, -1), cb.reshape(1, -1), wd]
    aliases = {}
    if prev is not None:
        in_specs.append(pl.BlockSpec(memory_space=pl.ANY))
        args.append(prev)
        aliases = {len(args) - 1: 0}
    return pl.pallas_call(
        kern,
        out_shape=jax.ShapeDtypeStruct((out_rows, d), F32),
        grid=(n_tiles, nfc),
        in_specs=in_specs,
        out_specs=pl.BlockSpec((tm, d), lambda i, j: (i + tile_off, 0)),
        scratch_shapes=[pltpu.VMEM((tm + 2 * SUBLANES, d), BF16)],
        input_output_aliases=aliases,
        compiler_params=_cparams(("parallel", "arbitrary")),
        name="conv_ffn",
    )(*args)


def _ffn(xs, mod, g2, wu, cw, cb, wd, *, layer, dm, need_ctx):
    rows = xs.shape[0]
    n_lat_rows, tm = dm["n_lat_rows"], dm["tm"]
    big = 2 * tm
    out_rows = rows if need_ctx else n_lat_rows
    y = _ffn_call(xs, mod, g2, wu, cw, cb, wd, None, layer=layer, dm=dm, tm=big, tile_off=0,
                  n_tiles=n_lat_rows // big, out_rows=out_rows)
    if need_ctx:
        y = _ffn_call(xs, mod, g2, wu, cw, cb, wd, y, layer=layer, dm=dm, tm=tm, tile_off=n_lat_rows // tm,
                      n_tiles=1, out_rows=out_rows)
    return y


def _qkv_kernel(*refs, block_kinds, rope):
    if rope:
        x_ref, mod_ref, g_ref, w_ref, qg_ref, kg_ref, cos_ref, sa_ref, sb_ref, o_ref, h_scr = refs
    else:
        x_ref, mod_ref, g_ref, w_ref, qg_ref, kg_ref, o_ref, h_scr = refs
    j = pl.program_id(1)

    @pl.when(j == 0)
    def _():
        h_scr[...] = _normmod(x_ref[...], g_ref[...], mod_ref[0, 0:1, :], mod_ref[0, 1:2, :]).astype(BF16)

    acc = _dot(h_scr[...], w_ref[...])

    def emit(kinds):
        for hh, kind in enumerate(kinds):
            hs = slice(hh * ATT_HEAD, (hh + 1) * ATT_HEAD)
            y = acc[:, hs]
            if kind != "v":
                gain = qg_ref[...] if kind == "q" else kg_ref[...]
                y = y * lax.rsqrt(jnp.mean(y * y, axis=-1, keepdims=True) + NORM_EPS) * gain
                if rope:
                    y = (y * cos_ref[...] + pltpu.roll(y, ATT_HEAD - 32, axis=1) * sa_ref[...]
                         + pltpu.roll(y, 32, axis=1) * sb_ref[...])
            o_ref[:, hs] = y.astype(BF16)

    for jb, kinds in enumerate(block_kinds):
        pl.when(j == jb)(functools.partial(emit, kinds))


def _qkv_proj(xs, mod, g1, w, qg, kg, rope_tabs, *, dm, kv_dim):
    rows, d = xs.shape
    n = w.shape[1]
    tn = min(d, 2 * kv_dim)
    hpb = tn // ATT_HEAD
    kinds = ["q"] * (d // ATT_HEAD) + ["k"] * (kv_dim // ATT_HEAD) + ["v"] * (kv_dim // ATT_HEAD)
    block_kinds = tuple(tuple(kinds[b * hpb:(b + 1) * hpb]) for b in range(n // tn))
    tm = dm["tm"]
    modmap = lambda i, j: (jnp.minimum(i // dm["tpb"], dm["nb"]), 0, 0)
    in_specs = [
        pl.BlockSpec((tm, d), lambda i, j: (i, 0)),
        pl.BlockSpec((1, 6, d), modmap),
        pl.BlockSpec((1, d), lambda i, j: (0, 0)),
        pl.BlockSpec((d, tn), lambda i, j: (0, j)),
        pl.BlockSpec((1, ATT_HEAD), lambda i, j: (0, 0)),
        pl.BlockSpec((1, ATT_HEAD), lambda i, j: (0, 0)),
    ]
    args = [xs, mod, g1.reshape(1, d), w, qg.reshape(1, ATT_HEAD), kg.reshape(1, ATT_HEAD)]
    if rope_tabs is not None:
        in_specs += [pl.BlockSpec((tm, ATT_HEAD), lambda i, j: (i, 0))] * 3
        args += list(rope_tabs)
    kern = functools.partial(_qkv_kernel, block_kinds=block_kinds, rope=rope_tabs is not None)
    return pl.pallas_call(
        kern,
        out_shape=jax.ShapeDtypeStruct((rows, n), BF16),
        grid=(rows // tm, n // tn),
        in_specs=in_specs,
        out_specs=pl.BlockSpec((tm, tn), lambda i, j: (i, j)),
        scratch_shapes=[pltpu.VMEM((tm, d), BF16)],
        compiler_params=_cparams(("parallel", "arbitrary")),
        name="qkv_proj",
    )(*args)


def _oproj_kernel(a_ref, w_ref, x_ref, mod_ref, o_ref):
    o_ref[...] = x_ref[...] + mod_ref[0, 2:3, :] * _dot(a_ref[...], w_ref[...])


def _out_proj(a, w, xs, mod, *, dm, n_tiles):
    d = xs.shape[1]
    tm = dm["tm"]
    modmap = lambda i: (jnp.minimum(i // dm["tpb"], dm["nb"]), 0, 0)
    return pl.pallas_call(
        _oproj_kernel,
        out_shape=jax.ShapeDtypeStruct((n_tiles * tm, d), F32),
        grid=(n_tiles,),
        in_specs=[
            pl.BlockSpec((tm, d), lambda i: (i, 0)),
            pl.BlockSpec((d, d), lambda i: (0, 0)),
            pl.BlockSpec((tm, d), lambda i: (i, 0)),
            pl.BlockSpec((1, 6, d), modmap),
        ],
        out_specs=pl.BlockSpec((tm, d), lambda i: (i, 0)),
        compiler_params=_cparams(("parallel",)),
        name="out_proj",
    )(a, w, xs, mod)


def _softmax_pv(q, segs):
    ss = [_dot_nt(q, k) for k, _ in segs]
    m = ss[0].max(axis=-1, keepdims=True)
    for s in ss[1:]:
        m = jnp.maximum(m, s.max(axis=-1, keepdims=True))
    ps = [jnp.exp2(s - m) for s in ss]
    l = ps[0].sum(axis=-1, keepdims=True)
    for p in ps[1:]:
        l = l + p.sum(axis=-1, keepdims=True)
    o = _dot(ps[0].astype(BF16), segs[0][1])
    for p, (_, v) in zip(ps[1:], segs[1:]):
        o = o + _dot(p.astype(BF16), v)
    return o / l


def _lane_tiles(x, op):
    acc = x[:, 0:LANES]
    for j in range(1, x.shape[1] // LANES):
        acc = op(acc, x[:, j * LANES:(j + 1) * LANES])
    return acc


def _gqa_kernel(q_ref, kl_ref, vl_ref, kc_ref, vc_ref, o_ref, s_scr, *, group, n_lat_tiles, kchunk):
    t = pl.program_id(2)
    seq, ctx = kl_ref.shape[0], kc_ref.shape[0]

    def run(chunks):
        m = [None] * group
        mrun = [None] * group
        lrun = [None] * group
        o = [None] * group
        for g in range(group + 1):
            for k_ref, v_ref, st, sz, off in chunks:
                if g < group:
                    s = _dot_nt(q_ref[:, g * ATT_HEAD:(g + 1) * ATT_HEAD], k_ref[st:st + sz, :])
                    s_scr[g % 2, :, off:off + sz] = s
                    tmax = _lane_tiles(s, jnp.maximum)
                    mrun[g] = tmax if mrun[g] is None else jnp.maximum(mrun[g], tmax)
                if g >= 1:
                    h = g - 1
                    p = jnp.exp2(s_scr[h % 2, :, off:off + sz] - m[h])
                    psum = _lane_tiles(p, jnp.add)
                    pv = _dot(p.astype(BF16), v_ref[st:st + sz, :])
                    lrun[h] = psum if lrun[h] is None else lrun[h] + psum
                    o[h] = pv if o[h] is None else o[h] + pv
            if g < group:
                m[g] = mrun[g].max(axis=-1, keepdims=True)
        for g in range(group):
            l = lrun[g].sum(axis=-1, keepdims=True)
            o_ref[:, g * ATT_HEAD:(g + 1) * ATT_HEAD] = (o[g] / l).astype(BF16)

    lat_chunks = [(kl_ref, vl_ref, st, kchunk, st) for st in range(0, seq, kchunk)]
    ctx_chunk = (kc_ref, vc_ref, 0, ctx, seq)

    @pl.when(t < n_lat_tiles)
    def _():
        run(lat_chunks + [ctx_chunk])

    @pl.when(t >= n_lat_tiles)
    def _():
        run([ctx_chunk])


def _gqa_attention(qkv, *, dm, d, kv_heads):
    rows = qkv.shape[0]
    nb, seq, ctx = dm["nb"], dm["seq"], dm["ctx"]
    group = d // ATT_HEAD // kv_heads
    gw = group * ATT_HEAD
    tq = min(256, ctx)
    nlt, nct = seq // tq, ctx // tq
    kcol = d // ATT_HEAD
    vcol = kcol + kv_heads
    ctx_blk0 = nb * seq // ctx

    def qmap(b, h, t):
        return (jnp.where(t < nlt, b * nlt + t, nb * nlt + b * nct + (t - nlt)), h)

    kern = functools.partial(_gqa_kernel, group=group, n_lat_tiles=nlt, kchunk=min(512, seq))
    return pl.pallas_call(
        kern,
        out_shape=jax.ShapeDtypeStruct((rows, d), BF16),
        scratch_shapes=[pltpu.VMEM((2, tq, seq + ctx), F32)],
        grid=(nb, kv_heads, nlt + nct),
        in_specs=[
            pl.BlockSpec((tq, gw), qmap),
            pl.BlockSpec((seq, ATT_HEAD), lambda b, h, t: (b, kcol + h)),
            pl.BlockSpec((seq, ATT_HEAD), lambda b, h, t: (b, vcol + h)),
            pl.BlockSpec((ctx, ATT_HEAD), lambda b, h, t: (ctx_blk0 + b, kcol + h)),
            pl.BlockSpec((ctx, ATT_HEAD), lambda b, h, t: (ctx_blk0 + b, vcol + h)),
        ],
        out_specs=pl.BlockSpec((tq, gw), qmap),
        compiler_params=_cparams(("parallel", "parallel", "arbitrary")),
        name="gqa_attention",
    )(qkv, qkv, qkv, qkv, qkv)


def _na_kernel(q_ref, k_ref, v_ref, kc_ref, vc_ref, bias_ref, o_ref, *, rb, hps, grid_rows, n_row_blocks):
    t = pl.program_id(2)
    win = NA_WIN_R * GRID_W

    @pl.when(t < n_row_blocks)
    def _():
        units = []
        for hh in range(hps):
            hs = slice(hh * ATT_HEAD, (hh + 1) * ATT_HEAD)
            for rr in range(rb):
                r = t * rb + rr
                rs = jnp.clip(r - NA_WIN_R // 2, 0, grid_rows - NA_WIN_R)
                units.append(dict(hh=hh, hs=hs, rows=slice(rr * GRID_W, (rr + 1) * GRID_W), off=r - rs,
                                  start=pl.multiple_of(rs * GRID_W, GRID_W)))
        for un in units:
            q = q_ref[un["rows"], un["hs"]]
            un["sw"] = _dot_nt(q, k_ref[pl.ds(un["start"], win), un["hs"]]) + bias_ref[un["hh"], un["off"]]
            un["sc"] = _dot_nt(q, kc_ref[:, un["hs"]])
        for un in units:
            m = jnp.maximum(un["sw"].max(axis=-1, keepdims=True), un["sc"].max(axis=-1, keepdims=True))
            un["pw"] = jnp.exp2(un["sw"] - m)
            un["pc"] = jnp.exp2(un["sc"] - m)
        for un in units:
            l = un["pw"].sum(axis=-1, keepdims=True) + un["pc"].sum(axis=-1, keepdims=True)
            o = (_dot(un["pw"].astype(BF16), v_ref[pl.ds(un["start"], win), un["hs"]])
                 + _dot(un["pc"].astype(BF16), vc_ref[:, un["hs"]]))
            o_ref[un["rows"], un["hs"]] = (o / l).astype(BF16)

    @pl.when(t >= n_row_blocks)
    def _():
        for hh in range(hps):
            hs = slice(hh * ATT_HEAD, (hh + 1) * ATT_HEAD)
            o_ref[:, hs] = _softmax_pv(q_ref[:, hs], [(kc_ref[:, hs], vc_ref[:, hs])]).astype(BF16)


def _na_bias_table(rpb):
    qc = jnp.arange(GRID_W)
    kc = jnp.arange(GRID_W)
    cs = jnp.clip(qc - NA_WIN_C // 2, 0, GRID_W - NA_WIN_C)
    inwin = (kc[None, :] >= cs[:, None]) & (kc[None, :] < cs[:, None] + NA_WIN_C)
    cidx = kc[None, :] - qc[:, None] + NA_WIN_C - 1
    sel = (cidx[None] == jnp.arange(2 * NA_WIN_C - 1)[:, None, None]) & inwin[None]
    cols = jnp.einsum('hrc,cqk->hrqk', rpb, sel.astype(F32), precision=lax.Precision.HIGHEST)
    cols = jnp.where(inwin[None, None], cols * math.log2(math.e), NEG_BIG)
    tab = jnp.stack([cols[:, NA_WIN_R - 1 - o:2 * NA_WIN_R - 1 - o] for o in range(NA_WIN_R)], axis=1)
    tab = tab.transpose(0, 1, 3, 2, 4)
    return tab.reshape(rpb.shape[0], NA_WIN_R, GRID_W, NA_WIN_R * GRID_W).astype(F32)


def _na_attention(qkv, rpb, *, dm, d):
    rows = qkv.shape[0]
    nb, seq, ctx = dm["nb"], dm["seq"], dm["ctx"]
    heads = d // ATT_HEAD
    grid_rows = seq // GRID_W
    rb = ctx // GRID_W
    nrb = grid_rows // rb
    ctx_blk0 = nb * seq // ctx
    bias = _na_bias_table(rpb)

    def qmap(b, h, t):
        return (jnp.where(t < nrb, b * nrb + t, ctx_blk0 + b), h)

    hps = 4 if heads % 4 == 0 else 2
    hw = hps * ATT_HEAD
    ng = heads // hps
    kern = functools.partial(_na_kernel, rb=rb, hps=hps, grid_rows=grid_rows, n_row_blocks=nrb)
    return pl.pallas_call(
        kern,
        out_shape=jax.ShapeDtypeStruct((rows, d), BF16),
        grid=(nb, ng, nrb + 1),
        in_specs=[
            pl.BlockSpec((ctx, hw), qmap),
            pl.BlockSpec((seq, hw), lambda b, h, t: (b, ng + h)),
            pl.BlockSpec((seq, hw), lambda b, h, t: (b, 2 * ng + h)),
            pl.BlockSpec((ctx, hw), lambda b, h, t: (ctx_blk0 + b, ng + h)),
            pl.BlockSpec((ctx, hw), lambda b, h, t: (ctx_blk0 + b, 2 * ng + h)),
            pl.BlockSpec((hps, NA_WIN_R, GRID_W, NA_WIN_R * GRID_W), lambda b, h, t: (h, 0, 0, 0)),
        ],
        out_specs=pl.BlockSpec((ctx, hw), qmap),
        compiler_params=_cparams(("parallel", "parallel", "arbitrary")),
        name="na_attention",
    )(qkv, qkv, qkv, qkv, qkv, bias)


def _rw_prep_kernel(*refs, mix, te, n_lat_rows, seq, ctx):
    (x_ref, xp_ref, xn_ref, mod_ref, g_ref, mu_ref, w1_ref, a1_ref, g1_ref, w2_ref, a2_ref, g2_ref,
     w0_ref, a0_ref) = refs[:14]
    if mix:
        v1_ref, v2_ref, v0_ref, xm_ref, lw_ref, a_ref, go_ref, vg_ref = refs[14:]
    else:
        xm_ref, lw_ref, a_ref, go_ref = refs[14:]
    i = pl.program_id(0)
    g = g_ref[...]
    shift = mod_ref[0, 0:1, :]
    scale = mod_ref[0, 1:2, :]
    h = _normmod(x_ref[...], g, shift, scale)
    halo = _normmod(jnp.concatenate([xp_ref[...], xn_ref[...]], axis=0), g, shift, scale)
    rid, first, last = _seq_edges(i, te, n_lat_rows, seq, ctx)
    hp, hn = _shift_rows(h, rid, first, last, halo[SUBLANES - 1:SUBLANES], halo[SUBLANES:SUBLANES + 1])
    xx = 0.5 * (hp + hn) - h

    def mixed(p):
        return (h + xx * mu_ref[p:p + 1, :]).astype(BF16)

    xv = mixed(2)
    xm_ref[0] = mixed(0)
    xm_ref[1] = mixed(1)
    xm_ref[2] = xv
    zw = jnp.tanh(_dot(mixed(3), w1_ref[...])).astype(BF16)
    za = _dot(mixed(4), a1_ref[...]).astype(BF16)
    zg = jax.nn.sigmoid(_dot(mixed(5), g1_ref[...])).astype(BF16)
    for dd in range(2):
        sl = slice(dd * LANES, (dd + 1) * LANES)
        wl = w0_ref[dd:dd + 1, :] + _dot(zw[:, sl], w2_ref[dd])
        lw_ref[dd] = (-math.exp(-0.5)) * jax.nn.sigmoid(wl)
        a_ref[dd] = jax.nn.sigmoid(a0_ref[dd:dd + 1, :] + _dot(za[:, sl], a2_ref[dd])).astype(BF16)
    go_ref[...] = _dot(zg, g2_ref[...]).astype(BF16)
    if mix:
        zv = _dot(xv, v1_ref[...]).astype(BF16)
        vg_ref[...] = jax.nn.sigmoid(v0_ref[...] + _dot(zv, v2_ref[...])).astype(BF16)


def _pad_rank(w1, w2):
    r = w1.shape[-1]
    pad = (-r) % LANES
    w1 = jnp.pad(w1, [(0, 0)] * (w1.ndim - 1) + [(0, pad)])
    w2 = jnp.pad(w2, [(0, 0)] * (w2.ndim - 2) + [(0, pad), (0, 0)])
    return w1.astype(BF16), w2.astype(BF16)


def _rw_prep(xs, mod, g1n, mu, w0, w1, w2, a0, a1, a2, g1, g2, vres, *, dm):
    rows, d = xs.shape
    te = dm["te"]
    hb = te // SUBLANES
    last_hb = rows // SUBLANES - 1
    mix = vres is not None
    w1p, w2p = _pad_rank(w1, w2)
    a1p, a2p = _pad_rank(a1, a2)
    w1c = jnp.concatenate([w1p[0], w1p[1]], axis=1)
    a1c = jnp.concatenate([a1p[0], a1p[1]], axis=1)
    rg = g1.shape[1]
    modmap = lambda i: (jnp.minimum((i * te) // dm["seq"], dm["nb"]), 0, 0)
    full = lambda shp: pl.BlockSpec(shp, lambda i: (0,) * len(shp))
    in_specs = [
        pl.BlockSpec((te, d), lambda i: (i, 0)),
        pl.BlockSpec((SUBLANES, d), lambda i: (jnp.maximum(i * hb - 1, 0), 0)),
        pl.BlockSpec((SUBLANES, d), lambda i: (jnp.minimum((i + 1) * hb, last_hb), 0)),
        pl.BlockSpec((1, 6, d), modmap),
        full((1, d)), full((6, d)),
        full((d, 2 * LANES)), full((d, 2 * LANES)), full((d, rg)),
        full((2, LANES, d)), full((2, LANES, d)), full((rg, d)),
        full((2, d)), full((2, d)),
    ]
    args = [xs, xs, xs, mod, g1n.reshape(1, d), mu, w1c, a1c, g1.astype(BF16), w2p, a2p, g2.astype(BF16), w0, a0]
    row_spec = pl.BlockSpec((te, d), lambda i: (i, 0))
    out_shape = [jax.ShapeDtypeStruct((3, rows, d), BF16), jax.ShapeDtypeStruct((2, rows, d), F32),
                 jax.ShapeDtypeStruct((2, rows, d), BF16), jax.ShapeDtypeStruct((rows, d), BF16)]
    out_specs = [pl.BlockSpec((3, te, d), lambda i: (0, i, 0)), pl.BlockSpec((2, te, d), lambda i: (0, i, 0)),
                 pl.BlockSpec((2, te, d), lambda i: (0, i, 0)), row_spec]
    if mix:
        v1p, v2p = _pad_rank(vres[1], vres[2])
        in_specs += [full((d, LANES)), full((LANES, d)), full((1, d))]
        args += [v1p, v2p, vres[0].reshape(1, d)]
        out_shape.append(jax.ShapeDtypeStruct((rows, d), BF16))
        out_specs.append(row_spec)
    kern = functools.partial(_rw_prep_kernel, mix=mix, te=te, n_lat_rows=dm["n_lat_rows"], seq=dm["seq"],
                             ctx=dm["ctx"])
    return pl.pallas_call(
        kern,
        out_shape=out_shape,
        grid=(rows // te,),
        in_specs=in_specs,
        out_specs=out_specs,
        compiler_params=_cparams(("parallel",)),
        name="rwkv_prep",
    )(*args)


def _rkv_kernel(*refs, mix):
    if mix:
        xm_ref, w_ref, vf_ref, vg_ref, o_ref = refs
    else:
        xm_ref, w_ref, o_ref = refs
    acc = _dot(xm_ref[0], w_ref[0])
    if mix:
        p = pl.program_id(1)

        @pl.when(p == 2)
        def _():
            o_ref[0] = (acc + (vf_ref[0].astype(F32) - acc) * vg_ref[...].astype(F32)).astype(BF16)

        @pl.when(p != 2)
        def _():
            o_ref[0] = acc.astype(BF16)
    else:
        o_ref[0] = acc.astype(BF16)


def _rkv_proj(xm, w, v_first, vgate, *, layer, dm):
    _, rows, d = xm.shape
    tm = dm["tm"]
    mix = v_first is not None
    in_specs = [pl.BlockSpec((1, tm, d), lambda i, p: (p, i, 0)),
                pl.BlockSpec((None, 1, d, d), lambda i, p: (layer, p, 0, 0))]
    args = [xm, w]
    if mix:
        in_specs += [pl.BlockSpec((1, tm, d), lambda i, p: (2, i, 0)), pl.BlockSpec((tm, d), lambda i, p: (i, 0))]
        args += [v_first, vgate]
    return pl.pallas_call(
        functools.partial(_rkv_kernel, mix=mix),
        out_shape=jax.ShapeDtypeStruct((3, rows, d), BF16),
        grid=(rows // tm, 3),
        in_specs=in_specs,
        out_specs=pl.BlockSpec((1, tm, d), lambda i, p: (p, i, 0)),
        compiler_params=_cparams(("parallel", "arbitrary")),
        name="rwkv_rkv_proj",
    )(*args)


def _wkv_kernel(rf_ref, kf_ref, vf_ref, lwf_ref, af_ref, rb_ref, kb_ref, vb_ref, lwb_ref, ab_ref,
                kk_ref, ka_ref, rk_ref, yf_ref, bonf_ref, yb_ref, bonb_ref, s_scr, *, gps):
    c = pl.program_id(2)
    ln = WKV_CHUNK
    pw = WKV_PACK * RW_HEAD

    hpt = LANES // RW_HEAD

    @pl.when(c == 0)
    def _():
        s_scr[...] = jnp.zeros_like(s_scr)

    row = lax.broadcasted_iota(jnp.int32, (ln, ln), 0)
    col = lax.broadcasted_iota(jnp.int32, (ln, ln), 1)
    trow = lax.broadcasted_iota(jnp.int32, (ln, pw), 0)
    tsrc = lax.broadcasted_iota(jnp.int32, (ln, pw), 1) & (ln - 1)
    ones64 = _ones_blockdiag64()

    def fmask(cond):
        return jnp.where(cond, 1.0, 0.0)

    def same_block(n):
        sh = n.bit_length() - 1
        return (tsrc >> sh) == (trow >> sh)

    eye = fmask(tsrc == trow)
    base_f = fmask(same_block(WKV_INV_BASE))
    off_f = {}
    n = WKV_INV_BASE
    while n < ln:
        off_f[n] = fmask(jnp.logical_and(same_block(2 * n), jnp.logical_not(same_block(n))))
        n *= 2
    k_k = kk_ref[...]
    k_a = ka_ref[...]
    r_k = rk_ref[...]

    lane = lax.broadcasted_iota(jnp.int32, (ln, LANES), 1)
    half_f = [fmask((lane >> (RW_HEAD.bit_length() - 1)) == hf) for hf in range(hpt)]
    half_b = [hm.astype(BF16) for hm in half_f]
    zeros_b = jnp.zeros((ln, LANES), BF16)

    def bd(z):
        zb = z.astype(BF16)
        blocks = []
        for jh in range(WKV_PACK):
            lt, hf = divmod(jh, hpt)
            piece = zb[:, lt * LANES:(lt + 1) * LANES] * half_b[hf]
            blocks.append(jnp.concatenate([piece if tt == lt else zeros_b for tt in range(pw // LANES)], axis=1))
        return jnp.concatenate(blocks, axis=0)

    streams = ((rf_ref, kf_ref, vf_ref, lwf_ref, af_ref, yf_ref, bonf_ref),
               (rb_ref, kb_ref, vb_ref, lwb_ref, ab_ref, yb_ref, bonb_ref))
    units = []
    for dd, (r_ref, k_ref, v_ref, lw_ref, a_ref, y_ref, bon_ref) in enumerate(streams):
        rev = dd == 1
        r = r_ref[0].astype(F32)
        k = k_ref[0].astype(F32)
        v = v_ref[0].astype(F32)
        a = a_ref[0].astype(F32)
        lw = lw_ref[0]
        kkr = k * k_k
        kd = k * (1.0 + (a - 1.0) * k_a)
        ssq, rkd = _segsum64_mxu([kkr * kkr, r * kd * r_k], ones64, split=False)
        kk = kkr * lax.rsqrt(jnp.maximum(ssq, 1e-24))
        bvec = kk * a
        bon_ref[...] = (rkd * v).astype(BF16)

        tri = jnp.where((col >= row) if rev else (col <= row), 1.0, 0.0).astype(BF16)
        hi = lw.astype(BF16)
        lo = (lw - hi.astype(F32)).astype(BF16)
        cum = _dot(tri, hi) + _dot(tri, lo)
        tot = cum[0:1, :] if rev else cum[ln - 1:ln, :]
        w_inv = jnp.exp(-cum)
        w_end = jnp.exp(tot - cum)
        a_t = -kk * jnp.exp(cum - lw)
        r_t = r * jnp.exp(cum)
        b_t = bvec * w_inv
        k_t = kd * w_inv
        b_e = bvec * w_end
        k_e = kd * w_end
        w_tot = jnp.exp(tot)

        strict = fmask(tsrc > trow) if rev else fmask(tsrc < trow)
        incl = strict + eye

        for gi in range(gps):
            sl = slice(gi * pw, (gi + 1) * pw)
            units.append(dict(
                dd=dd, gi=gi, sl=sl, y_ref=y_ref, strict=strict, incl=incl,
                ar=jnp.concatenate([a_t[:, sl], r_t[:, sl]], axis=0).astype(BF16),
                b_t=b_t[:, sl], k_t=k_t[:, sl], v=v[:, sl], w_tot=w_tot[:, sl],
                bk=jnp.concatenate([b_e[:, sl], k_e[:, sl]], axis=0).astype(BF16)))

    for un in units:
        sb = _dot_nt(un["ar"], bd(un["b_t"]))
        sk = _dot_nt(un["ar"], bd(un["k_t"]))
        un["m_ab"] = sb[:ln] * un["strict"]
        un["p_rb"] = sb[ln:] * un["incl"]
        un["m_ak"] = sk[:ln] * un["strict"]
        un["p_rk"] = sk[ln:] * un["incl"]
    for un in units:
        un["s0"] = s_scr[un["dd"], un["gi"]]
        un["ars"] = _dot_nt(un["ar"], un["s0"].astype(BF16))
        un["mv"] = _dot(jnp.concatenate([un["m_ak"], un["p_rk"]], axis=0).astype(BF16), bd(un["v"]))
    for un in units:
        m0 = un["m_ab"] * base_f
        un["pinv"] = eye + m0
        un["mp"] = _dot(m0.astype(BF16), bd(m0))
    for un in units:
        both = _dot(jnp.concatenate([un["mp"], un["pinv"]], axis=0).astype(BF16), bd(un["mp"]))
        un["pinv"] = un["pinv"] + both[ln:]
        un["mp"] = both[:ln]
    for un in units:
        un["pinv"] = un["pinv"] + _dot(un["pinv"].astype(BF16), bd(un["mp"]))
    n = WKV_INV_BASE
    while n < ln:
        for un in units:
            un["t1"] = _dot((un["m_ab"] * off_f[n]).astype(BF16), bd(un["pinv"]))
        for un in units:
            un["pinv"] = un["pinv"] + _dot(un["pinv"].astype(BF16), bd(un["t1"]))
        n *= 2
    for un in units:
        un["u"] = _dot(un["pinv"].astype(BF16), bd(un["ars"][:ln] + un["mv"][:ln]))
    for un in units:
        un["y_ref"][:, un["sl"]] = (un["ars"][ln:] + _dot(un["p_rb"].astype(BF16), bd(un["u"]))
                                    + un["mv"][ln:]).astype(BF16)
        uv = jnp.concatenate([un["u"], un["v"]], axis=0).astype(BF16)
        res = _dot_tn(uv, un["bk"])
        for jh in range(WKV_PACK):
            lt, hf = divmod(jh, hpt)
            rsl = slice(jh * RW_HEAD, (jh + 1) * RW_HEAD)
            csl = slice(lt * LANES, (lt + 1) * LANES)
            s_scr[un["dd"], un["gi"], rsl, csl] = (un["s0"][rsl, csl] * un["w_tot"][:, csl]
                                                    + res[rsl, csl] * half_f[hf])


def _wkv(rkv, lw, a, k_k, k_a, r_k, *, dm, gps):
    _, rows, d = rkv.shape
    nb, seq, ctx = dm["nb"], dm["seq"], dm["ctx"]
    ln = WKV_CHUNK
    sw = gps * WKV_PACK * RW_HEAD
    ncc, nlc = ctx // ln, seq // ln
    ctx_c0 = nb * seq // ln

    def fblk(b, c):
        return jnp.where(c < ncc, ctx_c0 + b * ncc + c, b * nlc + (c - ncc))

    def bblk(b, c):
        return jnp.where(c < ncc, ctx_c0 + b * ncc + (ncc - 1 - c), b * nlc + (nlc - 1 - (c - ncc)))

    def spec3(p, blk):
        return pl.BlockSpec((1, ln, sw), lambda b, s, c, p=p, blk=blk: (p, blk(b, c), s))

    def spec2(blk):
        return pl.BlockSpec((ln, sw), lambda b, s, c, blk=blk: (blk(b, c), s))

    pspec = pl.BlockSpec((1, sw), lambda b, s, c: (0, s))
    in_specs = [spec3(0, fblk), spec3(1, fblk), spec3(2, fblk), spec3(0, fblk), spec3(0, fblk),
                spec3(0, bblk), spec3(1, bblk), spec3(2, bblk), spec3(1, bblk), spec3(1, bblk),
                pspec, pspec, pspec]
    return pl.pallas_call(
        functools.partial(_wkv_kernel, gps=gps),
        out_shape=[jax.ShapeDtypeStruct((rows, d), BF16)] * 4,
        grid=(nb, d // sw, ncc + nlc),
        in_specs=in_specs,
        out_specs=[spec2(fblk), spec2(fblk), spec2(bblk), spec2(bblk)],
        scratch_shapes=[pltpu.VMEM((2, gps, WKV_PACK * RW_HEAD, WKV_PACK * RW_HEAD), F32)],
        compiler_params=_cparams(("parallel", "parallel", "arbitrary")),
        name="wkv_scan",
    )(rkv, rkv, rkv, lw, a, rkv, rkv, rkv, lw, a, k_k.reshape(1, d), k_a.reshape(1, d), r_k.reshape(1, d))


def _rw_out_kernel(yf_ref, yb_ref, bf_ref, bb_ref, g_ref, x_ref, mod_ref, lg_ref, lb_ref, w_ref, o_ref):
    y = yf_ref[...].astype(F32) + yb_ref[...].astype(F32)
    ones64 = _ones_blockdiag64()
    mean = _segsum64_mxu([y], ones64)[0] * (1.0 / RW_HEAD)
    yc = y - mean
    var = _segsum64_mxu([yc * yc], ones64)[0] * (1.0 / RW_HEAD)
    yn = yc * lax.rsqrt(var + RW_GN_EPS)
    bonus = bf_ref[...].astype(F32) + bb_ref[...].astype(F32)
    o = (yn * lg_ref[...] + lb_ref[...] + bonus) * g_ref[...].astype(F32)
    o_ref[...] = x_ref[...] + mod_ref[0, 2:3, :] * _dot(o.astype(BF16), w_ref[...])


def _rw_out(yf, bonf, yb, bonb, g, xs, mod, ln_g, ln_b, w_o, *, dm, n_rows):
    d = xs.shape[1]
    te = dm["te"]
    modmap = lambda i: (jnp.minimum((i * te) // dm["seq"], dm["nb"]), 0, 0)
    rspec = pl.BlockSpec((te, d), lambda i: (i, 0))
    vspec = pl.BlockSpec((1, d), lambda i: (0, 0))
    return pl.pallas_call(
        _rw_out_kernel,
        out_shape=jax.ShapeDtypeStruct((n_rows, d), F32),
        grid=(n_rows // te,),
        in_specs=[rspec, rspec, rspec, rspec, rspec, rspec, pl.BlockSpec((1, 6, d), modmap), vspec, vspec,
                  pl.BlockSpec((d, d), lambda i: (0, 0))],
        out_specs=rspec,
        compiler_params=_cparams(("parallel",)),
        name="rwkv_out",
    )(yf, yb, bonf, bonb, g, xs, mod, ln_g.reshape(1, d), ln_b.reshape(1, d), w_o)


def _rope_tables(dm):
    seq, nb, ctx = dm["seq"], dm["nb"], dm["ctx"]
    t = jnp.arange(seq, dtype=jnp.int32)
    pos = jnp.stack([t // GRID_W, t % GRID_W], axis=-1).astype(F32)
    n_freq = ATT_HEAD // 4
    inv = ROPE_THETA ** (-jnp.arange(n_freq, dtype=F32) / n_freq)
    ang = pos[:, :, None] * inv
    cos, sin = jnp.cos(ang), jnp.sin(ang)
    zero = jnp.zeros_like(sin)
    cos_t = jnp.stack([cos, cos], axis=2).reshape(seq, ATT_HEAD)
    sa_t = jnp.stack([-sin, zero], axis=2).reshape(seq, ATT_HEAD)
    sb_t = jnp.stack([zero, sin], axis=2).reshape(seq, ATT_HEAD)
    nctx = nb * ctx
    full = lambda tab, fill: jnp.concatenate([jnp.tile(tab, (nb, 1)), jnp.full((nctx, ATT_HEAD), fill, F32)], axis=0)
    return full(cos_t, 1.0), full(sa_t, 0.0), full(sb_t, 0.0)


def kernel(x, c, ctx, c_ctx, mod_w, mod_b, norm1_g, norm2_g, ffn_up, ffn_conv_w, ffn_conv_b, ffn_down, rw_mu, rw_w_rkv, rw_w0, rw_w1, rw_w2, rw_a0, rw_a1, rw_a2, rw_g1, rw_g2, rw_k_k, rw_k_a, rw_r_k, rw_ln_g, rw_ln_b, rw_w_o, rw_v0, rw_v1, rw_v2, na_w_qkv, na_q_g, na_k_g, na_rpb, na_w_o, ga_w_qkv, ga_q_g, ga_k_g, ga_w_o):
    nb, seq, d = x.shape
    nctx = ctx.shape[1]
    depth = mod_w.shape[0]
    tm = nb * nctx
    assert seq % tm == 0 and seq & (seq - 1) == 0 and nctx & (nctx - 1) == 0 and nb + 1 <= SUBLANES
    assert seq // GRID_W >= NA_WIN_R and nctx % GRID_W == 0
    dm = dict(nb=nb, seq=seq, ctx=nctx, tm=tm, te=tm // 2, tpb=seq // tm, n_lat_rows=nb * seq)
    n_lat_tiles = nb * seq // tm
    n_tiles = n_lat_tiles + 1
    att_scale = ATT_HEAD ** -0.5 * math.log2(math.e)

    xs = jnp.concatenate([x.reshape(nb * seq, d), ctx.reshape(nb * nctx, d)], axis=0)
    c_all = jnp.concatenate([c, c_ctx[None], jnp.zeros((SUBLANES - nb - 1, d), F32)], axis=0)
    mods = _modulations(c_all, mod_w, mod_b)
    rope_tabs = None
    v_first = None
    ffn_wu, ffn_wd = _ffn_weights(ffn_up, ffn_down)
    rw_rkv_w = rw_w_rkv.astype(BF16)

    for i in range(depth):
        kind, j = i % 3, i // 3
        need_ctx = i < depth - 1
        nt_out = n_tiles if need_ctx else n_lat_tiles
        mod = mods[i, :nb + 1].reshape(nb + 1, 6, d)
        if kind == 0:
            vres = None if j == 0 else (rw_v0[j - 1], rw_v1[j - 1], rw_v2[j - 1])
            prep = _rw_prep(xs, mod, norm1_g[i], rw_mu[j], rw_w0[j], rw_w1[j], rw_w2[j], rw_a0[j], rw_a1[j],
                            rw_a2[j], rw_g1[j], rw_g2[j], vres, dm=dm)
            xm, lw, a, g = prep[0], prep[1], prep[2], prep[3]
            rkv = _rkv_proj(xm, rw_rkv_w, v_first if vres is not None else None,
                            prep[4] if vres is not None else None, layer=j, dm=dm)
            if v_first is None:
                v_first = rkv
            yf, bonf, yb, bonb = _wkv(rkv, lw, a, rw_k_k[j], rw_k_a[j], rw_r_k[j].reshape(-1), dm=dm,
                                      gps=min(8, d // (WKV_PACK * RW_HEAD)))
            xs = _rw_out(yf, bonf, yb, bonb, g, xs, mod, rw_ln_g[j], rw_ln_b[j], rw_w_o[j].astype(BF16),
                         dm=dm, n_rows=nt_out * tm)
        elif kind == 1:
            qkv = _qkv_proj(xs, mod, norm1_g[i], na_w_qkv[j].astype(BF16), na_q_g[j] * att_scale, na_k_g[j],
                            None, dm=dm, kv_dim=d)
            o = _na_attention(qkv, na_rpb[j], dm=dm, d=d)
            xs = _out_proj(o, na_w_o[j].astype(BF16), xs, mod, dm=dm, n_tiles=nt_out)
        else:
            if rope_tabs is None:
                rope_tabs = _rope_tables(dm)
            kv_dim = (ga_w_qkv.shape[-1] - d) // 2
            qkv = _qkv_proj(xs, mod, norm1_g[i], ga_w_qkv[j].astype(BF16), ga_q_g[j] * att_scale, ga_k_g[j],
                            rope_tabs, dm=dm, kv_dim=kv_dim)
            o = _gqa_attention(qkv, dm=dm, d=d, kv_heads=kv_dim // ATT_HEAD)
            xs = _out_proj(o, ga_w_o[j].astype(BF16), xs, mod, dm=dm, n_tiles=nt_out)
        xs = _ffn(xs, mod, norm2_g[i], ffn_wu, ffn_conv_w[i], ffn_conv_b[i], ffn_wd, layer=i, dm=dm,
                  need_ctx=need_ctx)
    return xs[:nb * seq].reshape(nb, seq, d)
```

```python
import functools
import math

import jax
import jax.numpy as jnp
from jax import lax
from jax.experimental import pallas as pl
from jax.experimental.pallas import tpu as pltpu

F32 = jnp.float32
BF16 = jnp.bfloat16

NORM_EPS = 1e-6
GRID_W = 64
ATT_HEAD = 128
RW_HEAD = 64
NA_WIN_R = 8
NA_WIN_C = 16
ROPE_THETA = 10000.0
RW_GN_EPS = 64e-5
LANES = 128
SUBLANES = 8
WKV_CHUNK = 64
WKV_PACK = 4
WKV_INV_BASE = 8
FFN_ROW_PIECES = 2
VMEM_LIMIT = 56 * 1024 * 1024
NEG_BIG = -1e30


def _cparams(sem):
    return pltpu.CompilerParams(dimension_semantics=sem, vmem_limit_bytes=VMEM_LIMIT)


def _dot(a, b):
    return jnp.dot(a, b, preferred_element_type=F32)


def _dot_nt(a, b):
    return lax.dot_general(a, b, (((1,), (1,)), ((), ())), preferred_element_type=F32)


def _dot_tn(a, b):
    return lax.dot_general(a, b, (((0,), (0,)), ((), ())), preferred_element_type=F32)


def _normmod(x, g, shift, scale):
    ms = jnp.mean(x * x, axis=-1, keepdims=True)
    y = x * lax.rsqrt(ms + NORM_EPS)
    return (y * g) * (1.0 + scale) + shift


def _silu(x):
    return x * jax.nn.sigmoid(x)


def _seq_edges(tile, rows, n_lat_rows, seq, ctx):
    rid = lax.broadcasted_iota(jnp.int32, (rows, 1), 0)
    base = tile * rows
    period = jnp.where(base >= n_lat_rows, ctx, seq)
    pos = (base + rid) & (period - 1)
    return rid, pos == 0, pos == period - 1


def _shift_rows(u, rid, first, last, prev_row, next_row):
    n = u.shape[0]
    up = pltpu.roll(u, 1, axis=0)
    up = jnp.where(rid == 0, prev_row, up)
    up = jnp.where(first, 0.0, up)
    un = pltpu.roll(u, n - 1, axis=0)
    un = jnp.where(rid == n - 1, next_row, un)
    un = jnp.where(last, 0.0, un)
    return up, un


def _ones_blockdiag64():
    sh = RW_HEAD.bit_length() - 1
    r = lax.broadcasted_iota(jnp.int32, (LANES, LANES), 0) >> sh
    c = lax.broadcasted_iota(jnp.int32, (LANES, LANES), 1) >> sh
    return jnp.where(r == c, 1.0, 0.0).astype(BF16)


def _segsum64_mxu(xs, ones, split=True):
    m, n = xs[0].shape
    nslab = n // LANES
    nparts = 2 if split else 1
    pieces = []
    for x in xs:
        hi = x.astype(BF16)
        parts = (hi, (x - hi.astype(F32)).astype(BF16)) if split else (hi,)
        for part in parts:
            pieces += [part[:, c * LANES:(c + 1) * LANES] for c in range(nslab)]
    res = _dot(jnp.concatenate(pieces, axis=0), ones)
    outs = []
    for i in range(len(xs)):
        base = i * nparts * nslab
        cols = []
        for c in range(nslab):
            col = res[(base + c) * m:(base + c + 1) * m]
            if split:
                col = col + res[(base + nslab + c) * m:(base + nslab + c + 1) * m]
            cols.append(col)
        outs.append(jnp.concatenate(cols, axis=1))
    return outs


def _mod_kernel(c_ref, w_ref, b_ref, o_ref):
    s = _silu(c_ref[...]).astype(BF16)
    o_ref[0] = _dot(s, w_ref[0].astype(BF16)) + b_ref[0]


def _modulations(c_all, mod_w, mod_b):
    depth, d, n = mod_w.shape
    tn = n // 8
    return pl.pallas_call(
        _mod_kernel,
        out_shape=jax.ShapeDtypeStruct((depth, SUBLANES, n), F32),
        grid=(depth, n // tn),
        in_specs=[
            pl.BlockSpec((SUBLANES, d), lambda l, j: (0, 0)),
            pl.BlockSpec((1, d, tn), lambda l, j: (l, 0, j)),
            pl.BlockSpec((1, 1, tn), lambda l, j: (l, 0, j)),
        ],
        out_specs=pl.BlockSpec((1, SUBLANES, tn), lambda l, j: (l, 0, j)),
        compiler_params=_cparams(("parallel", "parallel")),
        name="modulation",
    )(c_all, mod_w, mod_b.reshape(depth, 1, n))


def _ffn_kernel(*refs, tm, tile_off, aliased, n_lat_rows, seq, ctx):
    (x_ref, xp_ref, xn_ref, mod_ref, g_ref, wug_ref, wuv_ref, cwg_ref, cwv_ref, cbg_ref, cbv_ref,
     wd_ref) = refs[:12]
    o_ref, h_scr = refs[-2:]
    i = pl.program_id(0) + tile_off
    j = pl.program_id(1)
    shift = mod_ref[0, 3:4, :]
    scale = mod_ref[0, 4:5, :]
    hr = 2 * SUBLANES
    th = tm // FFN_ROW_PIECES

    @pl.when(j == 0)
    def _():
        g = g_ref[...]
        halo = jnp.concatenate([xp_ref[...], xn_ref[...]], axis=0)
        h_scr[0:hr, :] = _normmod(halo, g, shift, scale).astype(BF16)
        h_scr[hr:hr + tm, :] = _normmod(x_ref[...], g, shift, scale).astype(BF16)
        o_ref[...] = jnp.zeros_like(o_ref)

    rid, first, last = _seq_edges(i, tm, n_lat_rows, seq, ctx)
    rid_h = rid[0:th]

    def up_proj(p):
        lo = 0 if p == 0 else hr + p * th
        hp = h_scr[lo:hr + (p + 1) * th, :]
        ug, uv = _dot(hp, wug_ref[...]), _dot(hp, wuv_ref[...])
        if p == 0:
            return dict(g=ug[hr:], v=uv[hr:], halo_g=ug[0:hr], halo_v=uv[0:hr])
        return dict(g=ug, v=uv)

    def conv(main, prev_row, next_row, fm, lm, cw_ref, cb_ref):
        up, un = _shift_rows(main, rid_h, fm, lm, prev_row, next_row)
        return cb_ref[...] + up * cw_ref[0:1, :] + main * cw_ref[1:2, :] + un * cw_ref[2:3, :]

    pv, nx = SUBLANES - 1, SUBLANES

    def act_down(p, us):
        rs = slice(p * th, (p + 1) * th)
        rows = {}
        for key in ("g", "v"):
            prev_row = us[0]["halo_" + key][pv:pv + 1] if p == 0 else us[p - 1][key][th - 1:th]
            next_row = us[0]["halo_" + key][nx:nx + 1] if p == FFN_ROW_PIECES - 1 else us[p + 1][key][0:1]
            rows[key] = (prev_row, next_row)
        a = (_silu(conv(us[p]["g"], *rows["g"], first[rs], last[rs], cwg_ref, cbg_ref))
             * conv(us[p]["v"], *rows["v"], first[rs], last[rs], cwv_ref, cbv_ref))
        o_ref[rs, :] += _dot(a.astype(BF16), wd_ref[...])

    us = {0: up_proj(0)}
    for p in range(FFN_ROW_PIECES):
        if p + 1 < FFN_ROW_PIECES:
            us[p + 1] = up_proj(p + 1)
        act_down(p, us)

    @pl.when(j == pl.num_programs(1) - 1)
    def _():
        o_ref[...] = x_ref[...] + mod_ref[0, 5:6, :] * o_ref[...]


FFN_COLS = 512


def _ffn_weights(ffn_up, ffn_down):
    return ffn_up.astype(BF16), ffn_down.astype(BF16)


def _ffn_call(xs, mod, g2, wu, cw, cb, wd, prev, *, layer, dm, tm, tile_off, n_tiles, out_rows):
    rows, d = xs.shape
    f = wd.shape[1]
    fc = FFN_COLS
    nfc = f // fc
    hb = tm // SUBLANES
    last_hb = rows // SUBLANES - 1
    seq, nb = dm["seq"], dm["nb"]
    assert seq % tm == 0 or tile_off * tm >= dm["n_lat_rows"]
    modmap = lambda i, j: (jnp.minimum(((i + tile_off) * tm) // seq, nb), 0, 0)
    kern = functools.partial(_ffn_kernel, tm=tm, tile_off=tile_off, aliased=prev is not None,
                             n_lat_rows=dm["n_lat_rows"], seq=seq, ctx=dm["ctx"])
    in_specs = [
        pl.BlockSpec((tm, d), lambda i, j: (i + tile_off, 0), pipeline_mode=pl.Buffered(1)),
        pl.BlockSpec((SUBLANES, d), lambda i, j: (jnp.maximum((i + tile_off) * hb - 1, 0), 0)),
        pl.BlockSpec((SUBLANES, d), lambda i, j: (jnp.minimum((i + tile_off + 1) * hb, last_hb), 0)),
        pl.BlockSpec((1, 6, d), modmap),
        pl.BlockSpec((1, d), lambda i, j: (0, 0)),
        pl.BlockSpec((None, d, fc), lambda i, j: (layer, 0, j)),
        pl.BlockSpec((None, d, fc), lambda i, j: (layer, 0, nfc + j)),
        pl.BlockSpec((3, fc), lambda i, j: (0, j)),
        pl.BlockSpec((3, fc), lambda i, j: (0, nfc + j)),
        pl.BlockSpec((1, fc), lambda i, j: (0, j)),
        pl.BlockSpec((1, fc), lambda i, j: (0, nfc + j)),
        pl.BlockSpec((None, fc, d), lambda i, j: (layer, j, 0)),
    ]
    args = [xs, xs, xs, mod, g2.reshape(1, d), wu, wu, cw, cw, cb.reshape(1, -1), cb.reshape(1, -1), wd]
    aliases = {}
    if prev is not None:
        in_specs.append(pl.BlockSpec(memory_space=pl.ANY))
        args.append(prev)
        aliases = {len(args) - 1: 0}
    return pl.pallas_call(
        kern,
        out_shape=jax.ShapeDtypeStruct((out_rows, d), F32),
        grid=(n_tiles, nfc),
        in_specs=in_specs,
        out_specs=pl.BlockSpec((tm, d), lambda i, j: (i + tile_off, 0)),
        scratch_shapes=[pltpu.VMEM((tm + 2 * SUBLANES, d), BF16)],
        input_output_aliases=aliases,
        compiler_params=_cparams(("parallel", "arbitrary")),
        name="conv_ffn",
    )(*args)


def _ffn(xs, mod, g2, wu, cw, cb, wd, *, layer, dm, need_ctx):
    rows = xs.shape[0]
    n_lat_rows, tm = dm["n_lat_rows"], dm["tm"]
    big = 2 * tm
    out_rows = rows if need_ctx else n_lat_rows
    y = _ffn_call(xs, mod, g2, wu, cw, cb, wd, None, layer=layer, dm=dm, tm=big, tile_off=0,
                  n_tiles=n_lat_rows // big, out_rows=out_rows)
    if need_ctx:
        y = _ffn_call(xs, mod, g2, wu, cw, cb, wd, y, layer=layer, dm=dm, tm=tm, tile_off=n_lat_rows // tm,
                      n_tiles=1, out_rows=out_rows)
    return y


def _qkv_kernel(*refs, block_kinds, rope):
    if rope:
        x_ref, mod_ref, g_ref, w_ref, qg_ref, kg_ref, cos_ref, sa_ref, sb_ref, o_ref, h_scr = refs
    else:
        x_ref, mod_ref, g_ref, w_ref, qg_ref, kg_ref, o_ref, h_scr = refs
    j = pl.program_id(1)

    @pl.when(j == 0)
    def _():
        h_scr[...] = _normmod(x_ref[...], g_ref[...], mod_ref[0, 0:1, :], mod_ref[0, 1:2, :]).astype(BF16)

    acc = _dot(h_scr[...], w_ref[...])

    def emit(kinds):
        for hh, kind in enumerate(kinds):
            hs = slice(hh * ATT_HEAD, (hh + 1) * ATT_HEAD)
            y = acc[:, hs]
            if kind != "v":
                gain = qg_ref[...] if kind == "q" else kg_ref[...]
                y = y * lax.rsqrt(jnp.mean(y * y, axis=-1, keepdims=True) + NORM_EPS) * gain
                if rope:
                    y = (y * cos_ref[...] + pltpu.roll(y, ATT_HEAD - 32, axis=1) * sa_ref[...]
                         + pltpu.roll(y, 32, axis=1) * sb_ref[...])
            o_ref[:, hs] = y.astype(BF16)

    for jb, kinds in enumerate(block_kinds):
        pl.when(j == jb)(functools.partial(emit, kinds))


def _qkv_proj(xs, mod, g1, w, qg, kg, rope_tabs, *, dm, kv_dim):
    rows, d = xs.shape
    n = w.shape[1]
    tn = min(d, 2 * kv_dim)
    hpb = tn // ATT_HEAD
    kinds = ["q"] * (d // ATT_HEAD) + ["k"] * (kv_dim // ATT_HEAD) + ["v"] * (kv_dim // ATT_HEAD)
    block_kinds = tuple(tuple(kinds[b * hpb:(b + 1) * hpb]) for b in range(n // tn))
    tm = dm["tm"]
    modmap = lambda i, j: (jnp.minimum(i // dm["tpb"], dm["nb"]), 0, 0)
    in_specs = [
        pl.BlockSpec((tm, d), lambda i, j: (i, 0)),
        pl.BlockSpec((1, 6, d), modmap),
        pl.BlockSpec((1, d), lambda i, j: (0, 0)),
        pl.BlockSpec((d, tn), lambda i, j: (0, j)),
        pl.BlockSpec((1, ATT_HEAD), lambda i, j: (0, 0)),
        pl.BlockSpec((1, ATT_HEAD), lambda i, j: (0, 0)),
    ]
    args = [xs, mod, g1.reshape(1, d), w, qg.reshape(1, ATT_HEAD), kg.reshape(1, ATT_HEAD)]
    if rope_tabs is not None:
        in_specs += [pl.BlockSpec((tm, ATT_HEAD), lambda i, j: (i, 0))] * 3
        args += list(rope_tabs)
    kern = functools.partial(_qkv_kernel, block_kinds=block_kinds, rope=rope_tabs is not None)
    return pl.pallas_call(
        kern,
        out_shape=jax.ShapeDtypeStruct((rows, n), BF16),
        grid=(rows // tm, n // tn),
        in_specs=in_specs,
        out_specs=pl.BlockSpec((tm, tn), lambda i, j: (i, j)),
        scratch_shapes=[pltpu.VMEM((tm, d), BF16)],
        compiler_params=_cparams(("parallel", "arbitrary")),
        name="qkv_proj",
    )(*args)


def _oproj_kernel(a_ref, w_ref, x_ref, mod_ref, o_ref):
    o_ref[...] = x_ref[...] + mod_ref[0, 2:3, :] * _dot(a_ref[...], w_ref[...])


def _out_proj(a, w, xs, mod, *, dm, n_tiles):
    d = xs.shape[1]
    tm = dm["tm"]
    modmap = lambda i: (jnp.minimum(i // dm["tpb"], dm["nb"]), 0, 0)
    return pl.pallas_call(
        _oproj_kernel,
        out_shape=jax.ShapeDtypeStruct((n_tiles * tm, d), F32),
        grid=(n_tiles,),
        in_specs=[
            pl.BlockSpec((tm, d), lambda i: (i, 0)),
            pl.BlockSpec((d, d), lambda i: (0, 0)),
            pl.BlockSpec((tm, d), lambda i: (i, 0)),
            pl.BlockSpec((1, 6, d), modmap),
        ],
        out_specs=pl.BlockSpec((tm, d), lambda i: (i, 0)),
        compiler_params=_cparams(("parallel",)),
        name="out_proj",
    )(a, w, xs, mod)


def _softmax_pv(q, segs):
    ss = [_dot_nt(q, k) for k, _ in segs]
    m = ss[0].max(axis=-1, keepdims=True)
    for s in ss[1:]:
        m = jnp.maximum(m, s.max(axis=-1, keepdims=True))
    ps = [jnp.exp2(s - m) for s in ss]
    l = ps[0].sum(axis=-1, keepdims=True)
    for p in ps[1:]:
        l = l + p.sum(axis=-1, keepdims=True)
    o = _dot(ps[0].astype(BF16), segs[0][1])
    for p, (_, v) in zip(ps[1:], segs[1:]):
        o = o + _dot(p.astype(BF16), v)
    return o / l


def _lane_tiles(x, op):
    acc = x[:, 0:LANES]
    for j in range(1, x.shape[1] // LANES):
        acc = op(acc, x[:, j * LANES:(j + 1) * LANES])
    return acc


def _gqa_kernel(q_ref, kl_ref, vl_ref, kc_ref, vc_ref, o_ref, s_scr, vx_scr, *, group, n_lat_tiles, kchunk):
    t = pl.program_id(2)
    seq, ctx = kl_ref.shape[0], kc_ref.shape[0]

    @pl.when(t == 0)
    def _():
        vx_scr[0:seq, 0:ATT_HEAD] = vl_ref[...]
        vx_scr[seq:seq + ctx, 0:ATT_HEAD] = vc_ref[...]
        vx_scr[:, ATT_HEAD:2 * ATT_HEAD] = jnp.ones((seq + ctx, ATT_HEAD), BF16)

    def run(chunks):
        m = [None] * group
        mrun = [None] * group
        o = [None] * group
        for g in range(group + 1):
            for k_ref, st, sz, off in chunks:
                if g < group:
                    s = _dot_nt(q_ref[:, g * ATT_HEAD:(g + 1) * ATT_HEAD], k_ref[st:st + sz, :])
                    s_scr[g % 2, :, off:off + sz] = s
                    tmax = _lane_tiles(s, jnp.maximum)
                    mrun[g] = tmax if mrun[g] is None else jnp.maximum(mrun[g], tmax)
                if g >= 1:
                    h = g - 1
                    p = jnp.exp2((s_scr[h % 2, :, off:off + sz] - m[h]).astype(BF16))
                    pv = _dot(p, vx_scr[off:off + sz, :])
                    o[h] = pv if o[h] is None else o[h] + pv
            if g < group:
                m[g] = mrun[g].max(axis=-1, keepdims=True)
        for g in range(group):
            o_ref[:, g * ATT_HEAD:(g + 1) * ATT_HEAD] = (o[g][:, 0:ATT_HEAD] / o[g][:, ATT_HEAD:]).astype(BF16)

    lat_chunks = [(kl_ref, st, kchunk, st) for st in range(0, seq, kchunk)]
    ctx_chunk = (kc_ref, 0, ctx, seq)

    @pl.when(t < n_lat_tiles)
    def _():
        run(lat_chunks + [ctx_chunk])

    @pl.when(t >= n_lat_tiles)
    def _():
        run([ctx_chunk])


def _gqa_attention(qkv, *, dm, d, kv_heads):
    rows = qkv.shape[0]
    nb, seq, ctx = dm["nb"], dm["seq"], dm["ctx"]
    group = d // ATT_HEAD // kv_heads
    gw = group * ATT_HEAD
    tq = min(256, ctx)
    nlt, nct = seq // tq, ctx // tq
    kcol = d // ATT_HEAD
    vcol = kcol + kv_heads
    ctx_blk0 = nb * seq // ctx

    def qmap(b, h, t):
        return (jnp.where(t < nlt, b * nlt + t, nb * nlt + b * nct + (t - nlt)), h)

    kern = functools.partial(_gqa_kernel, group=group, n_lat_tiles=nlt, kchunk=min(512, seq))
    return pl.pallas_call(
        kern,
        out_shape=jax.ShapeDtypeStruct((rows, d), BF16),
        scratch_shapes=[pltpu.VMEM((2, tq, seq + ctx), F32), pltpu.VMEM((seq + ctx, 2 * ATT_HEAD), BF16)],
        grid=(nb, kv_heads, nlt + nct),
        in_specs=[
            pl.BlockSpec((tq, gw), qmap),
            pl.BlockSpec((seq, ATT_HEAD), lambda b, h, t: (b, kcol + h)),
            pl.BlockSpec((seq, ATT_HEAD), lambda b, h, t: (b, vcol + h)),
            pl.BlockSpec((ctx, ATT_HEAD), lambda b, h, t: (ctx_blk0 + b, kcol + h)),
            pl.BlockSpec((ctx, ATT_HEAD), lambda b, h, t: (ctx_blk0 + b, vcol + h)),
        ],
        out_specs=pl.BlockSpec((tq, gw), qmap),
        compiler_params=_cparams(("parallel", "parallel", "arbitrary")),
        name="gqa_attention",
    )(qkv, qkv, qkv, qkv, qkv)


def _na_kernel(q_ref, k_ref, v_ref, kc_ref, vc_ref, bias_ref, o_ref, *, rb, hps, grid_rows, n_row_blocks):
    t = pl.program_id(2)
    win = NA_WIN_R * GRID_W

    @pl.when(t < n_row_blocks)
    def _():
        units = []
        for hh in range(hps):
            hs = slice(hh * ATT_HEAD, (hh + 1) * ATT_HEAD)
            for rr in range(rb):
                r = t * rb + rr
                rs = jnp.clip(r - NA_WIN_R // 2, 0, grid_rows - NA_WIN_R)
                units.append(dict(hh=hh, hs=hs, rows=slice(rr * GRID_W, (rr + 1) * GRID_W), off=r - rs,
                                  start=pl.multiple_of(rs * GRID_W, GRID_W)))
        for un in units:
            q = q_ref[un["rows"], un["hs"]]
            un["sw"] = _dot_nt(q, k_ref[pl.ds(un["start"], win), un["hs"]]) + bias_ref[un["hh"], un["off"]]
            un["sc"] = _dot_nt(q, kc_ref[:, un["hs"]])
        for un in units:
            m = jnp.maximum(un["sw"].max(axis=-1, keepdims=True), un["sc"].max(axis=-1, keepdims=True))
            un["pw"] = jnp.exp2(un["sw"] - m)
            un["pc"] = jnp.exp2(un["sc"] - m)
        for un in units:
            l = un["pw"].sum(axis=-1, keepdims=True) + un["pc"].sum(axis=-1, keepdims=True)
            o = (_dot(un["pw"].astype(BF16), v_ref[pl.ds(un["start"], win), un["hs"]])
                 + _dot(un["pc"].astype(BF16), vc_ref[:, un["hs"]]))
            o_ref[un["rows"], un["hs"]] = (o / l).astype(BF16)

    @pl.when(t >= n_row_blocks)
    def _():
        for hh in range(hps):
            hs = slice(hh * ATT_HEAD, (hh + 1) * ATT_HEAD)
            o_ref[:, hs] = _softmax_pv(q_ref[:, hs], [(kc_ref[:, hs], vc_ref[:, hs])]).astype(BF16)


def _na_bias_table(rpb):
    qc = jnp.arange(GRID_W)
    kc = jnp.arange(GRID_W)
    cs = jnp.clip(qc - NA_WIN_C // 2, 0, GRID_W - NA_WIN_C)
    inwin = (kc[None, :] >= cs[:, None]) & (kc[None, :] < cs[:, None] + NA_WIN_C)
    cidx = kc[None, :] - qc[:, None] + NA_WIN_C - 1
    sel = (cidx[None] == jnp.arange(2 * NA_WIN_C - 1)[:, None, None]) & inwin[None]
    cols = jnp.einsum('hrc,cqk->hrqk', rpb, sel.astype(F32), precision=lax.Precision.HIGHEST)
    cols = jnp.where(inwin[None, None], cols * math.log2(math.e), NEG_BIG)
    tab = jnp.stack([cols[:, NA_WIN_R - 1 - o:2 * NA_WIN_R - 1 - o] for o in range(NA_WIN_R)], axis=1)
    tab = tab.transpose(0, 1, 3, 2, 4)
    return tab.reshape(rpb.shape[0], NA_WIN_R, GRID_W, NA_WIN_R * GRID_W).astype(F32)


def _na_attention(qkv, rpb, *, dm, d):
    rows = qkv.shape[0]
    nb, seq, ctx = dm["nb"], dm["seq"], dm["ctx"]
    heads = d // ATT_HEAD
    grid_rows = seq // GRID_W
    rb = ctx // GRID_W
    nrb = grid_rows // rb
    ctx_blk0 = nb * seq // ctx
    bias = _na_bias_table(rpb)

    def qmap(b, h, t):
        return (jnp.where(t < nrb, b * nrb + t, ctx_blk0 + b), h)

    hps = 4 if heads % 4 == 0 else 2
    hw = hps * ATT_HEAD
    ng = heads // hps
    kern = functools.partial(_na_kernel, rb=rb, hps=hps, grid_rows=grid_rows, n_row_blocks=nrb)
    return pl.pallas_call(
        kern,
        out_shape=jax.ShapeDtypeStruct((rows, d), BF16),
        grid=(nb, ng, nrb + 1),
        in_specs=[
            pl.BlockSpec((ctx, hw), qmap),
            pl.BlockSpec((seq, hw), lambda b, h, t: (b, ng + h)),
            pl.BlockSpec((seq, hw), lambda b, h, t: (b, 2 * ng + h)),
            pl.BlockSpec((ctx, hw), lambda b, h, t: (ctx_blk0 + b, ng + h)),
            pl.BlockSpec((ctx, hw), lambda b, h, t: (ctx_blk0 + b, 2 * ng + h)),
            pl.BlockSpec((hps, NA_WIN_R, GRID_W, NA_WIN_R * GRID_W), lambda b, h, t: (h, 0, 0, 0)),
        ],
        out_specs=pl.BlockSpec((ctx, hw), qmap),
        compiler_params=_cparams(("parallel", "parallel", "arbitrary")),
        name="na_attention",
    )(qkv, qkv, qkv, qkv, qkv, bias)


def _rw_prep_kernel(*refs, mix, te, n_lat_rows, seq, ctx):
    (x_ref, xp_ref, xn_ref, mod_ref, g_ref, mu_ref, w1_ref, a1_ref, g1_ref, w2_ref, a2_ref, g2_ref,
     w0_ref, a0_ref) = refs[:14]
    if mix:
        v1_ref, v2_ref, v0_ref, xm_ref, lw_ref, a_ref, go_ref, vg_ref = refs[14:]
    else:
        xm_ref, lw_ref, a_ref, go_ref = refs[14:]
    i = pl.program_id(0)
    g = g_ref[...]
    shift = mod_ref[0, 0:1, :]
    scale = mod_ref[0, 1:2, :]
    h = _normmod(x_ref[...], g, shift, scale)
    halo = _normmod(jnp.concatenate([xp_ref[...], xn_ref[...]], axis=0), g, shift, scale)
    rid, first, last = _seq_edges(i, te, n_lat_rows, seq, ctx)
    hp, hn = _shift_rows(h, rid, first, last, halo[SUBLANES - 1:SUBLANES], halo[SUBLANES:SUBLANES + 1])
    xx = 0.5 * (hp + hn) - h

    def mixed(p):
        return (h + xx * mu_ref[p:p + 1, :]).astype(BF16)

    xv = mixed(2)
    xm_ref[0] = mixed(0)
    xm_ref[1] = mixed(1)
    xm_ref[2] = xv
    zw = jnp.tanh(_dot(mixed(3), w1_ref[...])).astype(BF16)
    za = _dot(mixed(4), a1_ref[...]).astype(BF16)
    zg = jax.nn.sigmoid(_dot(mixed(5), g1_ref[...])).astype(BF16)
    for dd in range(2):
        sl = slice(dd * LANES, (dd + 1) * LANES)
        wl = w0_ref[dd:dd + 1, :] + _dot(zw[:, sl], w2_ref[dd])
        lw_ref[dd] = (-math.exp(-0.5)) * jax.nn.sigmoid(wl)
        a_ref[dd] = jax.nn.sigmoid(a0_ref[dd:dd + 1, :] + _dot(za[:, sl], a2_ref[dd])).astype(BF16)
    go_ref[...] = _dot(zg, g2_ref[...]).astype(BF16)
    if mix:
        zv = _dot(xv, v1_ref[...]).astype(BF16)
        vg_ref[...] = jax.nn.sigmoid(v0_ref[...] + _dot(zv, v2_ref[...])).astype(BF16)


def _pad_rank(w1, w2):
    r = w1.shape[-1]
    pad = (-r) % LANES
    w1 = jnp.pad(w1, [(0, 0)] * (w1.ndim - 1) + [(0, pad)])
    w2 = jnp.pad(w2, [(0, 0)] * (w2.ndim - 2) + [(0, pad), (0, 0)])
    return w1.astype(BF16), w2.astype(BF16)


def _rw_prep(xs, mod, g1n, mu, w0, w1, w2, a0, a1, a2, g1, g2, vres, *, dm):
    rows, d = xs.shape
    te = dm["te"]
    hb = te // SUBLANES
    last_hb = rows // SUBLANES - 1
    mix = vres is not None
    w1p, w2p = _pad_rank(w1, w2)
    a1p, a2p = _pad_rank(a1, a2)
    w1c = jnp.concatenate([w1p[0], w1p[1]], axis=1)
    a1c = jnp.concatenate([a1p[0], a1p[1]], axis=1)
    rg = g1.shape[1]
    modmap = lambda i: (jnp.minimum((i * te) // dm["seq"], dm["nb"]), 0, 0)
    full = lambda shp: pl.BlockSpec(shp, lambda i: (0,) * len(shp))
    in_specs = [
        pl.BlockSpec((te, d), lambda i: (i, 0)),
        pl.BlockSpec((SUBLANES, d), lambda i: (jnp.maximum(i * hb - 1, 0), 0)),
        pl.BlockSpec((SUBLANES, d), lambda i: (jnp.minimum((i + 1) * hb, last_hb), 0)),
        pl.BlockSpec((1, 6, d), modmap),
        full((1, d)), full((6, d)),
        full((d, 2 * LANES)), full((d, 2 * LANES)), full((d, rg)),
        full((2, LANES, d)), full((2, LANES, d)), full((rg, d)),
        full((2, d)), full((2, d)),
    ]
    args = [xs, xs, xs, mod, g1n.reshape(1, d), mu, w1c, a1c, g1.astype(BF16), w2p, a2p, g2.astype(BF16), w0, a0]
    row_spec = pl.BlockSpec((te, d), lambda i: (i, 0))
    out_shape = [jax.ShapeDtypeStruct((3, rows, d), BF16), jax.ShapeDtypeStruct((2, rows, d), F32),
                 jax.ShapeDtypeStruct((2, rows, d), BF16), jax.ShapeDtypeStruct((rows, d), BF16)]
    out_specs = [pl.BlockSpec((3, te, d), lambda i: (0, i, 0)), pl.BlockSpec((2, te, d), lambda i: (0, i, 0)),
                 pl.BlockSpec((2, te, d), lambda i: (0, i, 0)), row_spec]
    if mix:
        v1p, v2p = _pad_rank(vres[1], vres[2])
        in_specs += [full((d, LANES)), full((LANES, d)), full((1, d))]
        args += [v1p, v2p, vres[0].reshape(1, d)]
        out_shape.append(jax.ShapeDtypeStruct((rows, d), BF16))
        out_specs.append(row_spec)
    kern = functools.partial(_rw_prep_kernel, mix=mix, te=te, n_lat_rows=dm["n_lat_rows"], seq=dm["seq"],
                             ctx=dm["ctx"])
    return pl.pallas_call(
        kern,
        out_shape=out_shape,
        grid=(rows // te,),
        in_specs=in_specs,
        out_specs=out_specs,
        compiler_params=_cparams(("parallel",)),
        name="rwkv_prep",
    )(*args)


def _rkv_kernel(*refs, mix):
    if mix:
        xm_ref, w_ref, vf_ref, vg_ref, o_ref = refs
    else:
        xm_ref, w_ref, o_ref = refs
    acc = _dot(xm_ref[0], w_ref[0])
    if mix:
        p = pl.program_id(1)

        @pl.when(p == 2)
        def _():
            o_ref[0] = (acc + (vf_ref[0].astype(F32) - acc) * vg_ref[...].astype(F32)).astype(BF16)

        @pl.when(p != 2)
        def _():
            o_ref[0] = acc.astype(BF16)
    else:
        o_ref[0] = acc.astype(BF16)


def _rkv_proj(xm, w, v_first, vgate, *, layer, dm):
    _, rows, d = xm.shape
    tm = dm["tm"]
    mix = v_first is not None
    in_specs = [pl.BlockSpec((1, tm, d), lambda i, p: (p, i, 0)),
                pl.BlockSpec((None, 1, d, d), lambda i, p: (layer, p, 0, 0))]
    args = [xm, w]
    if mix:
        in_specs += [pl.BlockSpec((1, tm, d), lambda i, p: (2, i, 0)), pl.BlockSpec((tm, d), lambda i, p: (i, 0))]
        args += [v_first, vgate]
    return pl.pallas_call(
        functools.partial(_rkv_kernel, mix=mix),
        out_shape=jax.ShapeDtypeStruct((3, rows, d), BF16),
        grid=(rows // tm, 3),
        in_specs=in_specs,
        out_specs=pl.BlockSpec((1, tm, d), lambda i, p: (p, i, 0)),
        compiler_params=_cparams(("parallel", "arbitrary")),
        name="rwkv_rkv_proj",
    )(*args)


def _wkv_kernel(rf_ref, kf_ref, vf_ref, lwf_ref, af_ref, rb_ref, kb_ref, vb_ref, lwb_ref, ab_ref,
                kk_ref, ka_ref, rk_ref, yf_ref, bonf_ref, yb_ref, bonb_ref, s_scr, *, gps):
    c = pl.program_id(2)
    ln = WKV_CHUNK
    pw = WKV_PACK * RW_HEAD

    hpt = LANES // RW_HEAD

    @pl.when(c == 0)
    def _():
        s_scr[...] = jnp.zeros_like(s_scr)

    row = lax.broadcasted_iota(jnp.int32, (ln, ln), 0)
    col = lax.broadcasted_iota(jnp.int32, (ln, ln), 1)
    trow = lax.broadcasted_iota(jnp.int32, (ln, pw), 0)
    tsrc = lax.broadcasted_iota(jnp.int32, (ln, pw), 1) & (ln - 1)
    ones64 = _ones_blockdiag64()

    def fmask(cond):
        return jnp.where(cond, 1.0, 0.0)

    def same_block(n):
        sh = n.bit_length() - 1
        return (tsrc >> sh) == (trow >> sh)

    eye = fmask(tsrc == trow)
    base_f = fmask(same_block(WKV_INV_BASE))
    off_f = {}
    n = WKV_INV_BASE
    while n < ln:
        off_f[n] = fmask(jnp.logical_and(same_block(2 * n), jnp.logical_not(same_block(n))))
        n *= 2
    k_k = kk_ref[...]
    k_a = ka_ref[...]
    r_k = rk_ref[...]

    lane = lax.broadcasted_iota(jnp.int32, (ln, LANES), 1)
    half_f = [fmask((lane >> (RW_HEAD.bit_length() - 1)) == hf) for hf in range(hpt)]
    half_b = [hm.astype(BF16) for hm in half_f]
    zeros_b = jnp.zeros((ln, LANES), BF16)

    def bd(z):
        zb = z.astype(BF16)
        blocks = []
        for jh in range(WKV_PACK):
            lt, hf = divmod(jh, hpt)
            piece = zb[:, lt * LANES:(lt + 1) * LANES] * half_b[hf]
            blocks.append(jnp.concatenate([piece if tt == lt else zeros_b for tt in range(pw // LANES)], axis=1))
        return jnp.concatenate(blocks, axis=0)

    streams = ((rf_ref, kf_ref, vf_ref, lwf_ref, af_ref, yf_ref, bonf_ref),
               (rb_ref, kb_ref, vb_ref, lwb_ref, ab_ref, yb_ref, bonb_ref))
    units = []
    for dd, (r_ref, k_ref, v_ref, lw_ref, a_ref, y_ref, bon_ref) in enumerate(streams):
        rev = dd == 1
        r = r_ref[0].astype(F32)
        k = k_ref[0].astype(F32)
        v = v_ref[0].astype(F32)
        a = a_ref[0].astype(F32)
        lw = lw_ref[0]
        kkr = k * k_k
        kd = k * (1.0 + (a - 1.0) * k_a)
        ssq, rkd = _segsum64_mxu([kkr * kkr, r * kd * r_k], ones64, split=False)
        kk = kkr * lax.rsqrt(jnp.maximum(ssq, 1e-24))
        bvec = kk * a
        bon_ref[...] = (rkd * v).astype(BF16)

        tri = jnp.where((col >= row) if rev else (col <= row), 1.0, 0.0).astype(BF16)
        hi = lw.astype(BF16)
        lo = (lw - hi.astype(F32)).astype(BF16)
        cum = _dot(tri, hi) + _dot(tri, lo)
        tot = cum[0:1, :] if rev else cum[ln - 1:ln, :]
        w_inv = jnp.exp(-cum)
        w_end = jnp.exp(tot - cum)
        a_t = -kk * jnp.exp(cum - lw)
        r_t = r * jnp.exp(cum)
        b_t = bvec * w_inv
        k_t = kd * w_inv
        b_e = bvec * w_end
        k_e = kd * w_end
        w_tot = jnp.exp(tot)

        strict = fmask(tsrc > trow) if rev else fmask(tsrc < trow)
        incl = strict + eye

        for gi in range(gps):
            sl = slice(gi * pw, (gi + 1) * pw)
            units.append(dict(
                dd=dd, gi=gi, sl=sl, y_ref=y_ref, strict=strict, incl=incl,
                ar=jnp.concatenate([a_t[:, sl], r_t[:, sl]], axis=0).astype(BF16),
                b_t=b_t[:, sl], k_t=k_t[:, sl], v=v[:, sl], w_tot=w_tot[:, sl],
                bk=jnp.concatenate([b_e[:, sl], k_e[:, sl]], axis=0).astype(BF16)))

    for un in units:
        sb = _dot_nt(un["ar"], bd(un["b_t"]))
        sk = _dot_nt(un["ar"], bd(un["k_t"]))
        un["m_ab"] = sb[:ln] * un["strict"]
        un["p_rb"] = sb[ln:] * un["incl"]
        un["m_ak"] = sk[:ln] * un["strict"]
        un["p_rk"] = sk[ln:] * un["incl"]
    for un in units:
        un["s0"] = s_scr[un["dd"], un["gi"]]
        un["ars"] = _dot_nt(un["ar"], un["s0"].astype(BF16))
        un["mv"] = _dot(jnp.concatenate([un["m_ak"], un["p_rk"]], axis=0).astype(BF16), bd(un["v"]))
    for un in units:
        m0 = un["m_ab"] * base_f
        un["pinv"] = eye + m0
        un["mp"] = _dot(m0.astype(BF16), bd(m0))
    for un in units:
        both = _dot(jnp.concatenate([un["mp"], un["pinv"]], axis=0).astype(BF16), bd(un["mp"]))
        un["pinv"] = un["pinv"] + both[ln:]
        un["mp"] = both[:ln]
    for un in units:
        un["pinv"] = un["pinv"] + _dot(un["pinv"].astype(BF16), bd(un["mp"]))
    n = WKV_INV_BASE
    while n < ln:
        for un in units:
            un["t1"] = _dot((un["m_ab"] * off_f[n]).astype(BF16), bd(un["pinv"]))
        for un in units:
            un["pinv"] = un["pinv"] + _dot(un["pinv"].astype(BF16), bd(un["t1"]))
        n *= 2
    for un in units:
        un["u"] = _dot(un["pinv"].astype(BF16), bd(un["ars"][:ln] + un["mv"][:ln]))
    for un in units:
        un["y_ref"][:, un["sl"]] = (un["ars"][ln:] + _dot(un["p_rb"].astype(BF16), bd(un["u"]))
                                    + un["mv"][ln:]).astype(BF16)
        uv = jnp.concatenate([un["u"], un["v"]], axis=0).astype(BF16)
        res = _dot_tn(uv, un["bk"])
        for jh in range(WKV_PACK):
            lt, hf = divmod(jh, hpt)
            rsl = slice(jh * RW_HEAD, (jh + 1) * RW_HEAD)
            csl = slice(lt * LANES, (lt + 1) * LANES)
            s_scr[un["dd"], un["gi"], rsl, csl] = (un["s0"][rsl, csl] * un["w_tot"][:, csl]
                                                    + res[rsl, csl] * half_f[hf])


def _wkv(rkv, lw, a, k_k, k_a, r_k, *, dm, gps):
    _, rows, d = rkv.shape
    nb, seq, ctx = dm["nb"], dm["seq"], dm["ctx"]
    ln = WKV_CHUNK
    sw = gps * WKV_PACK * RW_HEAD
    ncc, nlc = ctx // ln, seq // ln
    ctx_c0 = nb * seq // ln

    def fblk(b, c):
        return jnp.where(c < ncc, ctx_c0 + b * ncc + c, b * nlc + (c - ncc))

    def bblk(b, c):
        return jnp.where(c < ncc, ctx_c0 + b * ncc + (ncc - 1 - c), b * nlc + (nlc - 1 - (c - ncc)))

    def spec3(p, blk):
        return pl.BlockSpec((1, ln, sw), lambda b, s, c, p=p, blk=blk: (p, blk(b, c), s))

    def spec2(blk):
        return pl.BlockSpec((ln, sw), lambda b, s, c, blk=blk: (blk(b, c), s))

    pspec = pl.BlockSpec((1, sw), lambda b, s, c: (0, s))
    in_specs = [spec3(0, fblk), spec3(1, fblk), spec3(2, fblk), spec3(0, fblk), spec3(0, fblk),
                spec3(0, bblk), spec3(1, bblk), spec3(2, bblk), spec3(1, bblk), spec3(1, bblk),
                pspec, pspec, pspec]
    return pl.pallas_call(
        functools.partial(_wkv_kernel, gps=gps),
        out_shape=[jax.ShapeDtypeStruct((rows, d), BF16)] * 4,
        grid=(nb, d // sw, ncc + nlc),
        in_specs=in_specs,
        out_specs=[spec2(fblk), spec2(fblk), spec2(bblk), spec2(bblk)],
        scratch_shapes=[pltpu.VMEM((2, gps, WKV_PACK * RW_HEAD, WKV_PACK * RW_HEAD), F32)],
        compiler_params=_cparams(("parallel", "parallel", "arbitrary")),
        name="wkv_scan",
    )(rkv, rkv, rkv, lw, a, rkv, rkv, rkv, lw, a, k_k.reshape(1, d), k_a.reshape(1, d), r_k.reshape(1, d))


def _rw_out_kernel(yf_ref, yb_ref, bf_ref, bb_ref, g_ref, x_ref, mod_ref, lg_ref, lb_ref, w_ref, o_ref):
    y = yf_ref[...].astype(F32) + yb_ref[...].astype(F32)
    ones64 = _ones_blockdiag64()
    mean = _segsum64_mxu([y], ones64)[0] * (1.0 / RW_HEAD)
    yc = y - mean
    var = _segsum64_mxu([yc * yc], ones64)[0] * (1.0 / RW_HEAD)
    yn = yc * lax.rsqrt(var + RW_GN_EPS)
    bonus = bf_ref[...].astype(F32) + bb_ref[...].astype(F32)
    o = (yn * lg_ref[...] + lb_ref[...] + bonus) * g_ref[...].astype(F32)
    o_ref[...] = x_ref[...] + mod_ref[0, 2:3, :] * _dot(o.astype(BF16), w_ref[...])


def _rw_out(yf, bonf, yb, bonb, g, xs, mod, ln_g, ln_b, w_o, *, dm, n_rows):
    d = xs.shape[1]
    te = dm["te"]
    modmap = lambda i: (jnp.minimum((i * te) // dm["seq"], dm["nb"]), 0, 0)
    rspec = pl.BlockSpec((te, d), lambda i: (i, 0))
    vspec = pl.BlockSpec((1, d), lambda i: (0, 0))
    return pl.pallas_call(
        _rw_out_kernel,
        out_shape=jax.ShapeDtypeStruct((n_rows, d), F32),
        grid=(n_rows // te,),
        in_specs=[rspec, rspec, rspec, rspec, rspec, rspec, pl.BlockSpec((1, 6, d), modmap), vspec, vspec,
                  pl.BlockSpec((d, d), lambda i: (0, 0))],
        out_specs=rspec,
        compiler_params=_cparams(("parallel",)),
        name="rwkv_out",
    )(yf, yb, bonf, bonb, g, xs, mod, ln_g.reshape(1, d), ln_b.reshape(1, d), w_o)


def _rope_tables(dm):
    seq, nb, ctx = dm["seq"], dm["nb"], dm["ctx"]
    t = jnp.arange(seq, dtype=jnp.int32)
    pos = jnp.stack([t // GRID_W, t % GRID_W], axis=-1).astype(F32)
    n_freq = ATT_HEAD // 4
    inv = ROPE_THETA ** (-jnp.arange(n_freq, dtype=F32) / n_freq)
    ang = pos[:, :, None] * inv
    cos, sin = jnp.cos(ang), jnp.sin(ang)
    zero = jnp.zeros_like(sin)
    cos_t = jnp.stack([cos, cos], axis=2).reshape(seq, ATT_HEAD)
    sa_t = jnp.stack([-sin, zero], axis=2).reshape(seq, ATT_HEAD)
    sb_t = jnp.stack([zero, sin], axis=2).reshape(seq, ATT_HEAD)
    nctx = nb * ctx
    full = lambda tab, fill: jnp.concatenate([jnp.tile(tab, (nb, 1)), jnp.full((nctx, ATT_HEAD), fill, F32)], axis=0)
    return full(cos_t, 1.0), full(sa_t, 0.0), full(sb_t, 0.0)


def kernel(x, c, ctx, c_ctx, mod_w, mod_b, norm1_g, norm2_g, ffn_up, ffn_conv_w, ffn_conv_b, ffn_down, rw_mu, rw_w_rkv, rw_w0, rw_w1, rw_w2, rw_a0, rw_a1, rw_a2, rw_g1, rw_g2, rw_k_k, rw_k_a, rw_r_k, rw_ln_g, rw_ln_b, rw_w_o, rw_v0, rw_v1, rw_v2, na_w_qkv, na_q_g, na_k_g, na_rpb, na_w_o, ga_w_qkv, ga_q_g, ga_k_g, ga_w_o):
    nb, seq, d = x.shape
    nctx = ctx.shape[1]
    depth = mod_w.shape[0]
    tm = nb * nctx
    assert seq % tm == 0 and seq & (seq - 1) == 0 and nctx & (nctx - 1) == 0 and nb + 1 <= SUBLANES
    assert seq // GRID_W >= NA_WIN_R and nctx % GRID_W == 0
    dm = dict(nb=nb, seq=seq, ctx=nctx, tm=tm, te=tm // 2, tpb=seq // tm, n_lat_rows=nb * seq)
    n_lat_tiles = nb * seq // tm
    n_tiles = n_lat_tiles + 1
    att_scale = ATT_HEAD ** -0.5 * math.log2(math.e)

    xs = jnp.concatenate([x.reshape(nb * seq, d), ctx.reshape(nb * nctx, d)], axis=0)
    c_all = jnp.concatenate([c, c_ctx[None], jnp.zeros((SUBLANES - nb - 1, d), F32)], axis=0)
    mods = _modulations(c_all, mod_w, mod_b)
    rope_tabs = None
    v_first = None
    ffn_wu, ffn_wd = _ffn_weights(ffn_up, ffn_down)
    rw_rkv_w = rw_w_rkv.astype(BF16)

    for i in range(depth):
        kind, j = i % 3, i // 3
        need_ctx = i < depth - 1
        nt_out = n_tiles if need_ctx else n_lat_tiles
        mod = mods[i, :nb + 1].reshape(nb + 1, 6, d)
        if kind == 0:
            vres = None if j == 0 else (rw_v0[j - 1], rw_v1[j - 1], rw_v2[j - 1])
            prep = _rw_prep(xs, mod, norm1_g[i], rw_mu[j], rw_w0[j], rw_w1[j], rw_w2[j], rw_a0[j], rw_a1[j],
                            rw_a2[j], rw_g1[j], rw_g2[j], vres, dm=dm)
            xm, lw, a, g = prep[0], prep[1], prep[2], prep[3]
            rkv = _rkv_proj(xm, rw_rkv_w, v_first if vres is not None else None,
                            prep[4] if vres is not None else None, layer=j, dm=dm)
            if v_first is None:
                v_first = rkv
            yf, bonf, yb, bonb = _wkv(rkv, lw, a, rw_k_k[j], rw_k_a[j], rw_r_k[j].reshape(-1), dm=dm,
                                      gps=min(8, d // (WKV_PACK * RW_HEAD)))
            xs = _rw_out(yf, bonf, yb, bonb, g, xs, mod, rw_ln_g[j], rw_ln_b[j], rw_w_o[j].astype(BF16),
                         dm=dm, n_rows=nt_out * tm)
        elif kind == 1:
            qkv = _qkv_proj(xs, mod, norm1_g[i], na_w_qkv[j].astype(BF16), na_q_g[j] * att_scale, na_k_g[j],
                            None, dm=dm, kv_dim=d)
            o = _na_attention(qkv, na_rpb[j], dm=dm, d=d)
            xs = _out_proj(o, na_w_o[j].astype(BF16), xs, mod, dm=dm, n_tiles=nt_out)
        else:
            if rope_tabs is None:
                rope_tabs = _rope_tables(dm)
            kv_dim = (ga_w_qkv.shape[-1] - d) // 2
            qkv = _qkv_proj(xs, mod, norm1_g[i], ga_w_qkv[j].astype(BF16), ga_q_g[j] * att_scale, ga_k_g[j],
                            rope_tabs, dm=dm, kv_dim=kv_dim)
            o = _gqa_attention(qkv, dm=dm, d=d, kv_heads=kv_dim // ATT_HEAD)
            xs = _out_proj(o, ga_w_o[j].astype(BF16), xs, mod, dm=dm, n_tiles=nt_out)
        xs = _ffn(xs, mod, norm2_g[i], ffn_wu, ffn_conv_w[i], ffn_conv_b[i], ffn_wd, layer=i, dm=dm,
                  need_ctx=need_ctx)
    return xs[:nb * seq].reshape(nb, seq, d)
```

```python
import functools
import math

import jax
import jax.numpy as jnp
from jax import lax
from jax.experimental import pallas as pl
from jax.experimental.pallas import tpu as pltpu

F32 = jnp.float32
BF16 = jnp.bfloat16

NORM_EPS = 1e-6
GRID_W = 64
ATT_HEAD = 128
RW_HEAD = 64
NA_WIN_R = 8
NA_WIN_C = 16
ROPE_THETA = 10000.0
RW_GN_EPS = 64e-5
LANES = 128
SUBLANES = 8
WKV_CHUNK = 64
WKV_PACK = 4
WKV_INV_BASE = 8
FFN_ROW_PIECES = 2
VMEM_LIMIT = 56 * 1024 * 1024
NEG_BIG = -1e30


def _cparams(sem):
    return pltpu.CompilerParams(dimension_semantics=sem, vmem_limit_bytes=VMEM_LIMIT)


def _dot(a, b):
    return jnp.dot(a, b, preferred_element_type=F32)


def _dot_nt(a, b):
    return lax.dot_general(a, b, (((1,), (1,)), ((), ())), preferred_element_type=F32)


def _dot_tn(a, b):
    return lax.dot_general(a, b, (((0,), (0,)), ((), ())), preferred_element_type=F32)


def _normmod(x, g, shift, scale):
    ms = jnp.mean(x * x, axis=-1, keepdims=True)
    y = x * lax.rsqrt(ms + NORM_EPS)
    return (y * g) * (1.0 + scale) + shift


def _silu(x):
    return x * jax.nn.sigmoid(x)


def _seq_edges(tile, rows, n_lat_rows, seq, ctx):
    rid = lax.broadcasted_iota(jnp.int32, (rows, 1), 0)
    base = tile * rows
    period = jnp.where(base >= n_lat_rows, ctx, seq)
    pos = (base + rid) & (period - 1)
    return rid, pos == 0, pos == period - 1


def _shift_rows(u, rid, first, last, prev_row, next_row):
    n = u.shape[0]
    up = pltpu.roll(u, 1, axis=0)
    up = jnp.where(rid == 0, prev_row, up)
    up = jnp.where(first, 0.0, up)
    un = pltpu.roll(u, n - 1, axis=0)
    un = jnp.where(rid == n - 1, next_row, un)
    un = jnp.where(last, 0.0, un)
    return up, un


def _ones_blockdiag64():
    sh = RW_HEAD.bit_length() - 1
    r = lax.broadcasted_iota(jnp.int32, (LANES, LANES), 0) >> sh
    c = lax.broadcasted_iota(jnp.int32, (LANES, LANES), 1) >> sh
    return jnp.where(r == c, 1.0, 0.0).astype(BF16)


def _segsum64_mxu(xs, ones, split=True):
    m, n = xs[0].shape
    nslab = n // LANES
    nparts = 2 if split else 1
    pieces = []
    for x in xs:
        hi = x.astype(BF16)
        parts = (hi, (x - hi.astype(F32)).astype(BF16)) if split else (hi,)
        for part in parts:
            pieces += [part[:, c * LANES:(c + 1) * LANES] for c in range(nslab)]
    res = _dot(jnp.concatenate(pieces, axis=0), ones)
    outs = []
    for i in range(len(xs)):
        base = i * nparts * nslab
        cols = []
        for c in range(nslab):
            col = res[(base + c) * m:(base + c + 1) * m]
            if split:
                col = col + res[(base + nslab + c) * m:(base + nslab + c + 1) * m]
            cols.append(col)
        outs.append(jnp.concatenate(cols, axis=1))
    return outs


def _mod_kernel(c_ref, w_ref, b_ref, o_ref):
    s = _silu(c_ref[...]).astype(BF16)
    o_ref[0] = _dot(s, w_ref[0].astype(BF16)) + b_ref[0]


def _modulations(c_all, mod_w, mod_b):
    depth, d, n = mod_w.shape
    tn = n // 8
    return pl.pallas_call(
        _mod_kernel,
        out_shape=jax.ShapeDtypeStruct((depth, SUBLANES, n), F32),
        grid=(depth, n // tn),
        in_specs=[
            pl.BlockSpec((SUBLANES, d), lambda l, j: (0, 0)),
            pl.BlockSpec((1, d, tn), lambda l, j: (l, 0, j)),
            pl.BlockSpec((1, 1, tn), lambda l, j: (l, 0, j)),
        ],
        out_specs=pl.BlockSpec((1, SUBLANES, tn), lambda l, j: (l, 0, j)),
        compiler_params=_cparams(("parallel", "parallel")),
        name="modulation",
    )(c_all, mod_w, mod_b.reshape(depth, 1, n))


def _ffn_kernel(*refs, tm, tile_off, aliased, n_lat_rows, seq, ctx):
    x_ref, xp_ref, xn_ref, mod_ref, g_ref, wug_ref, wuv_ref, cp_ref, wd_ref = refs[:9]
    o_ref, h_scr = refs[-2:]
    i = pl.program_id(0) + tile_off
    j = pl.program_id(1)
    cpg = cp_ref[j]
    cpv = cp_ref[pl.num_programs(1) + j]
    shift = mod_ref[0, 3:4, :]
    scale = mod_ref[0, 4:5, :]
    hr = 2 * SUBLANES
    th = tm // FFN_ROW_PIECES

    @pl.when(j == 0)
    def _():
        g = g_ref[...]
        halo = jnp.concatenate([xp_ref[...], xn_ref[...]], axis=0)
        h_scr[0:hr, :] = _normmod(halo, g, shift, scale).astype(BF16)
        h_scr[hr:hr + tm, :] = _normmod(x_ref[...], g, shift, scale).astype(BF16)
        o_ref[...] = jnp.zeros_like(o_ref)

    rid, first, last = _seq_edges(i, tm, n_lat_rows, seq, ctx)
    rid_h = rid[0:th]

    def up_proj(p):
        lo = 0 if p == 0 else hr + p * th
        hp = h_scr[lo:hr + (p + 1) * th, :]
        ug, uv = _dot(hp, wug_ref[...]), _dot(hp, wuv_ref[...])
        if p == 0:
            return dict(g=ug[hr:], v=uv[hr:], halo_g=ug[0:hr], halo_v=uv[0:hr])
        return dict(g=ug, v=uv)

    def conv(main, prev_row, next_row, fm, lm, cp):
        up, un = _shift_rows(main, rid_h, fm, lm, prev_row, next_row)
        return cp[3:4, :] + up * cp[0:1, :] + main * cp[1:2, :] + un * cp[2:3, :]

    pv, nx = SUBLANES - 1, SUBLANES

    def act_down(p, us):
        rs = slice(p * th, (p + 1) * th)
        rows = {}
        for key in ("g", "v"):
            prev_row = us[0]["halo_" + key][pv:pv + 1] if p == 0 else us[p - 1][key][th - 1:th]
            next_row = us[0]["halo_" + key][nx:nx + 1] if p == FFN_ROW_PIECES - 1 else us[p + 1][key][0:1]
            rows[key] = (prev_row, next_row)
        a = (_silu(conv(us[p]["g"], *rows["g"], first[rs], last[rs], cpg))
             * conv(us[p]["v"], *rows["v"], first[rs], last[rs], cpv))
        o_ref[rs, :] += _dot(a.astype(BF16), wd_ref[...])

    us = {0: up_proj(0)}
    for p in range(FFN_ROW_PIECES):
        if p + 1 < FFN_ROW_PIECES:
            us[p + 1] = up_proj(p + 1)
        act_down(p, us)

    @pl.when(j == pl.num_programs(1) - 1)
    def _():
        o_ref[...] = x_ref[...] + mod_ref[0, 5:6, :] * o_ref[...]


FFN_COLS = 512


def _ffn_weights(ffn_up, ffn_down):
    return ffn_up.astype(BF16), ffn_down.astype(BF16)


def _ffn_call(xs, mod, g2, wu, cw, cb, wd, prev, *, layer, dm, tm, tile_off, n_tiles, out_rows):
    rows, d = xs.shape
    f = wd.shape[1]
    fc = FFN_COLS
    nfc = f // fc
    hb = tm // SUBLANES
    last_hb = rows // SUBLANES - 1
    seq, nb = dm["seq"], dm["nb"]
    assert seq % tm == 0 or tile_off * tm >= dm["n_lat_rows"]
    modmap = lambda i, j: (jnp.minimum(((i + tile_off) * tm) // seq, nb), 0, 0)
    kern = functools.partial(_ffn_kernel, tm=tm, tile_off=tile_off, aliased=prev is not None,
                             n_lat_rows=dm["n_lat_rows"], seq=seq, ctx=dm["ctx"])
    in_specs = [
        pl.BlockSpec((tm, d), lambda i, j: (i + tile_off, 0), pipeline_mode=pl.Buffered(1)),
        pl.BlockSpec((SUBLANES, d), lambda i, j: (jnp.maximum((i + tile_off) * hb - 1, 0), 0)),
        pl.BlockSpec((SUBLANES, d), lambda i, j: (jnp.minimum((i + tile_off + 1) * hb, last_hb), 0)),
        pl.BlockSpec((1, 6, d), modmap),
        pl.BlockSpec((1, d), lambda i, j: (0, 0)),
        pl.BlockSpec((None, d, fc), lambda i, j: (layer, 0, j)),
        pl.BlockSpec((None, d, fc), lambda i, j: (layer, 0, nfc + j)),
        pl.BlockSpec((2 * nfc, 4, fc), lambda i, j: (0, 0, 0)),
        pl.BlockSpec((None, fc, d), lambda i, j: (layer, j, 0)),
    ]
    cp = jnp.concatenate([cw, cb.reshape(1, -1)], axis=0).reshape(4, 2 * nfc, fc).transpose(1, 0, 2)
    args = [xs, xs, xs, mod, g2.reshape(1, d), wu, wu, cp, wd]
    aliases = {}
    if prev is not None:
        in_specs.append(pl.BlockSpec(memory_space=pl.ANY))
        args.append(prev)
        aliases = {len(args) - 1: 0}
    return pl.pallas_call(
        kern,
        out_shape=jax.ShapeDtypeStruct((out_rows, d), F32),
        grid=(n_tiles, nfc),
        in_specs=in_specs,
        out_specs=pl.BlockSpec((tm, d), lambda i, j: (i + tile_off, 0)),
        scratch_shapes=[pltpu.VMEM((tm + 2 * SUBLANES, d), BF16)],
        input_output_aliases=aliases,
        compiler_params=_cparams(("parallel", "arbitrary")),
        name="conv_ffn",
    )(*args)


def _ffn(xs, mod, g2, wu, cw, cb, wd, *, layer, dm, need_ctx):
    rows = xs.shape[0]
    n_lat_rows, tm = dm["n_lat_rows"], dm["tm"]
    big = 2 * tm
    out_rows = rows if need_ctx else n_lat_rows
    y = _ffn_call(xs, mod, g2, wu, cw, cb, wd, None, layer=layer, dm=dm, tm=big, tile_off=0,
                  n_tiles=n_lat_rows // big, out_rows=out_rows)
    if need_ctx:
        y = _ffn_call(xs, mod, g2, wu, cw, cb, wd, y, layer=layer, dm=dm, tm=tm, tile_off=n_lat_rows // tm,
                      n_tiles=1, out_rows=out_rows)
    return y


def _qkv_kernel(*refs, block_kinds, rope):
    if rope:
        x_ref, mod_ref, g_ref, w_ref, qg_ref, kg_ref, cos_ref, sa_ref, sb_ref, o_ref, h_scr = refs
    else:
        x_ref, mod_ref, g_ref, w_ref, qg_ref, kg_ref, o_ref, h_scr = refs
    j = pl.program_id(1)

    @pl.when(j == 0)
    def _():
        h_scr[...] = _normmod(x_ref[...], g_ref[...], mod_ref[0, 0:1, :], mod_ref[0, 1:2, :]).astype(BF16)

    acc = _dot(h_scr[...], w_ref[...])

    def emit(kinds):
        for hh, kind in enumerate(kinds):
            hs = slice(hh * ATT_HEAD, (hh + 1) * ATT_HEAD)
            y = acc[:, hs]
            if kind != "v":
                gain = qg_ref[...] if kind == "q" else kg_ref[...]
                y = y * lax.rsqrt(jnp.mean(y * y, axis=-1, keepdims=True) + NORM_EPS) * gain
                if rope:
                    y = (y * cos_ref[...] + pltpu.roll(y, ATT_HEAD - 32, axis=1) * sa_ref[...]
                         + pltpu.roll(y, 32, axis=1) * sb_ref[...])
            o_ref[:, hs] = y.astype(BF16)

    for jb, kinds in enumerate(block_kinds):
        pl.when(j == jb)(functools.partial(emit, kinds))


def _qkv_proj(xs, mod, g1, w, qg, kg, rope_tabs, *, dm, kv_dim):
    rows, d = xs.shape
    n = w.shape[1]
    tn = min(d, 2 * kv_dim)
    hpb = tn // ATT_HEAD
    kinds = ["q"] * (d // ATT_HEAD) + ["k"] * (kv_dim // ATT_HEAD) + ["v"] * (kv_dim // ATT_HEAD)
    block_kinds = tuple(tuple(kinds[b * hpb:(b + 1) * hpb]) for b in range(n // tn))
    tm = dm["tm"]
    modmap = lambda i, j: (jnp.minimum(i // dm["tpb"], dm["nb"]), 0, 0)
    in_specs = [
        pl.BlockSpec((tm, d), lambda i, j: (i, 0)),
        pl.BlockSpec((1, 6, d), modmap),
        pl.BlockSpec((1, d), lambda i, j: (0, 0)),
        pl.BlockSpec((d, tn), lambda i, j: (0, j)),
        pl.BlockSpec((1, ATT_HEAD), lambda i, j: (0, 0)),
        pl.BlockSpec((1, ATT_HEAD), lambda i, j: (0, 0)),
    ]
    args = [xs, mod, g1.reshape(1, d), w, qg.reshape(1, ATT_HEAD), kg.reshape(1, ATT_HEAD)]
    if rope_tabs is not None:
        in_specs += [pl.BlockSpec((tm, ATT_HEAD), lambda i, j: (i, 0))] * 3
        args += list(rope_tabs)
    kern = functools.partial(_qkv_kernel, block_kinds=block_kinds, rope=rope_tabs is not None)
    return pl.pallas_call(
        kern,
        out_shape=jax.ShapeDtypeStruct((rows, n), BF16),
        grid=(rows // tm, n // tn),
        in_specs=in_specs,
        out_specs=pl.BlockSpec((tm, tn), lambda i, j: (i, j)),
        scratch_shapes=[pltpu.VMEM((tm, d), BF16)],
        compiler_params=_cparams(("parallel", "arbitrary")),
        name="qkv_proj",
    )(*args)


def _oproj_kernel(a_ref, w_ref, x_ref, mod_ref, o_ref):
    o_ref[...] = x_ref[...] + mod_ref[0, 2:3, :] * _dot(a_ref[...], w_ref[...])


def _out_proj(a, w, xs, mod, *, dm, n_tiles):
    d = xs.shape[1]
    tm = dm["tm"]
    modmap = lambda i: (jnp.minimum(i // dm["tpb"], dm["nb"]), 0, 0)
    return pl.pallas_call(
        _oproj_kernel,
        out_shape=jax.ShapeDtypeStruct((n_tiles * tm, d), F32),
        grid=(n_tiles,),
        in_specs=[
            pl.BlockSpec((tm, d), lambda i: (i, 0)),
            pl.BlockSpec((d, d), lambda i: (0, 0)),
            pl.BlockSpec((tm, d), lambda i: (i, 0)),
            pl.BlockSpec((1, 6, d), modmap),
        ],
        out_specs=pl.BlockSpec((tm, d), lambda i: (i, 0)),
        compiler_params=_cparams(("parallel",)),
        name="out_proj",
    )(a, w, xs, mod)


def _softmax_pv(q, segs):
    ss = [_dot_nt(q, k) for k, _ in segs]
    m = ss[0].max(axis=-1, keepdims=True)
    for s in ss[1:]:
        m = jnp.maximum(m, s.max(axis=-1, keepdims=True))
    ps = [jnp.exp2(s - m) for s in ss]
    l = ps[0].sum(axis=-1, keepdims=True)
    for p in ps[1:]:
        l = l + p.sum(axis=-1, keepdims=True)
    o = _dot(ps[0].astype(BF16), segs[0][1])
    for p, (_, v) in zip(ps[1:], segs[1:]):
        o = o + _dot(p.astype(BF16), v)
    return o / l


def _lane_tiles(x, op):
    acc = x[:, 0:LANES]
    for j in range(1, x.shape[1] // LANES):
        acc = op(acc, x[:, j * LANES:(j + 1) * LANES])
    return acc


def _gqa_kernel(q_ref, kl_ref, vl_ref, kc_ref, vc_ref, o_ref, s_scr, vx_scr, *, group, n_lat_tiles, kchunk):
    t = pl.program_id(2)
    seq, ctx = kl_ref.shape[0], kc_ref.shape[0]

    @pl.when(t == 0)
    def _():
        vx_scr[0:seq, 0:ATT_HEAD] = vl_ref[...]
        vx_scr[seq:seq + ctx, 0:ATT_HEAD] = vc_ref[...]
        vx_scr[:, ATT_HEAD:2 * ATT_HEAD] = jnp.ones((seq + ctx, ATT_HEAD), BF16)

    def run(chunks):
        m = [None] * group
        mrun = [None] * group
        o = [None] * group
        for g in range(group + 1):
            for k_ref, st, sz, off in chunks:
                if g < group:
                    s = _dot_nt(q_ref[:, g * ATT_HEAD:(g + 1) * ATT_HEAD], k_ref[st:st + sz, :])
                    s_scr[g % 2, :, off:off + sz] = s
                    tmax = _lane_tiles(s, jnp.maximum)
                    mrun[g] = tmax if mrun[g] is None else jnp.maximum(mrun[g], tmax)
                if g >= 1:
                    h = g - 1
                    p = jnp.exp2((s_scr[h % 2, :, off:off + sz] - m[h]).astype(BF16))
                    pv = _dot(p, vx_scr[off:off + sz, :])
                    o[h] = pv if o[h] is None else o[h] + pv
            if g < group:
                m[g] = mrun[g].max(axis=-1, keepdims=True)
        for g in range(group):
            o_ref[:, g * ATT_HEAD:(g + 1) * ATT_HEAD] = (o[g][:, 0:ATT_HEAD] / o[g][:, ATT_HEAD:]).astype(BF16)

    lat_chunks = [(kl_ref, st, kchunk, st) for st in range(0, seq, kchunk)]
    ctx_chunk = (kc_ref, 0, ctx, seq)

    @pl.when(t < n_lat_tiles)
    def _():
        run(lat_chunks + [ctx_chunk])

    @pl.when(t >= n_lat_tiles)
    def _():
        run([ctx_chunk])


def _gqa_attention(qkv, *, dm, d, kv_heads):
    rows = qkv.shape[0]
    nb, seq, ctx = dm["nb"], dm["seq"], dm["ctx"]
    group = d // ATT_HEAD // kv_heads
    gw = group * ATT_HEAD
    tq = min(256, ctx)
    nlt, nct = seq // tq, ctx // tq
    kcol = d // ATT_HEAD
    vcol = kcol + kv_heads
    ctx_blk0 = nb * seq // ctx

    def qmap(b, h, t):
        return (jnp.where(t < nlt, b * nlt + t, nb * nlt + b * nct + (t - nlt)), h)

    kern = functools.partial(_gqa_kernel, group=group, n_lat_tiles=nlt, kchunk=min(512, seq))
    return pl.pallas_call(
        kern,
        out_shape=jax.ShapeDtypeStruct((rows, d), BF16),
        scratch_shapes=[pltpu.VMEM((2, tq, seq + ctx), F32), pltpu.VMEM((seq + ctx, 2 * ATT_HEAD), BF16)],
        grid=(nb, kv_heads, nlt + nct),
        in_specs=[
            pl.BlockSpec((tq, gw), qmap),
            pl.BlockSpec((seq, ATT_HEAD), lambda b, h, t: (b, kcol + h)),
            pl.BlockSpec((seq, ATT_HEAD), lambda b, h, t: (b, vcol + h)),
            pl.BlockSpec((ctx, ATT_HEAD), lambda b, h, t: (ctx_blk0 + b, kcol + h)),
            pl.BlockSpec((ctx, ATT_HEAD), lambda b, h, t: (ctx_blk0 + b, vcol + h)),
        ],
        out_specs=pl.BlockSpec((tq, gw), qmap),
        compiler_params=_cparams(("parallel", "parallel", "arbitrary")),
        name="gqa_attention",
    )(qkv, qkv, qkv, qkv, qkv)


def _na_kernel(q_ref, k_ref, v_ref, kc_ref, vc_ref, bias_ref, o_ref, *, rb, hps, grid_rows, n_row_blocks):
    t = pl.program_id(2)
    win = NA_WIN_R * GRID_W

    @pl.when(t < n_row_blocks)
    def _():
        units = []
        for hh in range(hps):
            hs = slice(hh * ATT_HEAD, (hh + 1) * ATT_HEAD)
            for rr in range(rb):
                r = t * rb + rr
                rs = jnp.clip(r - NA_WIN_R // 2, 0, grid_rows - NA_WIN_R)
                units.append(dict(hh=hh, hs=hs, rows=slice(rr * GRID_W, (rr + 1) * GRID_W), off=r - rs,
                                  start=pl.multiple_of(rs * GRID_W, GRID_W)))
        for un in units:
            q = q_ref[un["rows"], un["hs"]]
            un["sw"] = _dot_nt(q, k_ref[pl.ds(un["start"], win), un["hs"]]) + bias_ref[un["hh"], un["off"]]
            un["sc"] = _dot_nt(q, kc_ref[:, un["hs"]])
        for un in units:
            m = jnp.maximum(un["sw"].max(axis=-1, keepdims=True), un["sc"].max(axis=-1, keepdims=True))
            un["pw"] = jnp.exp2(un["sw"] - m)
            un["pc"] = jnp.exp2(un["sc"] - m)
        for un in units:
            l = un["pw"].sum(axis=-1, keepdims=True) + un["pc"].sum(axis=-1, keepdims=True)
            o = (_dot(un["pw"].astype(BF16), v_ref[pl.ds(un["start"], win), un["hs"]])
                 + _dot(un["pc"].astype(BF16), vc_ref[:, un["hs"]]))
            o_ref[un["rows"], un["hs"]] = (o / l).astype(BF16)

    @pl.when(t >= n_row_blocks)
    def _():
        for hh in range(hps):
            hs = slice(hh * ATT_HEAD, (hh + 1) * ATT_HEAD)
            o_ref[:, hs] = _softmax_pv(q_ref[:, hs], [(kc_ref[:, hs], vc_ref[:, hs])]).astype(BF16)


def _na_bias_table(rpb):
    qc = jnp.arange(GRID_W)
    kc = jnp.arange(GRID_W)
    cs = jnp.clip(qc - NA_WIN_C // 2, 0, GRID_W - NA_WIN_C)
    inwin = (kc[None, :] >= cs[:, None]) & (kc[None, :] < cs[:, None] + NA_WIN_C)
    cidx = kc[None, :] - qc[:, None] + NA_WIN_C - 1
    sel = (cidx[None] == jnp.arange(2 * NA_WIN_C - 1)[:, None, None]) & inwin[None]
    cols = jnp.einsum('hrc,cqk->hrqk', rpb, sel.astype(F32), precision=lax.Precision.HIGHEST)
    cols = jnp.where(inwin[None, None], cols * math.log2(math.e), NEG_BIG)
    tab = jnp.stack([cols[:, NA_WIN_R - 1 - o:2 * NA_WIN_R - 1 - o] for o in range(NA_WIN_R)], axis=1)
    tab = tab.transpose(0, 1, 3, 2, 4)
    return tab.reshape(rpb.shape[0], NA_WIN_R, GRID_W, NA_WIN_R * GRID_W).astype(F32)


def _na_attention(qkv, rpb, *, dm, d):
    rows = qkv.shape[0]
    nb, seq, ctx = dm["nb"], dm["seq"], dm["ctx"]
    heads = d // ATT_HEAD
    grid_rows = seq // GRID_W
    rb = ctx // GRID_W
    nrb = grid_rows // rb
    ctx_blk0 = nb * seq // ctx
    bias = _na_bias_table(rpb)

    def qmap(b, h, t):
        return (jnp.where(t < nrb, b * nrb + t, ctx_blk0 + b), h)

    hps = 4 if heads % 4 == 0 else 2
    hw = hps * ATT_HEAD
    ng = heads // hps
    kern = functools.partial(_na_kernel, rb=rb, hps=hps, grid_rows=grid_rows, n_row_blocks=nrb)
    return pl.pallas_call(
        kern,
        out_shape=jax.ShapeDtypeStruct((rows, d), BF16),
        grid=(nb, ng, nrb + 1),
        in_specs=[
            pl.BlockSpec((ctx, hw), qmap),
            pl.BlockSpec((seq, hw), lambda b, h, t: (b, ng + h)),
            pl.BlockSpec((seq, hw), lambda b, h, t: (b, 2 * ng + h)),
            pl.BlockSpec((ctx, hw), lambda b, h, t: (ctx_blk0 + b, ng + h)),
            pl.BlockSpec((ctx, hw), lambda b, h, t: (ctx_blk0 + b, 2 * ng + h)),
            pl.BlockSpec((hps, NA_WIN_R, GRID_W, NA_WIN_R * GRID_W), lambda b, h, t: (h, 0, 0, 0)),
        ],
        out_specs=pl.BlockSpec((ctx, hw), qmap),
        compiler_params=_cparams(("parallel", "parallel", "arbitrary")),
        name="na_attention",
    )(qkv, qkv, qkv, qkv, qkv, bias)


def _rw_prep_kernel(*refs, mix, te, n_lat_rows, seq, ctx):
    (x_ref, xp_ref, xn_ref, mod_ref, g_ref, mu_ref, w1_ref, a1_ref, g1_ref, w2_ref, a2_ref, g2_ref,
     w0_ref, a0_ref) = refs[:14]
    if mix:
        v1_ref, v2_ref, v0_ref, xm_ref, lw_ref, a_ref, go_ref, vg_ref = refs[14:]
    else:
        xm_ref, lw_ref, a_ref, go_ref = refs[14:]
    i = pl.program_id(0)
    g = g_ref[...]
    shift = mod_ref[0, 0:1, :]
    scale = mod_ref[0, 1:2, :]
    h = _normmod(x_ref[...], g, shift, scale)
    halo = _normmod(jnp.concatenate([xp_ref[...], xn_ref[...]], axis=0), g, shift, scale)
    rid, first, last = _seq_edges(i, te, n_lat_rows, seq, ctx)
    hp, hn = _shift_rows(h, rid, first, last, halo[SUBLANES - 1:SUBLANES], halo[SUBLANES:SUBLANES + 1])
    xx = 0.5 * (hp + hn) - h

    def mixed(p):
        return (h + xx * mu_ref[p:p + 1, :]).astype(BF16)

    xv = mixed(2)
    xm_ref[0] = mixed(0)
    xm_ref[1] = mixed(1)
    xm_ref[2] = xv
    zw = jnp.tanh(_dot(mixed(3), w1_ref[...])).astype(BF16)
    za = _dot(mixed(4), a1_ref[...]).astype(BF16)
    zg = jax.nn.sigmoid(_dot(mixed(5), g1_ref[...])).astype(BF16)
    for dd in range(2):
        sl = slice(dd * LANES, (dd + 1) * LANES)
        wl = w0_ref[dd:dd + 1, :] + _dot(zw[:, sl], w2_ref[dd])
        lw_ref[dd] = (-math.exp(-0.5)) * jax.nn.sigmoid(wl)
        a_ref[dd] = jax.nn.sigmoid(a0_ref[dd:dd + 1, :] + _dot(za[:, sl], a2_ref[dd])).astype(BF16)
    go_ref[...] = _dot(zg, g2_ref[...]).astype(BF16)
    if mix:
        zv = _dot(xv, v1_ref[...]).astype(BF16)
        vg_ref[...] = jax.nn.sigmoid(v0_ref[...] + _dot(zv, v2_ref[...])).astype(BF16)


def _pad_rank(w1, w2):
    r = w1.shape[-1]
    pad = (-r) % LANES
    w1 = jnp.pad(w1, [(0, 0)] * (w1.ndim - 1) + [(0, pad)])
    w2 = jnp.pad(w2, [(0, 0)] * (w2.ndim - 2) + [(0, pad), (0, 0)])
    return w1.astype(BF16), w2.astype(BF16)


def _rw_prep(xs, mod, g1n, mu, w0, w1, w2, a0, a1, a2, g1, g2, vres, *, dm):
    rows, d = xs.shape
    te = dm["te"]
    hb = te // SUBLANES
    last_hb = rows // SUBLANES - 1
    mix = vres is not None
    w1p, w2p = _pad_rank(w1, w2)
    a1p, a2p = _pad_rank(a1, a2)
    w1c = jnp.concatenate([w1p[0], w1p[1]], axis=1)
    a1c = jnp.concatenate([a1p[0], a1p[1]], axis=1)
    rg = g1.shape[1]
    modmap = lambda i: (jnp.minimum((i * te) // dm["seq"], dm["nb"]), 0, 0)
    full = lambda shp: pl.BlockSpec(shp, lambda i: (0,) * len(shp))
    in_specs = [
        pl.BlockSpec((te, d), lambda i: (i, 0)),
        pl.BlockSpec((SUBLANES, d), lambda i: (jnp.maximum(i * hb - 1, 0), 0)),
        pl.BlockSpec((SUBLANES, d), lambda i: (jnp.minimum((i + 1) * hb, last_hb), 0)),
        pl.BlockSpec((1, 6, d), modmap),
        full((1, d)), full((6, d)),
        full((d, 2 * LANES)), full((d, 2 * LANES)), full((d, rg)),
        full((2, LANES, d)), full((2, LANES, d)), full((rg, d)),
        full((2, d)), full((2, d)),
    ]
    args = [xs, xs, xs, mod, g1n.reshape(1, d), mu, w1c, a1c, g1.astype(BF16), w2p, a2p, g2.astype(BF16), w0, a0]
    row_spec = pl.BlockSpec((te, d), lambda i: (i, 0))
    out_shape = [jax.ShapeDtypeStruct((3, rows, d), BF16), jax.ShapeDtypeStruct((2, rows, d), F32),
                 jax.ShapeDtypeStruct((2, rows, d), BF16), jax.ShapeDtypeStruct((rows, d), BF16)]
    out_specs = [pl.BlockSpec((3, te, d), lambda i: (0, i, 0)), pl.BlockSpec((2, te, d), lambda i: (0, i, 0)),
                 pl.BlockSpec((2, te, d), lambda i: (0, i, 0)), row_spec]
    if mix:
        v1p, v2p = _pad_rank(vres[1], vres[2])
        in_specs += [full((d, LANES)), full((LANES, d)), full((1, d))]
        args += [v1p, v2p, vres[0].reshape(1, d)]
        out_shape.append(jax.ShapeDtypeStruct((rows, d), BF16))
        out_specs.append(row_spec)
    kern = functools.partial(_rw_prep_kernel, mix=mix, te=te, n_lat_rows=dm["n_lat_rows"], seq=dm["seq"],
                             ctx=dm["ctx"])
    return pl.pallas_call(
        kern,
        out_shape=out_shape,
        grid=(rows // te,),
        in_specs=in_specs,
        out_specs=out_specs,
        compiler_params=_cparams(("parallel",)),
        name="rwkv_prep",
    )(*args)


def _rkv_kernel(*refs, mix):
    if mix:
        xm_ref, w_ref, vf_ref, vg_ref, o_ref = refs
    else:
        xm_ref, w_ref, o_ref = refs
    acc = _dot(xm_ref[0], w_ref[0])
    if mix:
        p = pl.program_id(1)

        @pl.when(p == 2)
        def _():
            o_ref[0] = (acc + (vf_ref[0].astype(F32) - acc) * vg_ref[...].astype(F32)).astype(BF16)

        @pl.when(p != 2)
        def _():
            o_ref[0] = acc.astype(BF16)
    else:
        o_ref[0] = acc.astype(BF16)


def _rkv_proj(xm, w, v_first, vgate, *, layer, dm):
    _, rows, d = xm.shape
    tm = dm["tm"]
    mix = v_first is not None
    in_specs = [pl.BlockSpec((1, tm, d), lambda i, p: (p, i, 0)),
                pl.BlockSpec((None, 1, d, d), lambda i, p: (layer, p, 0, 0))]
    args = [xm, w]
    if mix:
        in_specs += [pl.BlockSpec((1, tm, d), lambda i, p: (2, i, 0)), pl.BlockSpec((tm, d), lambda i, p: (i, 0))]
        args += [v_first, vgate]
    return pl.pallas_call(
        functools.partial(_rkv_kernel, mix=mix),
        out_shape=jax.ShapeDtypeStruct((3, rows, d), BF16),
        grid=(rows // tm, 3),
        in_specs=in_specs,
        out_specs=pl.BlockSpec((1, tm, d), lambda i, p: (p, i, 0)),
        compiler_params=_cparams(("parallel", "arbitrary")),
        name="rwkv_rkv_proj",
    )(*args)


def _wkv_kernel(rf_ref, kf_ref, vf_ref, lwf_ref, af_ref, rb_ref, kb_ref, vb_ref, lwb_ref, ab_ref,
                kk_ref, ka_ref, rk_ref, yf_ref, bonf_ref, yb_ref, bonb_ref, s_scr, *, gps):
    c = pl.program_id(2)
    ln = WKV_CHUNK
    pw = WKV_PACK * RW_HEAD

    hpt = LANES // RW_HEAD

    @pl.when(c == 0)
    def _():
        s_scr[...] = jnp.zeros_like(s_scr)

    row = lax.broadcasted_iota(jnp.int32, (ln, ln), 0)
    col = lax.broadcasted_iota(jnp.int32, (ln, ln), 1)
    trow = lax.broadcasted_iota(jnp.int32, (ln, pw), 0)
    tsrc = lax.broadcasted_iota(jnp.int32, (ln, pw), 1) & (ln - 1)
    ones64 = _ones_blockdiag64()

    def fmask(cond):
        return jnp.where(cond, 1.0, 0.0)

    def same_block(n):
        sh = n.bit_length() - 1
        return (tsrc >> sh) == (trow >> sh)

    eye = fmask(tsrc == trow)
    base_f = fmask(same_block(WKV_INV_BASE))
    off_f = {}
    n = WKV_INV_BASE
    while n < ln:
        off_f[n] = fmask(jnp.logical_and(same_block(2 * n), jnp.logical_not(same_block(n))))
        n *= 2
    k_k = kk_ref[...]
    k_a = ka_ref[...]
    r_k = rk_ref[...]

    lane = lax.broadcasted_iota(jnp.int32, (ln, LANES), 1)
    half_f = [fmask((lane >> (RW_HEAD.bit_length() - 1)) == hf) for hf in range(hpt)]
    half_b = [hm.astype(BF16) for hm in half_f]
    zeros_b = jnp.zeros((ln, LANES), BF16)

    def bd(z):
        zb = z.astype(BF16)
        blocks = []
        for jh in range(WKV_PACK):
            lt, hf = divmod(jh, hpt)
            piece = zb[:, lt * LANES:(lt + 1) * LANES] * half_b[hf]
            blocks.append(jnp.concatenate([piece if tt == lt else zeros_b for tt in range(pw // LANES)], axis=1))
        return jnp.concatenate(blocks, axis=0)

    streams = ((rf_ref, kf_ref, vf_ref, lwf_ref, af_ref, yf_ref, bonf_ref),
               (rb_ref, kb_ref, vb_ref, lwb_ref, ab_ref, yb_ref, bonb_ref))
    units = []
    for dd, (r_ref, k_ref, v_ref, lw_ref, a_ref, y_ref, bon_ref) in enumerate(streams):
        rev = dd == 1
        r = r_ref[0].astype(F32)
        k = k_ref[0].astype(F32)
        v = v_ref[0].astype(F32)
        a = a_ref[0].astype(F32)
        lw = lw_ref[0]
        kkr = k * k_k
        kd = k * (1.0 + (a - 1.0) * k_a)
        ssq, rkd = _segsum64_mxu([kkr * kkr, r * kd * r_k], ones64, split=False)
        kk = kkr * lax.rsqrt(jnp.maximum(ssq, 1e-24))
        bvec = kk * a
        bon_ref[...] = (rkd * v).astype(BF16)

        tri = jnp.where((col >= row) if rev else (col <= row), 1.0, 0.0).astype(BF16)
        hi = lw.astype(BF16)
        lo = (lw - hi.astype(F32)).astype(BF16)
        cum = _dot(tri, hi) + _dot(tri, lo)
        tot = cum[0:1, :] if rev else cum[ln - 1:ln, :]
        w_inv = jnp.exp(-cum)
        w_end = jnp.exp(tot - cum)
        a_t = -kk * jnp.exp(cum - lw)
        r_t = r * jnp.exp(cum)
        b_t = bvec * w_inv
        k_t = kd * w_inv
        b_e = bvec * w_end
        k_e = kd * w_end
        w_tot = jnp.exp(tot)

        strict = fmask(tsrc > trow) if rev else fmask(tsrc < trow)
        incl = strict + eye

        for gi in range(gps):
            sl = slice(gi * pw, (gi + 1) * pw)
            units.append(dict(
                dd=dd, gi=gi, sl=sl, y_ref=y_ref, strict=strict, incl=incl,
                ar=jnp.concatenate([a_t[:, sl], r_t[:, sl]], axis=0).astype(BF16),
                b_t=b_t[:, sl], k_t=k_t[:, sl], v=v[:, sl], w_tot=w_tot[:, sl],
                bk=jnp.concatenate([b_e[:, sl], k_e[:, sl]], axis=0).astype(BF16)))

    for un in units:
        sb = _dot_nt(un["ar"], bd(un["b_t"]))
        sk = _dot_nt(un["ar"], bd(un["k_t"]))
        un["m_ab"] = sb[:ln] * un["strict"]
        un["p_rb"] = sb[ln:] * un["incl"]
        un["m_ak"] = sk[:ln] * un["strict"]
        un["p_rk"] = sk[ln:] * un["incl"]
    for un in units:
        un["s0"] = s_scr[un["dd"], un["gi"]]
        un["ars"] = _dot_nt(un["ar"], un["s0"].astype(BF16))
        un["mv"] = _dot(jnp.concatenate([un["m_ak"], un["p_rk"]], axis=0).astype(BF16), bd(un["v"]))
    for un in units:
        m0 = un["m_ab"] * base_f
        un["pinv"] = eye + m0
        un["mp"] = _dot(m0.astype(BF16), bd(m0))
    for un in units:
        both = _dot(jnp.concatenate([un["mp"], un["pinv"]], axis=0).astype(BF16), bd(un["mp"]))
        un["pinv"] = un["pinv"] + both[ln:]
        un["mp"] = both[:ln]
    for un in units:
        un["pinv"] = un["pinv"] + _dot(un["pinv"].astype(BF16), bd(un["mp"]))
    n = WKV_INV_BASE
    while n < ln:
        for un in units:
            un["t1"] = _dot((un["m_ab"] * off_f[n]).astype(BF16), bd(un["pinv"]))
        for un in units:
            un["pinv"] = un["pinv"] + _dot(un["pinv"].astype(BF16), bd(un["t1"]))
        n *= 2
    for un in units:
        un["u"] = _dot(un["pinv"].astype(BF16), bd(un["ars"][:ln] + un["mv"][:ln]))
    for un in units:
        un["y_ref"][:, un["sl"]] = (un["ars"][ln:] + _dot(un["p_rb"].astype(BF16), bd(un["u"]))
                                    + un["mv"][ln:]).astype(BF16)
        uv = jnp.concatenate([un["u"], un["v"]], axis=0).astype(BF16)
        res = _dot_tn(uv, un["bk"])
        for jh in range(WKV_PACK):
            lt, hf = divmod(jh, hpt)
            rsl = slice(jh * RW_HEAD, (jh + 1) * RW_HEAD)
            csl = slice(lt * LANES, (lt + 1) * LANES)
            s_scr[un["dd"], un["gi"], rsl, csl] = (un["s0"][rsl, csl] * un["w_tot"][:, csl]
                                                    + res[rsl, csl] * half_f[hf])


def _wkv(rkv, lw, a, k_k, k_a, r_k, *, dm, gps):
    _, rows, d = rkv.shape
    nb, seq, ctx = dm["nb"], dm["seq"], dm["ctx"]
    ln = WKV_CHUNK
    sw = gps * WKV_PACK * RW_HEAD
    ncc, nlc = ctx // ln, seq // ln
    ctx_c0 = nb * seq // ln

    def fblk(b, c):
        return jnp.where(c < ncc, ctx_c0 + b * ncc + c, b * nlc + (c - ncc))

    def bblk(b, c):
        return jnp.where(c < ncc, ctx_c0 + b * ncc + (ncc - 1 - c), b * nlc + (nlc - 1 - (c - ncc)))

    def spec3(p, blk):
        return pl.BlockSpec((1, ln, sw), lambda b, s, c, p=p, blk=blk: (p, blk(b, c), s))

    def spec2(blk):
        return pl.BlockSpec((ln, sw), lambda b, s, c, blk=blk: (blk(b, c), s))

    pspec = pl.BlockSpec((1, sw), lambda b, s, c: (0, s))
    in_specs = [spec3(0, fblk), spec3(1, fblk), spec3(2, fblk), spec3(0, fblk), spec3(0, fblk),
                spec3(0, bblk), spec3(1, bblk), spec3(2, bblk), spec3(1, bblk), spec3(1, bblk),
                pspec, pspec, pspec]
    return pl.pallas_call(
        functools.partial(_wkv_kernel, gps=gps),
        out_shape=[jax.ShapeDtypeStruct((rows, d), BF16)] * 4,
        grid=(nb, d // sw, ncc + nlc),
        in_specs=in_specs,
        out_specs=[spec2(fblk), spec2(fblk), spec2(bblk), spec2(bblk)],
        scratch_shapes=[pltpu.VMEM((2, gps, WKV_PACK * RW_HEAD, WKV_PACK * RW_HEAD), F32)],
        compiler_params=_cparams(("parallel", "parallel", "arbitrary")),
        name="wkv_scan",
    )(rkv, rkv, rkv, lw, a, rkv, rkv, rkv, lw, a, k_k.reshape(1, d), k_a.reshape(1, d), r_k.reshape(1, d))


def _rw_out_kernel(yf_ref, yb_ref, bf_ref, bb_ref, g_ref, x_ref, mod_ref, lg_ref, lb_ref, w_ref, o_ref):
    y = yf_ref[...].astype(F32) + yb_ref[...].astype(F32)
    ones64 = _ones_blockdiag64()
    mean = _segsum64_mxu([y], ones64)[0] * (1.0 / RW_HEAD)
    yc = y - mean
    var = _segsum64_mxu([yc * yc], ones64)[0] * (1.0 / RW_HEAD)
    yn = yc * lax.rsqrt(var + RW_GN_EPS)
    bonus = bf_ref[...].astype(F32) + bb_ref[...].astype(F32)
    o = (yn * lg_ref[...] + lb_ref[...] + bonus) * g_ref[...].astype(F32)
    o_ref[...] = x_ref[...] + mod_ref[0, 2:3, :] * _dot(o.astype(BF16), w_ref[...])


def _rw_out(yf, bonf, yb, bonb, g, xs, mod, ln_g, ln_b, w_o, *, dm, n_rows):
    d = xs.shape[1]
    te = dm["te"]
    modmap = lambda i: (jnp.minimum((i * te) // dm["seq"], dm["nb"]), 0, 0)
    rspec = pl.BlockSpec((te, d), lambda i: (i, 0))
    vspec = pl.BlockSpec((1, d), lambda i: (0, 0))
    return pl.pallas_call(
        _rw_out_kernel,
        out_shape=jax.ShapeDtypeStruct((n_rows, d), F32),
        grid=(n_rows // te,),
        in_specs=[rspec, rspec, rspec, rspec, rspec, rspec, pl.BlockSpec((1, 6, d), modmap), vspec, vspec,
                  pl.BlockSpec((d, d), lambda i: (0, 0))],
        out_specs=rspec,
        compiler_params=_cparams(("parallel",)),
        name="rwkv_out",
    )(yf, yb, bonf, bonb, g, xs, mod, ln_g.reshape(1, d), ln_b.reshape(1, d), w_o)


def _rope_tables(dm):
    seq, nb, ctx = dm["seq"], dm["nb"], dm["ctx"]
    t = jnp.arange(seq, dtype=jnp.int32)
    pos = jnp.stack([t // GRID_W, t % GRID_W], axis=-1).astype(F32)
    n_freq = ATT_HEAD // 4
    inv = ROPE_THETA ** (-jnp.arange(n_freq, dtype=F32) / n_freq)
    ang = pos[:, :, None] * inv
    cos, sin = jnp.cos(ang), jnp.sin(ang)
    zero = jnp.zeros_like(sin)
    cos_t = jnp.stack([cos, cos], axis=2).reshape(seq, ATT_HEAD)
    sa_t = jnp.stack([-sin, zero], axis=2).reshape(seq, ATT_HEAD)
    sb_t = jnp.stack([zero, sin], axis=2).reshape(seq, ATT_HEAD)
    nctx = nb * ctx
    full = lambda tab, fill: jnp.concatenate([jnp.tile(tab, (nb, 1)), jnp.full((nctx, ATT_HEAD), fill, F32)], axis=0)
    return full(cos_t, 1.0), full(sa_t, 0.0), full(sb_t, 0.0)


def kernel(x, c, ctx, c_ctx, mod_w, mod_b, norm1_g, norm2_g, ffn_up, ffn_conv_w, ffn_conv_b, ffn_down, rw_mu, rw_w_rkv, rw_w0, rw_w1, rw_w2, rw_a0, rw_a1, rw_a2, rw_g1, rw_g2, rw_k_k, rw_k_a, rw_r_k, rw_ln_g, rw_ln_b, rw_w_o, rw_v0, rw_v1, rw_v2, na_w_qkv, na_q_g, na_k_g, na_rpb, na_w_o, ga_w_qkv, ga_q_g, ga_k_g, ga_w_o):
    nb, seq, d = x.shape
    nctx = ctx.shape[1]
    depth = mod_w.shape[0]
    tm = nb * nctx
    assert seq % tm == 0 and seq & (seq - 1) == 0 and nctx & (nctx - 1) == 0 and nb + 1 <= SUBLANES
    assert seq // GRID_W >= NA_WIN_R and nctx % GRID_W == 0
    dm = dict(nb=nb, seq=seq, ctx=nctx, tm=tm, te=tm // 2, tpb=seq // tm, n_lat_rows=nb * seq)
    n_lat_tiles = nb * seq // tm
    n_tiles = n_lat_tiles + 1
    att_scale = ATT_HEAD ** -0.5 * math.log2(math.e)

    xs = jnp.concatenate([x.reshape(nb * seq, d), ctx.reshape(nb * nctx, d)], axis=0)
    c_all = jnp.concatenate([c, c_ctx[None], jnp.zeros((SUBLANES - nb - 1, d), F32)], axis=0)
    mods = _modulations(c_all, mod_w, mod_b)
    rope_tabs = None
    v_first = None
    ffn_wu, ffn_wd = _ffn_weights(ffn_up, ffn_down)
    rw_rkv_w = rw_w_rkv.astype(BF16)

    for i in range(depth):
        kind, j = i % 3, i // 3
        need_ctx = i < depth - 1
        nt_out = n_tiles if need_ctx else n_lat_tiles
        mod = mods[i, :nb + 1].reshape(nb + 1, 6, d)
        if kind == 0:
            vres = None if j == 0 else (rw_v0[j - 1], rw_v1[j - 1], rw_v2[j - 1])
            prep = _rw_prep(xs, mod, norm1_g[i], rw_mu[j], rw_w0[j], rw_w1[j], rw_w2[j], rw_a0[j], rw_a1[j],
                            rw_a2[j], rw_g1[j], rw_g2[j], vres, dm=dm)
            xm, lw, a, g = prep[0], prep[1], prep[2], prep[3]
            rkv = _rkv_proj(xm, rw_rkv_w, v_first if vres is not None else None,
                            prep[4] if vres is not None else None, layer=j, dm=dm)
            if v_first is None:
                v_first = rkv
            yf, bonf, yb, bonb = _wkv(rkv, lw, a, rw_k_k[j], rw_k_a[j], rw_r_k[j].reshape(-1), dm=dm,
                                      gps=min(8, d // (WKV_PACK * RW_HEAD)))
            xs = _rw_out(yf, bonf, yb, bonb, g, xs, mod, rw_ln_g[j], rw_ln_b[j], rw_w_o[j].astype(BF16),
                         dm=dm, n_rows=nt_out * tm)
        elif kind == 1:
            qkv = _qkv_proj(xs, mod, norm1_g[i], na_w_qkv[j].astype(BF16), na_q_g[j] * att_scale, na_k_g[j],
                            None, dm=dm, kv_dim=d)
            o = _na_attention(qkv, na_rpb[j], dm=dm, d=d)
            xs = _out_proj(o, na_w_o[j].astype(BF16), xs, mod, dm=dm, n_tiles=nt_out)
        else:
            if rope_tabs is None:
                rope_tabs = _rope_tables(dm)
            kv_dim = (ga_w_qkv.shape[-1] - d) // 2
            qkv = _qkv_proj(xs, mod, norm1_g[i], ga_w_qkv[j].astype(BF16), ga_q_g[j] * att_scale, ga_k_g[j],
                            rope_tabs, dm=dm, kv_dim=kv_dim)
            o = _gqa_attention(qkv, dm=dm, d=d, kv_heads=kv_dim // ATT_HEAD)
            xs = _out_proj(o, ga_w_o[j].astype(BF16), xs, mod, dm=dm, n_tiles=nt_out)
        xs = _ffn(xs, mod, norm2_g[i], ffn_wu, ffn_conv_w[i], ffn_conv_b[i], ffn_wd, layer=i, dm=dm,
                  need_ctx=need_ctx)
    return xs[:nb * seq].reshape(nb, seq, d)
```

```python
import functools
import math

import jax
import jax.numpy as jnp
from jax import lax
from jax.experimental import pallas as pl
from jax.experimental.pallas import tpu as pltpu

F32 = jnp.float32
BF16 = jnp.bfloat16

NORM_EPS = 1e-6
GRID_W = 64
ATT_HEAD = 128
RW_HEAD = 64
NA_WIN_R = 8
NA_WIN_C = 16
ROPE_THETA = 10000.0
RW_GN_EPS = 64e-5
LANES = 128
SUBLANES = 8
WKV_CHUNK = 64
WKV_PACK = 4
WKV_INV_BASE = 8
FFN_ROW_PIECES = 2
VMEM_LIMIT = 56 * 1024 * 1024
NEG_BIG = -1e30


def _cparams(sem):
    return pltpu.CompilerParams(dimension_semantics=sem, vmem_limit_bytes=VMEM_LIMIT)


def _dot(a, b):
    return jnp.dot(a, b, preferred_element_type=F32)


def _dot_nt(a, b):
    return lax.dot_general(a, b, (((1,), (1,)), ((), ())), preferred_element_type=F32)


def _dot_tn(a, b):
    return lax.dot_general(a, b, (((0,), (0,)), ((), ())), preferred_element_type=F32)


def _normmod(x, g, shift, scale):
    ms = jnp.mean(x * x, axis=-1, keepdims=True)
    y = x * lax.rsqrt(ms + NORM_EPS)
    return (y * g) * (1.0 + scale) + shift


def _silu(x):
    return x * jax.nn.sigmoid(x)


def _seq_edges(tile, rows, n_lat_rows, seq, ctx):
    rid = lax.broadcasted_iota(jnp.int32, (rows, 1), 0)
    base = tile * rows
    period = jnp.where(base >= n_lat_rows, ctx, seq)
    pos = (base + rid) & (period - 1)
    return rid, pos == 0, pos == period - 1


def _shift_rows(u, rid, first, last, prev_row, next_row):
    n = u.shape[0]
    up = pltpu.roll(u, 1, axis=0)
    up = jnp.where(rid == 0, prev_row, up)
    up = jnp.where(first, 0.0, up)
    un = pltpu.roll(u, n - 1, axis=0)
    un = jnp.where(rid == n - 1, next_row, un)
    un = jnp.where(last, 0.0, un)
    return up, un


def _ones_blockdiag64():
    sh = RW_HEAD.bit_length() - 1
    r = lax.broadcasted_iota(jnp.int32, (LANES, LANES), 0) >> sh
    c = lax.broadcasted_iota(jnp.int32, (LANES, LANES), 1) >> sh
    return jnp.where(r == c, 1.0, 0.0).astype(BF16)


def _segsum64_mxu(xs, ones, split=True):
    m, n = xs[0].shape
    nslab = n // LANES
    nparts = 2 if split else 1
    pieces = []
    for x in xs:
        hi = x.astype(BF16)
        parts = (hi, (x - hi.astype(F32)).astype(BF16)) if split else (hi,)
        for part in parts:
            pieces += [part[:, c * LANES:(c + 1) * LANES] for c in range(nslab)]
    res = _dot(jnp.concatenate(pieces, axis=0), ones)
    outs = []
    for i in range(len(xs)):
        base = i * nparts * nslab
        cols = []
        for c in range(nslab):
            col = res[(base + c) * m:(base + c + 1) * m]
            if split:
                col = col + res[(base + nslab + c) * m:(base + nslab + c + 1) * m]
            cols.append(col)
        outs.append(jnp.concatenate(cols, axis=1))
    return outs


def _mod_kernel(c_ref, w_ref, b_ref, o_ref):
    s = _silu(c_ref[...]).astype(BF16)
    o_ref[0] = _dot(s, w_ref[0].astype(BF16)) + b_ref[0]


def _modulations(c_all, mod_w, mod_b):
    depth, d, n = mod_w.shape
    tn = n // 8
    return pl.pallas_call(
        _mod_kernel,
        out_shape=jax.ShapeDtypeStruct((depth, SUBLANES, n), F32),
        grid=(depth, n // tn),
        in_specs=[
            pl.BlockSpec((SUBLANES, d), lambda l, j: (0, 0)),
            pl.BlockSpec((1, d, tn), lambda l, j: (l, 0, j)),
            pl.BlockSpec((1, 1, tn), lambda l, j: (l, 0, j)),
        ],
        out_specs=pl.BlockSpec((1, SUBLANES, tn), lambda l, j: (l, 0, j)),
        compiler_params=_cparams(("parallel", "parallel")),
        name="modulation",
    )(c_all, mod_w, mod_b.reshape(depth, 1, n))


def _ffn_kernel(*refs, tm, tile_off, n_lat_rows, seq, ctx):
    x_ref, xp_ref, xn_ref, mod_ref, g_ref, wug_ref, wuv_ref, cp_ref, wd_ref = refs[:9]
    o_ref, h_scr = refs[-2:]
    i = pl.program_id(0) + tile_off
    j = pl.program_id(1)
    cpg = cp_ref[j]
    cpv = cp_ref[pl.num_programs(1) + j]
    shift = mod_ref[0, 3:4, :]
    scale = mod_ref[0, 4:5, :]
    hr = 2 * SUBLANES
    th = tm // FFN_ROW_PIECES

    @pl.when(j == 0)
    def _():
        g = g_ref[...]
        halo = jnp.concatenate([xp_ref[...], xn_ref[...]], axis=0)
        h_scr[0:hr, :] = _normmod(halo, g, shift, scale).astype(BF16)
        h_scr[hr:hr + tm, :] = _normmod(x_ref[...], g, shift, scale).astype(BF16)
        o_ref[...] = jnp.zeros_like(o_ref)

    rid, first, last = _seq_edges(i, tm, n_lat_rows, seq, ctx)
    rid_h = rid[0:th]

    def up_proj(p):
        lo = 0 if p == 0 else hr + p * th
        hp = h_scr[lo:hr + (p + 1) * th, :]
        ug, uv = _dot(hp, wug_ref[...]), _dot(hp, wuv_ref[...])
        if p == 0:
            return dict(g=ug[hr:], v=uv[hr:], halo_g=ug[0:hr], halo_v=uv[0:hr])
        return dict(g=ug, v=uv)

    def conv(main, prev_row, next_row, fm, lm, cp):
        up, un = _shift_rows(main, rid_h, fm, lm, prev_row, next_row)
        return cp[3:4, :] + up * cp[0:1, :] + main * cp[1:2, :] + un * cp[2:3, :]

    pv, nx = SUBLANES - 1, SUBLANES

    def act_down(p, us):
        rs = slice(p * th, (p + 1) * th)
        rows = {}
        for key in ("g", "v"):
            prev_row = us[0]["halo_" + key][pv:pv + 1] if p == 0 else us[p - 1][key][th - 1:th]
            next_row = us[0]["halo_" + key][nx:nx + 1] if p == FFN_ROW_PIECES - 1 else us[p + 1][key][0:1]
            rows[key] = (prev_row, next_row)
        a = (_silu(conv(us[p]["g"], *rows["g"], first[rs], last[rs], cpg))
             * conv(us[p]["v"], *rows["v"], first[rs], last[rs], cpv))
        o_ref[rs, :] += _dot(a.astype(BF16), wd_ref[...])

    us = {0: up_proj(0)}
    for p in range(FFN_ROW_PIECES):
        if p + 1 < FFN_ROW_PIECES:
            us[p + 1] = up_proj(p + 1)
        act_down(p, us)

    @pl.when(j == pl.num_programs(1) - 1)
    def _():
        o_ref[...] = x_ref[...] + mod_ref[0, 5:6, :] * o_ref[...]


FFN_COLS = 512


def _ffn_weights(ffn_up, ffn_down):
    return ffn_up.astype(BF16), ffn_down.astype(BF16)


def _ffn_call(xs, mod, g2, wu, cw, cb, wd, prev, *, layer, dm, tm, tile_off, n_tiles, out_rows):
    rows, d = xs.shape
    f = wd.shape[1]
    fc = FFN_COLS
    nfc = f // fc
    hb = tm // SUBLANES
    last_hb = rows // SUBLANES - 1
    seq, nb = dm["seq"], dm["nb"]
    assert seq % tm == 0 or tile_off * tm >= dm["n_lat_rows"]
    modmap = lambda i, j: (jnp.minimum(((i + tile_off) * tm) // seq, nb), 0, 0)
    kern = functools.partial(_ffn_kernel, tm=tm, tile_off=tile_off,
                             n_lat_rows=dm["n_lat_rows"], seq=seq, ctx=dm["ctx"])
    in_specs = [
        pl.BlockSpec((tm, d), lambda i, j: (i + tile_off, 0), pipeline_mode=pl.Buffered(1)),
        pl.BlockSpec((SUBLANES, d), lambda i, j: (jnp.maximum((i + tile_off) * hb - 1, 0), 0)),
        pl.BlockSpec((SUBLANES, d), lambda i, j: (jnp.minimum((i + tile_off + 1) * hb, last_hb), 0)),
        pl.BlockSpec((1, 6, d), modmap),
        pl.BlockSpec((1, d), lambda i, j: (0, 0)),
        pl.BlockSpec((None, d, fc), lambda i, j: (layer, 0, j)),
        pl.BlockSpec((None, d, fc), lambda i, j: (layer, 0, nfc + j)),
        pl.BlockSpec((2 * nfc, 4, fc), lambda i, j: (0, 0, 0)),
        pl.BlockSpec((None, fc, d), lambda i, j: (layer, j, 0)),
    ]
    cp = jnp.concatenate([cw, cb.reshape(1, -1)], axis=0).reshape(4, 2 * nfc, fc).transpose(1, 0, 2)
    args = [xs, xs, xs, mod, g2.reshape(1, d), wu, wu, cp, wd]
    aliases = {}
    if prev is not None:
        in_specs.append(pl.BlockSpec(memory_space=pl.ANY))
        args.append(prev)
        aliases = {len(args) - 1: 0}
    return pl.pallas_call(
        kern,
        out_shape=jax.ShapeDtypeStruct((out_rows, d), F32),
        grid=(n_tiles, nfc),
        in_specs=in_specs,
        out_specs=pl.BlockSpec((tm, d), lambda i, j: (i + tile_off, 0)),
        scratch_shapes=[pltpu.VMEM((tm + 2 * SUBLANES, d), BF16)],
        input_output_aliases=aliases,
        compiler_params=_cparams(("parallel", "arbitrary")),
        name="conv_ffn",
    )(*args)


def _ffn(xs, mod, g2, wu, cw, cb, wd, *, layer, dm, need_ctx):
    rows = xs.shape[0]
    n_lat_rows, tm = dm["n_lat_rows"], dm["tm"]
    big = 2 * tm
    out_rows = rows if need_ctx else n_lat_rows
    y = _ffn_call(xs, mod, g2, wu, cw, cb, wd, None, layer=layer, dm=dm, tm=big, tile_off=0,
                  n_tiles=n_lat_rows // big, out_rows=out_rows)
    if need_ctx:
        y = _ffn_call(xs, mod, g2, wu, cw, cb, wd, y, layer=layer, dm=dm, tm=tm, tile_off=n_lat_rows // tm,
                      n_tiles=1, out_rows=out_rows)
    return y


def _qkv_kernel(*refs, block_kinds, rope):
    if rope:
        x_ref, mod_ref, g_ref, w_ref, qg_ref, kg_ref, cos_ref, sa_ref, sb_ref, o_ref, h_scr = refs
    else:
        x_ref, mod_ref, g_ref, w_ref, qg_ref, kg_ref, o_ref, h_scr = refs
    j = pl.program_id(1)

    @pl.when(j == 0)
    def _():
        h_scr[...] = _normmod(x_ref[...], g_ref[...], mod_ref[0, 0:1, :], mod_ref[0, 1:2, :]).astype(BF16)

    acc = _dot(h_scr[...], w_ref[...])

    def emit(kinds):
        for hh, kind in enumerate(kinds):
            hs = slice(hh * ATT_HEAD, (hh + 1) * ATT_HEAD)
            y = acc[:, hs]
            if kind != "v":
                gain = qg_ref[...] if kind == "q" else kg_ref[...]
                y = y * lax.rsqrt(jnp.mean(y * y, axis=-1, keepdims=True) + NORM_EPS) * gain
                if rope:
                    y = (y * cos_ref[...] + pltpu.roll(y, ATT_HEAD - 32, axis=1) * sa_ref[...]
                         + pltpu.roll(y, 32, axis=1) * sb_ref[...])
            o_ref[:, hs] = y.astype(BF16)

    for jb, kinds in enumerate(block_kinds):
        pl.when(j == jb)(functools.partial(emit, kinds))


def _qkv_proj(xs, mod, g1, w, qg, kg, rope_tabs, *, dm, kv_dim):
    rows, d = xs.shape
    n = w.shape[1]
    tn = min(d, 2 * kv_dim)
    hpb = tn // ATT_HEAD
    kinds = ["q"] * (d // ATT_HEAD) + ["k"] * (kv_dim // ATT_HEAD) + ["v"] * (kv_dim // ATT_HEAD)
    block_kinds = tuple(tuple(kinds[b * hpb:(b + 1) * hpb]) for b in range(n // tn))
    tm = dm["tm"]
    modmap = lambda i, j: (jnp.minimum(i // dm["tpb"], dm["nb"]), 0, 0)
    in_specs = [
        pl.BlockSpec((tm, d), lambda i, j: (i, 0)),
        pl.BlockSpec((1, 6, d), modmap),
        pl.BlockSpec((1, d), lambda i, j: (0, 0)),
        pl.BlockSpec((d, tn), lambda i, j: (0, j)),
        pl.BlockSpec((1, ATT_HEAD), lambda i, j: (0, 0)),
        pl.BlockSpec((1, ATT_HEAD), lambda i, j: (0, 0)),
    ]
    args = [xs, mod, g1.reshape(1, d), w, qg.reshape(1, ATT_HEAD), kg.reshape(1, ATT_HEAD)]
    if rope_tabs is not None:
        in_specs += [pl.BlockSpec((tm, ATT_HEAD), lambda i, j: (i, 0))] * 3
        args += list(rope_tabs)
    kern = functools.partial(_qkv_kernel, block_kinds=block_kinds, rope=rope_tabs is not None)
    return pl.pallas_call(
        kern,
        out_shape=jax.ShapeDtypeStruct((rows, n), BF16),
        grid=(rows // tm, n // tn),
        in_specs=in_specs,
        out_specs=pl.BlockSpec((tm, tn), lambda i, j: (i, j)),
        scratch_shapes=[pltpu.VMEM((tm, d), BF16)],
        compiler_params=_cparams(("parallel", "arbitrary")),
        name="qkv_proj",
    )(*args)


def _oproj_kernel(a_ref, w_ref, x_ref, mod_ref, o_ref):
    o_ref[...] = x_ref[...] + mod_ref[0, 2:3, :] * _dot(a_ref[...], w_ref[...])


def _out_proj(a, w, xs, mod, *, dm, n_tiles):
    d = xs.shape[1]
    tm = dm["tm"]
    modmap = lambda i: (jnp.minimum(i // dm["tpb"], dm["nb"]), 0, 0)
    return pl.pallas_call(
        _oproj_kernel,
        out_shape=jax.ShapeDtypeStruct((n_tiles * tm, d), F32),
        grid=(n_tiles,),
        in_specs=[
            pl.BlockSpec((tm, d), lambda i: (i, 0)),
            pl.BlockSpec((d, d), lambda i: (0, 0)),
            pl.BlockSpec((tm, d), lambda i: (i, 0)),
            pl.BlockSpec((1, 6, d), modmap),
        ],
        out_specs=pl.BlockSpec((tm, d), lambda i: (i, 0)),
        compiler_params=_cparams(("parallel",)),
        name="out_proj",
    )(a, w, xs, mod)


def _softmax_pv(q, segs):
    ss = [_dot_nt(q, k) for k, _ in segs]
    m = ss[0].max(axis=-1, keepdims=True)
    for s in ss[1:]:
        m = jnp.maximum(m, s.max(axis=-1, keepdims=True))
    ps = [jnp.exp2(s - m) for s in ss]
    l = ps[0].sum(axis=-1, keepdims=True)
    for p in ps[1:]:
        l = l + p.sum(axis=-1, keepdims=True)
    o = _dot(ps[0].astype(BF16), segs[0][1])
    for p, (_, v) in zip(ps[1:], segs[1:]):
        o = o + _dot(p.astype(BF16), v)
    return o / l


def _lane_tiles(x, op):
    acc = x[:, 0:LANES]
    for j in range(1, x.shape[1] // LANES):
        acc = op(acc, x[:, j * LANES:(j + 1) * LANES])
    return acc


def _gqa_kernel(q_ref, kl_ref, vl_ref, kc_ref, vc_ref, o_ref, s_scr, vx_scr, *, group, n_lat_tiles, kchunk):
    t = pl.program_id(2)
    seq, ctx = kl_ref.shape[0], kc_ref.shape[0]

    @pl.when(t == 0)
    def _():
        vx_scr[0:seq, 0:ATT_HEAD] = vl_ref[...]
        vx_scr[seq:seq + ctx, 0:ATT_HEAD] = vc_ref[...]
        vx_scr[:, ATT_HEAD:2 * ATT_HEAD] = jnp.ones((seq + ctx, ATT_HEAD), BF16)

    def run(chunks):
        m = [None] * group
        mrun = [None] * group
        o = [None] * group
        for g in range(group + 1):
            for k_ref, st, sz, off in chunks:
                if g < group:
                    s = _dot_nt(q_ref[:, g * ATT_HEAD:(g + 1) * ATT_HEAD], k_ref[st:st + sz, :])
                    s_scr[g % 2, :, off:off + sz] = s
                    tmax = _lane_tiles(s, jnp.maximum)
                    mrun[g] = tmax if mrun[g] is None else jnp.maximum(mrun[g], tmax)
                if g >= 1:
                    h = g - 1
                    p = jnp.exp2((s_scr[h % 2, :, off:off + sz] - m[h]).astype(BF16))
                    pv = _dot(p, vx_scr[off:off + sz, :])
                    o[h] = pv if o[h] is None else o[h] + pv
            if g < group:
                m[g] = mrun[g].max(axis=-1, keepdims=True)
        for g in range(group):
            o_ref[:, g * ATT_HEAD:(g + 1) * ATT_HEAD] = (o[g][:, 0:ATT_HEAD] / o[g][:, ATT_HEAD:]).astype(BF16)

    lat_chunks = [(kl_ref, st, kchunk, st) for st in range(0, seq, kchunk)]
    ctx_chunk = (kc_ref, 0, ctx, seq)

    @pl.when(t < n_lat_tiles)
    def _():
        run(lat_chunks + [ctx_chunk])

    @pl.when(t >= n_lat_tiles)
    def _():
        run([ctx_chunk])


def _gqa_attention(qkv, *, dm, d, kv_heads):
    rows = qkv.shape[0]
    nb, seq, ctx = dm["nb"], dm["seq"], dm["ctx"]
    group = d // ATT_HEAD // kv_heads
    gw = group * ATT_HEAD
    tq = min(256, ctx)
    nlt, nct = seq // tq, ctx // tq
    kcol = d // ATT_HEAD
    vcol = kcol + kv_heads
    ctx_blk0 = nb * seq // ctx

    def qmap(b, h, t):
        return (jnp.where(t < nlt, b * nlt + t, nb * nlt + b * nct + (t - nlt)), h)

    kern = functools.partial(_gqa_kernel, group=group, n_lat_tiles=nlt, kchunk=min(512, seq))
    return pl.pallas_call(
        kern,
        out_shape=jax.ShapeDtypeStruct((rows, d), BF16),
        scratch_shapes=[pltpu.VMEM((2, tq, seq + ctx), F32), pltpu.VMEM((seq + ctx, 2 * ATT_HEAD), BF16)],
        grid=(nb, kv_heads, nlt + nct),
        in_specs=[
            pl.BlockSpec((tq, gw), qmap),
            pl.BlockSpec((seq, ATT_HEAD), lambda b, h, t: (b, kcol + h)),
            pl.BlockSpec((seq, ATT_HEAD), lambda b, h, t: (b, vcol + h)),
            pl.BlockSpec((ctx, ATT_HEAD), lambda b, h, t: (ctx_blk0 + b, kcol + h)),
            pl.BlockSpec((ctx, ATT_HEAD), lambda b, h, t: (ctx_blk0 + b, vcol + h)),
        ],
        out_specs=pl.BlockSpec((tq, gw), qmap),
        compiler_params=_cparams(("parallel", "parallel", "arbitrary")),
        name="gqa_attention",
    )(qkv, qkv, qkv, qkv, qkv)


def _na_kernel(q_ref, k_ref, v_ref, kc_ref, vc_ref, bias_ref, o_ref, *, rb, hps, grid_rows, n_row_blocks):
    t = pl.program_id(2)
    win = NA_WIN_R * GRID_W

    @pl.when(t < n_row_blocks)
    def _():
        units = []
        for hh in range(hps):
            hs = slice(hh * ATT_HEAD, (hh + 1) * ATT_HEAD)
            for rr in range(rb):
                r = t * rb + rr
                rs = jnp.clip(r - NA_WIN_R // 2, 0, grid_rows - NA_WIN_R)
                units.append(dict(hh=hh, hs=hs, rows=slice(rr * GRID_W, (rr + 1) * GRID_W), off=r - rs,
                                  start=pl.multiple_of(rs * GRID_W, GRID_W)))
        for un in units:
            q = q_ref[un["rows"], un["hs"]]
            un["sw"] = _dot_nt(q, k_ref[pl.ds(un["start"], win), un["hs"]]) + bias_ref[un["hh"], un["off"]]
            un["sc"] = _dot_nt(q, kc_ref[:, un["hs"]])
        for un in units:
            m = jnp.maximum(un["sw"].max(axis=-1, keepdims=True), un["sc"].max(axis=-1, keepdims=True))
            un["pw"] = jnp.exp2(un["sw"] - m)
            un["pc"] = jnp.exp2(un["sc"] - m)
        for un in units:
            l = un["pw"].sum(axis=-1, keepdims=True) + un["pc"].sum(axis=-1, keepdims=True)
            o = (_dot(un["pw"].astype(BF16), v_ref[pl.ds(un["start"], win), un["hs"]])
                 + _dot(un["pc"].astype(BF16), vc_ref[:, un["hs"]]))
            o_ref[un["rows"], un["hs"]] = (o / l).astype(BF16)

    @pl.when(t >= n_row_blocks)
    def _():
        for hh in range(hps):
            hs = slice(hh * ATT_HEAD, (hh + 1) * ATT_HEAD)
            o_ref[:, hs] = _softmax_pv(q_ref[:, hs], [(kc_ref[:, hs], vc_ref[:, hs])]).astype(BF16)


def _na_bias_table(rpb):
    qc = jnp.arange(GRID_W)
    kc = jnp.arange(GRID_W)
    cs = jnp.clip(qc - NA_WIN_C // 2, 0, GRID_W - NA_WIN_C)
    inwin = (kc[None, :] >= cs[:, None]) & (kc[None, :] < cs[:, None] + NA_WIN_C)
    cidx = kc[None, :] - qc[:, None] + NA_WIN_C - 1
    sel = (cidx[None] == jnp.arange(2 * NA_WIN_C - 1)[:, None, None]) & inwin[None]
    cols = jnp.einsum('hrc,cqk->hrqk', rpb, sel.astype(F32), precision=lax.Precision.HIGHEST)
    cols = jnp.where(inwin[None, None], cols * math.log2(math.e), NEG_BIG)
    tab = jnp.stack([cols[:, NA_WIN_R - 1 - o:2 * NA_WIN_R - 1 - o] for o in range(NA_WIN_R)], axis=1)
    tab = tab.transpose(0, 1, 3, 2, 4)
    return tab.reshape(rpb.shape[0], NA_WIN_R, GRID_W, NA_WIN_R * GRID_W).astype(F32)


def _na_attention(qkv, rpb, *, dm, d):
    rows = qkv.shape[0]
    nb, seq, ctx = dm["nb"], dm["seq"], dm["ctx"]
    heads = d // ATT_HEAD
    grid_rows = seq // GRID_W
    rb = ctx // GRID_W
    nrb = grid_rows // rb
    ctx_blk0 = nb * seq // ctx
    bias = _na_bias_table(rpb)

    def qmap(b, h, t):
        return (jnp.where(t < nrb, b * nrb + t, ctx_blk0 + b), h)

    hps = 4 if heads % 4 == 0 else 2
    hw = hps * ATT_HEAD
    ng = heads // hps
    kern = functools.partial(_na_kernel, rb=rb, hps=hps, grid_rows=grid_rows, n_row_blocks=nrb)
    return pl.pallas_call(
        kern,
        out_shape=jax.ShapeDtypeStruct((rows, d), BF16),
        grid=(nb, ng, nrb + 1),
        in_specs=[
            pl.BlockSpec((ctx, hw), qmap),
            pl.BlockSpec((seq, hw), lambda b, h, t: (b, ng + h)),
            pl.BlockSpec((seq, hw), lambda b, h, t: (b, 2 * ng + h)),
            pl.BlockSpec((ctx, hw), lambda b, h, t: (ctx_blk0 + b, ng + h)),
            pl.BlockSpec((ctx, hw), lambda b, h, t: (ctx_blk0 + b, 2 * ng + h)),
            pl.BlockSpec((hps, NA_WIN_R, GRID_W, NA_WIN_R * GRID_W), lambda b, h, t: (h, 0, 0, 0)),
        ],
        out_specs=pl.BlockSpec((ctx, hw), qmap),
        compiler_params=_cparams(("parallel", "parallel", "arbitrary")),
        name="na_attention",
    )(qkv, qkv, qkv, qkv, qkv, bias)


def _rw_prep_kernel(*refs, mix, te, n_lat_rows, seq, ctx):
    (x_ref, xp_ref, xn_ref, mod_ref, g_ref, mu_ref, w1_ref, a1_ref, g1_ref, w2_ref, a2_ref, g2_ref,
     w0_ref, a0_ref) = refs[:14]
    if mix:
        v1_ref, v2_ref, v0_ref, xm_ref, lw_ref, a_ref, go_ref, vg_ref = refs[14:]
    else:
        xm_ref, lw_ref, a_ref, go_ref = refs[14:]
    i = pl.program_id(0)
    g = g_ref[...]
    shift = mod_ref[0, 0:1, :]
    scale = mod_ref[0, 1:2, :]
    h = _normmod(x_ref[...], g, shift, scale)
    halo = _normmod(jnp.concatenate([xp_ref[...], xn_ref[...]], axis=0), g, shift, scale)
    rid, first, last = _seq_edges(i, te, n_lat_rows, seq, ctx)
    hp, hn = _shift_rows(h, rid, first, last, halo[SUBLANES - 1:SUBLANES], halo[SUBLANES:SUBLANES + 1])
    xx = 0.5 * (hp + hn) - h

    def mixed(p):
        return (h + xx * mu_ref[p:p + 1, :]).astype(BF16)

    xv = mixed(2)
    xm_ref[0] = mixed(0)
    xm_ref[1] = mixed(1)
    xm_ref[2] = xv
    zw = jnp.tanh(_dot(mixed(3), w1_ref[...])).astype(BF16)
    za = _dot(mixed(4), a1_ref[...]).astype(BF16)
    zg = jax.nn.sigmoid(_dot(mixed(5), g1_ref[...])).astype(BF16)
    for dd in range(2):
        sl = slice(dd * LANES, (dd + 1) * LANES)
        wl = w0_ref[dd:dd + 1, :] + _dot(zw[:, sl], w2_ref[dd])
        lw_ref[dd] = (-math.exp(-0.5)) * jax.nn.sigmoid(wl)
        a_ref[dd] = jax.nn.sigmoid(a0_ref[dd:dd + 1, :] + _dot(za[:, sl], a2_ref[dd])).astype(BF16)
    go_ref[...] = _dot(zg, g2_ref[...]).astype(BF16)
    if mix:
        zv = _dot(xv, v1_ref[...]).astype(BF16)
        vg_ref[...] = jax.nn.sigmoid(v0_ref[...] + _dot(zv, v2_ref[...])).astype(BF16)


def _pad_rank(w1, w2):
    r = w1.shape[-1]
    pad = (-r) % LANES
    w1 = jnp.pad(w1, [(0, 0)] * (w1.ndim - 1) + [(0, pad)])
    w2 = jnp.pad(w2, [(0, 0)] * (w2.ndim - 2) + [(0, pad), (0, 0)])
    return w1.astype(BF16), w2.astype(BF16)


def _rw_prep(xs, mod, g1n, mu, w0, w1, w2, a0, a1, a2, g1, g2, vres, *, dm):
    rows, d = xs.shape
    te = dm["te"]
    hb = te // SUBLANES
    last_hb = rows // SUBLANES - 1
    mix = vres is not None
    w1p, w2p = _pad_rank(w1, w2)
    a1p, a2p = _pad_rank(a1, a2)
    w1c = jnp.concatenate([w1p[0], w1p[1]], axis=1)
    a1c = jnp.concatenate([a1p[0], a1p[1]], axis=1)
    rg = g1.shape[1]
    modmap = lambda i: (jnp.minimum((i * te) // dm["seq"], dm["nb"]), 0, 0)
    full = lambda shp: pl.BlockSpec(shp, lambda i: (0,) * len(shp))
    in_specs = [
        pl.BlockSpec((te, d), lambda i: (i, 0)),
        pl.BlockSpec((SUBLANES, d), lambda i: (jnp.maximum(i * hb - 1, 0), 0)),
        pl.BlockSpec((SUBLANES, d), lambda i: (jnp.minimum((i + 1) * hb, last_hb), 0)),
        pl.BlockSpec((1, 6, d), modmap),
        full((1, d)), full((6, d)),
        full((d, 2 * LANES)), full((d, 2 * LANES)), full((d, rg)),
        full((2, LANES, d)), full((2, LANES, d)), full((rg, d)),
        full((2, d)), full((2, d)),
    ]
    args = [xs, xs, xs, mod, g1n.reshape(1, d), mu, w1c, a1c, g1.astype(BF16), w2p, a2p, g2.astype(BF16), w0, a0]
    row_spec = pl.BlockSpec((te, d), lambda i: (i, 0))
    out_shape = [jax.ShapeDtypeStruct((3, rows, d), BF16), jax.ShapeDtypeStruct((2, rows, d), F32),
                 jax.ShapeDtypeStruct((2, rows, d), BF16), jax.ShapeDtypeStruct((rows, d), BF16)]
    out_specs = [pl.BlockSpec((3, te, d), lambda i: (0, i, 0)), pl.BlockSpec((2, te, d), lambda i: (0, i, 0)),
                 pl.BlockSpec((2, te, d), lambda i: (0, i, 0)), row_spec]
    if mix:
        v1p, v2p = _pad_rank(vres[1], vres[2])
        in_specs += [full((d, LANES)), full((LANES, d)), full((1, d))]
        args += [v1p, v2p, vres[0].reshape(1, d)]
        out_shape.append(jax.ShapeDtypeStruct((rows, d), BF16))
        out_specs.append(row_spec)
    kern = functools.partial(_rw_prep_kernel, mix=mix, te=te, n_lat_rows=dm["n_lat_rows"], seq=dm["seq"],
                             ctx=dm["ctx"])
    return pl.pallas_call(
        kern,
        out_shape=out_shape,
        grid=(rows // te,),
        in_specs=in_specs,
        out_specs=out_specs,
        compiler_params=_cparams(("parallel",)),
        name="rwkv_prep",
    )(*args)


def _rkv_kernel(*refs, mix):
    if mix:
        xm_ref, w_ref, vf_ref, vg_ref, o_ref = refs
    else:
        xm_ref, w_ref, o_ref = refs
    acc = _dot(xm_ref[0], w_ref[0])
    if mix:
        p = pl.program_id(1)

        @pl.when(p == 2)
        def _():
            o_ref[0] = (acc + (vf_ref[0].astype(F32) - acc) * vg_ref[...].astype(F32)).astype(BF16)

        @pl.when(p != 2)
        def _():
            o_ref[0] = acc.astype(BF16)
    else:
        o_ref[0] = acc.astype(BF16)


def _rkv_proj(xm, w, v_first, vgate, *, layer, dm):
    _, rows, d = xm.shape
    tm = dm["tm"]
    mix = v_first is not None
    in_specs = [pl.BlockSpec((1, tm, d), lambda i, p: (p, i, 0)),
                pl.BlockSpec((None, 1, d, d), lambda i, p: (layer, p, 0, 0))]
    args = [xm, w]
    if mix:
        in_specs += [pl.BlockSpec((1, tm, d), lambda i, p: (2, i, 0)), pl.BlockSpec((tm, d), lambda i, p: (i, 0))]
        args += [v_first, vgate]
    return pl.pallas_call(
        functools.partial(_rkv_kernel, mix=mix),
        out_shape=jax.ShapeDtypeStruct((3, rows, d), BF16),
        grid=(rows // tm, 3),
        in_specs=in_specs,
        out_specs=pl.BlockSpec((1, tm, d), lambda i, p: (p, i, 0)),
        compiler_params=_cparams(("parallel", "arbitrary")),
        name="rwkv_rkv_proj",
    )(*args)


def _wkv_kernel(rf_ref, kf_ref, vf_ref, lwf_ref, af_ref, rb_ref, kb_ref, vb_ref, lwb_ref, ab_ref,
                kk_ref, ka_ref, rk_ref, yf_ref, bonf_ref, yb_ref, bonb_ref, s_scr, *, gps):
    c = pl.program_id(2)
    ln = WKV_CHUNK
    pw = WKV_PACK * RW_HEAD

    hpt = LANES // RW_HEAD

    @pl.when(c == 0)
    def _():
        s_scr[...] = jnp.zeros_like(s_scr)

    row = lax.broadcasted_iota(jnp.int32, (ln, ln), 0)
    col = lax.broadcasted_iota(jnp.int32, (ln, ln), 1)
    trow = lax.broadcasted_iota(jnp.int32, (ln, pw), 0)
    tsrc = lax.broadcasted_iota(jnp.int32, (ln, pw), 1) & (ln - 1)
    ones64 = _ones_blockdiag64()

    def fmask(cond):
        return jnp.where(cond, 1.0, 0.0)

    def same_block(n):
        sh = n.bit_length() - 1
        return (tsrc >> sh) == (trow >> sh)

    eye = fmask(tsrc == trow)
    base_f = fmask(same_block(WKV_INV_BASE))
    off_f = {}
    n = WKV_INV_BASE
    while n < ln:
        off_f[n] = fmask(jnp.logical_and(same_block(2 * n), jnp.logical_not(same_block(n))))
        n *= 2
    k_k = kk_ref[...]
    k_a = ka_ref[...]
    r_k = rk_ref[...]

    lane = lax.broadcasted_iota(jnp.int32, (ln, LANES), 1)
    half_f = [fmask((lane >> (RW_HEAD.bit_length() - 1)) == hf) for hf in range(hpt)]
    half_b = [hm.astype(BF16) for hm in half_f]
    zeros_b = jnp.zeros((ln, LANES), BF16)

    def bd(z):
        zb = z.astype(BF16)
        blocks = []
        for jh in range(WKV_PACK):
            lt, hf = divmod(jh, hpt)
            piece = zb[:, lt * LANES:(lt + 1) * LANES] * half_b[hf]
            blocks.append(jnp.concatenate([piece if tt == lt else zeros_b for tt in range(pw // LANES)], axis=1))
        return jnp.concatenate(blocks, axis=0)

    streams = ((rf_ref, kf_ref, vf_ref, lwf_ref, af_ref, yf_ref, bonf_ref),
               (rb_ref, kb_ref, vb_ref, lwb_ref, ab_ref, yb_ref, bonb_ref))
    units = []
    for dd, (r_ref, k_ref, v_ref, lw_ref, a_ref, y_ref, bon_ref) in enumerate(streams):
        rev = dd == 1
        r = r_ref[0].astype(F32)
        k = k_ref[0].astype(F32)
        v = v_ref[0].astype(F32)
        a = a_ref[0].astype(F32)
        lw = lw_ref[0]
        kkr = k * k_k
        kd = k * (1.0 + (a - 1.0) * k_a)
        ssq, rkd = _segsum64_mxu([kkr * kkr, r * kd * r_k], ones64, split=False)
        kk = kkr * lax.rsqrt(jnp.maximum(ssq, 1e-24))
        bvec = kk * a
        bon_ref[...] = (rkd * v).astype(BF16)

        tri = jnp.where((col >= row) if rev else (col <= row), 1.0, 0.0).astype(BF16)
        hi = lw.astype(BF16)
        lo = (lw - hi.astype(F32)).astype(BF16)
        cum = _dot(tri, hi) + _dot(tri, lo)
        tot = cum[0:1, :] if rev else cum[ln - 1:ln, :]
        w_inv = jnp.exp(-cum)
        w_end = jnp.exp(tot - cum)
        a_t = -kk * jnp.exp(cum - lw)
        r_t = r * jnp.exp(cum)
        b_t = bvec * w_inv
        k_t = kd * w_inv
        b_e = bvec * w_end
        k_e = kd * w_end
        w_tot = jnp.exp(tot)

        strict = fmask(tsrc > trow) if rev else fmask(tsrc < trow)
        incl = strict + eye

        for gi in range(gps):
            sl = slice(gi * pw, (gi + 1) * pw)
            units.append(dict(
                dd=dd, gi=gi, sl=sl, y_ref=y_ref, strict=strict, incl=incl,
                ar=jnp.concatenate([a_t[:, sl], r_t[:, sl]], axis=0).astype(BF16),
                b_t=b_t[:, sl], k_t=k_t[:, sl], v=v[:, sl], w_tot=w_tot[:, sl],
                bk=jnp.concatenate([b_e[:, sl], k_e[:, sl]], axis=0).astype(BF16)))

    for un in units:
        sb = _dot_nt(un["ar"], bd(un["b_t"]))
        sk = _dot_nt(un["ar"], bd(un["k_t"]))
        un["m_ab"] = sb[:ln] * un["strict"]
        un["p_rb"] = sb[ln:] * un["incl"]
        un["m_ak"] = sk[:ln] * un["strict"]
        un["p_rk"] = sk[ln:] * un["incl"]
    for un in units:
        un["s0"] = s_scr[un["dd"], un["gi"]]
        un["ars"] = _dot(un["ar"], un["s0"].T.astype(BF16))
        un["mv"] = _dot(jnp.concatenate([un["m_ak"], un["p_rk"]], axis=0).astype(BF16), bd(un["v"]))
    for un in units:
        m0 = un["m_ab"] * base_f
        un["pinv"] = eye + m0
        un["mp"] = _dot(m0.astype(BF16), bd(m0))
    for un in units:
        both = _dot(jnp.concatenate([un["mp"], un["pinv"]], axis=0).astype(BF16), bd(un["mp"]))
        un["pinv"] = un["pinv"] + both[ln:]
        un["mp"] = both[:ln]
    for un in units:
        un["pinv"] = un["pinv"] + _dot(un["pinv"].astype(BF16), bd(un["mp"]))
    n = WKV_INV_BASE
    while n < ln:
        for un in units:
            un["t1"] = _dot((un["m_ab"] * off_f[n]).astype(BF16), bd(un["pinv"]))
        for un in units:
            un["pinv"] = un["pinv"] + _dot(un["pinv"].astype(BF16), bd(un["t1"]))
        n *= 2
    for un in units:
        un["u"] = _dot(un["pinv"].astype(BF16), bd(un["ars"][:ln] + un["mv"][:ln]))
    for un in units:
        un["y_ref"][:, un["sl"]] = (un["ars"][ln:] + _dot(un["p_rb"].astype(BF16), bd(un["u"]))
                                    + un["mv"][ln:]).astype(BF16)
        uv = jnp.concatenate([un["u"], un["v"]], axis=0).astype(BF16)
        res = _dot_tn(uv, un["bk"])
        for jh in range(WKV_PACK):
            lt, hf = divmod(jh, hpt)
            rsl = slice(jh * RW_HEAD, (jh + 1) * RW_HEAD)
            csl = slice(lt * LANES, (lt + 1) * LANES)
            s_scr[un["dd"], un["gi"], rsl, csl] = (un["s0"][rsl, csl] * un["w_tot"][:, csl]
                                                    + res[rsl, csl] * half_f[hf])


def _wkv(rkv, lw, a, k_k, k_a, r_k, *, dm, gps):
    _, rows, d = rkv.shape
    nb, seq, ctx = dm["nb"], dm["seq"], dm["ctx"]
    ln = WKV_CHUNK
    sw = gps * WKV_PACK * RW_HEAD
    ncc, nlc = ctx // ln, seq // ln
    ctx_c0 = nb * seq // ln

    def fblk(b, c):
        return jnp.where(c < ncc, ctx_c0 + b * ncc + c, b * nlc + (c - ncc))

    def bblk(b, c):
        return jnp.where(c < ncc, ctx_c0 + b * ncc + (ncc - 1 - c), b * nlc + (nlc - 1 - (c - ncc)))

    def spec3(p, blk):
        return pl.BlockSpec((1, ln, sw), lambda b, s, c, p=p, blk=blk: (p, blk(b, c), s))

    def spec2(blk):
        return pl.BlockSpec((ln, sw), lambda b, s, c, blk=blk: (blk(b, c), s))

    pspec = pl.BlockSpec((1, sw), lambda b, s, c: (0, s))
    in_specs = [spec3(0, fblk), spec3(1, fblk), spec3(2, fblk), spec3(0, fblk), spec3(0, fblk),
                spec3(0, bblk), spec3(1, bblk), spec3(2, bblk), spec3(1, bblk), spec3(1, bblk),
                pspec, pspec, pspec]
    return pl.pallas_call(
        functools.partial(_wkv_kernel, gps=gps),
        out_shape=[jax.ShapeDtypeStruct((rows, d), BF16)] * 4,
        grid=(nb, d // sw, ncc + nlc),
        in_specs=in_specs,
        out_specs=[spec2(fblk), spec2(fblk), spec2(bblk), spec2(bblk)],
        scratch_shapes=[pltpu.VMEM((2, gps, WKV_PACK * RW_HEAD, WKV_PACK * RW_HEAD), F32)],
        compiler_params=_cparams(("parallel", "parallel", "arbitrary")),
        name="wkv_scan",
    )(rkv, rkv, rkv, lw, a, rkv, rkv, rkv, lw, a, k_k.reshape(1, d), k_a.reshape(1, d), r_k.reshape(1, d))


def _rw_out_kernel(yf_ref, yb_ref, bf_ref, bb_ref, g_ref, x_ref, mod_ref, lg_ref, lb_ref, w_ref, o_ref):
    y = yf_ref[...].astype(F32) + yb_ref[...].astype(F32)
    ones64 = _ones_blockdiag64()
    mean = _segsum64_mxu([y], ones64)[0] * (1.0 / RW_HEAD)
    yc = y - mean
    var = _segsum64_mxu([yc * yc], ones64, split=False)[0] * (1.0 / RW_HEAD)
    yn = yc * lax.rsqrt(var + RW_GN_EPS)
    bonus = bf_ref[...].astype(F32) + bb_ref[...].astype(F32)
    o = (yn * lg_ref[...] + lb_ref[...] + bonus) * g_ref[...].astype(F32)
    o_ref[...] = x_ref[...] + mod_ref[0, 2:3, :] * _dot(o.astype(BF16), w_ref[...])


def _rw_out(yf, bonf, yb, bonb, g, xs, mod, ln_g, ln_b, w_o, *, dm, n_rows):
    d = xs.shape[1]
    te = dm["te"]
    modmap = lambda i: (jnp.minimum((i * te) // dm["seq"], dm["nb"]), 0, 0)
    rspec = pl.BlockSpec((te, d), lambda i: (i, 0))
    vspec = pl.BlockSpec((1, d), lambda i: (0, 0))
    return pl.pallas_call(
        _rw_out_kernel,
        out_shape=jax.ShapeDtypeStruct((n_rows, d), F32),
        grid=(n_rows // te,),
        in_specs=[rspec, rspec, rspec, rspec, rspec, rspec, pl.BlockSpec((1, 6, d), modmap), vspec, vspec,
                  pl.BlockSpec((d, d), lambda i: (0, 0))],
        out_specs=rspec,
        compiler_params=_cparams(("parallel",)),
        name="rwkv_out",
    )(yf, yb, bonf, bonb, g, xs, mod, ln_g.reshape(1, d), ln_b.reshape(1, d), w_o)


def _rope_tables(dm):
    seq, nb, ctx = dm["seq"], dm["nb"], dm["ctx"]
    t = jnp.arange(seq, dtype=jnp.int32)
    pos = jnp.stack([t // GRID_W, t % GRID_W], axis=-1).astype(F32)
    n_freq = ATT_HEAD // 4
    inv = ROPE_THETA ** (-jnp.arange(n_freq, dtype=F32) / n_freq)
    ang = pos[:, :, None] * inv
    cos, sin = jnp.cos(ang), jnp.sin(ang)
    zero = jnp.zeros_like(sin)
    cos_t = jnp.stack([cos, cos], axis=2).reshape(seq, ATT_HEAD)
    sa_t = jnp.stack([-sin, zero], axis=2).reshape(seq, ATT_HEAD)
    sb_t = jnp.stack([zero, sin], axis=2).reshape(seq, ATT_HEAD)
    nctx = nb * ctx
    full = lambda tab, fill: jnp.concatenate([jnp.tile(tab, (nb, 1)), jnp.full((nctx, ATT_HEAD), fill, F32)], axis=0)
    return full(cos_t, 1.0), full(sa_t, 0.0), full(sb_t, 0.0)


def kernel(x, c, ctx, c_ctx, mod_w, mod_b, norm1_g, norm2_g, ffn_up, ffn_conv_w, ffn_conv_b, ffn_down, rw_mu, rw_w_rkv, rw_w0, rw_w1, rw_w2, rw_a0, rw_a1, rw_a2, rw_g1, rw_g2, rw_k_k, rw_k_a, rw_r_k, rw_ln_g, rw_ln_b, rw_w_o, rw_v0, rw_v1, rw_v2, na_w_qkv, na_q_g, na_k_g, na_rpb, na_w_o, ga_w_qkv, ga_q_g, ga_k_g, ga_w_o):
    nb, seq, d = x.shape
    nctx = ctx.shape[1]
    depth = mod_w.shape[0]
    tm = nb * nctx
    assert seq % tm == 0 and seq & (seq - 1) == 0 and nctx & (nctx - 1) == 0 and nb + 1 <= SUBLANES
    assert seq // GRID_W >= NA_WIN_R and nctx % GRID_W == 0
    dm = dict(nb=nb, seq=seq, ctx=nctx, tm=tm, te=tm // 2, tpb=seq // tm, n_lat_rows=nb * seq)
    n_lat_tiles = nb * seq // tm
    n_tiles = n_lat_tiles + 1
    att_scale = ATT_HEAD ** -0.5 * math.log2(math.e)

    xs = jnp.concatenate([x.reshape(nb * seq, d), ctx.reshape(nb * nctx, d)], axis=0)
    c_all = jnp.concatenate([c, c_ctx[None], jnp.zeros((SUBLANES - nb - 1, d), F32)], axis=0)
    mods = _modulations(c_all, mod_w, mod_b)
    rope_tabs = None
    v_first = None
    ffn_wu, ffn_wd = _ffn_weights(ffn_up, ffn_down)
    rw_rkv_w = rw_w_rkv.astype(BF16)

    for i in range(depth):
        kind, j = i % 3, i // 3
        need_ctx = i < depth - 1
        nt_out = n_tiles if need_ctx else n_lat_tiles
        mod = mods[i, :nb + 1].reshape(nb + 1, 6, d)
        if kind == 0:
            vres = None if j == 0 else (rw_v0[j - 1], rw_v1[j - 1], rw_v2[j - 1])
            prep = _rw_prep(xs, mod, norm1_g[i], rw_mu[j], rw_w0[j], rw_w1[j], rw_w2[j], rw_a0[j], rw_a1[j],
                            rw_a2[j], rw_g1[j], rw_g2[j], vres, dm=dm)
            xm, lw, a, g = prep[0], prep[1], prep[2], prep[3]
            rkv = _rkv_proj(xm, rw_rkv_w, v_first if vres is not None else None,
                            prep[4] if vres is not None else None, layer=j, dm=dm)
            if v_first is None:
                v_first = rkv
            yf, bonf, yb, bonb = _wkv(rkv, lw, a, rw_k_k[j], rw_k_a[j], rw_r_k[j].reshape(-1), dm=dm,
                                      gps=min(8, d // (WKV_PACK * RW_HEAD)))
            xs = _rw_out(yf, bonf, yb, bonb, g, xs, mod, rw_ln_g[j], rw_ln_b[j], rw_w_o[j].astype(BF16),
                         dm=dm, n_rows=nt_out * tm)
        elif kind == 1:
            qkv = _qkv_proj(xs, mod, norm1_g[i], na_w_qkv[j].astype(BF16), na_q_g[j] * att_scale, na_k_g[j],
                            None, dm=dm, kv_dim=d)
            o = _na_attention(qkv, na_rpb[j], dm=dm, d=d)
            xs = _out_proj(o, na_w_o[j].astype(BF16), xs, mod, dm=dm, n_tiles=nt_out)
        else:
            if rope_tabs is None:
                rope_tabs = _rope_tables(dm)
            kv_dim = (ga_w_qkv.shape[-1] - d) // 2
            qkv = _qkv_proj(xs, mod, norm1_g[i], ga_w_qkv[j].astype(BF16), ga_q_g[j] * att_scale, ga_k_g[j],
                            rope_tabs, dm=dm, kv_dim=kv_dim)
            o = _gqa_attention(qkv, dm=dm, d=d, kv_heads=kv_dim // ATT_HEAD)
            xs = _out_proj(o, ga_w_o[j].astype(BF16), xs, mod, dm=dm, n_tiles=nt_out)
        xs = _ffn(xs, mod, norm2_g[i], ffn_wu, ffn_conv_w[i], ffn_conv_b[i], ffn_wd, layer=i, dm=dm,
                  need_ctx=need_ctx)
    return xs[:nb * seq].reshape(nb, seq, d)
```

```python
import functools
import math

import jax
import jax.numpy as jnp
from jax import lax
from jax.experimental import pallas as pl
from jax.experimental.pallas import tpu as pltpu

F32 = jnp.float32
BF16 = jnp.bfloat16

NORM_EPS = 1e-6
GRID_W = 64
ATT_HEAD = 128
RW_HEAD = 64
NA_WIN_R = 8
NA_WIN_C = 16
ROPE_THETA = 10000.0
RW_GN_EPS = 64e-5
LANES = 128
SUBLANES = 8
WKV_CHUNK = 64
WKV_PACK = 4
WKV_INV_BASE = 8
FFN_ROW_PIECES = 2
VMEM_LIMIT = 56 * 1024 * 1024
NEG_BIG = -1e30


def _cparams(sem):
    return pltpu.CompilerParams(dimension_semantics=sem, vmem_limit_bytes=VMEM_LIMIT)


def _dot(a, b):
    return jnp.dot(a, b, preferred_element_type=F32)


def _dot_nt(a, b):
    return lax.dot_general(a, b, (((1,), (1,)), ((), ())), preferred_element_type=F32)


def _dot_tn(a, b):
    return lax.dot_general(a, b, (((0,), (0,)), ((), ())), preferred_element_type=F32)


def _normmod(x, g, shift, scale):
    ms = jnp.mean(x * x, axis=-1, keepdims=True)
    y = x * lax.rsqrt(ms + NORM_EPS)
    return (y * g) * (1.0 + scale) + shift


def _silu(x):
    return x * jax.nn.sigmoid(x)


def _seq_edges(tile, rows, n_lat_rows, seq, ctx):
    rid = lax.broadcasted_iota(jnp.int32, (rows, 1), 0)
    base = tile * rows
    period = jnp.where(base >= n_lat_rows, ctx, seq)
    pos = (base + rid) & (period - 1)
    return rid, pos == 0, pos == period - 1


def _shift_rows(u, rid, first, last, prev_row, next_row):
    n = u.shape[0]
    up = pltpu.roll(u, 1, axis=0)
    up = jnp.where(rid == 0, prev_row, up)
    up = jnp.where(first, 0.0, up)
    un = pltpu.roll(u, n - 1, axis=0)
    un = jnp.where(rid == n - 1, next_row, un)
    un = jnp.where(last, 0.0, un)
    return up, un


def _ones_blockdiag64():
    sh = RW_HEAD.bit_length() - 1
    r = lax.broadcasted_iota(jnp.int32, (LANES, LANES), 0) >> sh
    c = lax.broadcasted_iota(jnp.int32, (LANES, LANES), 1) >> sh
    return jnp.where(r == c, 1.0, 0.0).astype(BF16)


def _segsum64_mxu(xs, ones, split=True):
    m, n = xs[0].shape
    nslab = n // LANES
    nparts = 2 if split else 1
    pieces = []
    for x in xs:
        hi = x.astype(BF16)
        parts = (hi, (x - hi.astype(F32)).astype(BF16)) if split else (hi,)
        for part in parts:
            pieces += [part[:, c * LANES:(c + 1) * LANES] for c in range(nslab)]
    res = _dot(jnp.concatenate(pieces, axis=0), ones)
    outs = []
    for i in range(len(xs)):
        base = i * nparts * nslab
        cols = []
        for c in range(nslab):
            col = res[(base + c) * m:(base + c + 1) * m]
            if split:
                col = col + res[(base + nslab + c) * m:(base + nslab + c + 1) * m]
            cols.append(col)
        outs.append(jnp.concatenate(cols, axis=1))
    return outs


def _mod_kernel(c_ref, w_ref, b_ref, o_ref):
    s = _silu(c_ref[...]).astype(BF16)
    o_ref[0] = _dot(s, w_ref[0].astype(BF16)) + b_ref[0]


def _modulations(c_all, mod_w, mod_b):
    depth, d, n = mod_w.shape
    tn = n // 8
    return pl.pallas_call(
        _mod_kernel,
        out_shape=jax.ShapeDtypeStruct((depth, SUBLANES, n), F32),
        grid=(depth, n // tn),
        in_specs=[
            pl.BlockSpec((SUBLANES, d), lambda l, j: (0, 0)),
            pl.BlockSpec((1, d, tn), lambda l, j: (l, 0, j)),
            pl.BlockSpec((1, 1, tn), lambda l, j: (l, 0, j)),
        ],
        out_specs=pl.BlockSpec((1, SUBLANES, tn), lambda l, j: (l, 0, j)),
        compiler_params=_cparams(("parallel", "parallel")),
        name="modulation",
    )(c_all, mod_w, mod_b.reshape(depth, 1, n))


def _ffn_kernel(*refs, tm, tile_off, n_lat_rows, seq, ctx):
    x_ref, xp_ref, xn_ref, mod_ref, g_ref, wug_ref, wuv_ref, cp_ref, wd_ref = refs[:9]
    o_ref, h_scr = refs[-2:]
    i = pl.program_id(0) + tile_off
    j = pl.program_id(1)
    cpg = cp_ref[j]
    cpv = cp_ref[pl.num_programs(1) + j]
    shift = mod_ref[0, 3:4, :]
    scale = mod_ref[0, 4:5, :]
    hr = 2 * SUBLANES
    th = tm // FFN_ROW_PIECES

    @pl.when(j == 0)
    def _():
        g = g_ref[...]
        halo = jnp.concatenate([xp_ref[...], xn_ref[...]], axis=0)
        h_scr[0:hr, :] = _normmod(halo, g, shift, scale).astype(BF16)
        h_scr[hr:hr + tm, :] = _normmod(x_ref[...], g, shift, scale).astype(BF16)
        o_ref[...] = jnp.zeros_like(o_ref)

    rid, first, last = _seq_edges(i, tm, n_lat_rows, seq, ctx)
    rid_h = rid[0:th]

    def up_proj(p):
        lo = 0 if p == 0 else hr + p * th
        hp = h_scr[lo:hr + (p + 1) * th, :]
        ug, uv = _dot(hp, wug_ref[...]), _dot(hp, wuv_ref[...])
        if p == 0:
            return dict(g=ug[hr:], v=uv[hr:], halo_g=ug[0:hr], halo_v=uv[0:hr])
        return dict(g=ug, v=uv)

    def conv(main, prev_row, next_row, fm, lm, cp):
        up, un = _shift_rows(main, rid_h, fm, lm, prev_row, next_row)
        return cp[3:4, :] + up * cp[0:1, :] + main * cp[1:2, :] + un * cp[2:3, :]

    pv, nx = SUBLANES - 1, SUBLANES

    def act_down(p, us):
        rs = slice(p * th, (p + 1) * th)
        rows = {}
        for key in ("g", "v"):
            prev_row = us[0]["halo_" + key][pv:pv + 1] if p == 0 else us[p - 1][key][th - 1:th]
            next_row = us[0]["halo_" + key][nx:nx + 1] if p == FFN_ROW_PIECES - 1 else us[p + 1][key][0:1]
            rows[key] = (prev_row, next_row)
        a = (_silu(conv(us[p]["g"], *rows["g"], first[rs], last[rs], cpg))
             * conv(us[p]["v"], *rows["v"], first[rs], last[rs], cpv))
        o_ref[rs, :] += _dot(a.astype(BF16), wd_ref[...])

    us = {0: up_proj(0)}
    for p in range(FFN_ROW_PIECES):
        if p + 1 < FFN_ROW_PIECES:
            us[p + 1] = up_proj(p + 1)
        act_down(p, us)

    @pl.when(j == pl.num_programs(1) - 1)
    def _():
        o_ref[...] = x_ref[...] + mod_ref[0, 5:6, :] * o_ref[...]


FFN_COLS = 512


def _ffn_weights(ffn_up, ffn_down):
    return ffn_up.astype(BF16), ffn_down.astype(BF16)


def _ffn_call(xs, mod, g2, wu, cw, cb, wd, prev, *, layer, dm, tm, tile_off, n_tiles, out_rows):
    rows, d = xs.shape
    f = wd.shape[1]
    fc = FFN_COLS
    nfc = f // fc
    hb = tm // SUBLANES
    last_hb = rows // SUBLANES - 1
    seq, nb = dm["seq"], dm["nb"]
    assert seq % tm == 0 or tile_off * tm >= dm["n_lat_rows"]
    modmap = lambda i, j: (jnp.minimum(((i + tile_off) * tm) // seq, nb), 0, 0)
    kern = functools.partial(_ffn_kernel, tm=tm, tile_off=tile_off,
                             n_lat_rows=dm["n_lat_rows"], seq=seq, ctx=dm["ctx"])
    in_specs = [
        pl.BlockSpec((tm, d), lambda i, j: (i + tile_off, 0), pipeline_mode=pl.Buffered(1)),
        pl.BlockSpec((SUBLANES, d), lambda i, j: (jnp.maximum((i + tile_off) * hb - 1, 0), 0)),
        pl.BlockSpec((SUBLANES, d), lambda i, j: (jnp.minimum((i + tile_off + 1) * hb, last_hb), 0)),
        pl.BlockSpec((1, 6, d), modmap),
        pl.BlockSpec((1, d), lambda i, j: (0, 0)),
        pl.BlockSpec((None, d, fc), lambda i, j: (layer, 0, j)),
        pl.BlockSpec((None, d, fc), lambda i, j: (layer, 0, nfc + j)),
        pl.BlockSpec((2 * nfc, 4, fc), lambda i, j: (0, 0, 0)),
        pl.BlockSpec((None, fc, d), lambda i, j: (layer, j, 0)),
    ]
    cp = jnp.concatenate([cw, cb.reshape(1, -1)], axis=0).reshape(4, 2 * nfc, fc).transpose(1, 0, 2)
    args = [xs, xs, xs, mod, g2.reshape(1, d), wu, wu, cp, wd]
    aliases = {}
    if prev is not None:
        in_specs.append(pl.BlockSpec(memory_space=pl.ANY))
        args.append(prev)
        aliases = {len(args) - 1: 0}
    return pl.pallas_call(
        kern,
        out_shape=jax.ShapeDtypeStruct((out_rows, d), F32),
        grid=(n_tiles, nfc),
        in_specs=in_specs,
        out_specs=pl.BlockSpec((tm, d), lambda i, j: (i + tile_off, 0)),
        scratch_shapes=[pltpu.VMEM((tm + 2 * SUBLANES, d), BF16)],
        input_output_aliases=aliases,
        compiler_params=_cparams(("parallel", "arbitrary")),
        name="conv_ffn",
    )(*args)


def _ffn(xs, mod, g2, wu, cw, cb, wd, *, layer, dm, need_ctx):
    rows = xs.shape[0]
    n_lat_rows, tm = dm["n_lat_rows"], dm["tm"]
    big = 2 * tm
    out_rows = rows if need_ctx else n_lat_rows
    y = _ffn_call(xs, mod, g2, wu, cw, cb, wd, None, layer=layer, dm=dm, tm=big, tile_off=0,
                  n_tiles=n_lat_rows // big, out_rows=out_rows)
    if need_ctx:
        y = _ffn_call(xs, mod, g2, wu, cw, cb, wd, y, layer=layer, dm=dm, tm=tm, tile_off=n_lat_rows // tm,
                      n_tiles=1, out_rows=out_rows)
    return y


def _qkv_kernel(*refs, block_kinds, rope):
    if rope:
        x_ref, mod_ref, g_ref, w_ref, qg_ref, kg_ref, cos_ref, sa_ref, sb_ref, o_ref, h_scr = refs
    else:
        x_ref, mod_ref, g_ref, w_ref, qg_ref, kg_ref, o_ref, h_scr = refs
    j = pl.program_id(1)

    @pl.when(j == 0)
    def _():
        h_scr[...] = _normmod(x_ref[...], g_ref[...], mod_ref[0, 0:1, :], mod_ref[0, 1:2, :]).astype(BF16)

    acc = _dot(h_scr[...], w_ref[...])

    def emit(kinds):
        for hh, kind in enumerate(kinds):
            hs = slice(hh * ATT_HEAD, (hh + 1) * ATT_HEAD)
            y = acc[:, hs]
            if kind != "v":
                gain = qg_ref[...] if kind == "q" else kg_ref[...]
                y = y * lax.rsqrt(jnp.mean(y * y, axis=-1, keepdims=True) + NORM_EPS) * gain
                if rope:
                    y = (y * cos_ref[...] + pltpu.roll(y, ATT_HEAD - 32, axis=1) * sa_ref[...]
                         + pltpu.roll(y, 32, axis=1) * sb_ref[...])
            o_ref[:, hs] = y.astype(BF16)

    for jb, kinds in enumerate(block_kinds):
        pl.when(j == jb)(functools.partial(emit, kinds))


def _qkv_proj(xs, mod, g1, w, qg, kg, rope_tabs, *, dm, kv_dim):
    rows, d = xs.shape
    n = w.shape[1]
    tn = min(d, 2 * kv_dim)
    hpb = tn // ATT_HEAD
    kinds = ["q"] * (d // ATT_HEAD) + ["k"] * (kv_dim // ATT_HEAD) + ["v"] * (kv_dim // ATT_HEAD)
    block_kinds = tuple(tuple(kinds[b * hpb:(b + 1) * hpb]) for b in range(n // tn))
    tm = dm["tm"]
    modmap = lambda i, j: (jnp.minimum(i // dm["tpb"], dm["nb"]), 0, 0)
    in_specs = [
        pl.BlockSpec((tm, d), lambda i, j: (i, 0)),
        pl.BlockSpec((1, 6, d), modmap),
        pl.BlockSpec((1, d), lambda i, j: (0, 0)),
        pl.BlockSpec((d, tn), lambda i, j: (0, j)),
        pl.BlockSpec((1, ATT_HEAD), lambda i, j: (0, 0)),
        pl.BlockSpec((1, ATT_HEAD), lambda i, j: (0, 0)),
    ]
    args = [xs, mod, g1.reshape(1, d), w, qg.reshape(1, ATT_HEAD), kg.reshape(1, ATT_HEAD)]
    if rope_tabs is not None:
        in_specs += [pl.BlockSpec((tm, ATT_HEAD), lambda i, j: (i, 0))] * 3
        args += list(rope_tabs)
    kern = functools.partial(_qkv_kernel, block_kinds=block_kinds, rope=rope_tabs is not None)
    return pl.pallas_call(
        kern,
        out_shape=jax.ShapeDtypeStruct((rows, n), BF16),
        grid=(rows // tm, n // tn),
        in_specs=in_specs,
        out_specs=pl.BlockSpec((tm, tn), lambda i, j: (i, j)),
        scratch_shapes=[pltpu.VMEM((tm, d), BF16)],
        compiler_params=_cparams(("parallel", "arbitrary")),
        name="qkv_proj",
    )(*args)


def _oproj_kernel(a_ref, w_ref, x_ref, mod_ref, o_ref):
    o_ref[...] = x_ref[...] + mod_ref[0, 2:3, :] * _dot(a_ref[...], w_ref[...])


def _out_proj(a, w, xs, mod, *, dm, n_tiles):
    d = xs.shape[1]
    tm = dm["tm"]
    modmap = lambda i: (jnp.minimum(i // dm["tpb"], dm["nb"]), 0, 0)
    return pl.pallas_call(
        _oproj_kernel,
        out_shape=jax.ShapeDtypeStruct((n_tiles * tm, d), F32),
        grid=(n_tiles,),
        in_specs=[
            pl.BlockSpec((tm, d), lambda i: (i, 0)),
            pl.BlockSpec((d, d), lambda i: (0, 0)),
            pl.BlockSpec((tm, d), lambda i: (i, 0)),
            pl.BlockSpec((1, 6, d), modmap),
        ],
        out_specs=pl.BlockSpec((tm, d), lambda i: (i, 0)),
        compiler_params=_cparams(("parallel",)),
        name="out_proj",
    )(a, w, xs, mod)


def _softmax_pv(q, segs):
    ss = [_dot_nt(q, k) for k, _ in segs]
    m = ss[0].max(axis=-1, keepdims=True)
    for s in ss[1:]:
        m = jnp.maximum(m, s.max(axis=-1, keepdims=True))
    ps = [jnp.exp2(s - m) for s in ss]
    l = ps[0].sum(axis=-1, keepdims=True)
    for p in ps[1:]:
        l = l + p.sum(axis=-1, keepdims=True)
    o = _dot(ps[0].astype(BF16), segs[0][1])
    for p, (_, v) in zip(ps[1:], segs[1:]):
        o = o + _dot(p.astype(BF16), v)
    return o / l


def _lane_tiles(x, op):
    acc = x[:, 0:LANES]
    for j in range(1, x.shape[1] // LANES):
        acc = op(acc, x[:, j * LANES:(j + 1) * LANES])
    return acc


def _gqa_kernel(q_ref, kl_ref, vl_ref, kc_ref, vc_ref, o_ref, s_scr, vx_scr, kt_scr, *, group, n_lat_tiles,
                kchunk):
    t = pl.program_id(2)
    seq, ctx = kl_ref.shape[0], kc_ref.shape[0]

    @pl.when(t == 0)
    def _():
        kt_scr[:, 0:seq] = kl_ref[...].astype(F32).T.astype(BF16)
        kt_scr[:, seq:seq + ctx] = kc_ref[...].astype(F32).T.astype(BF16)
        vx_scr[0:seq, 0:ATT_HEAD] = vl_ref[...]
        vx_scr[seq:seq + ctx, 0:ATT_HEAD] = vc_ref[...]
        vx_scr[:, ATT_HEAD:2 * ATT_HEAD] = jnp.ones((seq + ctx, ATT_HEAD), BF16)

    def run(chunks):
        m = [None] * group
        mrun = [None] * group
        o = [None] * group
        for g in range(group + 1):
            for sz, off in chunks:
                if g < group:
                    s = _dot(q_ref[:, g * ATT_HEAD:(g + 1) * ATT_HEAD], kt_scr[:, off:off + sz])
                    s_scr[g % 2, :, off:off + sz] = s
                    tmax = _lane_tiles(s, jnp.maximum)
                    mrun[g] = tmax if mrun[g] is None else jnp.maximum(mrun[g], tmax)
                if g >= 1:
                    h = g - 1
                    p = jnp.exp2((s_scr[h % 2, :, off:off + sz] - m[h]).astype(BF16))
                    pv = _dot(p, vx_scr[off:off + sz, :])
                    o[h] = pv if o[h] is None else o[h] + pv
            if g < group:
                m[g] = mrun[g].max(axis=-1, keepdims=True)
        for g in range(group):
            o_ref[:, g * ATT_HEAD:(g + 1) * ATT_HEAD] = (o[g][:, 0:ATT_HEAD] / o[g][:, ATT_HEAD:]).astype(BF16)

    lat_chunks = [(kchunk, st) for st in range(0, seq, kchunk)]
    ctx_chunk = (ctx, seq)

    @pl.when(t < n_lat_tiles)
    def _():
        run(lat_chunks + [ctx_chunk])

    @pl.when(t >= n_lat_tiles)
    def _():
        run([ctx_chunk])


def _gqa_attention(qkv, *, dm, d, kv_heads):
    rows = qkv.shape[0]
    nb, seq, ctx = dm["nb"], dm["seq"], dm["ctx"]
    group = d // ATT_HEAD // kv_heads
    gw = group * ATT_HEAD
    tq = min(256, ctx)
    nlt, nct = seq // tq, ctx // tq
    kcol = d // ATT_HEAD
    vcol = kcol + kv_heads
    ctx_blk0 = nb * seq // ctx

    def qmap(b, h, t):
        return (jnp.where(t < nlt, b * nlt + t, nb * nlt + b * nct + (t - nlt)), h)

    kern = functools.partial(_gqa_kernel, group=group, n_lat_tiles=nlt, kchunk=min(512, seq))
    return pl.pallas_call(
        kern,
        out_shape=jax.ShapeDtypeStruct((rows, d), BF16),
        scratch_shapes=[pltpu.VMEM((2, tq, seq + ctx), F32), pltpu.VMEM((seq + ctx, 2 * ATT_HEAD), BF16),
                        pltpu.VMEM((ATT_HEAD, seq + ctx), BF16)],
        grid=(nb, kv_heads, nlt + nct),
        in_specs=[
            pl.BlockSpec((tq, gw), qmap),
            pl.BlockSpec((seq, ATT_HEAD), lambda b, h, t: (b, kcol + h)),
            pl.BlockSpec((seq, ATT_HEAD), lambda b, h, t: (b, vcol + h)),
            pl.BlockSpec((ctx, ATT_HEAD), lambda b, h, t: (ctx_blk0 + b, kcol + h)),
            pl.BlockSpec((ctx, ATT_HEAD), lambda b, h, t: (ctx_blk0 + b, vcol + h)),
        ],
        out_specs=pl.BlockSpec((tq, gw), qmap),
        compiler_params=_cparams(("parallel", "parallel", "arbitrary")),
        name="gqa_attention",
    )(qkv, qkv, qkv, qkv, qkv)


def _na_kernel(q_ref, k_ref, v_ref, kc_ref, vc_ref, bias_ref, o_ref, kct_scr, *, rb, hps, grid_rows,
               n_row_blocks):
    t = pl.program_id(2)
    win = NA_WIN_R * GRID_W

    @pl.when(t == 0)
    def _():
        for hh in range(hps):
            kct_scr[hh] = kc_ref[:, hh * ATT_HEAD:(hh + 1) * ATT_HEAD].astype(F32).T.astype(BF16)

    @pl.when(t < n_row_blocks)
    def _():
        units = []
        for hh in range(hps):
            hs = slice(hh * ATT_HEAD, (hh + 1) * ATT_HEAD)
            for rr in range(rb):
                r = t * rb + rr
                rs = jnp.clip(r - NA_WIN_R // 2, 0, grid_rows - NA_WIN_R)
                units.append(dict(hh=hh, hs=hs, rows=slice(rr * GRID_W, (rr + 1) * GRID_W), off=r - rs,
                                  start=pl.multiple_of(rs * GRID_W, GRID_W)))
        for un in units:
            q = q_ref[un["rows"], un["hs"]]
            un["sw"] = _dot_nt(q, k_ref[pl.ds(un["start"], win), un["hs"]]) + bias_ref[un["hh"], un["off"]]
            un["sc"] = _dot(q, kct_scr[un["hh"]])
        for un in units:
            m = jnp.maximum(un["sw"].max(axis=-1, keepdims=True), un["sc"].max(axis=-1, keepdims=True))
            un["pw"] = jnp.exp2(un["sw"] - m)
            un["pc"] = jnp.exp2(un["sc"] - m)
        for un in units:
            l = un["pw"].sum(axis=-1, keepdims=True) + un["pc"].sum(axis=-1, keepdims=True)
            o = (_dot(un["pw"].astype(BF16), v_ref[pl.ds(un["start"], win), un["hs"]])
                 + _dot(un["pc"].astype(BF16), vc_ref[:, un["hs"]]))
            o_ref[un["rows"], un["hs"]] = (o / l).astype(BF16)

    @pl.when(t >= n_row_blocks)
    def _():
        for hh in range(hps):
            hs = slice(hh * ATT_HEAD, (hh + 1) * ATT_HEAD)
            o_ref[:, hs] = _softmax_pv(q_ref[:, hs], [(kc_ref[:, hs], vc_ref[:, hs])]).astype(BF16)


def _na_bias_table(rpb):
    qc = jnp.arange(GRID_W)
    kc = jnp.arange(GRID_W)
    cs = jnp.clip(qc - NA_WIN_C // 2, 0, GRID_W - NA_WIN_C)
    inwin = (kc[None, :] >= cs[:, None]) & (kc[None, :] < cs[:, None] + NA_WIN_C)
    cidx = kc[None, :] - qc[:, None] + NA_WIN_C - 1
    sel = (cidx[None] == jnp.arange(2 * NA_WIN_C - 1)[:, None, None]) & inwin[None]
    cols = jnp.einsum('hrc,cqk->hrqk', rpb, sel.astype(F32), precision=lax.Precision.HIGHEST)
    cols = jnp.where(inwin[None, None], cols * math.log2(math.e), NEG_BIG)
    tab = jnp.stack([cols[:, NA_WIN_R - 1 - o:2 * NA_WIN_R - 1 - o] for o in range(NA_WIN_R)], axis=1)
    tab = tab.transpose(0, 1, 3, 2, 4)
    return tab.reshape(rpb.shape[0], NA_WIN_R, GRID_W, NA_WIN_R * GRID_W).astype(F32)


def _na_attention(qkv, rpb, *, dm, d):
    rows = qkv.shape[0]
    nb, seq, ctx = dm["nb"], dm["seq"], dm["ctx"]
    heads = d // ATT_HEAD
    grid_rows = seq // GRID_W
    rb = ctx // GRID_W
    nrb = grid_rows // rb
    ctx_blk0 = nb * seq // ctx
    bias = _na_bias_table(rpb)

    def qmap(b, h, t):
        return (jnp.where(t < nrb, b * nrb + t, ctx_blk0 + b), h)

    hps = 4 if heads % 4 == 0 else 2
    hw = hps * ATT_HEAD
    ng = heads // hps
    kern = functools.partial(_na_kernel, rb=rb, hps=hps, grid_rows=grid_rows, n_row_blocks=nrb)
    return pl.pallas_call(
        kern,
        out_shape=jax.ShapeDtypeStruct((rows, d), BF16),
        grid=(nb, ng, nrb + 1),
        in_specs=[
            pl.BlockSpec((ctx, hw), qmap),
            pl.BlockSpec((seq, hw), lambda b, h, t: (b, ng + h)),
            pl.BlockSpec((seq, hw), lambda b, h, t: (b, 2 * ng + h)),
            pl.BlockSpec((ctx, hw), lambda b, h, t: (ctx_blk0 + b, ng + h)),
            pl.BlockSpec((ctx, hw), lambda b, h, t: (ctx_blk0 + b, 2 * ng + h)),
            pl.BlockSpec((hps, NA_WIN_R, GRID_W, NA_WIN_R * GRID_W), lambda b, h, t: (h, 0, 0, 0)),
        ],
        out_specs=pl.BlockSpec((ctx, hw), qmap),
        scratch_shapes=[pltpu.VMEM((hps, ATT_HEAD, ctx), BF16)],
        compiler_params=_cparams(("parallel", "parallel", "arbitrary")),
        name="na_attention",
    )(qkv, qkv, qkv, qkv, qkv, bias)


def _rw_prep_kernel(*refs, mix, te, n_lat_rows, seq, ctx):
    (x_ref, xp_ref, xn_ref, mod_ref, g_ref, mu_ref, w1_ref, a1_ref, g1_ref, w2_ref, a2_ref, g2_ref,
     w0_ref, a0_ref) = refs[:14]
    if mix:
        v1_ref, v2_ref, v0_ref, xm_ref, lw_ref, a_ref, go_ref, vg_ref = refs[14:]
    else:
        xm_ref, lw_ref, a_ref, go_ref = refs[14:]
    i = pl.program_id(0)
    g = g_ref[...]
    shift = mod_ref[0, 0:1, :]
    scale = mod_ref[0, 1:2, :]
    h = _normmod(x_ref[...], g, shift, scale)
    halo = _normmod(jnp.concatenate([xp_ref[...], xn_ref[...]], axis=0), g, shift, scale)
    rid, first, last = _seq_edges(i, te, n_lat_rows, seq, ctx)
    hp, hn = _shift_rows(h, rid, first, last, halo[SUBLANES - 1:SUBLANES], halo[SUBLANES:SUBLANES + 1])
    xx = 0.5 * (hp + hn) - h

    def mixed(p):
        return (h + xx * mu_ref[p:p + 1, :]).astype(BF16)

    xv = mixed(2)
    xm_ref[0] = mixed(0)
    xm_ref[1] = mixed(1)
    xm_ref[2] = xv
    zw = jnp.tanh(_dot(mixed(3), w1_ref[...])).astype(BF16)
    za = _dot(mixed(4), a1_ref[...]).astype(BF16)
    zg = jax.nn.sigmoid(_dot(mixed(5), g1_ref[...])).astype(BF16)
    for dd in range(2):
        sl = slice(dd * LANES, (dd + 1) * LANES)
        wl = w0_ref[dd:dd + 1, :] + _dot(zw[:, sl], w2_ref[dd])
        lw_ref[dd] = (-math.exp(-0.5)) * jax.nn.sigmoid(wl)
        a_ref[dd] = jax.nn.sigmoid(a0_ref[dd:dd + 1, :] + _dot(za[:, sl], a2_ref[dd])).astype(BF16)
    go_ref[...] = _dot(zg, g2_ref[...]).astype(BF16)
    if mix:
        zv = _dot(xv, v1_ref[...]).astype(BF16)
        vg_ref[...] = jax.nn.sigmoid(v0_ref[...] + _dot(zv, v2_ref[...])).astype(BF16)


def _pad_rank(w1, w2):
    r = w1.shape[-1]
    pad = (-r) % LANES
    w1 = jnp.pad(w1, [(0, 0)] * (w1.ndim - 1) + [(0, pad)])
    w2 = jnp.pad(w2, [(0, 0)] * (w2.ndim - 2) + [(0, pad), (0, 0)])
    return w1.astype(BF16), w2.astype(BF16)


def _rw_prep(xs, mod, g1n, mu, w0, w1, w2, a0, a1, a2, g1, g2, vres, *, dm):
    rows, d = xs.shape
    te = dm["te"]
    hb = te // SUBLANES
    last_hb = rows // SUBLANES - 1
    mix = vres is not None
    w1p, w2p = _pad_rank(w1, w2)
    a1p, a2p = _pad_rank(a1, a2)
    w1c = jnp.concatenate([w1p[0], w1p[1]], axis=1)
    a1c = jnp.concatenate([a1p[0], a1p[1]], axis=1)
    rg = g1.shape[1]
    modmap = lambda i: (jnp.minimum((i * te) // dm["seq"], dm["nb"]), 0, 0)
    full = lambda shp: pl.BlockSpec(shp, lambda i: (0,) * len(shp))
    in_specs = [
        pl.BlockSpec((te, d), lambda i: (i, 0)),
        pl.BlockSpec((SUBLANES, d), lambda i: (jnp.maximum(i * hb - 1, 0), 0)),
        pl.BlockSpec((SUBLANES, d), lambda i: (jnp.minimum((i + 1) * hb, last_hb), 0)),
        pl.BlockSpec((1, 6, d), modmap),
        full((1, d)), full((6, d)),
        full((d, 2 * LANES)), full((d, 2 * LANES)), full((d, rg)),
        full((2, LANES, d)), full((2, LANES, d)), full((rg, d)),
        full((2, d)), full((2, d)),
    ]
    args = [xs, xs, xs, mod, g1n.reshape(1, d), mu, w1c, a1c, g1.astype(BF16), w2p, a2p, g2.astype(BF16), w0, a0]
    row_spec = pl.BlockSpec((te, d), lambda i: (i, 0))
    out_shape = [jax.ShapeDtypeStruct((3, rows, d), BF16), jax.ShapeDtypeStruct((2, rows, d), F32),
                 jax.ShapeDtypeStruct((2, rows, d), BF16), jax.ShapeDtypeStruct((rows, d), BF16)]
    out_specs = [pl.BlockSpec((3, te, d), lambda i: (0, i, 0)), pl.BlockSpec((2, te, d), lambda i: (0, i, 0)),
                 pl.BlockSpec((2, te, d), lambda i: (0, i, 0)), row_spec]
    if mix:
        v1p, v2p = _pad_rank(vres[1], vres[2])
        in_specs += [full((d, LANES)), full((LANES, d)), full((1, d))]
        args += [v1p, v2p, vres[0].reshape(1, d)]
        out_shape.append(jax.ShapeDtypeStruct((rows, d), BF16))
        out_specs.append(row_spec)
    kern = functools.partial(_rw_prep_kernel, mix=mix, te=te, n_lat_rows=dm["n_lat_rows"], seq=dm["seq"],
                             ctx=dm["ctx"])
    return pl.pallas_call(
        kern,
        out_shape=out_shape,
        grid=(rows // te,),
        in_specs=in_specs,
        out_specs=out_specs,
        compiler_params=_cparams(("parallel",)),
        name="rwkv_prep",
    )(*args)


def _rkv_kernel(*refs, mix):
    if mix:
        xm_ref, w_ref, vf_ref, vg_ref, o_ref = refs
    else:
        xm_ref, w_ref, o_ref = refs
    acc = _dot(xm_ref[0], w_ref[0])
    if mix:
        p = pl.program_id(1)

        @pl.when(p == 2)
        def _():
            o_ref[0] = (acc + (vf_ref[0].astype(F32) - acc) * vg_ref[...].astype(F32)).astype(BF16)

        @pl.when(p != 2)
        def _():
            o_ref[0] = acc.astype(BF16)
    else:
        o_ref[0] = acc.astype(BF16)


def _rkv_proj(xm, w, v_first, vgate, *, layer, dm):
    _, rows, d = xm.shape
    tm = dm["tm"]
    mix = v_first is not None
    in_specs = [pl.BlockSpec((1, tm, d), lambda i, p: (p, i, 0)),
                pl.BlockSpec((None, 1, d, d), lambda i, p: (layer, p, 0, 0))]
    args = [xm, w]
    if mix:
        in_specs += [pl.BlockSpec((1, tm, d), lambda i, p: (2, i, 0)), pl.BlockSpec((tm, d), lambda i, p: (i, 0))]
        args += [v_first, vgate]
    return pl.pallas_call(
        functools.partial(_rkv_kernel, mix=mix),
        out_shape=jax.ShapeDtypeStruct((3, rows, d), BF16),
        grid=(rows // tm, 3),
        in_specs=in_specs,
        out_specs=pl.BlockSpec((1, tm, d), lambda i, p: (p, i, 0)),
        compiler_params=_cparams(("parallel", "arbitrary")),
        name="rwkv_rkv_proj",
    )(*args)


def _wkv_kernel(rf_ref, kf_ref, vf_ref, lwf_ref, af_ref, rb_ref, kb_ref, vb_ref, lwb_ref, ab_ref,
                kk_ref, ka_ref, rk_ref, yf_ref, bonf_ref, yb_ref, bonb_ref, s_scr, *, gps):
    c = pl.program_id(2)
    ln = WKV_CHUNK
    pw = WKV_PACK * RW_HEAD

    hpt = LANES // RW_HEAD

    @pl.when(c == 0)
    def _():
        s_scr[...] = jnp.zeros_like(s_scr)

    row = lax.broadcasted_iota(jnp.int32, (ln, ln), 0)
    col = lax.broadcasted_iota(jnp.int32, (ln, ln), 1)
    trow = lax.broadcasted_iota(jnp.int32, (ln, pw), 0)
    tsrc = lax.broadcasted_iota(jnp.int32, (ln, pw), 1) & (ln - 1)
    ones64 = _ones_blockdiag64()

    def fmask(cond):
        return jnp.where(cond, 1.0, 0.0)

    def same_block(n):
        sh = n.bit_length() - 1
        return (tsrc >> sh) == (trow >> sh)

    eye = fmask(tsrc == trow)
    base_f = fmask(same_block(WKV_INV_BASE))
    off_f = {}
    n = WKV_INV_BASE
    while n < ln:
        off_f[n] = fmask(jnp.logical_and(same_block(2 * n), jnp.logical_not(same_block(n))))
        n *= 2
    k_k = kk_ref[...]
    k_a = ka_ref[...]
    r_k = rk_ref[...]

    lane = lax.broadcasted_iota(jnp.int32, (ln, LANES), 1)
    half_f = [fmask((lane >> (RW_HEAD.bit_length() - 1)) == hf) for hf in range(hpt)]
    half_b = [hm.astype(BF16) for hm in half_f]
    zeros_b = jnp.zeros((ln, LANES), BF16)

    def bd(z):
        zb = z.astype(BF16)
        blocks = []
        for jh in range(WKV_PACK):
            lt, hf = divmod(jh, hpt)
            piece = zb[:, lt * LANES:(lt + 1) * LANES] * half_b[hf]
            blocks.append(jnp.concatenate([piece if tt == lt else zeros_b for tt in range(pw // LANES)], axis=1))
        return jnp.concatenate(blocks, axis=0)

    streams = ((rf_ref, kf_ref, vf_ref, lwf_ref, af_ref, yf_ref, bonf_ref),
               (rb_ref, kb_ref, vb_ref, lwb_ref, ab_ref, yb_ref, bonb_ref))
    units = []
    for dd, (r_ref, k_ref, v_ref, lw_ref, a_ref, y_ref, bon_ref) in enumerate(streams):
        rev = dd == 1
        r = r_ref[0].astype(F32)
        k = k_ref[0].astype(F32)
        v = v_ref[0].astype(F32)
        a = a_ref[0].astype(F32)
        lw = lw_ref[0]
        kkr = k * k_k
        kd = k * (1.0 + (a - 1.0) * k_a)
        ssq, rkd = _segsum64_mxu([kkr * kkr, r * kd * r_k], ones64, split=False)
        kk = kkr * lax.rsqrt(jnp.maximum(ssq, 1e-24))
        bvec = kk * a
        bon_ref[...] = (rkd * v).astype(BF16)

        tri = jnp.where((col >= row) if rev else (col <= row), 1.0, 0.0).astype(BF16)
        hi = lw.astype(BF16)
        lo = (lw - hi.astype(F32)).astype(BF16)
        cum = _dot(tri, hi) + _dot(tri, lo)
        tot = cum[0:1, :] if rev else cum[ln - 1:ln, :]
        w_inv = jnp.exp(-cum)
        w_end = jnp.exp(tot - cum)
        a_t = -kk * jnp.exp(cum - lw)
        r_t = r * jnp.exp(cum)
        b_t = bvec * w_inv
        k_t = kd * w_inv
        b_e = bvec * w_end
        k_e = kd * w_end
        w_tot = jnp.exp(tot)

        strict = fmask(tsrc > trow) if rev else fmask(tsrc < trow)
        incl = strict + eye

        for gi in range(gps):
            sl = slice(gi * pw, (gi + 1) * pw)
            units.append(dict(
                dd=dd, gi=gi, sl=sl, y_ref=y_ref, strict=strict, incl=incl,
                ar=jnp.concatenate([a_t[:, sl], r_t[:, sl]], axis=0).astype(BF16),
                b_t=b_t[:, sl], k_t=k_t[:, sl], v=v[:, sl], w_tot=w_tot[:, sl],
                bk=jnp.concatenate([b_e[:, sl], k_e[:, sl]], axis=0).astype(BF16)))

    for un in units:
        sb = _dot_nt(un["ar"], bd(un["b_t"]))
        sk = _dot_nt(un["ar"], bd(un["k_t"]))
        un["m_ab"] = sb[:ln] * un["strict"]
        un["p_rb"] = sb[ln:] * un["incl"]
        un["m_ak"] = sk[:ln] * un["strict"]
        un["p_rk"] = sk[ln:] * un["incl"]
    for un in units:
        un["s0"] = s_scr[un["dd"], un["gi"]]
        un["ars"] = _dot(un["ar"], un["s0"].T.astype(BF16))
        un["mv"] = _dot(jnp.concatenate([un["m_ak"], un["p_rk"]], axis=0).astype(BF16), bd(un["v"]))
    for un in units:
        m0 = un["m_ab"] * base_f
        un["pinv"] = eye + m0
        un["mp"] = _dot(m0.astype(BF16), bd(m0))
    for un in units:
        both = _dot(jnp.concatenate([un["mp"], un["pinv"]], axis=0).astype(BF16), bd(un["mp"]))
        un["pinv"] = un["pinv"] + both[ln:]
        un["mp"] = both[:ln]
    for un in units:
        un["pinv"] = un["pinv"] + _dot(un["pinv"].astype(BF16), bd(un["mp"]))
    n = WKV_INV_BASE
    while n < ln:
        for un in units:
            un["t1"] = _dot((un["m_ab"] * off_f[n]).astype(BF16), bd(un["pinv"]))
        for un in units:
            un["pinv"] = un["pinv"] + _dot(un["pinv"].astype(BF16), bd(un["t1"]))
        n *= 2
    for un in units:
        un["u"] = _dot(un["pinv"].astype(BF16), bd(un["ars"][:ln] + un["mv"][:ln]))
    for un in units:
        un["y_ref"][:, un["sl"]] = (un["ars"][ln:] + _dot(un["p_rb"].astype(BF16), bd(un["u"]))
                                    + un["mv"][ln:]).astype(BF16)
        uv = jnp.concatenate([un["u"], un["v"]], axis=0).astype(BF16)
        res = _dot_tn(uv, un["bk"])
        for jh in range(WKV_PACK):
            lt, hf = divmod(jh, hpt)
            rsl = slice(jh * RW_HEAD, (jh + 1) * RW_HEAD)
            csl = slice(lt * LANES, (lt + 1) * LANES)
            s_scr[un["dd"], un["gi"], rsl, csl] = (un["s0"][rsl, csl] * un["w_tot"][:, csl]
                                                    + res[rsl, csl] * half_f[hf])


def _wkv(rkv, lw, a, k_k, k_a, r_k, *, dm, gps):
    _, rows, d = rkv.shape
    nb, seq, ctx = dm["nb"], dm["seq"], dm["ctx"]
    ln = WKV_CHUNK
    sw = gps * WKV_PACK * RW_HEAD
    ncc, nlc = ctx // ln, seq // ln
    ctx_c0 = nb * seq // ln

    def fblk(b, c):
        return jnp.where(c < ncc, ctx_c0 + b * ncc + c, b * nlc + (c - ncc))

    def bblk(b, c):
        return jnp.where(c < ncc, ctx_c0 + b * ncc + (ncc - 1 - c), b * nlc + (nlc - 1 - (c - ncc)))

    def spec3(p, blk):
        return pl.BlockSpec((1, ln, sw), lambda b, s, c, p=p, blk=blk: (p, blk(b, c), s))

    def spec2(blk):
        return pl.BlockSpec((ln, sw), lambda b, s, c, blk=blk: (blk(b, c), s))

    pspec = pl.BlockSpec((1, sw), lambda b, s, c: (0, s))
    in_specs = [spec3(0, fblk), spec3(1, fblk), spec3(2, fblk), spec3(0, fblk), spec3(0, fblk),
                spec3(0, bblk), spec3(1, bblk), spec3(2, bblk), spec3(1, bblk), spec3(1, bblk),
                pspec, pspec, pspec]
    return pl.pallas_call(
        functools.partial(_wkv_kernel, gps=gps),
        out_shape=[jax.ShapeDtypeStruct((rows, d), BF16)] * 4,
        grid=(nb, d // sw, ncc + nlc),
        in_specs=in_specs,
        out_specs=[spec2(fblk), spec2(fblk), spec2(bblk), spec2(bblk)],
        scratch_shapes=[pltpu.VMEM((2, gps, WKV_PACK * RW_HEAD, WKV_PACK * RW_HEAD), F32)],
        compiler_params=_cparams(("parallel", "parallel", "arbitrary")),
        name="wkv_scan",
    )(rkv, rkv, rkv, lw, a, rkv, rkv, rkv, lw, a, k_k.reshape(1, d), k_a.reshape(1, d), r_k.reshape(1, d))


def _rw_out_kernel(yf_ref, yb_ref, bf_ref, bb_ref, g_ref, x_ref, mod_ref, lg_ref, lb_ref, w_ref, o_ref):
    y = yf_ref[...].astype(F32) + yb_ref[...].astype(F32)
    ones64 = _ones_blockdiag64()
    mean = _segsum64_mxu([y], ones64)[0] * (1.0 / RW_HEAD)
    yc = y - mean
    var = _segsum64_mxu([yc * yc], ones64, split=False)[0] * (1.0 / RW_HEAD)
    yn = yc * lax.rsqrt(var + RW_GN_EPS)
    bonus = bf_ref[...].astype(F32) + bb_ref[...].astype(F32)
    o = (yn * lg_ref[...] + lb_ref[...] + bonus) * g_ref[...].astype(F32)
    o_ref[...] = x_ref[...] + mod_ref[0, 2:3, :] * _dot(o.astype(BF16), w_ref[...])


def _rw_out(yf, bonf, yb, bonb, g, xs, mod, ln_g, ln_b, w_o, *, dm, n_rows):
    d = xs.shape[1]
    te = dm["te"]
    modmap = lambda i: (jnp.minimum((i * te) // dm["seq"], dm["nb"]), 0, 0)
    rspec = pl.BlockSpec((te, d), lambda i: (i, 0))
    vspec = pl.BlockSpec((1, d), lambda i: (0, 0))
    return pl.pallas_call(
        _rw_out_kernel,
        out_shape=jax.ShapeDtypeStruct((n_rows, d), F32),
        grid=(n_rows // te,),
        in_specs=[rspec, rspec, rspec, rspec, rspec, rspec, pl.BlockSpec((1, 6, d), modmap), vspec, vspec,
                  pl.BlockSpec((d, d), lambda i: (0, 0))],
        out_specs=rspec,
        compiler_params=_cparams(("parallel",)),
        name="rwkv_out",
    )(yf, yb, bonf, bonb, g, xs, mod, ln_g.reshape(1, d), ln_b.reshape(1, d), w_o)


def _rope_tables(dm):
    seq, nb, ctx = dm["seq"], dm["nb"], dm["ctx"]
    t = jnp.arange(seq, dtype=jnp.int32)
    pos = jnp.stack([t // GRID_W, t % GRID_W], axis=-1).astype(F32)
    n_freq = ATT_HEAD // 4
    inv = ROPE_THETA ** (-jnp.arange(n_freq, dtype=F32) / n_freq)
    ang = pos[:, :, None] * inv
    cos, sin = jnp.cos(ang), jnp.sin(ang)
    zero = jnp.zeros_like(sin)
    cos_t = jnp.stack([cos, cos], axis=2).reshape(seq, ATT_HEAD)
    sa_t = jnp.stack([-sin, zero], axis=2).reshape(seq, ATT_HEAD)
    sb_t = jnp.stack([zero, sin], axis=2).reshape(seq, ATT_HEAD)
    nctx = nb * ctx
    full = lambda tab, fill: jnp.concatenate([jnp.tile(tab, (nb, 1)), jnp.full((nctx, ATT_HEAD), fill, F32)], axis=0)
    return full(cos_t, 1.0), full(sa_t, 0.0), full(sb_t, 0.0)


def kernel(x, c, ctx, c_ctx, mod_w, mod_b, norm1_g, norm2_g, ffn_up, ffn_conv_w, ffn_conv_b, ffn_down, rw_mu, rw_w_rkv, rw_w0, rw_w1, rw_w2, rw_a0, rw_a1, rw_a2, rw_g1, rw_g2, rw_k_k, rw_k_a, rw_r_k, rw_ln_g, rw_ln_b, rw_w_o, rw_v0, rw_v1, rw_v2, na_w_qkv, na_q_g, na_k_g, na_rpb, na_w_o, ga_w_qkv, ga_q_g, ga_k_g, ga_w_o):
    nb, seq, d = x.shape
    nctx = ctx.shape[1]
    depth = mod_w.shape[0]
    tm = nb * nctx
    assert seq % tm == 0 and seq & (seq - 1) == 0 and nctx & (nctx - 1) == 0 and nb + 1 <= SUBLANES
    assert seq // GRID_W >= NA_WIN_R and nctx % GRID_W == 0
    dm = dict(nb=nb, seq=seq, ctx=nctx, tm=tm, te=tm // 2, tpb=seq // tm, n_lat_rows=nb * seq)
    n_lat_tiles = nb * seq // tm
    n_tiles = n_lat_tiles + 1
    att_scale = ATT_HEAD ** -0.5 * math.log2(math.e)

    xs = jnp.concatenate([x.reshape(nb * seq, d), ctx.reshape(nb * nctx, d)], axis=0)
    c_all = jnp.concatenate([c, c_ctx[None], jnp.zeros((SUBLANES - nb - 1, d), F32)], axis=0)
    mods = _modulations(c_all, mod_w, mod_b)
    rope_tabs = None
    v_first = None
    ffn_wu, ffn_wd = _ffn_weights(ffn_up, ffn_down)
    rw_rkv_w = rw_w_rkv.astype(BF16)

    for i in range(depth):
        kind, j = i % 3, i // 3
        need_ctx = i < depth - 1
        nt_out = n_tiles if need_ctx else n_lat_tiles
        mod = mods[i, :nb + 1].reshape(nb + 1, 6, d)
        if kind == 0:
            vres = None if j == 0 else (rw_v0[j - 1], rw_v1[j - 1], rw_v2[j - 1])
            prep = _rw_prep(xs, mod, norm1_g[i], rw_mu[j], rw_w0[j], rw_w1[j], rw_w2[j], rw_a0[j], rw_a1[j],
                            rw_a2[j], rw_g1[j], rw_g2[j], vres, dm=dm)
            xm, lw, a, g = prep[0], prep[1], prep[2], prep[3]
            rkv = _rkv_proj(xm, rw_rkv_w, v_first if vres is not None else None,
                            prep[4] if vres is not None else None, layer=j, dm=dm)
            if v_first is None:
                v_first = rkv
            yf, bonf, yb, bonb = _wkv(rkv, lw, a, rw_k_k[j], rw_k_a[j], rw_r_k[j].reshape(-1), dm=dm,
                                      gps=min(8, d // (WKV_PACK * RW_HEAD)))
            xs = _rw_out(yf, bonf, yb, bonb, g, xs, mod, rw_ln_g[j], rw_ln_b[j], rw_w_o[j].astype(BF16),
                         dm=dm, n_rows=nt_out * tm)
        elif kind == 1:
            qkv = _qkv_proj(xs, mod, norm1_g[i], na_w_qkv[j].astype(BF16), na_q_g[j] * att_scale, na_k_g[j],
                            None, dm=dm, kv_dim=d)
            o = _na_attention(qkv, na_rpb[j], dm=dm, d=d)
            xs = _out_proj(o, na_w_o[j].astype(BF16), xs, mod, dm=dm, n_tiles=nt_out)
        else:
            if rope_tabs is None:
                rope_tabs = _rope_tables(dm)
            kv_dim = (ga_w_qkv.shape[-1] - d) // 2
            qkv = _qkv_proj(xs, mod, norm1_g[i], ga_w_qkv[j].astype(BF16), ga_q_g[j] * att_scale, ga_k_g[j],
                            rope_tabs, dm=dm, kv_dim=kv_dim)
            o = _gqa_attention(qkv, dm=dm, d=d, kv_heads=kv_dim // ATT_HEAD)
            xs = _out_proj(o, ga_w_o[j].astype(BF16), xs, mod, dm=dm, n_tiles=nt_out)
        xs = _ffn(xs, mod, norm2_g[i], ffn_wu, ffn_conv_w[i], ffn_conv_b[i], ffn_wd, layer=i, dm=dm,
                  need_ctx=need_ctx)
    return xs[:nb * seq].reshape(nb, seq, d)
```

```python
import functools
import math

import jax
import jax.numpy as jnp
from jax import lax
from jax.experimental import pallas as pl
from jax.experimental.pallas import tpu as pltpu

F32 = jnp.float32
BF16 = jnp.bfloat16

NORM_EPS = 1e-6
GRID_W = 64
ATT_HEAD = 128
RW_HEAD = 64
NA_WIN_R = 8
NA_WIN_C = 16
ROPE_THETA = 10000.0
RW_GN_EPS = 64e-5
LANES = 128
SUBLANES = 8
WKV_CHUNK = 64
WKV_PACK = 4
WKV_INV_BASE = 8
QKV_ROW_PIECES = 2
FFN_ROW_PIECES = 2
VMEM_LIMIT = 56 * 1024 * 1024
NEG_BIG = -1e30


def _cparams(sem):
    return pltpu.CompilerParams(dimension_semantics=sem, vmem_limit_bytes=VMEM_LIMIT)


def _dot(a, b):
    return jnp.dot(a, b, preferred_element_type=F32)


def _dot_nt(a, b):
    return lax.dot_general(a, b, (((1,), (1,)), ((), ())), preferred_element_type=F32)


def _dot_tn(a, b):
    return lax.dot_general(a, b, (((0,), (0,)), ((), ())), preferred_element_type=F32)


def _normmod(x, g, shift, scale):
    ms = jnp.mean(x * x, axis=-1, keepdims=True)
    y = x * lax.rsqrt(ms + NORM_EPS)
    return (y * g) * (1.0 + scale) + shift


def _silu(x):
    return x * jax.nn.sigmoid(x)


def _seq_edges(tile, rows, n_lat_rows, seq, ctx):
    rid = lax.broadcasted_iota(jnp.int32, (rows, 1), 0)
    base = tile * rows
    period = jnp.where(base >= n_lat_rows, ctx, seq)
    pos = (base + rid) & (period - 1)
    return rid, pos == 0, pos == period - 1


def _shift_rows(u, rid, first, last, prev_row, next_row):
    n = u.shape[0]
    up = pltpu.roll(u, 1, axis=0)
    up = jnp.where(rid == 0, prev_row, up)
    up = jnp.where(first, 0.0, up)
    un = pltpu.roll(u, n - 1, axis=0)
    un = jnp.where(rid == n - 1, next_row, un)
    un = jnp.where(last, 0.0, un)
    return up, un


def _ones_blockdiag64():
    sh = RW_HEAD.bit_length() - 1
    r = lax.broadcasted_iota(jnp.int32, (LANES, LANES), 0) >> sh
    c = lax.broadcasted_iota(jnp.int32, (LANES, LANES), 1) >> sh
    return jnp.where(r == c, 1.0, 0.0).astype(BF16)


def _segsum64_mxu(xs, ones, split=True):
    m, n = xs[0].shape
    nslab = n // LANES
    nparts = 2 if split else 1
    pieces = []
    for x in xs:
        hi = x.astype(BF16)
        parts = (hi, (x - hi.astype(F32)).astype(BF16)) if split else (hi,)
        for part in parts:
            pieces += [part[:, c * LANES:(c + 1) * LANES] for c in range(nslab)]
    res = _dot(jnp.concatenate(pieces, axis=0), ones)
    outs = []
    for i in range(len(xs)):
        base = i * nparts * nslab
        cols = []
        for c in range(nslab):
            col = res[(base + c) * m:(base + c + 1) * m]
            if split:
                col = col + res[(base + nslab + c) * m:(base + nslab + c + 1) * m]
            cols.append(col)
        outs.append(jnp.concatenate(cols, axis=1))
    return outs


def _mod_kernel(c_ref, w_ref, b_ref, o_ref):
    s = _silu(c_ref[...]).astype(BF16)
    o_ref[0] = _dot(s, w_ref[0].astype(BF16)) + b_ref[0]


def _modulations(c_all, mod_w, mod_b):
    depth, d, n = mod_w.shape
    tn = n // 8
    return pl.pallas_call(
        _mod_kernel,
        out_shape=jax.ShapeDtypeStruct((depth, SUBLANES, n), F32),
        grid=(depth, n // tn),
        in_specs=[
            pl.BlockSpec((SUBLANES, d), lambda l, j: (0, 0)),
            pl.BlockSpec((1, d, tn), lambda l, j: (l, 0, j)),
            pl.BlockSpec((1, 1, tn), lambda l, j: (l, 0, j)),
        ],
        out_specs=pl.BlockSpec((1, SUBLANES, tn), lambda l, j: (l, 0, j)),
        compiler_params=_cparams(("parallel", "parallel")),
        name="modulation",
    )(c_all, mod_w, mod_b.reshape(depth, 1, n))


def _ffn_kernel(*refs, tm, tile_off, n_lat_rows, seq, ctx):
    x_ref, xp_ref, xn_ref, mod_ref, g_ref, wug_ref, wuv_ref, cp_ref, wd_ref = refs[:9]
    o_ref, h_scr = refs[-2:]
    i = pl.program_id(0) + tile_off
    j = pl.program_id(1)
    cpg = cp_ref[j]
    cpv = cp_ref[pl.num_programs(1) + j]
    shift = mod_ref[0, 3:4, :]
    scale = mod_ref[0, 4:5, :]
    hr = 2 * SUBLANES
    th = tm // FFN_ROW_PIECES

    @pl.when(j == 0)
    def _():
        g = g_ref[...]
        halo = jnp.concatenate([xp_ref[...], xn_ref[...]], axis=0)
        h_scr[0:hr, :] = _normmod(halo, g, shift, scale).astype(BF16)
        h_scr[hr:hr + tm, :] = _normmod(x_ref[...], g, shift, scale).astype(BF16)
        o_ref[...] = jnp.zeros_like(o_ref)

    rid, first, last = _seq_edges(i, tm, n_lat_rows, seq, ctx)
    rid_h = rid[0:th]

    def up_proj(p):
        lo = 0 if p == 0 else hr + p * th
        hp = h_scr[lo:hr + (p + 1) * th, :]
        ug, uv = _dot(hp, wug_ref[...]), _dot(hp, wuv_ref[...])
        if p == 0:
            return dict(g=ug[hr:], v=uv[hr:], halo_g=ug[0:hr], halo_v=uv[0:hr])
        return dict(g=ug, v=uv)

    def conv(main, prev_row, next_row, fm, lm, cp):
        up, un = _shift_rows(main, rid_h, fm, lm, prev_row, next_row)
        return cp[3:4, :] + up * cp[0:1, :] + main * cp[1:2, :] + un * cp[2:3, :]

    pv, nx = SUBLANES - 1, SUBLANES

    def act_down(p, us):
        rs = slice(p * th, (p + 1) * th)
        rows = {}
        for key in ("g", "v"):
            prev_row = us[0]["halo_" + key][pv:pv + 1] if p == 0 else us[p - 1][key][th - 1:th]
            next_row = us[0]["halo_" + key][nx:nx + 1] if p == FFN_ROW_PIECES - 1 else us[p + 1][key][0:1]
            rows[key] = (prev_row, next_row)
        a = (_silu(conv(us[p]["g"], *rows["g"], first[rs], last[rs], cpg))
             * conv(us[p]["v"], *rows["v"], first[rs], last[rs], cpv))
        o_ref[rs, :] += _dot(a.astype(BF16), wd_ref[...])

    us = {0: up_proj(0)}
    for p in range(FFN_ROW_PIECES):
        if p + 1 < FFN_ROW_PIECES:
            us[p + 1] = up_proj(p + 1)
        act_down(p, us)

    @pl.when(j == pl.num_programs(1) - 1)
    def _():
        o_ref[...] = x_ref[...] + mod_ref[0, 5:6, :] * o_ref[...]


FFN_COLS = 512


def _ffn_weights(ffn_up, ffn_down):
    return ffn_up.astype(BF16), ffn_down.astype(BF16)


def _ffn_call(xs, mod, g2, wu, cw, cb, wd, prev, *, layer, dm, tm, tile_off, n_tiles, out_rows):
    rows, d = xs.shape
    f = wd.shape[1]
    fc = FFN_COLS
    nfc = f // fc
    hb = tm // SUBLANES
    last_hb = rows // SUBLANES - 1
    seq, nb = dm["seq"], dm["nb"]
    assert seq % tm == 0 or tile_off * tm >= dm["n_lat_rows"]
    modmap = lambda i, j: (jnp.minimum(((i + tile_off) * tm) // seq, nb), 0, 0)
    kern = functools.partial(_ffn_kernel, tm=tm, tile_off=tile_off,
                             n_lat_rows=dm["n_lat_rows"], seq=seq, ctx=dm["ctx"])
    in_specs = [
        pl.BlockSpec((tm, d), lambda i, j: (i + tile_off, 0), pipeline_mode=pl.Buffered(1)),
        pl.BlockSpec((SUBLANES, d), lambda i, j: (jnp.maximum((i + tile_off) * hb - 1, 0), 0)),
        pl.BlockSpec((SUBLANES, d), lambda i, j: (jnp.minimum((i + tile_off + 1) * hb, last_hb), 0)),
        pl.BlockSpec((1, 6, d), modmap),
        pl.BlockSpec((1, d), lambda i, j: (0, 0)),
        pl.BlockSpec((None, d, fc), lambda i, j: (layer, 0, j)),
        pl.BlockSpec((None, d, fc), lambda i, j: (layer, 0, nfc + j)),
        pl.BlockSpec((2 * nfc, 4, fc), lambda i, j: (0, 0, 0)),
        pl.BlockSpec((None, fc, d), lambda i, j: (layer, j, 0)),
    ]
    cp = jnp.concatenate([cw, cb.reshape(1, -1)], axis=0).reshape(4, 2 * nfc, fc).transpose(1, 0, 2)
    args = [xs, xs, xs, mod, g2.reshape(1, d), wu, wu, cp, wd]
    aliases = {}
    if prev is not None:
        in_specs.append(pl.BlockSpec(memory_space=pl.ANY))
        args.append(prev)
        aliases = {len(args) - 1: 0}
    return pl.pallas_call(
        kern,
        out_shape=jax.ShapeDtypeStruct((out_rows, d), F32),
        grid=(n_tiles, nfc),
        in_specs=in_specs,
        out_specs=pl.BlockSpec((tm, d), lambda i, j: (i + tile_off, 0)),
        scratch_shapes=[pltpu.VMEM((tm + 2 * SUBLANES, d), BF16)],
        input_output_aliases=aliases,
        compiler_params=_cparams(("parallel", "arbitrary")),
        name="conv_ffn",
    )(*args)


def _ffn(xs, mod, g2, wu, cw, cb, wd, *, layer, dm, need_ctx):
    rows = xs.shape[0]
    n_lat_rows, tm = dm["n_lat_rows"], dm["tm"]
    big = 2 * tm
    out_rows = rows if need_ctx else n_lat_rows
    y = _ffn_call(xs, mod, g2, wu, cw, cb, wd, None, layer=layer, dm=dm, tm=big, tile_off=0,
                  n_tiles=n_lat_rows // big, out_rows=out_rows)
    if need_ctx:
        y = _ffn_call(xs, mod, g2, wu, cw, cb, wd, y, layer=layer, dm=dm, tm=tm, tile_off=n_lat_rows // tm,
                      n_tiles=1, out_rows=out_rows)
    return y


def _qkv_kernel(*refs, block_kinds, rope):
    if rope:
        x_ref, mod_ref, g_ref, w_ref, qg_ref, kg_ref, cos_ref, sa_ref, sb_ref, o_ref, h_scr = refs
    else:
        x_ref, mod_ref, g_ref, w_ref, qg_ref, kg_ref, o_ref, h_scr = refs
    j = pl.program_id(1)

    @pl.when(j == 0)
    def _():
        h_scr[...] = _normmod(x_ref[...], g_ref[...], mod_ref[0, 0:1, :], mod_ref[0, 1:2, :]).astype(BF16)

    tmr = h_scr.shape[0] // QKV_ROW_PIECES

    def emit(kinds, acc, rs):
        for hh, kind in enumerate(kinds):
            hs = slice(hh * ATT_HEAD, (hh + 1) * ATT_HEAD)
            y = acc[:, hs]
            if kind != "v":
                gain = qg_ref[...] if kind == "q" else kg_ref[...]
                y = y * lax.rsqrt(jnp.mean(y * y, axis=-1, keepdims=True) + NORM_EPS) * gain
                if rope:
                    y = (y * cos_ref[rs, :] + pltpu.roll(y, ATT_HEAD - 32, axis=1) * sa_ref[rs, :]
                         + pltpu.roll(y, 32, axis=1) * sb_ref[rs, :])
            o_ref[rs, hs] = y.astype(BF16)

    def block(kinds):
        accs = {0: _dot(h_scr[0:tmr, :], w_ref[...])}
        for p in range(QKV_ROW_PIECES):
            if p + 1 < QKV_ROW_PIECES:
                accs[p + 1] = _dot(h_scr[(p + 1) * tmr:(p + 2) * tmr, :], w_ref[...])
            emit(kinds, accs[p], slice(p * tmr, (p + 1) * tmr))

    for jb, kinds in enumerate(block_kinds):
        pl.when(j == jb)(functools.partial(block, kinds))


def _qkv_proj(xs, mod, g1, w, qg, kg, rope_tabs, *, dm, kv_dim):
    rows, d = xs.shape
    n = w.shape[1]
    tn = min(d, 2 * kv_dim)
    hpb = tn // ATT_HEAD
    kinds = ["q"] * (d // ATT_HEAD) + ["k"] * (kv_dim // ATT_HEAD) + ["v"] * (kv_dim // ATT_HEAD)
    block_kinds = tuple(tuple(kinds[b * hpb:(b + 1) * hpb]) for b in range(n // tn))
    tm = dm["tm"]
    modmap = lambda i, j: (jnp.minimum(i // dm["tpb"], dm["nb"]), 0, 0)
    in_specs = [
        pl.BlockSpec((tm, d), lambda i, j: (i, 0)),
        pl.BlockSpec((1, 6, d), modmap),
        pl.BlockSpec((1, d), lambda i, j: (0, 0)),
        pl.BlockSpec((d, tn), lambda i, j: (0, j)),
        pl.BlockSpec((1, ATT_HEAD), lambda i, j: (0, 0)),
        pl.BlockSpec((1, ATT_HEAD), lambda i, j: (0, 0)),
    ]
    args = [xs, mod, g1.reshape(1, d), w, qg.reshape(1, ATT_HEAD), kg.reshape(1, ATT_HEAD)]
    if rope_tabs is not None:
        in_specs += [pl.BlockSpec((tm, ATT_HEAD), lambda i, j: (i, 0))] * 3
        args += list(rope_tabs)
    kern = functools.partial(_qkv_kernel, block_kinds=block_kinds, rope=rope_tabs is not None)
    return pl.pallas_call(
        kern,
        out_shape=jax.ShapeDtypeStruct((rows, n), BF16),
        grid=(rows // tm, n // tn),
        in_specs=in_specs,
        out_specs=pl.BlockSpec((tm, tn), lambda i, j: (i, j)),
        scratch_shapes=[pltpu.VMEM((tm, d), BF16)],
        compiler_params=_cparams(("parallel", "arbitrary")),
        name="qkv_proj",
    )(*args)


def _oproj_kernel(a_ref, w_ref, x_ref, mod_ref, o_ref):
    o_ref[...] = x_ref[...] + mod_ref[0, 2:3, :] * _dot(a_ref[...], w_ref[...])


def _out_proj(a, w, xs, mod, *, dm, n_tiles):
    d = xs.shape[1]
    tm = dm["tm"]
    modmap = lambda i: (jnp.minimum(i // dm["tpb"], dm["nb"]), 0, 0)
    return pl.pallas_call(
        _oproj_kernel,
        out_shape=jax.ShapeDtypeStruct((n_tiles * tm, d), F32),
        grid=(n_tiles,),
        in_specs=[
            pl.BlockSpec((tm, d), lambda i: (i, 0)),
            pl.BlockSpec((d, d), lambda i: (0, 0)),
            pl.BlockSpec((tm, d), lambda i: (i, 0)),
            pl.BlockSpec((1, 6, d), modmap),
        ],
        out_specs=pl.BlockSpec((tm, d), lambda i: (i, 0)),
        compiler_params=_cparams(("parallel",)),
        name="out_proj",
    )(a, w, xs, mod)


def _softmax_pv(q, segs):
    ss = [_dot_nt(q, k) for k, _ in segs]
    m = ss[0].max(axis=-1, keepdims=True)
    for s in ss[1:]:
        m = jnp.maximum(m, s.max(axis=-1, keepdims=True))
    ps = [jnp.exp2(s - m) for s in ss]
    l = ps[0].sum(axis=-1, keepdims=True)
    for p in ps[1:]:
        l = l + p.sum(axis=-1, keepdims=True)
    o = _dot(ps[0].astype(BF16), segs[0][1])
    for p, (_, v) in zip(ps[1:], segs[1:]):
        o = o + _dot(p.astype(BF16), v)
    return o / l


def _lane_tiles(x, op):
    acc = x[:, 0:LANES]
    for j in range(1, x.shape[1] // LANES):
        acc = op(acc, x[:, j * LANES:(j + 1) * LANES])
    return acc


def _gqa_kernel(q_ref, kl_ref, vl_ref, kc_ref, vc_ref, o_ref, s_scr, vx_scr, kt_scr, *, group, n_lat_tiles,
                kchunk):
    t = pl.program_id(2)
    seq, ctx = kl_ref.shape[0], kc_ref.shape[0]

    @pl.when(t == 0)
    def _():
        kt_scr[:, 0:seq] = kl_ref[...].astype(F32).T.astype(BF16)
        kt_scr[:, seq:seq + ctx] = kc_ref[...].astype(F32).T.astype(BF16)
        vx_scr[0:seq, 0:ATT_HEAD] = vl_ref[...]
        vx_scr[seq:seq + ctx, 0:ATT_HEAD] = vc_ref[...]
        vx_scr[:, ATT_HEAD:2 * ATT_HEAD] = jnp.ones((seq + ctx, ATT_HEAD), BF16)

    def run(chunks):
        m = [None] * group
        mrun = [None] * group
        o = [None] * group
        for g in range(group + 1):
            for sz, off in chunks:
                if g < group:
                    s = _dot(q_ref[:, g * ATT_HEAD:(g + 1) * ATT_HEAD], kt_scr[:, off:off + sz])
                    s_scr[g % 2, :, off:off + sz] = s
                    tmax = _lane_tiles(s, jnp.maximum)
                    mrun[g] = tmax if mrun[g] is None else jnp.maximum(mrun[g], tmax)
                if g >= 1:
                    h = g - 1
                    p = jnp.exp2((s_scr[h % 2, :, off:off + sz] - m[h]).astype(BF16))
                    pv = _dot(p, vx_scr[off:off + sz, :])
                    o[h] = pv if o[h] is None else o[h] + pv
            if g < group:
                m[g] = mrun[g].max(axis=-1, keepdims=True)
        for g in range(group):
            o_ref[:, g * ATT_HEAD:(g + 1) * ATT_HEAD] = (o[g][:, 0:ATT_HEAD] / o[g][:, ATT_HEAD:]).astype(BF16)

    lat_chunks = [(kchunk, st) for st in range(0, seq, kchunk)]
    ctx_chunk = (ctx, seq)

    @pl.when(t < n_lat_tiles)
    def _():
        run(lat_chunks + [ctx_chunk])

    @pl.when(t >= n_lat_tiles)
    def _():
        run([ctx_chunk])


def _gqa_attention(qkv, *, dm, d, kv_heads):
    rows = qkv.shape[0]
    nb, seq, ctx = dm["nb"], dm["seq"], dm["ctx"]
    group = d // ATT_HEAD // kv_heads
    gw = group * ATT_HEAD
    tq = min(256, ctx)
    nlt, nct = seq // tq, ctx // tq
    kcol = d // ATT_HEAD
    vcol = kcol + kv_heads
    ctx_blk0 = nb * seq // ctx

    def qmap(b, h, t):
        return (jnp.where(t < nlt, b * nlt + t, nb * nlt + b * nct + (t - nlt)), h)

    kern = functools.partial(_gqa_kernel, group=group, n_lat_tiles=nlt, kchunk=min(512, seq))
    return pl.pallas_call(
        kern,
        out_shape=jax.ShapeDtypeStruct((rows, d), BF16),
        scratch_shapes=[pltpu.VMEM((2, tq, seq + ctx), F32), pltpu.VMEM((seq + ctx, 2 * ATT_HEAD), BF16),
                        pltpu.VMEM((ATT_HEAD, seq + ctx), BF16)],
        grid=(nb, kv_heads, nlt + nct),
        in_specs=[
            pl.BlockSpec((tq, gw), qmap),
            pl.BlockSpec((seq, ATT_HEAD), lambda b, h, t: (b, kcol + h)),
            pl.BlockSpec((seq, ATT_HEAD), lambda b, h, t: (b, vcol + h)),
            pl.BlockSpec((ctx, ATT_HEAD), lambda b, h, t: (ctx_blk0 + b, kcol + h)),
            pl.BlockSpec((ctx, ATT_HEAD), lambda b, h, t: (ctx_blk0 + b, vcol + h)),
        ],
        out_specs=pl.BlockSpec((tq, gw), qmap),
        compiler_params=_cparams(("parallel", "parallel", "arbitrary")),
        name="gqa_attention",
    )(qkv, qkv, qkv, qkv, qkv)


def _na_kernel(q_ref, k_ref, v_ref, kc_ref, vc_ref, bias_ref, o_ref, kct_scr, *, rb, hps, grid_rows,
               n_row_blocks):
    t = pl.program_id(2)
    win = NA_WIN_R * GRID_W

    @pl.when(t == 0)
    def _():
        for hh in range(hps):
            kct_scr[hh] = kc_ref[:, hh * ATT_HEAD:(hh + 1) * ATT_HEAD].astype(F32).T.astype(BF16)

    @pl.when(t < n_row_blocks)
    def _():
        units = []
        for hh in range(hps):
            hs = slice(hh * ATT_HEAD, (hh + 1) * ATT_HEAD)
            for rr in range(rb):
                r = t * rb + rr
                rs = jnp.clip(r - NA_WIN_R // 2, 0, grid_rows - NA_WIN_R)
                units.append(dict(hh=hh, hs=hs, rows=slice(rr * GRID_W, (rr + 1) * GRID_W), off=r - rs,
                                  start=pl.multiple_of(rs * GRID_W, GRID_W)))
        for un in units:
            q = q_ref[un["rows"], un["hs"]]
            un["sw"] = _dot_nt(q, k_ref[pl.ds(un["start"], win), un["hs"]]) + bias_ref[un["hh"], un["off"]]
            un["sc"] = _dot(q, kct_scr[un["hh"]])
        for un in units:
            m = jnp.maximum(un["sw"].max(axis=-1, keepdims=True), un["sc"].max(axis=-1, keepdims=True))
            un["pw"] = jnp.exp2(un["sw"] - m)
            un["pc"] = jnp.exp2(un["sc"] - m)
        for un in units:
            l = un["pw"].sum(axis=-1, keepdims=True) + un["pc"].sum(axis=-1, keepdims=True)
            o = (_dot(un["pw"].astype(BF16), v_ref[pl.ds(un["start"], win), un["hs"]])
                 + _dot(un["pc"].astype(BF16), vc_ref[:, un["hs"]]))
            o_ref[un["rows"], un["hs"]] = (o / l).astype(BF16)

    @pl.when(t >= n_row_blocks)
    def _():
        for hh in range(hps):
            hs = slice(hh * ATT_HEAD, (hh + 1) * ATT_HEAD)
            o_ref[:, hs] = _softmax_pv(q_ref[:, hs], [(kc_ref[:, hs], vc_ref[:, hs])]).astype(BF16)


def _na_bias_table(rpb):
    qc = jnp.arange(GRID_W)
    kc = jnp.arange(GRID_W)
    cs = jnp.clip(qc - NA_WIN_C // 2, 0, GRID_W - NA_WIN_C)
    inwin = (kc[None, :] >= cs[:, None]) & (kc[None, :] < cs[:, None] + NA_WIN_C)
    cidx = kc[None, :] - qc[:, None] + NA_WIN_C - 1
    sel = (cidx[None] == jnp.arange(2 * NA_WIN_C - 1)[:, None, None]) & inwin[None]
    cols = jnp.einsum('hrc,cqk->hrqk', rpb, sel.astype(F32), precision=lax.Precision.HIGHEST)
    cols = jnp.where(inwin[None, None], cols * math.log2(math.e), NEG_BIG)
    tab = jnp.stack([cols[:, NA_WIN_R - 1 - o:2 * NA_WIN_R - 1 - o] for o in range(NA_WIN_R)], axis=1)
    tab = tab.transpose(0, 1, 3, 2, 4)
    return tab.reshape(rpb.shape[0], NA_WIN_R, GRID_W, NA_WIN_R * GRID_W).astype(F32)


def _na_attention(qkv, rpb, *, dm, d):
    rows = qkv.shape[0]
    nb, seq, ctx = dm["nb"], dm["seq"], dm["ctx"]
    heads = d // ATT_HEAD
    grid_rows = seq // GRID_W
    rb = ctx // GRID_W
    nrb = grid_rows // rb
    ctx_blk0 = nb * seq // ctx
    bias = _na_bias_table(rpb)

    def qmap(b, h, t):
        return (jnp.where(t < nrb, b * nrb + t, ctx_blk0 + b), h)

    hps = 4 if heads % 4 == 0 else 2
    hw = hps * ATT_HEAD
    ng = heads // hps
    kern = functools.partial(_na_kernel, rb=rb, hps=hps, grid_rows=grid_rows, n_row_blocks=nrb)
    return pl.pallas_call(
        kern,
        out_shape=jax.ShapeDtypeStruct((rows, d), BF16),
        grid=(nb, ng, nrb + 1),
        in_specs=[
            pl.BlockSpec((ctx, hw), qmap),
            pl.BlockSpec((seq, hw), lambda b, h, t: (b, ng + h)),
            pl.BlockSpec((seq, hw), lambda b, h, t: (b, 2 * ng + h)),
            pl.BlockSpec((ctx, hw), lambda b, h, t: (ctx_blk0 + b, ng + h)),
            pl.BlockSpec((ctx, hw), lambda b, h, t: (ctx_blk0 + b, 2 * ng + h)),
            pl.BlockSpec((hps, NA_WIN_R, GRID_W, NA_WIN_R * GRID_W), lambda b, h, t: (h, 0, 0, 0)),
        ],
        out_specs=pl.BlockSpec((ctx, hw), qmap),
        scratch_shapes=[pltpu.VMEM((hps, ATT_HEAD, ctx), BF16)],
        compiler_params=_cparams(("parallel", "parallel", "arbitrary")),
        name="na_attention",
    )(qkv, qkv, qkv, qkv, qkv, bias)


def _rw_prep_kernel(*refs, mix, te, n_lat_rows, seq, ctx):
    (x_ref, xp_ref, xn_ref, mod_ref, g_ref, mu_ref, w1_ref, a1_ref, g1_ref, w2_ref, a2_ref, g2_ref,
     w0_ref, a0_ref) = refs[:14]
    if mix:
        v1_ref, v2_ref, v0_ref, xm_ref, lw_ref, a_ref, go_ref, vg_ref = refs[14:]
    else:
        xm_ref, lw_ref, a_ref, go_ref = refs[14:]
    i = pl.program_id(0)
    g = g_ref[...]
    shift = mod_ref[0, 0:1, :]
    scale = mod_ref[0, 1:2, :]
    h = _normmod(x_ref[...], g, shift, scale)
    halo = _normmod(jnp.concatenate([xp_ref[...], xn_ref[...]], axis=0), g, shift, scale)
    rid, first, last = _seq_edges(i, te, n_lat_rows, seq, ctx)
    hp, hn = _shift_rows(h, rid, first, last, halo[SUBLANES - 1:SUBLANES], halo[SUBLANES:SUBLANES + 1])
    xx = 0.5 * (hp + hn) - h

    def mixed(p):
        return (h + xx * mu_ref[p:p + 1, :]).astype(BF16)

    xv = mixed(2)
    xm_ref[0] = mixed(0)
    xm_ref[1] = mixed(1)
    xm_ref[2] = xv
    zw = jnp.tanh(_dot(mixed(3), w1_ref[...])).astype(BF16)
    za = _dot(mixed(4), a1_ref[...]).astype(BF16)
    zg = jax.nn.sigmoid(_dot(mixed(5), g1_ref[...])).astype(BF16)
    for dd in range(2):
        sl = slice(dd * LANES, (dd + 1) * LANES)
        wl = w0_ref[dd:dd + 1, :] + _dot(zw[:, sl], w2_ref[dd])
        lw_ref[dd] = (-math.exp(-0.5)) * jax.nn.sigmoid(wl)
        a_ref[dd] = jax.nn.sigmoid(a0_ref[dd:dd + 1, :] + _dot(za[:, sl], a2_ref[dd])).astype(BF16)
    go_ref[...] = _dot(zg, g2_ref[...]).astype(BF16)
    if mix:
        zv = _dot(xv, v1_ref[...]).astype(BF16)
        vg_ref[...] = jax.nn.sigmoid(v0_ref[...] + _dot(zv, v2_ref[...])).astype(BF16)


def _pad_rank(w1, w2):
    r = w1.shape[-1]
    pad = (-r) % LANES
    w1 = jnp.pad(w1, [(0, 0)] * (w1.ndim - 1) + [(0, pad)])
    w2 = jnp.pad(w2, [(0, 0)] * (w2.ndim - 2) + [(0, pad), (0, 0)])
    return w1.astype(BF16), w2.astype(BF16)


def _rw_prep(xs, mod, g1n, mu, w0, w1, w2, a0, a1, a2, g1, g2, vres, *, dm):
    rows, d = xs.shape
    te = dm["te"]
    hb = te // SUBLANES
    last_hb = rows // SUBLANES - 1
    mix = vres is not None
    w1p, w2p = _pad_rank(w1, w2)
    a1p, a2p = _pad_rank(a1, a2)
    w1c = jnp.concatenate([w1p[0], w1p[1]], axis=1)
    a1c = jnp.concatenate([a1p[0], a1p[1]], axis=1)
    rg = g1.shape[1]
    modmap = lambda i: (jnp.minimum((i * te) // dm["seq"], dm["nb"]), 0, 0)
    full = lambda shp: pl.BlockSpec(shp, lambda i: (0,) * len(shp))
    in_specs = [
        pl.BlockSpec((te, d), lambda i: (i, 0)),
        pl.BlockSpec((SUBLANES, d), lambda i: (jnp.maximum(i * hb - 1, 0), 0)),
        pl.BlockSpec((SUBLANES, d), lambda i: (jnp.minimum((i + 1) * hb, last_hb), 0)),
        pl.BlockSpec((1, 6, d), modmap),
        full((1, d)), full((6, d)),
        full((d, 2 * LANES)), full((d, 2 * LANES)), full((d, rg)),
        full((2, LANES, d)), full((2, LANES, d)), full((rg, d)),
        full((2, d)), full((2, d)),
    ]
    args = [xs, xs, xs, mod, g1n.reshape(1, d), mu, w1c, a1c, g1.astype(BF16), w2p, a2p, g2.astype(BF16), w0, a0]
    row_spec = pl.BlockSpec((te, d), lambda i: (i, 0))
    out_shape = [jax.ShapeDtypeStruct((3, rows, d), BF16), jax.ShapeDtypeStruct((2, rows, d), F32),
                 jax.ShapeDtypeStruct((2, rows, d), BF16), jax.ShapeDtypeStruct((rows, d), BF16)]
    out_specs = [pl.BlockSpec((3, te, d), lambda i: (0, i, 0)), pl.BlockSpec((2, te, d), lambda i: (0, i, 0)),
                 pl.BlockSpec((2, te, d), lambda i: (0, i, 0)), row_spec]
    if mix:
        v1p, v2p = _pad_rank(vres[1], vres[2])
        in_specs += [full((d, LANES)), full((LANES, d)), full((1, d))]
        args += [v1p, v2p, vres[0].reshape(1, d)]
        out_shape.append(jax.ShapeDtypeStruct((rows, d), BF16))
        out_specs.append(row_spec)
    kern = functools.partial(_rw_prep_kernel, mix=mix, te=te, n_lat_rows=dm["n_lat_rows"], seq=dm["seq"],
                             ctx=dm["ctx"])
    return pl.pallas_call(
        kern,
        out_shape=out_shape,
        grid=(rows // te,),
        in_specs=in_specs,
        out_specs=out_specs,
        compiler_params=_cparams(("parallel",)),
        name="rwkv_prep",
    )(*args)


def _rkv_kernel(*refs, mix):
    if mix:
        xm_ref, w_ref, vf_ref, vg_ref, o_ref = refs
    else:
        xm_ref, w_ref, o_ref = refs
    acc = _dot(xm_ref[0], w_ref[0])
    if mix:
        p = pl.program_id(1)

        @pl.when(p == 2)
        def _():
            o_ref[0] = (acc + (vf_ref[0].astype(F32) - acc) * vg_ref[...].astype(F32)).astype(BF16)

        @pl.when(p != 2)
        def _():
            o_ref[0] = acc.astype(BF16)
    else:
        o_ref[0] = acc.astype(BF16)


def _rkv_proj(xm, w, v_first, vgate, *, layer, dm):
    _, rows, d = xm.shape
    tm = dm["tm"]
    mix = v_first is not None
    in_specs = [pl.BlockSpec((1, tm, d), lambda i, p: (p, i, 0)),
                pl.BlockSpec((None, 1, d, d), lambda i, p: (layer, p, 0, 0))]
    args = [xm, w]
    if mix:
        in_specs += [pl.BlockSpec((1, tm, d), lambda i, p: (2, i, 0)), pl.BlockSpec((tm, d), lambda i, p: (i, 0))]
        args += [v_first, vgate]
    return pl.pallas_call(
        functools.partial(_rkv_kernel, mix=mix),
        out_shape=jax.ShapeDtypeStruct((3, rows, d), BF16),
        grid=(rows // tm, 3),
        in_specs=in_specs,
        out_specs=pl.BlockSpec((1, tm, d), lambda i, p: (p, i, 0)),
        compiler_params=_cparams(("parallel", "arbitrary")),
        name="rwkv_rkv_proj",
    )(*args)


def _wkv_kernel(rf_ref, kf_ref, vf_ref, lwf_ref, af_ref, rb_ref, kb_ref, vb_ref, lwb_ref, ab_ref,
                kk_ref, ka_ref, rk_ref, yf_ref, bonf_ref, yb_ref, bonb_ref, s_scr, *, gps):
    c = pl.program_id(2)
    ln = WKV_CHUNK
    pw = WKV_PACK * RW_HEAD

    hpt = LANES // RW_HEAD

    @pl.when(c == 0)
    def _():
        s_scr[...] = jnp.zeros_like(s_scr)

    row = lax.broadcasted_iota(jnp.int32, (ln, ln), 0)
    col = lax.broadcasted_iota(jnp.int32, (ln, ln), 1)
    trow = lax.broadcasted_iota(jnp.int32, (ln, pw), 0)
    tsrc = lax.broadcasted_iota(jnp.int32, (ln, pw), 1) & (ln - 1)
    ones64 = _ones_blockdiag64()

    def fmask(cond):
        return jnp.where(cond, 1.0, 0.0)

    def same_block(n):
        sh = n.bit_length() - 1
        return (tsrc >> sh) == (trow >> sh)

    eye = fmask(tsrc == trow)
    base_f = fmask(same_block(WKV_INV_BASE))
    off_f = {}
    n = WKV_INV_BASE
    while n < ln:
        off_f[n] = fmask(jnp.logical_and(same_block(2 * n), jnp.logical_not(same_block(n))))
        n *= 2
    k_k = kk_ref[...]
    k_a = ka_ref[...]
    r_k = rk_ref[...]

    lane = lax.broadcasted_iota(jnp.int32, (ln, LANES), 1)
    half_f = [fmask((lane >> (RW_HEAD.bit_length() - 1)) == hf) for hf in range(hpt)]
    half_b = [hm.astype(BF16) for hm in half_f]
    zeros_b = jnp.zeros((ln, LANES), BF16)

    def bd(z):
        zb = z.astype(BF16)
        blocks = []
        for jh in range(WKV_PACK):
            lt, hf = divmod(jh, hpt)
            piece = zb[:, lt * LANES:(lt + 1) * LANES] * half_b[hf]
            blocks.append(jnp.concatenate([piece if tt == lt else zeros_b for tt in range(pw // LANES)], axis=1))
        return jnp.concatenate(blocks, axis=0)

    streams = ((rf_ref, kf_ref, vf_ref, lwf_ref, af_ref, yf_ref, bonf_ref),
               (rb_ref, kb_ref, vb_ref, lwb_ref, ab_ref, yb_ref, bonb_ref))
    units = []
    for dd, (r_ref, k_ref, v_ref, lw_ref, a_ref, y_ref, bon_ref) in enumerate(streams):
        rev = dd == 1
        r = r_ref[0].astype(F32)
        k = k_ref[0].astype(F32)
        v = v_ref[0].astype(F32)
        a = a_ref[0].astype(F32)
        lw = lw_ref[0]
        kkr = k * k_k
        kd = k * (1.0 + (a - 1.0) * k_a)
        ssq, rkd = _segsum64_mxu([kkr * kkr, r * kd * r_k], ones64, split=False)
        kk = kkr * lax.rsqrt(jnp.maximum(ssq, 1e-24))
        bvec = kk * a
        bon_ref[...] = (rkd * v).astype(BF16)

        tri = jnp.where((col >= row) if rev else (col <= row), 1.0, 0.0).astype(BF16)
        hi = lw.astype(BF16)
        lo = (lw - hi.astype(F32)).astype(BF16)
        cum = _dot(tri, hi) + _dot(tri, lo)
        tot = cum[0:1, :] if rev else cum[ln - 1:ln, :]
        w_inv = jnp.exp(-cum)
        w_end = jnp.exp(tot - cum)
        a_t = -kk * jnp.exp(cum - lw)
        r_t = r * jnp.exp(cum)
        b_t = bvec * w_inv
        k_t = kd * w_inv
        b_e = bvec * w_end
        k_e = kd * w_end
        w_tot = jnp.exp(tot)

        strict = fmask(tsrc > trow) if rev else fmask(tsrc < trow)
        incl = strict + eye

        for gi in range(gps):
            sl = slice(gi * pw, (gi + 1) * pw)
            units.append(dict(
                dd=dd, gi=gi, sl=sl, y_ref=y_ref, strict=strict, incl=incl,
                ar=jnp.concatenate([a_t[:, sl], r_t[:, sl]], axis=0).astype(BF16),
                b_t=b_t[:, sl], k_t=k_t[:, sl], v=v[:, sl], w_tot=w_tot[:, sl],
                bk=jnp.concatenate([b_e[:, sl], k_e[:, sl]], axis=0).astype(BF16)))

    for un in units:
        sb = _dot_nt(un["ar"], bd(un["b_t"]))
        sk = _dot_nt(un["ar"], bd(un["k_t"]))
        un["m_ab"] = sb[:ln] * un["strict"]
        un["p_rb"] = sb[ln:] * un["incl"]
        un["m_ak"] = sk[:ln] * un["strict"]
        un["p_rk"] = sk[ln:] * un["incl"]
    for un in units:
        un["s0"] = s_scr[un["dd"], un["gi"]]
        un["ars"] = _dot(un["ar"], un["s0"].T.astype(BF16))
        un["mv"] = _dot(jnp.concatenate([un["m_ak"], un["p_rk"]], axis=0).astype(BF16), bd(un["v"]))
    for un in units:
        m0 = un["m_ab"] * base_f
        un["pinv"] = eye + m0
        un["mp"] = _dot(m0.astype(BF16), bd(m0))
    for un in units:
        both = _dot(jnp.concatenate([un["mp"], un["pinv"]], axis=0).astype(BF16), bd(un["mp"]))
        un["pinv"] = un["pinv"] + both[ln:]
        un["mp"] = both[:ln]
    for un in units:
        un["pinv"] = un["pinv"] + _dot(un["pinv"].astype(BF16), bd(un["mp"]))
    n = WKV_INV_BASE
    while n < ln:
        for un in units:
            un["t1"] = _dot((un["m_ab"] * off_f[n]).astype(BF16), bd(un["pinv"]))
        for un in units:
            un["pinv"] = un["pinv"] + _dot(un["pinv"].astype(BF16), bd(un["t1"]))
        n *= 2
    for un in units:
        un["u"] = _dot(un["pinv"].astype(BF16), bd(un["ars"][:ln] + un["mv"][:ln]))
    for un in units:
        un["y_ref"][:, un["sl"]] = (un["ars"][ln:] + _dot(un["p_rb"].astype(BF16), bd(un["u"]))
                                    + un["mv"][ln:]).astype(BF16)
        uv = jnp.concatenate([un["u"], un["v"]], axis=0).astype(BF16)
        res = _dot_tn(uv, un["bk"])
        for jh in range(WKV_PACK):
            lt, hf = divmod(jh, hpt)
            rsl = slice(jh * RW_HEAD, (jh + 1) * RW_HEAD)
            csl = slice(lt * LANES, (lt + 1) * LANES)
            s_scr[un["dd"], un["gi"], rsl, csl] = (un["s0"][rsl, csl] * un["w_tot"][:, csl]
                                                    + res[rsl, csl] * half_f[hf])


def _wkv(rkv, lw, a, k_k, k_a, r_k, *, dm, gps):
    _, rows, d = rkv.shape
    nb, seq, ctx = dm["nb"], dm["seq"], dm["ctx"]
    ln = WKV_CHUNK
    sw = gps * WKV_PACK * RW_HEAD
    ncc, nlc = ctx // ln, seq // ln
    ctx_c0 = nb * seq // ln

    def fblk(b, c):
        return jnp.where(c < ncc, ctx_c0 + b * ncc + c, b * nlc + (c - ncc))

    def bblk(b, c):
        return jnp.where(c < ncc, ctx_c0 + b * ncc + (ncc - 1 - c), b * nlc + (nlc - 1 - (c - ncc)))

    def spec3(p, blk):
        return pl.BlockSpec((1, ln, sw), lambda b, s, c, p=p, blk=blk: (p, blk(b, c), s))

    def spec2(blk):
        return pl.BlockSpec((ln, sw), lambda b, s, c, blk=blk: (blk(b, c), s))

    pspec = pl.BlockSpec((1, sw), lambda b, s, c: (0, s))
    in_specs = [spec3(0, fblk), spec3(1, fblk), spec3(2, fblk), spec3(0, fblk), spec3(0, fblk),
                spec3(0, bblk), spec3(1, bblk), spec3(2, bblk), spec3(1, bblk), spec3(1, bblk),
                pspec, pspec, pspec]
    return pl.pallas_call(
        functools.partial(_wkv_kernel, gps=gps),
        out_shape=[jax.ShapeDtypeStruct((rows, d), BF16)] * 4,
        grid=(nb, d // sw, ncc + nlc),
        in_specs=in_specs,
        out_specs=[spec2(fblk), spec2(fblk), spec2(bblk), spec2(bblk)],
        scratch_shapes=[pltpu.VMEM((2, gps, WKV_PACK * RW_HEAD, WKV_PACK * RW_HEAD), F32)],
        compiler_params=_cparams(("parallel", "parallel", "arbitrary")),
        name="wkv_scan",
    )(rkv, rkv, rkv, lw, a, rkv, rkv, rkv, lw, a, k_k.reshape(1, d), k_a.reshape(1, d), r_k.reshape(1, d))


def _rw_out_kernel(yf_ref, yb_ref, bf_ref, bb_ref, g_ref, x_ref, mod_ref, lg_ref, lb_ref, w_ref, o_ref):
    y = yf_ref[...].astype(F32) + yb_ref[...].astype(F32)
    ones64 = _ones_blockdiag64()
    mean = _segsum64_mxu([y], ones64)[0] * (1.0 / RW_HEAD)
    yc = y - mean
    var = _segsum64_mxu([yc * yc], ones64, split=False)[0] * (1.0 / RW_HEAD)
    yn = yc * lax.rsqrt(var + RW_GN_EPS)
    bonus = bf_ref[...].astype(F32) + bb_ref[...].astype(F32)
    o = (yn * lg_ref[...] + lb_ref[...] + bonus) * g_ref[...].astype(F32)
    o_ref[...] = x_ref[...] + mod_ref[0, 2:3, :] * _dot(o.astype(BF16), w_ref[...])


def _rw_out(yf, bonf, yb, bonb, g, xs, mod, ln_g, ln_b, w_o, *, dm, n_rows):
    d = xs.shape[1]
    te = dm["te"]
    modmap = lambda i: (jnp.minimum((i * te) // dm["seq"], dm["nb"]), 0, 0)
    rspec = pl.BlockSpec((te, d), lambda i: (i, 0))
    vspec = pl.BlockSpec((1, d), lambda i: (0, 0))
    return pl.pallas_call(
        _rw_out_kernel,
        out_shape=jax.ShapeDtypeStruct((n_rows, d), F32),
        grid=(n_rows // te,),
        in_specs=[rspec, rspec, rspec, rspec, rspec, rspec, pl.BlockSpec((1, 6, d), modmap), vspec, vspec,
                  pl.BlockSpec((d, d), lambda i: (0, 0))],
        out_specs=rspec,
        compiler_params=_cparams(("parallel",)),
        name="rwkv_out",
    )(yf, yb, bonf, bonb, g, xs, mod, ln_g.reshape(1, d), ln_b.reshape(1, d), w_o)


def _rope_tables(dm):
    seq, nb, ctx = dm["seq"], dm["nb"], dm["ctx"]
    t = jnp.arange(seq, dtype=jnp.int32)
    pos = jnp.stack([t // GRID_W, t % GRID_W], axis=-1).astype(F32)
    n_freq = ATT_HEAD // 4
    inv = ROPE_THETA ** (-jnp.arange(n_freq, dtype=F32) / n_freq)
    ang = pos[:, :, None] * inv
    cos, sin = jnp.cos(ang), jnp.sin(ang)
    zero = jnp.zeros_like(sin)
    cos_t = jnp.stack([cos, cos], axis=2).reshape(seq, ATT_HEAD)
    sa_t = jnp.stack([-sin, zero], axis=2).reshape(seq, ATT_HEAD)
    sb_t = jnp.stack([zero, sin], axis=2).reshape(seq, ATT_HEAD)
    nctx = nb * ctx
    full = lambda tab, fill: jnp.concatenate([jnp.tile(tab, (nb, 1)), jnp.full((nctx, ATT_HEAD), fill, F32)], axis=0)
    return full(cos_t, 1.0), full(sa_t, 0.0), full(sb_t, 0.0)


def kernel(x, c, ctx, c_ctx, mod_w, mod_b, norm1_g, norm2_g, ffn_up, ffn_conv_w, ffn_conv_b, ffn_down, rw_mu, rw_w_rkv, rw_w0, rw_w1, rw_w2, rw_a0, rw_a1, rw_a2, rw_g1, rw_g2, rw_k_k, rw_k_a, rw_r_k, rw_ln_g, rw_ln_b, rw_w_o, rw_v0, rw_v1, rw_v2, na_w_qkv, na_q_g, na_k_g, na_rpb, na_w_o, ga_w_qkv, ga_q_g, ga_k_g, ga_w_o):
    nb, seq, d = x.shape
    nctx = ctx.shape[1]
    depth = mod_w.shape[0]
    tm = nb * nctx
    assert seq % tm == 0 and seq & (seq - 1) == 0 and nctx & (nctx - 1) == 0 and nb + 1 <= SUBLANES
    assert seq // GRID_W >= NA_WIN_R and nctx % GRID_W == 0
    dm = dict(nb=nb, seq=seq, ctx=nctx, tm=tm, te=tm // 2, tpb=seq // tm, n_lat_rows=nb * seq)
    n_lat_tiles = nb * seq // tm
    n_tiles = n_lat_tiles + 1
    att_scale = ATT_HEAD ** -0.5 * math.log2(math.e)

    xs = jnp.concatenate([x.reshape(nb * seq, d), ctx.reshape(nb * nctx, d)], axis=0)
    c_all = jnp.concatenate([c, c_ctx[None], jnp.zeros((SUBLANES - nb - 1, d), F32)], axis=0)
    mods = _modulations(c_all, mod_w, mod_b)
    rope_tabs = None
    v_first = None
    ffn_wu, ffn_wd = _ffn_weights(ffn_up, ffn_down)
    rw_rkv_w = rw_w_rkv.astype(BF16)

    for i in range(depth):
        kind, j = i % 3, i // 3
        need_ctx = i < depth - 1
        nt_out = n_tiles if need_ctx else n_lat_tiles
        mod = mods[i, :nb + 1].reshape(nb + 1, 6, d)
        if kind == 0:
            vres = None if j == 0 else (rw_v0[j - 1], rw_v1[j - 1], rw_v2[j - 1])
            prep = _rw_prep(xs, mod, norm1_g[i], rw_mu[j], rw_w0[j], rw_w1[j], rw_w2[j], rw_a0[j], rw_a1[j],
                            rw_a2[j], rw_g1[j], rw_g2[j], vres, dm=dm)
            xm, lw, a, g = prep[0], prep[1], prep[2], prep[3]
            rkv = _rkv_proj(xm, rw_rkv_w, v_first if vres is not None else None,
                            prep[4] if vres is not None else None, layer=j, dm=dm)
            if v_first is None:
                v_first = rkv
            yf, bonf, yb, bonb = _wkv(rkv, lw, a, rw_k_k[j], rw_k_a[j], rw_r_k[j].reshape(-1), dm=dm,
                                      gps=min(8, d // (WKV_PACK * RW_HEAD)))
            xs = _rw_out(yf, bonf, yb, bonb, g, xs, mod, rw_ln_g[j], rw_ln_b[j], rw_w_o[j].astype(BF16),
                         dm=dm, n_rows=nt_out * tm)
        elif kind == 1:
            qkv = _qkv_proj(xs, mod, norm1_g[i], na_w_qkv[j].astype(BF16), na_q_g[j] * att_scale, na_k_g[j],
                            None, dm=dm, kv_dim=d)
            o = _na_attention(qkv, na_rpb[j], dm=dm, d=d)
            xs = _out_proj(o, na_w_o[j].astype(BF16), xs, mod, dm=dm, n_tiles=nt_out)
        else:
            if rope_tabs is None:
                rope_tabs = _rope_tables(dm)
            kv_dim = (ga_w_qkv.shape[-1] - d) // 2
            qkv = _qkv_proj(xs, mod, norm1_g[i], ga_w_qkv[j].astype(BF16), ga_q_g[j] * att_scale, ga_k_g[j],
                            rope_tabs, dm=dm, kv_dim=kv_dim)
            o = _gqa_attention(qkv, dm=dm, d=d, kv_heads=kv_dim // ATT_HEAD)
            xs = _out_proj(o, ga_w_o[j].astype(BF16), xs, mod, dm=dm, n_tiles=nt_out)
        xs = _ffn(xs, mod, norm2_g[i], ffn_wu, ffn_conv_w[i], ffn_conv_b[i], ffn_wd, layer=i, dm=dm,
                  need_ctx=need_ctx)
    return xs[:nb * seq].reshape(nb, seq, d)
```

```python
import functools
import math

import jax
import jax.numpy as jnp
from jax import lax
from jax.experimental import pallas as pl
from jax.experimental.pallas import tpu as pltpu

F32 = jnp.float32
BF16 = jnp.bfloat16

NORM_EPS = 1e-6
GRID_W = 64
ATT_HEAD = 128
RW_HEAD = 64
NA_WIN_R = 8
NA_WIN_C = 16
ROPE_THETA = 10000.0
RW_GN_EPS = 64e-5
LANES = 128
SUBLANES = 8
WKV_CHUNK = 64
WKV_PACK = 4
WKV_INV_BASE = 8
QKV_ROW_PIECES = 2
FFN_ROW_PIECES = 2
VMEM_LIMIT = 56 * 1024 * 1024
NEG_BIG = -1e30


def _cparams(sem):
    return pltpu.CompilerParams(dimension_semantics=sem, vmem_limit_bytes=VMEM_LIMIT)


def _dot(a, b):
    return jnp.dot(a, b, preferred_element_type=F32)


def _dot_nt(a, b):
    return lax.dot_general(a, b, (((1,), (1,)), ((), ())), preferred_element_type=F32)


def _dot_tn(a, b):
    return lax.dot_general(a, b, (((0,), (0,)), ((), ())), preferred_element_type=F32)


def _normmod(x, g, shift, scale):
    ms = jnp.mean(x * x, axis=-1, keepdims=True)
    y = x * lax.rsqrt(ms + NORM_EPS)
    return (y * g) * (1.0 + scale) + shift


def _silu(x):
    return x * jax.nn.sigmoid(x)


def _seq_edges(tile, rows, n_lat_rows, seq, ctx):
    rid = lax.broadcasted_iota(jnp.int32, (rows, 1), 0)
    base = tile * rows
    period = jnp.where(base >= n_lat_rows, ctx, seq)
    pos = (base + rid) & (period - 1)
    return rid, pos == 0, pos == period - 1


def _shift_rows(u, rid, first, last, prev_row, next_row):
    n = u.shape[0]
    up = pltpu.roll(u, 1, axis=0)
    up = jnp.where(rid == 0, prev_row, up)
    up = jnp.where(first, 0.0, up)
    un = pltpu.roll(u, n - 1, axis=0)
    un = jnp.where(rid == n - 1, next_row, un)
    un = jnp.where(last, 0.0, un)
    return up, un


def _ones_blockdiag64():
    sh = RW_HEAD.bit_length() - 1
    r = lax.broadcasted_iota(jnp.int32, (LANES, LANES), 0) >> sh
    c = lax.broadcasted_iota(jnp.int32, (LANES, LANES), 1) >> sh
    return jnp.where(r == c, 1.0, 0.0).astype(BF16)


def _segsum64_mxu(xs, ones, split=True):
    m, n = xs[0].shape
    nslab = n // LANES
    nparts = 2 if split else 1
    pieces = []
    for x in xs:
        hi = x.astype(BF16)
        parts = (hi, (x - hi.astype(F32)).astype(BF16)) if split else (hi,)
        for part in parts:
            pieces += [part[:, c * LANES:(c + 1) * LANES] for c in range(nslab)]
    res = _dot(jnp.concatenate(pieces, axis=0), ones)
    outs = []
    for i in range(len(xs)):
        base = i * nparts * nslab
        cols = []
        for c in range(nslab):
            col = res[(base + c) * m:(base + c + 1) * m]
            if split:
                col = col + res[(base + nslab + c) * m:(base + nslab + c + 1) * m]
            cols.append(col)
        outs.append(jnp.concatenate(cols, axis=1))
    return outs


def _mod_kernel(c_ref, w_ref, b_ref, o_ref):
    s = _silu(c_ref[...]).astype(BF16)
    o_ref[0] = _dot(s, w_ref[0].astype(BF16)) + b_ref[0]


def _modulations(c_all, mod_w, mod_b):
    depth, d, n = mod_w.shape
    tn = n // 8
    return pl.pallas_call(
        _mod_kernel,
        out_shape=jax.ShapeDtypeStruct((depth, SUBLANES, n), F32),
        grid=(depth, n // tn),
        in_specs=[
            pl.BlockSpec((SUBLANES, d), lambda l, j: (0, 0)),
            pl.BlockSpec((1, d, tn), lambda l, j: (l, 0, j)),
            pl.BlockSpec((1, 1, tn), lambda l, j: (l, 0, j)),
        ],
        out_specs=pl.BlockSpec((1, SUBLANES, tn), lambda l, j: (l, 0, j)),
        compiler_params=_cparams(("parallel", "parallel")),
        name="modulation",
    )(c_all, mod_w, mod_b.reshape(depth, 1, n))


def _ffn_kernel(*refs, tm, tile_off, n_lat_rows, seq, ctx):
    x_ref, xp_ref, xn_ref, mod_ref, g_ref, wug_ref, wuv_ref, cp_ref, wd_ref = refs[:9]
    o_ref, h_scr = refs[-2:]
    i = pl.program_id(0) + tile_off
    j = pl.program_id(1)
    cpg = cp_ref[j]
    cpv = cp_ref[pl.num_programs(1) + j]
    shift = mod_ref[0, 3:4, :]
    scale = mod_ref[0, 4:5, :]
    hr = 2 * SUBLANES
    th = tm // FFN_ROW_PIECES

    @pl.when(j == 0)
    def _():
        g = g_ref[...]
        halo = jnp.concatenate([xp_ref[...], xn_ref[...]], axis=0)
        h_scr[0:hr, :] = _normmod(halo, g, shift, scale).astype(BF16)
        h_scr[hr:hr + tm, :] = _normmod(x_ref[...], g, shift, scale).astype(BF16)
        o_ref[...] = jnp.zeros_like(o_ref)

    rid, first, last = _seq_edges(i, tm, n_lat_rows, seq, ctx)
    rid_h = rid[0:th]

    def up_proj(p):
        lo = 0 if p == 0 else hr + p * th
        hp = h_scr[lo:hr + (p + 1) * th, :]
        ug, uv = _dot(hp, wug_ref[...]), _dot(hp, wuv_ref[...])
        if p == 0:
            return dict(g=ug[hr:], v=uv[hr:], halo_g=ug[0:hr], halo_v=uv[0:hr])
        return dict(g=ug, v=uv)

    def conv(main, prev_row, next_row, fm, lm, cp):
        up, un = _shift_rows(main, rid_h, fm, lm, prev_row, next_row)
        return cp[3:4, :] + up * cp[0:1, :] + main * cp[1:2, :] + un * cp[2:3, :]

    pv, nx = SUBLANES - 1, SUBLANES

    def act_down(p, us):
        rs = slice(p * th, (p + 1) * th)
        rows = {}
        for key in ("g", "v"):
            prev_row = us[0]["halo_" + key][pv:pv + 1] if p == 0 else us[p - 1][key][th - 1:th]
            next_row = us[0]["halo_" + key][nx:nx + 1] if p == FFN_ROW_PIECES - 1 else us[p + 1][key][0:1]
            rows[key] = (prev_row, next_row)
        a = (_silu(conv(us[p]["g"], *rows["g"], first[rs], last[rs], cpg))
             * conv(us[p]["v"], *rows["v"], first[rs], last[rs], cpv))
        o_ref[rs, :] += _dot(a.astype(BF16), wd_ref[...])

    us = {0: up_proj(0)}
    for p in range(FFN_ROW_PIECES):
        if p + 1 < FFN_ROW_PIECES:
            us[p + 1] = up_proj(p + 1)
        act_down(p, us)

    @pl.when(j == pl.num_programs(1) - 1)
    def _():
        o_ref[...] = x_ref[...] + mod_ref[0, 5:6, :] * o_ref[...]


FFN_COLS = 512


def _ffn_weights(ffn_up, ffn_down):
    return ffn_up.astype(BF16), ffn_down.astype(BF16)


def _ffn_call(xs, mod, g2, wu, cw, cb, wd, prev, *, layer, dm, tm, tile_off, n_tiles, out_rows):
    rows, d = xs.shape
    f = wd.shape[1]
    fc = FFN_COLS
    nfc = f // fc
    hb = tm // SUBLANES
    last_hb = rows // SUBLANES - 1
    seq, nb = dm["seq"], dm["nb"]
    assert seq % tm == 0 or tile_off * tm >= dm["n_lat_rows"]
    modmap = lambda i, j: (jnp.minimum(((i + tile_off) * tm) // seq, nb), 0, 0)
    kern = functools.partial(_ffn_kernel, tm=tm, tile_off=tile_off,
                             n_lat_rows=dm["n_lat_rows"], seq=seq, ctx=dm["ctx"])
    in_specs = [
        pl.BlockSpec((tm, d), lambda i, j: (i + tile_off, 0), pipeline_mode=pl.Buffered(1)),
        pl.BlockSpec((SUBLANES, d), lambda i, j: (jnp.maximum((i + tile_off) * hb - 1, 0), 0)),
        pl.BlockSpec((SUBLANES, d), lambda i, j: (jnp.minimum((i + tile_off + 1) * hb, last_hb), 0)),
        pl.BlockSpec((1, 6, d), modmap),
        pl.BlockSpec((1, d), lambda i, j: (0, 0)),
        pl.BlockSpec((None, d, fc), lambda i, j: (layer, 0, j)),
        pl.BlockSpec((None, d, fc), lambda i, j: (layer, 0, nfc + j)),
        pl.BlockSpec((2 * nfc, 4, fc), lambda i, j: (0, 0, 0)),
        pl.BlockSpec((None, fc, d), lambda i, j: (layer, j, 0)),
    ]
    cp = jnp.concatenate([cw, cb.reshape(1, -1)], axis=0).reshape(4, 2 * nfc, fc).transpose(1, 0, 2)
    args = [xs, xs, xs, mod, g2.reshape(1, d), wu, wu, cp, wd]
    aliases = {}
    if prev is not None:
        in_specs.append(pl.BlockSpec(memory_space=pl.ANY))
        args.append(prev)
        aliases = {len(args) - 1: 0}
    return pl.pallas_call(
        kern,
        out_shape=jax.ShapeDtypeStruct((out_rows, d), F32),
        grid=(n_tiles, nfc),
        in_specs=in_specs,
        out_specs=pl.BlockSpec((tm, d), lambda i, j: (i + tile_off, 0)),
        scratch_shapes=[pltpu.VMEM((tm + 2 * SUBLANES, d), BF16)],
        input_output_aliases=aliases,
        compiler_params=_cparams(("parallel", "arbitrary")),
        name="conv_ffn",
    )(*args)


def _ffn(xs, mod, g2, wu, cw, cb, wd, *, layer, dm, need_ctx):
    rows = xs.shape[0]
    n_lat_rows, tm = dm["n_lat_rows"], dm["tm"]
    big = 2 * tm
    out_rows = rows if need_ctx else n_lat_rows
    y = _ffn_call(xs, mod, g2, wu, cw, cb, wd, None, layer=layer, dm=dm, tm=big, tile_off=0,
                  n_tiles=n_lat_rows // big, out_rows=out_rows)
    if need_ctx:
        y = _ffn_call(xs, mod, g2, wu, cw, cb, wd, y, layer=layer, dm=dm, tm=tm, tile_off=n_lat_rows // tm,
                      n_tiles=1, out_rows=out_rows)
    return y


def _qkv_kernel(*refs, block_kinds, rope):
    if rope:
        x_ref, mod_ref, g_ref, w_ref, qg_ref, kg_ref, cos_ref, sa_ref, sb_ref, o_ref, h_scr = refs
    else:
        x_ref, mod_ref, g_ref, w_ref, qg_ref, kg_ref, o_ref, h_scr = refs
    j = pl.program_id(1)

    @pl.when(j == 0)
    def _():
        h_scr[...] = _normmod(x_ref[...], g_ref[...], mod_ref[0, 0:1, :], mod_ref[0, 1:2, :]).astype(BF16)

    tmr = h_scr.shape[0] // QKV_ROW_PIECES

    def emit(kinds, acc, rs):
        for hh, kind in enumerate(kinds):
            hs = slice(hh * ATT_HEAD, (hh + 1) * ATT_HEAD)
            y = acc[:, hs]
            if kind != "v":
                gain = qg_ref[...] if kind == "q" else kg_ref[...]
                y = y * lax.rsqrt(jnp.mean(y * y, axis=-1, keepdims=True) + NORM_EPS) * gain
                if rope:
                    y = (y * cos_ref[rs, :] + pltpu.roll(y, ATT_HEAD - 32, axis=1) * sa_ref[rs, :]
                         + pltpu.roll(y, 32, axis=1) * sb_ref[rs, :])
            o_ref[rs, hs] = y.astype(BF16)

    def block(kinds):
        accs = {0: _dot(h_scr[0:tmr, :], w_ref[...])}
        for p in range(QKV_ROW_PIECES):
            if p + 1 < QKV_ROW_PIECES:
                accs[p + 1] = _dot(h_scr[(p + 1) * tmr:(p + 2) * tmr, :], w_ref[...])
            emit(kinds, accs[p], slice(p * tmr, (p + 1) * tmr))

    for jb, kinds in enumerate(block_kinds):
        pl.when(j == jb)(functools.partial(block, kinds))


def _qkv_proj(xs, mod, g1, w, qg, kg, rope_tabs, *, dm, kv_dim):
    rows, d = xs.shape
    n = w.shape[1]
    tn = min(d, 2 * kv_dim)
    hpb = tn // ATT_HEAD
    kinds = ["q"] * (d // ATT_HEAD) + ["k"] * (kv_dim // ATT_HEAD) + ["v"] * (kv_dim // ATT_HEAD)
    block_kinds = tuple(tuple(kinds[b * hpb:(b + 1) * hpb]) for b in range(n // tn))
    tm = dm["tm"]
    modmap = lambda i, j: (jnp.minimum(i // dm["tpb"], dm["nb"]), 0, 0)
    in_specs = [
        pl.BlockSpec((tm, d), lambda i, j: (i, 0)),
        pl.BlockSpec((1, 6, d), modmap),
        pl.BlockSpec((1, d), lambda i, j: (0, 0)),
        pl.BlockSpec((d, tn), lambda i, j: (0, j)),
        pl.BlockSpec((1, ATT_HEAD), lambda i, j: (0, 0)),
        pl.BlockSpec((1, ATT_HEAD), lambda i, j: (0, 0)),
    ]
    args = [xs, mod, g1.reshape(1, d), w, qg.reshape(1, ATT_HEAD), kg.reshape(1, ATT_HEAD)]
    if rope_tabs is not None:
        in_specs += [pl.BlockSpec((tm, ATT_HEAD), lambda i, j: (i, 0))] * 3
        args += list(rope_tabs)
    kern = functools.partial(_qkv_kernel, block_kinds=block_kinds, rope=rope_tabs is not None)
    return pl.pallas_call(
        kern,
        out_shape=jax.ShapeDtypeStruct((rows, n), BF16),
        grid=(rows // tm, n // tn),
        in_specs=in_specs,
        out_specs=pl.BlockSpec((tm, tn), lambda i, j: (i, j)),
        scratch_shapes=[pltpu.VMEM((tm, d), BF16)],
        compiler_params=_cparams(("parallel", "arbitrary")),
        name="qkv_proj",
    )(*args)


def _oproj_kernel(a_ref, w_ref, x_ref, mod_ref, o_ref):
    o_ref[...] = x_ref[...] + mod_ref[0, 2:3, :] * _dot(a_ref[...], w_ref[...])


def _out_proj(a, w, xs, mod, *, dm, n_tiles):
    d = xs.shape[1]
    tm = dm["tm"]
    modmap = lambda i: (jnp.minimum(i // dm["tpb"], dm["nb"]), 0, 0)
    return pl.pallas_call(
        _oproj_kernel,
        out_shape=jax.ShapeDtypeStruct((n_tiles * tm, d), F32),
        grid=(n_tiles,),
        in_specs=[
            pl.BlockSpec((tm, d), lambda i: (i, 0)),
            pl.BlockSpec((d, d), lambda i: (0, 0)),
            pl.BlockSpec((tm, d), lambda i: (i, 0)),
            pl.BlockSpec((1, 6, d), modmap),
        ],
        out_specs=pl.BlockSpec((tm, d), lambda i: (i, 0)),
        compiler_params=_cparams(("parallel",)),
        name="out_proj",
    )(a, w, xs, mod)


def _softmax_pv(q, segs):
    ss = [_dot_nt(q, k) for k, _ in segs]
    m = ss[0].max(axis=-1, keepdims=True)
    for s in ss[1:]:
        m = jnp.maximum(m, s.max(axis=-1, keepdims=True))
    ps = [jnp.exp2(s - m) for s in ss]
    l = ps[0].sum(axis=-1, keepdims=True)
    for p in ps[1:]:
        l = l + p.sum(axis=-1, keepdims=True)
    o = _dot(ps[0].astype(BF16), segs[0][1])
    for p, (_, v) in zip(ps[1:], segs[1:]):
        o = o + _dot(p.astype(BF16), v)
    return o / l


def _lane_tiles(x, op):
    acc = x[:, 0:LANES]
    for j in range(1, x.shape[1] // LANES):
        acc = op(acc, x[:, j * LANES:(j + 1) * LANES])
    return acc


def _gqa_kernel(q_ref, kl_ref, vl_ref, kc_ref, vc_ref, o_ref, s_scr, vx_scr, kt_scr, *, group, n_lat_tiles,
                kchunk):
    t = pl.program_id(2)
    seq, ctx = kl_ref.shape[0], kc_ref.shape[0]

    @pl.when(t == 0)
    def _():
        kt_scr[:, 0:seq] = kl_ref[...].astype(F32).T.astype(BF16)
        kt_scr[:, seq:seq + ctx] = kc_ref[...].astype(F32).T.astype(BF16)
        vx_scr[0:seq, 0:ATT_HEAD] = vl_ref[...]
        vx_scr[seq:seq + ctx, 0:ATT_HEAD] = vc_ref[...]
        vx_scr[:, ATT_HEAD:2 * ATT_HEAD] = jnp.ones((seq + ctx, ATT_HEAD), BF16)

    def run(chunks):
        m = [None] * group
        mrun = [None] * group
        o = [None] * group
        for g in range(group + 1):
            for sz, off in chunks:
                if g < group:
                    s = _dot(q_ref[:, g * ATT_HEAD:(g + 1) * ATT_HEAD], kt_scr[:, off:off + sz])
                    s_scr[g % 2, :, off:off + sz] = s
                    tmax = _lane_tiles(s, jnp.maximum)
                    mrun[g] = tmax if mrun[g] is None else jnp.maximum(mrun[g], tmax)
                if g >= 1:
                    h = g - 1
                    p = jnp.exp2((s_scr[h % 2, :, off:off + sz] - m[h]).astype(BF16))
                    pv = _dot(p, vx_scr[off:off + sz, :])
                    o[h] = pv if o[h] is None else o[h] + pv
            if g < group:
                m[g] = mrun[g].max(axis=-1, keepdims=True)
        for g in range(group):
            o_ref[:, g * ATT_HEAD:(g + 1) * ATT_HEAD] = (o[g][:, 0:ATT_HEAD] / o[g][:, ATT_HEAD:]).astype(BF16)

    lat_chunks = [(kchunk, st) for st in range(0, seq, kchunk)]
    ctx_chunk = (ctx, seq)

    @pl.when(t < n_lat_tiles)
    def _():
        run(lat_chunks + [ctx_chunk])

    @pl.when(t >= n_lat_tiles)
    def _():
        run([ctx_chunk])


def _gqa_attention(qkv, *, dm, d, kv_heads):
    rows = qkv.shape[0]
    nb, seq, ctx = dm["nb"], dm["seq"], dm["ctx"]
    group = d // ATT_HEAD // kv_heads
    gw = group * ATT_HEAD
    tq = min(256, ctx)
    nlt, nct = seq // tq, ctx // tq
    kcol = d // ATT_HEAD
    vcol = kcol + kv_heads
    ctx_blk0 = nb * seq // ctx

    def qmap(b, h, t):
        return (jnp.where(t < nlt, b * nlt + t, nb * nlt + b * nct + (t - nlt)), h)

    kern = functools.partial(_gqa_kernel, group=group, n_lat_tiles=nlt, kchunk=min(512, seq))
    return pl.pallas_call(
        kern,
        out_shape=jax.ShapeDtypeStruct((rows, d), BF16),
        scratch_shapes=[pltpu.VMEM((2, tq, seq + ctx), F32), pltpu.VMEM((seq + ctx, 2 * ATT_HEAD), BF16),
                        pltpu.VMEM((ATT_HEAD, seq + ctx), BF16)],
        grid=(nb, kv_heads, nlt + nct),
        in_specs=[
            pl.BlockSpec((tq, gw), qmap),
            pl.BlockSpec((seq, ATT_HEAD), lambda b, h, t: (b, kcol + h)),
            pl.BlockSpec((seq, ATT_HEAD), lambda b, h, t: (b, vcol + h)),
            pl.BlockSpec((ctx, ATT_HEAD), lambda b, h, t: (ctx_blk0 + b, kcol + h)),
            pl.BlockSpec((ctx, ATT_HEAD), lambda b, h, t: (ctx_blk0 + b, vcol + h)),
        ],
        out_specs=pl.BlockSpec((tq, gw), qmap),
        compiler_params=_cparams(("parallel", "parallel", "arbitrary")),
        name="gqa_attention",
    )(qkv, qkv, qkv, qkv, qkv)


def _na_kernel(q_ref, k_ref, v_ref, kc_ref, vc_ref, bias_ref, o_ref, kct_scr, *, rb, hps, grid_rows,
               n_row_blocks):
    t = pl.program_id(2)
    win = NA_WIN_R * GRID_W

    @pl.when(t == 0)
    def _():
        for hh in range(hps):
            kct_scr[hh] = kc_ref[:, hh * ATT_HEAD:(hh + 1) * ATT_HEAD].astype(F32).T.astype(BF16)

    @pl.when(t < n_row_blocks)
    def _():
        units = []
        for hh in range(hps):
            hs = slice(hh * ATT_HEAD, (hh + 1) * ATT_HEAD)
            for rr in range(rb):
                r = t * rb + rr
                rs = jnp.clip(r - NA_WIN_R // 2, 0, grid_rows - NA_WIN_R)
                units.append(dict(hh=hh, hs=hs, rows=slice(rr * GRID_W, (rr + 1) * GRID_W), off=r - rs,
                                  start=pl.multiple_of(rs * GRID_W, GRID_W)))
        for un in units:
            q = q_ref[un["rows"], un["hs"]]
            un["sw"] = _dot_nt(q, k_ref[pl.ds(un["start"], win), un["hs"]]) + bias_ref[un["hh"], un["off"]]
            un["sc"] = _dot(q, kct_scr[un["hh"]])
        for un in units:
            m = jnp.maximum(un["sw"].max(axis=-1, keepdims=True), un["sc"].max(axis=-1, keepdims=True))
            un["pw"] = jnp.exp2(un["sw"] - m)
            un["pc"] = jnp.exp2(un["sc"] - m)
        for un in units:
            l = un["pw"].sum(axis=-1, keepdims=True) + un["pc"].sum(axis=-1, keepdims=True)
            o = (_dot(un["pw"].astype(BF16), v_ref[pl.ds(un["start"], win), un["hs"]])
                 + _dot(un["pc"].astype(BF16), vc_ref[:, un["hs"]]))
            o_ref[un["rows"], un["hs"]] = (o / l).astype(BF16)

    @pl.when(t >= n_row_blocks)
    def _():
        for hh in range(hps):
            hs = slice(hh * ATT_HEAD, (hh + 1) * ATT_HEAD)
            o_ref[:, hs] = _softmax_pv(q_ref[:, hs], [(kc_ref[:, hs], vc_ref[:, hs])]).astype(BF16)


def _na_bias_table(rpb):
    qc = jnp.arange(GRID_W)
    kc = jnp.arange(GRID_W)
    cs = jnp.clip(qc - NA_WIN_C // 2, 0, GRID_W - NA_WIN_C)
    inwin = (kc[None, :] >= cs[:, None]) & (kc[None, :] < cs[:, None] + NA_WIN_C)
    cidx = kc[None, :] - qc[:, None] + NA_WIN_C - 1
    sel = (cidx[None] == jnp.arange(2 * NA_WIN_C - 1)[:, None, None]) & inwin[None]
    cols = jnp.einsum('hrc,cqk->hrqk', rpb, sel.astype(F32), precision=lax.Precision.HIGHEST)
    cols = jnp.where(inwin[None, None], cols * math.log2(math.e), NEG_BIG)
    tab = jnp.stack([cols[:, NA_WIN_R - 1 - o:2 * NA_WIN_R - 1 - o] for o in range(NA_WIN_R)], axis=1)
    tab = tab.transpose(0, 1, 3, 2, 4)
    return tab.reshape(rpb.shape[0], NA_WIN_R, GRID_W, NA_WIN_R * GRID_W).astype(F32)


def _na_attention(qkv, rpb, *, dm, d):
    rows = qkv.shape[0]
    nb, seq, ctx = dm["nb"], dm["seq"], dm["ctx"]
    heads = d // ATT_HEAD
    grid_rows = seq // GRID_W
    rb = ctx // GRID_W
    nrb = grid_rows // rb
    ctx_blk0 = nb * seq // ctx
    bias = _na_bias_table(rpb)

    def qmap(b, h, t):
        return (jnp.where(t < nrb, b * nrb + t, ctx_blk0 + b), h)

    hps = 4 if heads % 4 == 0 else 2
    hw = hps * ATT_HEAD
    ng = heads // hps
    kern = functools.partial(_na_kernel, rb=rb, hps=hps, grid_rows=grid_rows, n_row_blocks=nrb)
    return pl.pallas_call(
        kern,
        out_shape=jax.ShapeDtypeStruct((rows, d), BF16),
        grid=(nb, ng, nrb + 1),
        in_specs=[
            pl.BlockSpec((ctx, hw), qmap),
            pl.BlockSpec((seq, hw), lambda b, h, t: (b, ng + h)),
            pl.BlockSpec((seq, hw), lambda b, h, t: (b, 2 * ng + h)),
            pl.BlockSpec((ctx, hw), lambda b, h, t: (ctx_blk0 + b, ng + h)),
            pl.BlockSpec((ctx, hw), lambda b, h, t: (ctx_blk0 + b, 2 * ng + h)),
            pl.BlockSpec((hps, NA_WIN_R, GRID_W, NA_WIN_R * GRID_W), lambda b, h, t: (h, 0, 0, 0)),
        ],
        out_specs=pl.BlockSpec((ctx, hw), qmap),
        scratch_shapes=[pltpu.VMEM((hps, ATT_HEAD, ctx), BF16)],
        compiler_params=_cparams(("parallel", "parallel", "arbitrary")),
        name="na_attention",
    )(qkv, qkv, qkv, qkv, qkv, bias)


def _rw_prep_kernel(*refs, mix, te, n_lat_rows, seq, ctx):
    (x_ref, xp_ref, xn_ref, mod_ref, g_ref, mu_ref, w1_ref, a1_ref, g1_ref, w2_ref, a2_ref, g2_ref,
     w0_ref, a0_ref) = refs[:14]
    if mix:
        v1_ref, v2_ref, v0_ref, xm_ref, lw_ref, a_ref, go_ref, vg_ref = refs[14:]
    else:
        xm_ref, lw_ref, a_ref, go_ref = refs[14:]
    i = pl.program_id(0)
    g = g_ref[...]
    shift = mod_ref[0, 0:1, :]
    scale = mod_ref[0, 1:2, :]
    h = _normmod(x_ref[...], g, shift, scale)
    halo = _normmod(jnp.concatenate([xp_ref[...], xn_ref[...]], axis=0), g, shift, scale)
    rid, first, last = _seq_edges(i, te, n_lat_rows, seq, ctx)
    hp, hn = _shift_rows(h, rid, first, last, halo[SUBLANES - 1:SUBLANES], halo[SUBLANES:SUBLANES + 1])
    xx = 0.5 * (hp + hn) - h

    def mixed(p):
        return (h + xx * mu_ref[p:p + 1, :]).astype(BF16)

    xv = mixed(2)
    xm_ref[0] = mixed(0)
    xm_ref[1] = mixed(1)
    xm_ref[2] = xv
    zw = jnp.tanh(_dot(mixed(3), w1_ref[...])).astype(BF16)
    za = _dot(mixed(4), a1_ref[...]).astype(BF16)
    zg = jax.nn.sigmoid(_dot(mixed(5), g1_ref[...])).astype(BF16)
    for dd in range(2):
        sl = slice(dd * LANES, (dd + 1) * LANES)
        wl = w0_ref[dd:dd + 1, :] + _dot(zw[:, sl], w2_ref[dd])
        lw_ref[dd] = (-math.exp(-0.5)) * jax.nn.sigmoid(wl)
        a_ref[dd] = jax.nn.sigmoid(a0_ref[dd:dd + 1, :] + _dot(za[:, sl], a2_ref[dd])).astype(BF16)
    go_ref[...] = _dot(zg, g2_ref[...]).astype(BF16)
    if mix:
        zv = _dot(xv, v1_ref[...]).astype(BF16)
        vg_ref[...] = jax.nn.sigmoid(v0_ref[...] + _dot(zv, v2_ref[...])).astype(BF16)


def _pad_rank(w1, w2):
    r = w1.shape[-1]
    pad = (-r) % LANES
    w1 = jnp.pad(w1, [(0, 0)] * (w1.ndim - 1) + [(0, pad)])
    w2 = jnp.pad(w2, [(0, 0)] * (w2.ndim - 2) + [(0, pad), (0, 0)])
    return w1.astype(BF16), w2.astype(BF16)


def _rw_prep(xs, mod, g1n, mu, w0, w1, w2, a0, a1, a2, g1, g2, vres, *, dm):
    rows, d = xs.shape
    te = dm["te"]
    hb = te // SUBLANES
    last_hb = rows // SUBLANES - 1
    mix = vres is not None
    w1p, w2p = _pad_rank(w1, w2)
    a1p, a2p = _pad_rank(a1, a2)
    w1c = jnp.concatenate([w1p[0], w1p[1]], axis=1)
    a1c = jnp.concatenate([a1p[0], a1p[1]], axis=1)
    rg = g1.shape[1]
    modmap = lambda i: (jnp.minimum((i * te) // dm["seq"], dm["nb"]), 0, 0)
    full = lambda shp: pl.BlockSpec(shp, lambda i: (0,) * len(shp))
    in_specs = [
        pl.BlockSpec((te, d), lambda i: (i, 0)),
        pl.BlockSpec((SUBLANES, d), lambda i: (jnp.maximum(i * hb - 1, 0), 0)),
        pl.BlockSpec((SUBLANES, d), lambda i: (jnp.minimum((i + 1) * hb, last_hb), 0)),
        pl.BlockSpec((1, 6, d), modmap),
        full((1, d)), full((6, d)),
        full((d, 2 * LANES)), full((d, 2 * LANES)), full((d, rg)),
        full((2, LANES, d)), full((2, LANES, d)), full((rg, d)),
        full((2, d)), full((2, d)),
    ]
    args = [xs, xs, xs, mod, g1n.reshape(1, d), mu, w1c, a1c, g1.astype(BF16), w2p, a2p, g2.astype(BF16), w0, a0]
    row_spec = pl.BlockSpec((te, d), lambda i: (i, 0))
    out_shape = [jax.ShapeDtypeStruct((3, rows, d), BF16), jax.ShapeDtypeStruct((2, rows, d), F32),
                 jax.ShapeDtypeStruct((2, rows, d), BF16), jax.ShapeDtypeStruct((rows, d), BF16)]
    out_specs = [pl.BlockSpec((3, te, d), lambda i: (0, i, 0)), pl.BlockSpec((2, te, d), lambda i: (0, i, 0)),
                 pl.BlockSpec((2, te, d), lambda i: (0, i, 0)), row_spec]
    if mix:
        v1p, v2p = _pad_rank(vres[1], vres[2])
        in_specs += [full((d, LANES)), full((LANES, d)), full((1, d))]
        args += [v1p, v2p, vres[0].reshape(1, d)]
        out_shape.append(jax.ShapeDtypeStruct((rows, d), BF16))
        out_specs.append(row_spec)
    kern = functools.partial(_rw_prep_kernel, mix=mix, te=te, n_lat_rows=dm["n_lat_rows"], seq=dm["seq"],
                             ctx=dm["ctx"])
    return pl.pallas_call(
        kern,
        out_shape=out_shape,
        grid=(rows // te,),
        in_specs=in_specs,
        out_specs=out_specs,
        compiler_params=_cparams(("parallel",)),
        name="rwkv_prep",
    )(*args)


def _rkv_kernel(*refs, mix):
    if mix:
        xm_ref, w_ref, vf_ref, vg_ref, o_ref = refs
    else:
        xm_ref, w_ref, o_ref = refs
    acc = _dot(xm_ref[0], w_ref[0])
    if mix:
        p = pl.program_id(1)

        @pl.when(p == 2)
        def _():
            o_ref[0] = (acc + (vf_ref[0].astype(F32) - acc) * vg_ref[...].astype(F32)).astype(BF16)

        @pl.when(p != 2)
        def _():
            o_ref[0] = acc.astype(BF16)
    else:
        o_ref[0] = acc.astype(BF16)


def _rkv_proj(xm, w, v_first, vgate, *, layer, dm):
    _, rows, d = xm.shape
    tm = dm["tm"]
    mix = v_first is not None
    in_specs = [pl.BlockSpec((1, tm, d), lambda i, p: (p, i, 0)),
                pl.BlockSpec((None, 1, d, d), lambda i, p: (layer, p, 0, 0))]
    args = [xm, w]
    if mix:
        in_specs += [pl.BlockSpec((1, tm, d), lambda i, p: (2, i, 0)), pl.BlockSpec((tm, d), lambda i, p: (i, 0))]
        args += [v_first, vgate]
    return pl.pallas_call(
        functools.partial(_rkv_kernel, mix=mix),
        out_shape=jax.ShapeDtypeStruct((3, rows, d), BF16),
        grid=(rows // tm, 3),
        in_specs=in_specs,
        out_specs=pl.BlockSpec((1, tm, d), lambda i, p: (p, i, 0)),
        compiler_params=_cparams(("parallel", "arbitrary")),
        name="rwkv_rkv_proj",
    )(*args)


def _wkv_kernel(rf_ref, kf_ref, vf_ref, lwf_ref, af_ref, rb_ref, kb_ref, vb_ref, lwb_ref, ab_ref,
                kk_ref, ka_ref, rk_ref, yf_ref, bonf_ref, yb_ref, bonb_ref, s_scr, *, gps):
    c = pl.program_id(2)
    ln = WKV_CHUNK
    pw = WKV_PACK * RW_HEAD

    hpt = LANES // RW_HEAD

    @pl.when(c == 0)
    def _():
        s_scr[...] = jnp.zeros_like(s_scr)

    row = lax.broadcasted_iota(jnp.int32, (ln, ln), 0)
    col = lax.broadcasted_iota(jnp.int32, (ln, ln), 1)
    trow = lax.broadcasted_iota(jnp.int32, (ln, pw), 0)
    tsrc = lax.broadcasted_iota(jnp.int32, (ln, pw), 1) & (ln - 1)
    ones64 = _ones_blockdiag64()

    def fmask(cond):
        return jnp.where(cond, 1.0, 0.0)

    def same_block(n):
        sh = n.bit_length() - 1
        return (tsrc >> sh) == (trow >> sh)

    eye = fmask(tsrc == trow)
    base_f = fmask(same_block(WKV_INV_BASE))
    off_f = {}
    n = WKV_INV_BASE
    while n < ln:
        off_f[n] = fmask(jnp.logical_and(same_block(2 * n), jnp.logical_not(same_block(n))))
        n *= 2
    k_k = kk_ref[...]
    k_a = ka_ref[...]
    r_k = rk_ref[...]

    lane = lax.broadcasted_iota(jnp.int32, (ln, LANES), 1)
    half_f = [fmask((lane >> (RW_HEAD.bit_length() - 1)) == hf) for hf in range(hpt)]
    half_b = [hm.astype(BF16) for hm in half_f]
    zeros_b = jnp.zeros((ln, LANES), BF16)

    def bd(z):
        zb = z.astype(BF16)
        blocks = []
        for jh in range(WKV_PACK):
            lt, hf = divmod(jh, hpt)
            piece = zb[:, lt * LANES:(lt + 1) * LANES] * half_b[hf]
            blocks.append(jnp.concatenate([piece if tt == lt else zeros_b for tt in range(pw // LANES)], axis=1))
        return jnp.concatenate(blocks, axis=0)

    streams = ((rf_ref, kf_ref, vf_ref, lwf_ref, af_ref, yf_ref, bonf_ref),
               (rb_ref, kb_ref, vb_ref, lwb_ref, ab_ref, yb_ref, bonb_ref))
    units = []
    for dd, (r_ref, k_ref, v_ref, lw_ref, a_ref, y_ref, bon_ref) in enumerate(streams):
        rev = dd == 1
        r = r_ref[0].astype(F32)
        k = k_ref[0].astype(F32)
        v = v_ref[0].astype(F32)
        a = a_ref[0].astype(F32)
        lw = lw_ref[0]
        kkr = k * k_k
        kd = k * (1.0 + (a - 1.0) * k_a)
        ssq, rkd = _segsum64_mxu([kkr * kkr, r * kd * r_k], ones64, split=False)
        kk = kkr * lax.rsqrt(jnp.maximum(ssq, 1e-24))
        bvec = kk * a
        bon_ref[...] = (rkd * v).astype(BF16)

        tri = jnp.where((col >= row) if rev else (col <= row), 1.0, 0.0).astype(BF16)
        hi = lw.astype(BF16)
        lo = (lw - hi.astype(F32)).astype(BF16)
        cum = _dot(tri, hi) + _dot(tri, lo)
        tot = cum[0:1, :] if rev else cum[ln - 1:ln, :]
        w_inv = jnp.exp(-cum)
        w_end = jnp.exp(tot - cum)
        a_t = -kk * jnp.exp(cum - lw)
        r_t = r * jnp.exp(cum)
        b_t = bvec * w_inv
        k_t = kd * w_inv
        b_e = bvec * w_end
        k_e = kd * w_end
        w_tot = jnp.exp(tot)

        strict = fmask(tsrc > trow) if rev else fmask(tsrc < trow)
        incl = strict + eye

        for gi in range(gps):
            sl = slice(gi * pw, (gi + 1) * pw)
            units.append(dict(
                dd=dd, gi=gi, sl=sl, y_ref=y_ref, strict=strict, incl=incl,
                ar=jnp.concatenate([a_t[:, sl], r_t[:, sl]], axis=0).astype(BF16),
                b_t=b_t[:, sl], k_t=k_t[:, sl], v=v[:, sl], w_tot=w_tot[:, sl],
                bk=jnp.concatenate([b_e[:, sl], k_e[:, sl]], axis=0).astype(BF16)))

    for un in units:
        sb = _dot_nt(un["ar"], bd(un["b_t"]))
        sk = _dot_nt(un["ar"], bd(un["k_t"]))
        un["m_ab"] = sb[:ln] * un["strict"]
        un["p_rb"] = sb[ln:] * un["incl"]
        un["m_ak"] = sk[:ln] * un["strict"]
        un["p_rk"] = sk[ln:] * un["incl"]
    for un in units:
        un["s0"] = s_scr[un["dd"], un["gi"]]
        un["ars"] = _dot(un["ar"], un["s0"].T.astype(BF16))
        un["mv"] = _dot(jnp.concatenate([un["m_ak"], un["p_rk"]], axis=0).astype(BF16), bd(un["v"]))
    for un in units:
        m0 = un["m_ab"] * base_f
        un["pinv"] = eye + m0
        un["mp"] = _dot(m0.astype(BF16), bd(m0))
    for un in units:
        both = _dot(jnp.concatenate([un["mp"], un["pinv"]], axis=0).astype(BF16), bd(un["mp"]))
        un["pinv"] = un["pinv"] + both[ln:]
        un["mp"] = both[:ln]
    for un in units:
        un["pinv"] = un["pinv"] + _dot(un["pinv"].astype(BF16), bd(un["mp"]))
    n = WKV_INV_BASE
    while n < ln:
        for un in units:
            un["t1"] = _dot((un["m_ab"] * off_f[n]).astype(BF16), bd(un["pinv"]))
        for un in units:
            un["pinv"] = un["pinv"] + _dot(un["pinv"].astype(BF16), bd(un["t1"]))
        n *= 2
    for un in units:
        un["u"] = _dot(un["pinv"].astype(BF16), bd(un["ars"][:ln] + un["mv"][:ln]))
    for un in units:
        un["y_ref"][:, un["sl"]] = (un["ars"][ln:] + _dot(un["p_rb"].astype(BF16), bd(un["u"]))
                                    + un["mv"][ln:]).astype(BF16)
        uv = jnp.concatenate([un["u"], un["v"]], axis=0).astype(BF16)
        res = _dot_tn(uv, un["bk"])
        for jh in range(WKV_PACK):
            lt, hf = divmod(jh, hpt)
            rsl = slice(jh * RW_HEAD, (jh + 1) * RW_HEAD)
            csl = slice(lt * LANES, (lt + 1) * LANES)
            s_scr[un["dd"], un["gi"], rsl, csl] = (un["s0"][rsl, csl] * un["w_tot"][:, csl]
                                                    + res[rsl, csl] * half_f[hf])


def _wkv(rkv, lw, a, k_k, k_a, r_k, *, dm, gps):
    _, rows, d = rkv.shape
    nb, seq, ctx = dm["nb"], dm["seq"], dm["ctx"]
    ln = WKV_CHUNK
    sw = gps * WKV_PACK * RW_HEAD
    ncc, nlc = ctx // ln, seq // ln
    ctx_c0 = nb * seq // ln

    def fblk(b, c):
        return jnp.where(c < ncc, ctx_c0 + b * ncc + c, b * nlc + (c - ncc))

    def bblk(b, c):
        return jnp.where(c < ncc, ctx_c0 + b * ncc + (ncc - 1 - c), b * nlc + (nlc - 1 - (c - ncc)))

    def spec3(p, blk):
        return pl.BlockSpec((1, ln, sw), lambda b, s, c, p=p, blk=blk: (p, blk(b, c), s))

    def spec2(blk):
        return pl.BlockSpec((ln, sw), lambda b, s, c, blk=blk: (blk(b, c), s))

    pspec = pl.BlockSpec((1, sw), lambda b, s, c: (0, s))
    in_specs = [spec3(0, fblk), spec3(1, fblk), spec3(2, fblk), spec3(0, fblk), spec3(0, fblk),
                spec3(0, bblk), spec3(1, bblk), spec3(2, bblk), spec3(1, bblk), spec3(1, bblk),
                pspec, pspec, pspec]
    return pl.pallas_call(
        functools.partial(_wkv_kernel, gps=gps),
        out_shape=[jax.ShapeDtypeStruct((rows, d), BF16)] * 4,
        grid=(nb, d // sw, ncc + nlc),
        in_specs=in_specs,
        out_specs=[spec2(fblk), spec2(fblk), spec2(bblk), spec2(bblk)],
        scratch_shapes=[pltpu.VMEM((2, gps, WKV_PACK * RW_HEAD, WKV_PACK * RW_HEAD), F32)],
        compiler_params=_cparams(("parallel", "parallel", "arbitrary")),
        name="wkv_scan",
    )(rkv, rkv, rkv, lw, a, rkv, rkv, rkv, lw, a, k_k.reshape(1, d), k_a.reshape(1, d), r_k.reshape(1, d))


def _rw_out_kernel(yf_ref, yb_ref, bf_ref, bb_ref, g_ref, x_ref, mod_ref, lg_ref, lb_ref, w_ref, o_ref):
    ones64 = _ones_blockdiag64()
    half = o_ref.shape[0] // 2

    def gated(rs):
        y = yf_ref[rs, :].astype(F32) + yb_ref[rs, :].astype(F32)
        mean = _segsum64_mxu([y], ones64)[0] * (1.0 / RW_HEAD)
        yc = y - mean
        var = _segsum64_mxu([yc * yc], ones64, split=False)[0] * (1.0 / RW_HEAD)
        yn = yc * lax.rsqrt(var + RW_GN_EPS)
        bonus = bf_ref[rs, :].astype(F32) + bb_ref[rs, :].astype(F32)
        return ((yn * lg_ref[...] + lb_ref[...] + bonus) * g_ref[rs, :].astype(F32)).astype(BF16)

    rows = [slice(0, half), slice(half, 2 * half)]
    o0 = gated(rows[0])
    p0 = _dot(o0, w_ref[...])
    o1 = gated(rows[1])
    o_ref[rows[0], :] = x_ref[rows[0], :] + mod_ref[0, 2:3, :] * p0
    o_ref[rows[1], :] = x_ref[rows[1], :] + mod_ref[0, 2:3, :] * _dot(o1, w_ref[...])


def _rw_out(yf, bonf, yb, bonb, g, xs, mod, ln_g, ln_b, w_o, *, dm, n_rows):
    d = xs.shape[1]
    te = dm["te"]
    modmap = lambda i: (jnp.minimum((i * te) // dm["seq"], dm["nb"]), 0, 0)
    rspec = pl.BlockSpec((te, d), lambda i: (i, 0))
    vspec = pl.BlockSpec((1, d), lambda i: (0, 0))
    return pl.pallas_call(
        _rw_out_kernel,
        out_shape=jax.ShapeDtypeStruct((n_rows, d), F32),
        grid=(n_rows // te,),
        in_specs=[rspec, rspec, rspec, rspec, rspec, rspec, pl.BlockSpec((1, 6, d), modmap), vspec, vspec,
                  pl.BlockSpec((d, d), lambda i: (0, 0))],
        out_specs=rspec,
        compiler_params=_cparams(("parallel",)),
        name="rwkv_out",
    )(yf, yb, bonf, bonb, g, xs, mod, ln_g.reshape(1, d), ln_b.reshape(1, d), w_o)


def _rope_tables(dm):
    seq, nb, ctx = dm["seq"], dm["nb"], dm["ctx"]
    t = jnp.arange(seq, dtype=jnp.int32)
    pos = jnp.stack([t // GRID_W, t % GRID_W], axis=-1).astype(F32)
    n_freq = ATT_HEAD // 4
    inv = ROPE_THETA ** (-jnp.arange(n_freq, dtype=F32) / n_freq)
    ang = pos[:, :, None] * inv
    cos, sin = jnp.cos(ang), jnp.sin(ang)
    zero = jnp.zeros_like(sin)
    cos_t = jnp.stack([cos, cos], axis=2).reshape(seq, ATT_HEAD)
    sa_t = jnp.stack([-sin, zero], axis=2).reshape(seq, ATT_HEAD)
    sb_t = jnp.stack([zero, sin], axis=2).reshape(seq, ATT_HEAD)
    nctx = nb * ctx
    full = lambda tab, fill: jnp.concatenate([jnp.tile(tab, (nb, 1)), jnp.full((nctx, ATT_HEAD), fill, F32)], axis=0)
    return full(cos_t, 1.0), full(sa_t, 0.0), full(sb_t, 0.0)


def kernel(x, c, ctx, c_ctx, mod_w, mod_b, norm1_g, norm2_g, ffn_up, ffn_conv_w, ffn_conv_b, ffn_down, rw_mu, rw_w_rkv, rw_w0, rw_w1, rw_w2, rw_a0, rw_a1, rw_a2, rw_g1, rw_g2, rw_k_k, rw_k_a, rw_r_k, rw_ln_g, rw_ln_b, rw_w_o, rw_v0, rw_v1, rw_v2, na_w_qkv, na_q_g, na_k_g, na_rpb, na_w_o, ga_w_qkv, ga_q_g, ga_k_g, ga_w_o):
    nb, seq, d = x.shape
    nctx = ctx.shape[1]
    depth = mod_w.shape[0]
    tm = nb * nctx
    assert seq % tm == 0 and seq & (seq - 1) == 0 and nctx & (nctx - 1) == 0 and nb + 1 <= SUBLANES
    assert seq // GRID_W >= NA_WIN_R and nctx % GRID_W == 0
    dm = dict(nb=nb, seq=seq, ctx=nctx, tm=tm, te=tm // 2, tpb=seq // tm, n_lat_rows=nb * seq)
    n_lat_tiles = nb * seq // tm
    n_tiles = n_lat_tiles + 1
    att_scale = ATT_HEAD ** -0.5 * math.log2(math.e)

    xs = jnp.concatenate([x.reshape(nb * seq, d), ctx.reshape(nb * nctx, d)], axis=0)
    c_all = jnp.concatenate([c, c_ctx[None], jnp.zeros((SUBLANES - nb - 1, d), F32)], axis=0)
    mods = _modulations(c_all, mod_w, mod_b)
    rope_tabs = None
    v_first = None
    ffn_wu, ffn_wd = _ffn_weights(ffn_up, ffn_down)
    rw_rkv_w = rw_w_rkv.astype(BF16)

    for i in range(depth):
        kind, j = i % 3, i // 3
        need_ctx = i < depth - 1
        nt_out = n_tiles if need_ctx else n_lat_tiles
        mod = mods[i, :nb + 1].reshape(nb + 1, 6, d)
        if kind == 0:
            vres = None if j == 0 else (rw_v0[j - 1], rw_v1[j - 1], rw_v2[j - 1])
            prep = _rw_prep(xs, mod, norm1_g[i], rw_mu[j], rw_w0[j], rw_w1[j], rw_w2[j], rw_a0[j], rw_a1[j],
                            rw_a2[j], rw_g1[j], rw_g2[j], vres, dm=dm)
            xm, lw, a, g = prep[0], prep[1], prep[2], prep[3]
            rkv = _rkv_proj(xm, rw_rkv_w, v_first if vres is not None else None,
                            prep[4] if vres is not None else None, layer=j, dm=dm)
            if v_first is None:
                v_first = rkv
            yf, bonf, yb, bonb = _wkv(rkv, lw, a, rw_k_k[j], rw_k_a[j], rw_r_k[j].reshape(-1), dm=dm,
                                      gps=min(8, d // (WKV_PACK * RW_HEAD)))
            xs = _rw_out(yf, bonf, yb, bonb, g, xs, mod, rw_ln_g[j], rw_ln_b[j], rw_w_o[j].astype(BF16),
                         dm=dm, n_rows=nt_out * tm)
        elif kind == 1:
            qkv = _qkv_proj(xs, mod, norm1_g[i], na_w_qkv[j].astype(BF16), na_q_g[j] * att_scale, na_k_g[j],
                            None, dm=dm, kv_dim=d)
            o = _na_attention(qkv, na_rpb[j], dm=dm, d=d)
            xs = _out_proj(o, na_w_o[j].astype(BF16), xs, mod, dm=dm, n_tiles=nt_out)
        else:
            if rope_tabs is None:
                rope_tabs = _rope_tables(dm)
            kv_dim = (ga_w_qkv.shape[-1] - d) // 2
            qkv = _qkv_proj(xs, mod, norm1_g[i], ga_w_qkv[j].astype(BF16), ga_q_g[j] * att_scale, ga_k_g[j],
                            rope_tabs, dm=dm, kv_dim=kv_dim)
            o = _gqa_attention(qkv, dm=dm, d=d, kv_heads=kv_dim // ATT_HEAD)
            xs = _out_proj(o, ga_w_o[j].astype(BF16), xs, mod, dm=dm, n_tiles=nt_out)
        xs = _ffn(xs, mod, norm2_g[i], ffn_wu, ffn_conv_w[i], ffn_conv_b[i], ffn_wd, layer=i, dm=dm,
                  need_ctx=need_ctx)
    return xs[:nb * seq].reshape(nb, seq, d)
```
